```python
import math
import jax
import jax.numpy as jnp
from jax import lax
import numpy as np

D_MODEL = 1024
BATCH = 16
SEQ = 2048
DEPTH = 1

SSM_EXPAND = 2
D_INNER = SSM_EXPAND * D_MODEL
SSM_HEAD_DIM = 64
SSM_HEADS = D_INNER // SSM_HEAD_DIM
SSM_GROUPS = 4
D_STATE = 128
CONV_WIDTH = 4
SSD_CHUNK = 128
CONV_DIM = D_INNER + 2 * SSM_GROUPS * D_STATE
ATTN_HEADS = 16
KV_HEADS = 4
HEAD_DIM = 64
WINDOW = 128
N_EXPERTS = 32
TOP_K = 4
D_FF = D_MODEL
SWIGLU_LIMIT = 7.0
SWIGLU_ALPHA = 1.702
MOE_BLOCK = 256
N_BRANCHES = 2
LN_EPS = 1e-5
RMS_EPS = 1e-5
DEEPNORM_ALPHA = (2 * DEPTH) ** 0.25
DEEPNORM_BETA = (8 * DEPTH) ** -0.25
IN_SPLITS = (D_INNER, CONV_DIM, SSM_HEADS, ATTN_HEADS * HEAD_DIM, KV_HEADS * HEAD_DIM, KV_HEADS * HEAD_DIM, N_BRANCHES * D_MODEL)
IN_DIM = sum(IN_SPLITS)
V_SPLIT_INDEX = 5

kernel_name = 'hybrid_ssd_swa_moe_deepnorm'


def _split_points():
    return [int(o) for o in np.cumsum(IN_SPLITS)[:-1]]


def layer_norm(x, g, b):
    xf = x.astype(jnp.float32)
    mu = jnp.mean(xf, axis=-1, keepdims=True)
    var = jnp.mean(jnp.square(xf - mu), axis=-1, keepdims=True)
    return ((xf - mu) * lax.rsqrt(var + LN_EPS) * g.astype(jnp.float32) + b.astype(jnp.float32)).astype(x.dtype)


def causal_depthwise_conv(u, w, bias):
    out = lax.conv_general_dilated(
        u, w[:, None, :].astype(u.dtype), window_strides=(1,), padding=[(CONV_WIDTH - 1, 0)],
        dimension_numbers=('NWC', 'WIO', 'NWC'), feature_group_count=u.shape[-1])
    return out + bias.astype(u.dtype)


def ssd_chunked_scan(xh, dt, A, Bm, Cm):
    b, s, h, p = xh.shape
    g, n = Bm.shape[2], Bm.shape[3]
    e = h // g
    l = SSD_CHUNK
    c = s // l
    xs = (xh.astype(jnp.float32) * dt[..., None]).reshape(b, c, l, g, e, p)
    Bc = Bm.astype(jnp.float32).reshape(b, c, l, g, n)
    Cc = Cm.astype(jnp.float32).reshape(b, c, l, g, n)
    a_cum = jnp.cumsum((dt * A).reshape(b, c, l, g, e), axis=2)
    seg = a_cum[:, :, :, None] - a_cum[:, :, None, :]
    causal = jnp.tril(jnp.ones((l, l), dtype=bool))[None, None, :, :, None, None]
    decay = jnp.exp(jnp.where(causal, seg, -jnp.inf))
    cb = jnp.einsum('bctgn,bcsgn->bctsg', Cc, Bc)
    y_diag = jnp.einsum('bctsg,bctsge,bcsgep->bctgep', cb, decay, xs)
    decay_to_end = jnp.exp(a_cum[:, :, -1:] - a_cum)
    states = jnp.einsum('bclgn,bclge,bclgep->bcgepn', Bc, decay_to_end, xs)
    chunk_decay = jnp.exp(a_cum[:, :, -1])

    def step(carry, inp):
        st, dec = inp
        return carry * dec[..., None, None] + st, carry

    init = jnp.zeros((b, g, e, p, n), jnp.float32)
    _, prev = lax.scan(step, init, (jnp.moveaxis(states, 1, 0), jnp.moveaxis(chunk_decay, 1, 0)))
    prev = jnp.moveaxis(prev, 0, 1)
    y_off = jnp.einsum('bctgn,bcgepn,bctge->bctgep', Cc, prev, jnp.exp(a_cum))
    return (y_diag + y_off).reshape(b, s, h, p)


def mamba2_branch(z, xbc, dt_raw, conv_w, conv_b, dt_bias, a_log, d_skip, norm_w):
    b, s, _ = z.shape
    xbc = jax.nn.silu(causal_depthwise_conv(xbc, conv_w, conv_b))
    xh, Bm, Cm = jnp.split(xbc, [D_INNER, D_INNER + SSM_GROUPS * D_STATE], axis=-1)
    xh = xh.reshape(b, s, SSM_HEADS, SSM_HEAD_DIM)
    Bm = Bm.reshape(b, s, SSM_GROUPS, D_STATE)
    Cm = Cm.reshape(b, s, SSM_GROUPS, D_STATE)
    dt = jax.nn.softplus(dt_raw.astype(jnp.float32) + dt_bias.astype(jnp.float32))
    A = -jnp.exp(a_log.astype(jnp.float32))
    y = ssd_chunked_scan(xh, dt, A, Bm, Cm)
    y = y + xh.astype(jnp.float32) * d_skip.astype(jnp.float32)[:, None]
    y = y.reshape(b, s, D_INNER) * jax.nn.silu(z.astype(jnp.float32))
    yg = y.reshape(b, s, SSM_GROUPS, D_INNER // SSM_GROUPS)
    yg = yg * lax.rsqrt(jnp.mean(yg * yg, axis=-1, keepdims=True) + RMS_EPS)
    return (yg.reshape(b, s, D_INNER) * norm_w.astype(jnp.float32)).astype(z.dtype)


def sliding_window_attention(q, k, v, sinks):
    b, s, _ = q.shape
    nb = s // WINDOW
    grp = ATTN_HEADS // KV_HEADS
    qb = q.reshape(b, nb, WINDOW, KV_HEADS, grp, HEAD_DIM)
    kb = k.reshape(b, nb, WINDOW, KV_HEADS, HEAD_DIM)
    vb = v.reshape(b, nb, WINDOW, KV_HEADS, HEAD_DIM)

    def with_prev(t):
        prev = jnp.pad(t, ((0, 0), (1, 0), (0, 0), (0, 0), (0, 0)))[:, :-1]
        return jnp.concatenate([prev, t], axis=2)

    kk, vv = with_prev(kb), with_prev(vb)
    scores = jnp.einsum('bnqkgd,bnskd->bnkgqs', qb, kk).astype(jnp.float32) * (HEAD_DIM ** -0.5)
    qpos = jnp.arange(WINDOW)[:, None] + WINDOW
    kpos = jnp.arange(2 * WINDOW)[None, :]
    band = (qpos - kpos >= 0) & (qpos - kpos < WINDOW)
    has_prev = (jnp.arange(nb)[:, None] > 0) | (kpos >= WINDOW)
    mask = band[None] & has_prev[:, None, :]
    scores = jnp.where(mask[None, :, None, None], scores, -jnp.inf)
    sink = jnp.broadcast_to(sinks.astype(jnp.float32).reshape(1, 1, KV_HEADS, grp, 1, 1), scores.shape[:-1] + (1,))
    probs = jax.nn.softmax(jnp.concatenate([scores, sink], axis=-1), axis=-1)[..., :-1]
    out = jnp.einsum('bnkgqs,bnskd->bnqkgd', probs.astype(v.dtype), vv)
    return out.reshape(b, s, ATTN_HEADS * HEAD_DIM)


def token_mixer(x, w_in, conv_w, conv_b, dt_bias, a_log, d_skip, ssm_norm_w, w_ssm_out,
                attn_sinks, w_attn_out, b_gates, w_mix_out):
    b, s, _ = x.shape
    proj = x @ w_in
    z, xbc, dt_raw, q, k, v, gate_logits = jnp.split(proj, _split_points(), axis=-1)
    y_ssm = mamba2_branch(z, xbc, dt_raw, conv_w, conv_b, dt_bias, a_log, d_skip, ssm_norm_w) @ w_ssm_out
    y_attn = sliding_window_attention(q, k, v, attn_sinks) @ w_attn_out
    gates = jax.nn.sigmoid((gate_logits + b_gates).astype(jnp.float32)).reshape(b, s, N_BRANCHES, D_MODEL)
    merged = gates[:, :, 0] * y_ssm.astype(jnp.float32) + gates[:, :, 1] * y_attn.astype(jnp.float32)
    return merged.astype(x.dtype) @ w_mix_out


def moe_ffn(h, w_router, b_router, w_gate_up, b_gate_up, w_down, b_down):
    b, s, d = h.shape
    n_tok = b * s
    n_slots = n_tok * TOP_K
    xt = h.reshape(n_tok, d)
    logits = xt.astype(jnp.float32) @ w_router.astype(jnp.float32) + b_router.astype(jnp.float32)
    top_logits, top_idx = lax.top_k(logits, TOP_K)
    gate_w = jax.nn.softmax(top_logits, axis=-1)
    flat_e = top_idx.reshape(-1)
    order = jnp.argsort(flat_e)
    sorted_e = flat_e[order]
    counts = jnp.bincount(flat_e, length=N_EXPERTS)
    padded = (counts + MOE_BLOCK - 1) // MOE_BLOCK * MOE_BLOCK
    pad_end = jnp.cumsum(padded)
    pad_start = pad_end - padded
    start = jnp.cumsum(counts) - counts
    dest = pad_start[sorted_e] + (jnp.arange(n_slots) - start[sorted_e])
    n_blocks = -(-n_slots // MOE_BLOCK) + N_EXPERTS
    n_rows = n_blocks * MOE_BLOCK
    row_tok = jnp.zeros((n_rows,), jnp.int32).at[dest].set((order // TOP_K).astype(jnp.int32))
    block_expert = jnp.minimum(jnp.searchsorted(pad_end, jnp.arange(n_blocks) * MOE_BLOCK, side='right'), N_EXPERTS - 1)
    xs = xt[row_tok].reshape(n_blocks, MOE_BLOCK, d)

    def expert_block(args):
        xb, e = args
        hgu = xb @ w_gate_up[e] + b_gate_up[e]
        glu, lin = jnp.split(hgu, 2, axis=-1)
        glu = jnp.minimum(glu, SWIGLU_LIMIT)
        lin = jnp.clip(lin, -SWIGLU_LIMIT, SWIGLU_LIMIT)
        act = glu * jax.nn.sigmoid(SWIGLU_ALPHA * glu) * (lin + 1.0)
        return act @ w_down[e] + b_down[e]

    ys = lax.map(expert_block, (xs, block_expert)).reshape(n_rows, d)
    slot_row = jnp.zeros((n_slots,), jnp.int32).at[order].set(dest.astype(jnp.int32))
    y_slots = ys[slot_row].reshape(n_tok, TOP_K, d)
    out = jnp.einsum('tk,tkd->td', gate_w, y_slots.astype(jnp.float32))
    return out.astype(h.dtype).reshape(b, s, d)


def _normal(key, shape, scale):
    return jax.random.normal(key, shape, jnp.float32) * scale


def setup_inputs(seed: int = 0) -> dict:
    key = jax.random.key(seed)
    ks = jax.random.split(key, 24)
    L = DEPTH
    beta = DEEPNORM_BETA
    x = jax.random.normal(ks[0], (BATCH, SEQ, D_MODEL), jnp.float32)
    offs = np.cumsum((0,) + IN_SPLITS)
    v0, v1 = int(offs[V_SPLIT_INDEX]), int(offs[V_SPLIT_INDEX + 1])
    w_in = _normal(ks[1], (L, D_MODEL, IN_DIM), D_MODEL ** -0.5)
    w_in = w_in.at[:, :, v0:v1].multiply(beta)
    conv_w = _normal(ks[2], (L, CONV_WIDTH, CONV_DIM), CONV_WIDTH ** -0.5)
    conv_b = _normal(ks[3], (L, CONV_DIM), 0.02)
    u = jax.random.uniform(ks[4], (L, SSM_HEADS), jnp.float32)
    dt0 = jnp.exp(u * (math.log(0.1) - math.log(0.001)) + math.log(0.001))
    dt_bias = dt0 + jnp.log(-jnp.expm1(-dt0))
    a_log = jnp.log(jax.random.uniform(ks[5], (L, SSM_HEADS), jnp.float32, minval=1.0, maxval=16.0))
    d_skip = 1.0 + _normal(ks[6], (L, SSM_HEADS), 0.1)
    ssm_norm_w = 1.0 + _normal(ks[7], (L, D_INNER), 0.02)
    w_ssm_out = _normal(ks[8], (L, D_INNER, D_MODEL), beta * D_INNER ** -0.5)
    attn_sinks = _normal(ks[9], (L, ATTN_HEADS), 0.5)
    w_attn_out = _normal(ks[10], (L, ATTN_HEADS * HEAD_DIM, D_MODEL), beta * (ATTN_HEADS * HEAD_DIM) ** -0.5)
    b_gates = _normal(ks[11], (L, N_BRANCHES * D_MODEL), 0.1)
    w_mix_out = _normal(ks[12], (L, D_MODEL, D_MODEL), beta * D_MODEL ** -0.5)
    ln1_g = 1.0 + _normal(ks[13], (L, D_MODEL), 0.02)
    ln1_b = _normal(ks[14], (L, D_MODEL), 0.02)
    w_router = _normal(ks[15], (L, D_MODEL, N_EXPERTS), D_MODEL ** -0.5)
    b_router = _normal(ks[16], (L, N_EXPERTS), 0.01)
    w_gate_up = _normal(ks[17], (L, N_EXPERTS, D_MODEL, 2 * D_FF), beta * D_MODEL ** -0.5)
    b_gate_up = _normal(ks[18], (L, N_EXPERTS, 2 * D_FF), 0.02)
    w_down = _normal(ks[19], (L, N_EXPERTS, D_FF, D_MODEL), beta * D_FF ** -0.5)
    b_down = _normal(ks[20], (L, N_EXPERTS, D_MODEL), 0.02)
    ln2_g = 1.0 + _normal(ks[21], (L, D_MODEL), 0.02)
    ln2_b = _normal(ks[22], (L, D_MODEL), 0.02)
    return {'x': x, 'w_in': w_in, 'conv_w': conv_w, 'conv_b': conv_b, 'dt_bias': dt_bias,
            'a_log': a_log, 'd_skip': d_skip, 'ssm_norm_w': ssm_norm_w, 'w_ssm_out': w_ssm_out,
            'attn_sinks': attn_sinks, 'w_attn_out': w_attn_out, 'b_gates': b_gates,
            'w_mix_out': w_mix_out, 'ln1_g': ln1_g, 'ln1_b': ln1_b, 'w_router': w_router,
            'b_router': b_router, 'w_gate_up': w_gate_up, 'b_gate_up': b_gate_up,
            'w_down': w_down, 'b_down': b_down, 'ln2_g': ln2_g, 'ln2_b': ln2_b}


def reference(x, w_in, conv_w, conv_b, dt_bias, a_log, d_skip, ssm_norm_w, w_ssm_out,
              attn_sinks, w_attn_out, b_gates, w_mix_out, ln1_g, ln1_b, w_router, b_router,
              w_gate_up, b_gate_up, w_down, b_down, ln2_g, ln2_b):
    h = x
    for i in range(DEPTH):
        mix = token_mixer(h, w_in[i], conv_w[i], conv_b[i], dt_bias[i], a_log[i], d_skip[i],
                          ssm_norm_w[i], w_ssm_out[i], attn_sinks[i], w_attn_out[i], b_gates[i],
                          w_mix_out[i])
        h = layer_norm(DEEPNORM_ALPHA * h + mix, ln1_g[i], ln1_b[i])
        ffn = moe_ffn(h, w_router[i], b_router[i], w_gate_up[i], b_gate_up[i], w_down[i], b_down[i])
        h = layer_norm(DEEPNORM_ALPHA * h + ffn, ln2_g[i], ln2_b[i])
    return h
```

```python
import functools

import jax
import jax.numpy as jnp
from jax import lax
from jax.experimental import pallas as pl
from jax.experimental.pallas import tpu as pltpu

SSM_HEAD_DIM = 64
SSM_GROUPS = 4
D_STATE = 128
CONV_WIDTH = 4
SSD_CHUNK = 128
KV_HEADS = 4
HEAD_DIM = 64
WINDOW = 128
TOP_K = 4
SWIGLU_LIMIT = 7.0
SWIGLU_ALPHA = 1.702
MOE_BLOCK = 256
LN_EPS = 1e-5
RMS_EPS = 1e-5

LANES = 128
SUBLANES = 8
NEG_BIG = -1e30
F32 = jnp.float32
BF16 = jnp.bfloat16
VMEM_LIMIT = 56 * 1024 * 1024


def _cparams(sem):
    return pltpu.CompilerParams(dimension_semantics=sem, vmem_limit_bytes=VMEM_LIMIT)


def _sigmoid(x):
    return 1.0 / (1.0 + jnp.exp(-x))


def _mm_kernel(x_ref, w_ref, o_ref):
    o_ref[...] = jnp.dot(x_ref[...].astype(BF16), w_ref[...],
                         preferred_element_type=F32).astype(o_ref.dtype)


def _matmul(x, w, out_dtype, tm, tn):
    m, k = x.shape
    n = w.shape[1]
    tm = min(tm, m)
    assert m % tm == 0 and n % tn == 0
    return pl.pallas_call(
        _mm_kernel,
        grid=(m // tm, n // tn),
        in_specs=[pl.BlockSpec((tm, k), lambda i, j: (i, 0)),
                  pl.BlockSpec((k, tn), lambda i, j: (0, j))],
        out_specs=pl.BlockSpec((tm, tn), lambda i, j: (i, j)),
        out_shape=jax.ShapeDtypeStruct((m, n), out_dtype),
        compiler_params=_cparams(("parallel", "parallel")),
        name="in_proj",
    )(x, w)


def _conv_silu(u_ref, prev_ref, w_ref, b_ref, ext_ref, first):
    n = u_ref.shape[0]
    prev = prev_ref[...]
    ext_ref[0:SUBLANES, :] = jnp.where(first, jnp.zeros_like(prev), prev)
    ext_ref[SUBLANES:SUBLANES + n, :] = u_ref[...]
    acc = b_ref[...]
    for j in range(CONV_WIDTH):
        off = SUBLANES - (CONV_WIDTH - 1) + j
        acc = acc + w_ref[j:j + 1, :] * ext_ref[off:off + n, :]
    return acc * _sigmoid(acc)


def _pair_expand(v, h0, lo_mask):
    n = v.shape[0]
    a = jnp.broadcast_to(v[:, h0:h0 + 1], (n, LANES))
    b = jnp.broadcast_to(v[:, h0 + 1:h0 + 2], (n, LANES))
    return jnp.where(lo_mask, a, b)


def _ssd_kernel(z_ref, x_ref, b_ref, c_ref, dt_ref, xp_ref, bp_ref, cp_ref,
                cwx_ref, cwb_ref, cwc_ref, cbx_ref, cbb_ref, cbc_ref,
                dtb_ref, alog_ref, dsk_ref, nw_ref,
                o_ref, st_ref, extx_ref, extb_ref, extc_ref):
    c = pl.program_id(1)
    first = c == 0
    n = SSD_CHUNK
    heads_per_group = st_ref.shape[2] // SSM_HEAD_DIM
    gw = heads_per_group * SSM_HEAD_DIM

    @pl.when(first)
    def _():
        st_ref[...] = jnp.zeros_like(st_ref)

    xc = _conv_silu(x_ref, xp_ref, cwx_ref, cbx_ref, extx_ref, first)
    bc = _conv_silu(b_ref, bp_ref, cwb_ref, cbb_ref, extb_ref, first)
    cc = _conv_silu(c_ref, cp_ref, cwc_ref, cbc_ref, extc_ref, first)

    dt_in = dt_ref[...] + dtb_ref[...]
    dt = jnp.maximum(dt_in, 0.0) + jnp.log(1.0 + jnp.exp(-jnp.abs(dt_in)))
    a = dt * (-jnp.exp(alog_ref[...]))
    row = lax.broadcasted_iota(jnp.int32, (n, n), 0)
    col = lax.broadcasted_iota(jnp.int32, (n, n), 1)
    causal = row >= col
    tri = jnp.where(causal, 1.0, 0.0).astype(F32)
    acum = jnp.dot(tri, a, preferred_element_type=F32, precision=lax.Precision.HIGHEST)
    acum_t = acum.T
    a_last = acum[n - 1:n, :]
    d2e = jnp.exp(a_last - acum)
    eac = jnp.exp(acum)
    lo_mask = lax.broadcasted_iota(jnp.int32, (n, LANES), 1) < SSM_HEAD_DIM

    for g in range(SSM_GROUPS):
        xg = xc[:, g * gw:(g + 1) * gw]
        bg = bc[:, g * D_STATE:(g + 1) * D_STATE]
        cg = cc[:, g * D_STATE:(g + 1) * D_STATE]
        bg_b = bg.astype(BF16)
        cg_b = cg.astype(BF16)
        cb = lax.dot_general(cg_b, bg_b, (((1,), (1,)), ((), ())), preferred_element_type=F32)
        bg_t = bg.T.astype(BF16)
        st_prev = st_ref[g]
        y_off = jnp.dot(cg_b, st_prev.astype(BF16), preferred_element_type=F32)
        y_parts, xw_parts, ea_parts = [], [], []
        for j in range(heads_per_group // 2):
            h0 = g * heads_per_group + 2 * j
            lanes = slice(j * LANES, (j + 1) * LANES)
            xs_pair = xg[:, lanes] * _pair_expand(dt, h0, lo_mask)
            xs_b = xs_pair.astype(BF16)
            ea_pair = _pair_expand(eac, h0, lo_mask)
            xw_parts.append((xs_pair * _pair_expand(d2e, h0, lo_mask)).astype(BF16))
            ea_parts.append(ea_pair)
            ys = []
            for h in (h0, h0 + 1):
                seg = acum[:, h:h + 1] - acum_t[h:h + 1, :]
                decay = jnp.exp(jnp.where(causal, seg, NEG_BIG))
                m = (cb * decay).astype(BF16)
                ys.append(jnp.dot(m, xs_b, preferred_element_type=F32))
            y_parts.append(jnp.where(lo_mask, ys[0], ys[1]))
        y_diag = jnp.concatenate(y_parts, axis=1)
        ea = jnp.concatenate(ea_parts, axis=1)
        xw = jnp.concatenate(xw_parts, axis=1)
        cols = slice(g * gw, (g + 1) * gw)
        y = y_diag + y_off * ea + xg * dsk_ref[:, cols]
        zg = z_ref[:, cols]
        y = y * (zg * _sigmoid(zg))
        ms = jnp.mean(y * y, axis=-1, keepdims=True)
        o_ref[:, cols] = (y * lax.rsqrt(ms + RMS_EPS) * nw_ref[:, cols]).astype(o_ref.dtype)
        st_ref[g] = st_prev * ea[n - 1:n, :] + jnp.dot(bg_t, xw, preferred_element_type=F32)


def _ssd(proj, batch, seq, col, conv_wx, conv_wb, conv_wc, conv_bx, conv_bb, conv_bc,
         dt_bias, a_log, d_skip, norm_w):
    t = proj.shape[0]
    d_inner = norm_w.shape[1]
    gn = SSM_GROUPS * D_STATE
    n = SSD_CHUNK
    nc = seq // n
    gw = d_inner // SSM_GROUPS

    def rows(b, c):
        return b * nc + c

    def prev_rows(b, c):
        return jnp.maximum((b * nc + c) * (n // SUBLANES) - 1, 0)

    def const(b, c):
        return (0, 0)

    in_specs = [
        pl.BlockSpec((n, d_inner), lambda b, c: (rows(b, c), col["z"] // d_inner)),
        pl.BlockSpec((n, d_inner), lambda b, c: (rows(b, c), col["x"] // d_inner)),
        pl.BlockSpec((n, gn), lambda b, c: (rows(b, c), col["B"] // gn)),
        pl.BlockSpec((n, gn), lambda b, c: (rows(b, c), col["C"] // gn)),
        pl.BlockSpec((n, LANES), lambda b, c: (rows(b, c), col["dt"] // LANES)),
        pl.BlockSpec((SUBLANES, d_inner), lambda b, c: (prev_rows(b, c), col["x"] // d_inner)),
        pl.BlockSpec((SUBLANES, gn), lambda b, c: (prev_rows(b, c), col["B"] // gn)),
        pl.BlockSpec((SUBLANES, gn), lambda b, c: (prev_rows(b, c), col["C"] // gn)),
        pl.BlockSpec((CONV_WIDTH, d_inner), const),
        pl.BlockSpec((CONV_WIDTH, gn), const),
        pl.BlockSpec((CONV_WIDTH, gn), const),
        pl.BlockSpec((1, d_inner), const),
        pl.BlockSpec((1, gn), const),
        pl.BlockSpec((1, gn), const),
        pl.BlockSpec((1, LANES), const),
        pl.BlockSpec((1, LANES), const),
        pl.BlockSpec((1, d_inner), const),
        pl.BlockSpec((1, d_inner), const),
    ]
    return pl.pallas_call(
        _ssd_kernel,
        grid=(batch, nc),
        in_specs=in_specs,
        out_specs=pl.BlockSpec((n, d_inner), lambda b, c: (rows(b, c), 0)),
        out_shape=jax.ShapeDtypeStruct((t, d_inner), BF16),
        scratch_shapes=[pltpu.VMEM((SSM_GROUPS, D_STATE, gw), F32),
                        pltpu.VMEM((SUBLANES + n, d_inner), F32),
                        pltpu.VMEM((SUBLANES + n, gn), F32),
                        pltpu.VMEM((SUBLANES + n, gn), F32)],
        compiler_params=_cparams(("arbitrary", "arbitrary")),
        name="ssd_mixer",
    )(proj, proj, proj, proj, proj, proj, proj, proj,
      conv_wx, conv_wb, conv_wc, conv_bx, conv_bb, conv_bc, dt_bias, a_log, d_skip, norm_w)


def _swa_kernel(sink_ref, q_ref, kc_ref, vc_ref, kp_ref, vp_ref, o_ref):
    i = pl.program_id(1)
    n = WINDOW
    n_heads = q_ref.shape[1] // HEAD_DIM
    grp = n_heads // KV_HEADS
    qpos = lax.broadcasted_iota(jnp.int32, (n, 2 * n), 0) + n
    kpos = lax.broadcasted_iota(jnp.int32, (n, 2 * n), 1)
    diff = qpos - kpos
    mask = (diff >= 0) & (diff < n) & ((kpos >= n) | (i > 0))
    lo_q = lax.broadcasted_iota(jnp.int32, (n, LANES), 1) < HEAD_DIM
    lo_k = lax.broadcasted_iota(jnp.int32, (2 * n, LANES), 1) < HEAD_DIM
    scale = HEAD_DIM ** -0.5
    for p in range(KV_HEADS // 2):
        lanes = slice(p * LANES, (p + 1) * LANES)
        kk = jnp.concatenate([kp_ref[:, lanes], kc_ref[:, lanes]], axis=0).astype(F32)
        vv = jnp.concatenate([vp_ref[:, lanes], vc_ref[:, lanes]], axis=0).astype(F32)
        kk_sw = pltpu.roll(kk, HEAD_DIM, axis=1)
        vv_sw = pltpu.roll(vv, HEAD_DIM, axis=1)
        for par in range(2):
            g = 2 * p + par
            k2 = (jnp.where(lo_k, kk, kk_sw) if par == 0 else jnp.where(lo_k, kk_sw, kk)).astype(BF16)
            v2 = (jnp.where(lo_k, vv, vv_sw) if par == 0 else jnp.where(lo_k, vv_sw, vv)).astype(BF16)
            for qp in range(grp // 2):
                h0 = g * grp + 2 * qp
                qlanes = slice((h0 // 2) * LANES, (h0 // 2 + 1) * LANES)
                q_pair = q_ref[:, qlanes]
                outs = []
                for hh in range(2):
                    keep = lo_q if hh == 0 else jnp.logical_not(lo_q)
                    qm = jnp.where(keep, q_pair, jnp.zeros_like(q_pair))
                    s = lax.dot_general(qm, k2, (((1,), (1,)), ((), ())),
                                        preferred_element_type=F32) * scale
                    s = jnp.where(mask, s, NEG_BIG)
                    sink = sink_ref[h0 + hh]
                    mx = jnp.maximum(jnp.max(s, axis=-1, keepdims=True), sink)
                    pr = jnp.exp(s - mx)
                    den = jnp.sum(pr, axis=-1, keepdims=True) + jnp.exp(sink - mx)
                    o = jnp.dot(pr.astype(BF16), v2, preferred_element_type=F32)
                    outs.append(o / den)
                o_ref[:, qlanes] = jnp.where(lo_q, outs[0], outs[1]).astype(o_ref.dtype)


def _swa(qkv, sinks, batch, seq, n_heads):
    t = qkv.shape[0]
    n = WINDOW
    nb = seq // n
    qw = n_heads * HEAD_DIM
    kw = KV_HEADS * HEAD_DIM
    k_blk = qw // kw
    v_blk = k_blk + 1

    def cur(b, i):
        return b * nb + i

    def prev(b, i):
        return b * nb + jnp.maximum(i - 1, 0)

    return pl.pallas_call(
        _swa_kernel,
        grid=(batch, nb),
        in_specs=[pl.BlockSpec(memory_space=pltpu.SMEM),
                  pl.BlockSpec((n, qw), lambda b, i: (cur(b, i), 0)),
                  pl.BlockSpec((n, kw), lambda b, i: (cur(b, i), k_blk)),
                  pl.BlockSpec((n, kw), lambda b, i: (cur(b, i), v_blk)),
                  pl.BlockSpec((n, kw), lambda b, i: (prev(b, i), k_blk)),
                  pl.BlockSpec((n, kw), lambda b, i: (prev(b, i), v_blk))],
        out_specs=pl.BlockSpec((n, qw), lambda b, i: (cur(b, i), 0)),
        out_shape=jax.ShapeDtypeStruct((t, qw), BF16),
        compiler_params=_cparams(("parallel", "parallel")),
        name="swa",
    )(sinks, qkv, qkv, qkv, qkv, qkv)


def _layer_norm(v, g, b):
    mu = jnp.mean(v, axis=-1, keepdims=True)
    d = v - mu
    var = jnp.mean(d * d, axis=-1, keepdims=True)
    return d * lax.rsqrt(var + LN_EPS) * g + b


def _merge_kernel(alpha, ys_ref, ya_ref, gs_ref, ga_ref, x_ref, wso_ref, wao_ref, wmix_ref,
                  bgs_ref, bga_ref, lg_ref, lb_ref, h_ref):
    y_ssm = jnp.dot(ys_ref[...], wso_ref[...], preferred_element_type=F32)
    y_att = jnp.dot(ya_ref[...], wao_ref[...], preferred_element_type=F32)
    merged = (_sigmoid(gs_ref[...] + bgs_ref[...]) * y_ssm
              + _sigmoid(ga_ref[...] + bga_ref[...]) * y_att)
    mix = jnp.dot(merged.astype(BF16), wmix_ref[...], preferred_element_type=F32)
    h_ref[...] = _layer_norm(alpha * x_ref[...] + mix, lg_ref[...], lb_ref[...])


def _merge(alpha, y_ssm, y_att, proj, col, x, wso, wao, wmix, bgs, bga, lg, lb, tm):
    t, d = x.shape
    tm = min(tm, t)
    di = y_ssm.shape[1]
    da = y_att.shape[1]

    def const(i):
        return (0, 0)

    return pl.pallas_call(
        functools.partial(_merge_kernel, alpha),
        grid=(t // tm,),
        in_specs=[pl.BlockSpec((tm, di), lambda i: (i, 0)),
                  pl.BlockSpec((tm, da), lambda i: (i, 0)),
                  pl.BlockSpec((tm, d), lambda i: (i, col["gs"] // d)),
                  pl.BlockSpec((tm, d), lambda i: (i, col["ga"] // d)),
                  pl.BlockSpec((tm, d), lambda i: (i, 0)),
                  pl.BlockSpec((di, d), const),
                  pl.BlockSpec((da, d), const),
                  pl.BlockSpec((d, d), const),
                  pl.BlockSpec((1, d), const),
                  pl.BlockSpec((1, d), const),
                  pl.BlockSpec((1, d), const),
                  pl.BlockSpec((1, d), const)],
        out_specs=pl.BlockSpec((tm, d), lambda i: (i, 0)),
        out_shape=jax.ShapeDtypeStruct((t, d), F32),
        compiler_params=_cparams(("parallel",)),
        name="merge_ln1",
    )(y_ssm, y_att, proj, proj, x, wso, wao, wmix, bgs, bga, lg, lb)


def _router_kernel(h_ref, wr_ref, br_ref, idx_ref, gw_ref, rank_ref, cnt_ref, tri_ref, base_ref):
    i = pl.program_id(0)
    tm = h_ref.shape[0]
    n_exp = wr_ref.shape[0]

    @pl.when(i == 0)
    def _():
        r = lax.broadcasted_iota(jnp.int32, (tm, tm), 0)
        c = lax.broadcasted_iota(jnp.int32, (tm, tm), 1)
        tri_ref[...] = jnp.where(r < c, 1.0, 0.0).astype(BF16)
        base_ref[...] = jnp.zeros_like(base_ref)

    logits = lax.dot_general(wr_ref[...], h_ref[...], (((1,), (1,)), ((), ())),
                             preferred_element_type=F32,
                             precision=lax.Precision.HIGHEST) + br_ref[...]
    eid = lax.broadcasted_iota(jnp.int32, (n_exp, tm), 0)
    work = logits
    vals, idxs = [], []
    for _ in range(TOP_K):
        mx = jnp.max(work, axis=0, keepdims=True)
        sel = jnp.min(jnp.where(work == mx, eid, n_exp), axis=0, keepdims=True)
        vals.append(mx)
        idxs.append(sel)
        work = jnp.where(eid == sel, -jnp.inf, work)
    exps = [jnp.exp(v - vals[0]) for v in vals]
    den = exps[0]
    for e in exps[1:]:
        den = den + e
    base = base_ref[...]
    for k in range(TOP_K):
        onehot = eid == idxs[k]
        oh = jnp.where(onehot, 1.0, 0.0)
        before = jnp.dot(oh.astype(BF16), tri_ref[...], preferred_element_type=F32)
        rank = jnp.sum(jnp.where(onehot, before + base, 0.0), axis=0, keepdims=True)
        idx_ref[k:k + 1, :] = idxs[k]
        gw_ref[k:k + 1, :] = exps[k] / den
        rank_ref[k:k + 1, :] = rank.astype(jnp.int32)
        base = base + jnp.sum(oh, axis=1, keepdims=True)
    base_ref[...] = base
    cnt_ref[...] = jnp.broadcast_to(base, cnt_ref.shape).astype(jnp.int32)


def _router(h, w_router_t, b_router, tm):
    t, d = h.shape
    n_exp = w_router_t.shape[0]
    tm = min(tm, t)
    outs = pl.pallas_call(
        _router_kernel,
        grid=(t // tm,),
        in_specs=[pl.BlockSpec((tm, d), lambda i: (i, 0)),
                  pl.BlockSpec((n_exp, d), lambda i: (0, 0)),
                  pl.BlockSpec((n_exp, 1), lambda i: (0, 0))],
        out_specs=[pl.BlockSpec((TOP_K, tm), lambda i: (0, i)),
                   pl.BlockSpec((TOP_K, tm), lambda i: (0, i)),
                   pl.BlockSpec((TOP_K, tm), lambda i: (0, i)),
                   pl.BlockSpec((n_exp, LANES), lambda i: (0, 0))],
        out_shape=[jax.ShapeDtypeStruct((TOP_K, t), jnp.int32),
                   jax.ShapeDtypeStruct((TOP_K, t), F32),
                   jax.ShapeDtypeStruct((TOP_K, t), jnp.int32),
                   jax.ShapeDtypeStruct((n_exp, LANES), jnp.int32)],
        scratch_shapes=[pltpu.VMEM((tm, tm), BF16), pltpu.VMEM((n_exp, 1), F32)],
        compiler_params=_cparams(("arbitrary",)),
        name="router",
    )(h, w_router_t, b_router)
    return outs


def _expert_kernel(be_ref, nu_ref, tok_ref, tokn_ref, h_hbm, wgu_ref, bgu_ref, wd_ref, bd_ref,
                   y_ref, xbuf, sem, wgu_b, wd_b):
    i = pl.program_id(0)
    nblk = pl.num_programs(0)
    rows = xbuf.shape[1]
    n_used = nu_ref[0]

    def gather(idx_ref, slot):
        def body(r, carry):
            pltpu.make_async_copy(h_hbm.at[pl.ds(idx_ref[0, 0, r], 1), :],
                                  xbuf.at[slot, pl.ds(r, 1), :], sem.at[slot]).start()
            return carry
        lax.fori_loop(0, rows, body, 0, unroll=8)

    def wait_all(slot):
        def body(r, carry):
            pltpu.make_async_copy(h_hbm.at[pl.ds(0, 1), :],
                                  xbuf.at[slot, pl.ds(r, 1), :], sem.at[slot]).wait()
            return carry
        lax.fori_loop(0, rows, body, 0, unroll=8)

    slot = lax.rem(i, 2)

    @pl.when(i == 0)
    def _():
        gather(tok_ref, 0)

    @pl.when(i + 1 < nblk)
    def _():
        gather(tokn_ref, 1 - slot)

    changed = jnp.logical_or(i == 0, be_ref[i] != be_ref[jnp.maximum(i - 1, 0)])

    @pl.when(changed)
    def _():
        wgu_b[...] = wgu_ref[0].astype(BF16)
        wd_b[...] = wd_ref[0].astype(BF16)

    wait_all(slot)

    @pl.when(i < n_used)
    def _():
        d_ff = wd_b.shape[0]
        xb = xbuf[slot].astype(BF16)
        hgu = jnp.dot(xb, wgu_b[...], preferred_element_type=F32) + bgu_ref[0]
        glu = jnp.minimum(hgu[:, :d_ff], SWIGLU_LIMIT)
        lin = jnp.clip(hgu[:, d_ff:], -SWIGLU_LIMIT, SWIGLU_LIMIT)
        act = glu * _sigmoid(SWIGLU_ALPHA * glu) * (lin + 1.0)
        y_ref[...] = jnp.dot(act.astype(BF16), wd_b[...], preferred_element_type=F32) + bd_ref[0]

    @pl.when(i >= n_used)
    def _():
        y_ref[...] = jnp.zeros_like(y_ref)


def _experts(h, row_tok, block_expert, n_used, wgu, bgu, wd, bd):
    t, d = h.shape
    n_blocks = block_expert.shape[0]
    n_exp, _, two_f = wgu.shape
    d_ff = wd.shape[1]
    rows = MOE_BLOCK
    tok3 = row_tok.reshape(n_blocks, 1, rows)
    grid_spec = pltpu.PrefetchScalarGridSpec(
        num_scalar_prefetch=2,
        grid=(n_blocks,),
        in_specs=[
            pl.BlockSpec((1, 1, rows), lambda i, be, nu: (i, 0, 0), memory_space=pltpu.SMEM),
            pl.BlockSpec((1, 1, rows), lambda i, be, nu: (jnp.minimum(i + 1, n_blocks - 1), 0, 0),
                         memory_space=pltpu.SMEM),
            pl.BlockSpec(memory_space=pl.ANY),
            pl.BlockSpec((1, d, two_f), lambda i, be, nu: (be[i], 0, 0)),
            pl.BlockSpec((1, 1, two_f), lambda i, be, nu: (be[i], 0, 0)),
            pl.BlockSpec((1, d_ff, d), lambda i, be, nu: (be[i], 0, 0)),
            pl.BlockSpec((1, 1, d), lambda i, be, nu: (be[i], 0, 0)),
        ],
        out_specs=pl.BlockSpec((rows, d), lambda i, be, nu: (i, 0)),
        scratch_shapes=[pltpu.VMEM((2, rows, d), F32),
                        pltpu.SemaphoreType.DMA((2,)),
                        pltpu.VMEM((d, two_f), BF16),
                        pltpu.VMEM((d_ff, d), BF16)],
    )
    return pl.pallas_call(
        _expert_kernel,
        grid_spec=grid_spec,
        out_shape=jax.ShapeDtypeStruct((n_blocks * rows, d), F32),
        compiler_params=_cparams(("arbitrary",)),
        name="experts",
    )(block_expert, n_used, tok3, tok3, h, wgu, bgu.reshape(n_exp, 1, two_f), wd,
      bd.reshape(n_exp, 1, d))


def _combine_kernel(alpha, dest_ref, destn_ref, ys_hbm, gw_ref, h_ref, lg_ref, lb_ref,
                    o_ref, ybuf, sem):
    i = pl.program_id(0)
    nsteps = pl.num_programs(0)
    tm = h_ref.shape[0]

    def gather(idx_ref, slot):
        def body(r, carry):
            for k in range(TOP_K):
                pltpu.make_async_copy(ys_hbm.at[pl.ds(idx_ref[0, k, r], 1), :],
                                      ybuf.at[slot, k, pl.ds(r, 1), :], sem.at[slot]).start()
            return carry
        lax.fori_loop(0, tm, body, 0, unroll=4)

    def wait_all(slot):
        def body(r, carry):
            for k in range(TOP_K):
                pltpu.make_async_copy(ys_hbm.at[pl.ds(0, 1), :],
                                      ybuf.at[slot, k, pl.ds(r, 1), :], sem.at[slot]).wait()
            return carry
        lax.fori_loop(0, tm, body, 0, unroll=4)

    slot = lax.rem(i, 2)

    @pl.when(i == 0)
    def _():
        gather(dest_ref, 0)

    @pl.when(i + 1 < nsteps)
    def _():
        gather(destn_ref, 1 - slot)

    wait_all(slot)
    gw = gw_ref[...]
    acc = alpha * h_ref[...]
    for k in range(TOP_K):
        acc = acc + gw[:, k:k + 1] * ybuf[slot, k]
    o_ref[...] = _layer_norm(acc, lg_ref[...], lb_ref[...])


def _combine(alpha, ys, dest_kt, gw_rows, h, lg, lb, tm):
    t, d = h.shape
    tm = min(tm, t)
    nsteps = t // tm
    dest3 = dest_kt.reshape(TOP_K, nsteps, tm).transpose(1, 0, 2)
    return pl.pallas_call(
        functools.partial(_combine_kernel, alpha),
        grid=(nsteps,),
        in_specs=[pl.BlockSpec((1, TOP_K, tm), lambda i: (i, 0, 0), memory_space=pltpu.SMEM),
                  pl.BlockSpec((1, TOP_K, tm), lambda i: (jnp.minimum(i + 1, nsteps - 1), 0, 0),
                               memory_space=pltpu.SMEM),
                  pl.BlockSpec(memory_space=pl.ANY),
                  pl.BlockSpec((tm, LANES), lambda i: (i, 0)),
                  pl.BlockSpec((tm, d), lambda i: (i, 0)),
                  pl.BlockSpec((1, d), lambda i: (0, 0)),
                  pl.BlockSpec((1, d), lambda i: (0, 0))],
        out_specs=pl.BlockSpec((tm, d), lambda i: (i, 0)),
        out_shape=jax.ShapeDtypeStruct((t, d), F32),
        scratch_shapes=[pltpu.VMEM((2, TOP_K, tm, d), F32), pltpu.SemaphoreType.DMA((2,))],
        compiler_params=_cparams(("arbitrary",)),
        name="combine_ln2",
    )(dest3, dest3, ys, gw_rows, h, lg, lb)


def _pack_in_proj(w_in, d_inner, n_ssm_heads, attn_dim):
    gn = SSM_GROUPS * D_STATE
    kv_dim = KV_HEADS * HEAD_DIM
    d = w_in.shape[0]
    o = 0
    seg = {}
    for name, width in (("z", d_inner), ("x", d_inner), ("B", gn), ("C", gn), ("dt", n_ssm_heads),
                        ("q", attn_dim), ("k", kv_dim), ("v", kv_dim), ("gs", d), ("ga", d)):
        seg[name] = w_in[:, o:o + width]
        o += width
    assert o == w_in.shape[1]
    dt_pad = jnp.pad(seg["dt"], ((0, 0), (0, LANES - n_ssm_heads)))
    order = (("z", seg["z"]), ("x", seg["x"]), ("gs", seg["gs"]), ("ga", seg["ga"]),
             ("B", seg["B"]), ("C", seg["C"]), ("dt", dt_pad))
    col, off = {}, 0
    for name, w in order:
        col[name] = off
        assert off % w.shape[1] == 0
        off += w.shape[1]
    w_f32grp = jnp.concatenate([w for _, w in order], axis=1).astype(BF16)
    w_qkv = jnp.concatenate([seg["q"], seg["k"], seg["v"]], axis=1).astype(BF16)
    return w_f32grp, w_qkv, col


def _largest_tile(n, cap):
    best = LANES
    for k in range(1, n // LANES + 1):
        if n % (k * LANES) == 0 and k * LANES <= cap:
            best = k * LANES
    return best


def _layer(h_in, batch, seq, alpha, w_in, conv_w, conv_b, dt_bias, a_log, d_skip, ssm_norm_w,
           w_ssm_out, attn_sinks, w_attn_out, b_gates, w_mix_out, ln1_g, ln1_b, w_router, b_router,
           w_gate_up, b_gate_up, w_down, b_down, ln2_g, ln2_b):
    t, d = h_in.shape
    d_inner = ssm_norm_w.shape[0]
    n_ssm_heads = dt_bias.shape[0]
    attn_dim = w_attn_out.shape[0]
    n_heads = attn_sinks.shape[0]
    n_exp = w_router.shape[1]
    gn = SSM_GROUPS * D_STATE

    w_f32grp, w_qkv, col = _pack_in_proj(w_in, d_inner, n_ssm_heads, attn_dim)
    proj = _matmul(h_in, w_f32grp, F32, 512, _largest_tile(w_f32grp.shape[1], 2560))
    qkv = _matmul(h_in, w_qkv, BF16, 1024, _largest_tile(w_qkv.shape[1], 1024))

    pad_h = (0, LANES - n_ssm_heads)
    y_ssm = _ssd(proj, batch, seq, col,
                 conv_w[:, :d_inner], conv_w[:, d_inner:d_inner + gn], conv_w[:, d_inner + gn:],
                 conv_b[None, :d_inner], conv_b[None, d_inner:d_inner + gn], conv_b[None, d_inner + gn:],
                 jnp.pad(dt_bias, pad_h)[None, :], jnp.pad(a_log, pad_h)[None, :],
                 jnp.repeat(d_skip, SSM_HEAD_DIM)[None, :], ssm_norm_w[None, :])
    y_att = _swa(qkv, attn_sinks, batch, seq, n_heads)

    h1 = _merge(alpha, y_ssm, y_att, proj, col, h_in,
                w_ssm_out.astype(BF16), w_attn_out.astype(BF16), w_mix_out.astype(BF16),
                b_gates[None, :d], b_gates[None, d:], ln1_g[None, :], ln1_b[None, :], 512)

    idx_kt, gw_kt, rank_kt, counts = _router(h1, w_router.T, b_router[:, None], 1024)

    counts = counts[:, 0]
    padded = (counts + MOE_BLOCK - 1) // MOE_BLOCK * MOE_BLOCK
    pad_end = jnp.cumsum(padded)
    pad_start = pad_end - padded
    n_slots = t * TOP_K
    n_blocks = -(-n_slots // MOE_BLOCK) + n_exp
    n_rows = n_blocks * MOE_BLOCK
    dest_kt = pad_start[idx_kt] + rank_kt
    tok_kt = jnp.broadcast_to(jnp.arange(t, dtype=jnp.int32)[None, :], (TOP_K, t))
    row_tok = jnp.zeros((n_rows,), jnp.int32).at[dest_kt.reshape(-1)].set(tok_kt.reshape(-1))
    block_expert = jnp.minimum(
        jnp.searchsorted(pad_end, jnp.arange(n_blocks, dtype=jnp.int32) * MOE_BLOCK, side="right"),
        n_exp - 1).astype(jnp.int32)
    n_used = (pad_end[-1:] // MOE_BLOCK).astype(jnp.int32)
    gw_rows = jnp.pad(gw_kt.T, ((0, 0), (0, LANES - TOP_K)))

    ys = _experts(h1, row_tok, block_expert, n_used, w_gate_up, b_gate_up, w_down, b_down)
    return _combine(alpha, ys, dest_kt, gw_rows, h1, ln2_g[None, :], ln2_b[None, :], 256)


def kernel(x, w_in, conv_w, conv_b, dt_bias, a_log, d_skip, ssm_norm_w, w_ssm_out, attn_sinks,
           w_attn_out, b_gates, w_mix_out, ln1_g, ln1_b, w_router, b_router, w_gate_up, b_gate_up,
           w_down, b_down, ln2_g, ln2_b):
    batch, seq, d = x.shape
    depth = w_in.shape[0]
    alpha = (2 * depth) ** 0.25
    h = x.reshape(batch * seq, d)
    for i in range(depth):
        h = _layer(h, batch, seq, alpha, w_in[i], conv_w[i], conv_b[i], dt_bias[i], a_log[i],
                   d_skip[i], ssm_norm_w[i], w_ssm_out[i], attn_sinks[i], w_attn_out[i], b_gates[i],
                   w_mix_out[i], ln1_g[i], ln1_b[i], w_router[i], b_router[i], w_gate_up[i],
                   b_gate_up[i], w_down[i], b_down[i], ln2_g[i], ln2_b[i])
    return h.reshape(batch, seq, d)
```

```python
import functools

import jax
import jax.numpy as jnp
from jax import lax
from jax.experimental import pallas as pl
from jax.experimental.pallas import tpu as pltpu
from jax.experimental.pallas import tpu_sc as plsc

SSM_HEAD_DIM = 64
SSM_GROUPS = 4
D_STATE = 128
CONV_WIDTH = 4
SSD_CHUNK = 128
KV_HEADS = 4
HEAD_DIM = 64
WINDOW = 128
TOP_K = 4
SWIGLU_LIMIT = 7.0
SWIGLU_ALPHA = 1.702
MOE_BLOCK = 256
LN_EPS = 1e-5
RMS_EPS = 1e-5

LANES = 128
SUBLANES = 8
NEG_BIG = -1e30
F32 = jnp.float32
BF16 = jnp.bfloat16
VMEM_LIMIT = 56 * 1024 * 1024


def _cparams(sem):
    return pltpu.CompilerParams(dimension_semantics=sem, vmem_limit_bytes=VMEM_LIMIT)


def _sigmoid(x):
    return 1.0 / (1.0 + jnp.exp(-x))


def _mm_kernel(x_ref, w_ref, o_ref):
    o_ref[...] = jnp.dot(x_ref[...].astype(BF16), w_ref[...],
                         preferred_element_type=F32).astype(o_ref.dtype)


def _matmul(x, w, out_dtype, tm, tn):
    m, k = x.shape
    n = w.shape[1]
    tm = min(tm, m)
    assert m % tm == 0 and n % tn == 0
    return pl.pallas_call(
        _mm_kernel,
        grid=(m // tm, n // tn),
        in_specs=[pl.BlockSpec((tm, k), lambda i, j: (i, 0)),
                  pl.BlockSpec((k, tn), lambda i, j: (0, j))],
        out_specs=pl.BlockSpec((tm, tn), lambda i, j: (i, j)),
        out_shape=jax.ShapeDtypeStruct((m, n), out_dtype),
        compiler_params=_cparams(("parallel", "parallel")),
        name="in_proj",
    )(x, w)


def _conv_silu(u_ref, prev_ref, w_ref, b_ref, ext_ref, first):
    n = u_ref.shape[0]
    prev = prev_ref[...]
    ext_ref[0:SUBLANES, :] = jnp.where(first, jnp.zeros_like(prev), prev)
    ext_ref[SUBLANES:SUBLANES + n, :] = u_ref[...]
    acc = b_ref[...]
    for j in range(CONV_WIDTH):
        off = SUBLANES - (CONV_WIDTH - 1) + j
        acc = acc + w_ref[j:j + 1, :] * ext_ref[off:off + n, :]
    return acc * _sigmoid(acc)


def _pair_expand(v, h0, lo_mask):
    n = v.shape[0]
    a = jnp.broadcast_to(v[:, h0:h0 + 1], (n, LANES))
    b = jnp.broadcast_to(v[:, h0 + 1:h0 + 2], (n, LANES))
    return jnp.where(lo_mask, a, b)


def _ssd_kernel(z_ref, x_ref, b_ref, c_ref, dt_ref, xp_ref, bp_ref, cp_ref,
                cwx_ref, cwb_ref, cwc_ref, cbx_ref, cbb_ref, cbc_ref,
                dtb_ref, alog_ref, dsk_ref, nw_ref,
                o_ref, st_ref, extx_ref, extb_ref, extc_ref):
    c = pl.program_id(1)
    first = c == 0
    n = SSD_CHUNK
    heads_per_group = st_ref.shape[2] // SSM_HEAD_DIM
    gw = heads_per_group * SSM_HEAD_DIM

    @pl.when(first)
    def _():
        st_ref[...] = jnp.zeros_like(st_ref)

    xc = _conv_silu(x_ref, xp_ref, cwx_ref, cbx_ref, extx_ref, first)
    bc = _conv_silu(b_ref, bp_ref, cwb_ref, cbb_ref, extb_ref, first)
    cc = _conv_silu(c_ref, cp_ref, cwc_ref, cbc_ref, extc_ref, first)

    dt_in = dt_ref[...] + dtb_ref[...]
    dt = jnp.maximum(dt_in, 0.0) + jnp.log(1.0 + jnp.exp(-jnp.abs(dt_in)))
    a = dt * (-jnp.exp(alog_ref[...]))
    row = lax.broadcasted_iota(jnp.int32, (n, n), 0)
    col = lax.broadcasted_iota(jnp.int32, (n, n), 1)
    causal = row >= col
    tri = jnp.where(causal, 1.0, 0.0).astype(F32)
    acum = jnp.dot(tri, a, preferred_element_type=F32, precision=lax.Precision.HIGHEST)
    acum_t = acum.T
    a_last = acum[n - 1:n, :]
    d2e = jnp.exp(a_last - acum)
    eac = jnp.exp(acum)
    lo_mask = lax.broadcasted_iota(jnp.int32, (n, LANES), 1) < SSM_HEAD_DIM

    for g in range(SSM_GROUPS):
        xg = xc[:, g * gw:(g + 1) * gw]
        bg = bc[:, g * D_STATE:(g + 1) * D_STATE]
        cg = cc[:, g * D_STATE:(g + 1) * D_STATE]
        bg_b = bg.astype(BF16)
        cg_b = cg.astype(BF16)
        cb = lax.dot_general(cg_b, bg_b, (((1,), (1,)), ((), ())), preferred_element_type=F32)
        bg_t = bg.T.astype(BF16)
        st_prev = st_ref[g]
        y_off = jnp.dot(cg_b, st_prev.astype(BF16), preferred_element_type=F32)
        y_parts, xw_parts, ea_parts = [], [], []
        for j in range(heads_per_group // 2):
            h0 = g * heads_per_group + 2 * j
            lanes = slice(j * LANES, (j + 1) * LANES)
            xs_pair = xg[:, lanes] * _pair_expand(dt, h0, lo_mask)
            xs_b = xs_pair.astype(BF16)
            ea_pair = _pair_expand(eac, h0, lo_mask)
            xw_parts.append((xs_pair * _pair_expand(d2e, h0, lo_mask)).astype(BF16))
            ea_parts.append(ea_pair)
            ys = []
            for h in (h0, h0 + 1):
                seg = acum[:, h:h + 1] - acum_t[h:h + 1, :]
                decay = jnp.exp(jnp.where(causal, seg, NEG_BIG))
                m = (cb * decay).astype(BF16)
                ys.append(jnp.dot(m, xs_b, preferred_element_type=F32))
            y_parts.append(jnp.where(lo_mask, ys[0], ys[1]))
        y_diag = jnp.concatenate(y_parts, axis=1)
        ea = jnp.concatenate(ea_parts, axis=1)
        xw = jnp.concatenate(xw_parts, axis=1)
        cols = slice(g * gw, (g + 1) * gw)
        y = y_diag + y_off * ea + xg * dsk_ref[:, cols]
        zg = z_ref[:, cols]
        y = y * (zg * _sigmoid(zg))
        ms = jnp.mean(y * y, axis=-1, keepdims=True)
        o_ref[:, cols] = (y * lax.rsqrt(ms + RMS_EPS) * nw_ref[:, cols]).astype(o_ref.dtype)
        st_ref[g] = st_prev * ea[n - 1:n, :] + jnp.dot(bg_t, xw, preferred_element_type=F32)


def _ssd(proj, batch, seq, col, conv_wx, conv_wb, conv_wc, conv_bx, conv_bb, conv_bc,
         dt_bias, a_log, d_skip, norm_w):
    t = proj.shape[0]
    d_inner = norm_w.shape[1]
    gn = SSM_GROUPS * D_STATE
    n = SSD_CHUNK
    nc = seq // n
    gw = d_inner // SSM_GROUPS

    def rows(b, c):
        return b * nc + c

    def prev_rows(b, c):
        return jnp.maximum((b * nc + c) * (n // SUBLANES) - 1, 0)

    def const(b, c):
        return (0, 0)

    in_specs = [
        pl.BlockSpec((n, d_inner), lambda b, c: (rows(b, c), col["z"] // d_inner)),
        pl.BlockSpec((n, d_inner), lambda b, c: (rows(b, c), col["x"] // d_inner)),
        pl.BlockSpec((n, gn), lambda b, c: (rows(b, c), col["B"] // gn)),
        pl.BlockSpec((n, gn), lambda b, c: (rows(b, c), col["C"] // gn)),
        pl.BlockSpec((n, LANES), lambda b, c: (rows(b, c), col["dt"] // LANES)),
        pl.BlockSpec((SUBLANES, d_inner), lambda b, c: (prev_rows(b, c), col["x"] // d_inner)),
        pl.BlockSpec((SUBLANES, gn), lambda b, c: (prev_rows(b, c), col["B"] // gn)),
        pl.BlockSpec((SUBLANES, gn), lambda b, c: (prev_rows(b, c), col["C"] // gn)),
        pl.BlockSpec((CONV_WIDTH, d_inner), const),
        pl.BlockSpec((CONV_WIDTH, gn), const),
        pl.BlockSpec((CONV_WIDTH, gn), const),
        pl.BlockSpec((1, d_inner), const),
        pl.BlockSpec((1, gn), const),
        pl.BlockSpec((1, gn), const),
        pl.BlockSpec((1, LANES), const),
        pl.BlockSpec((1, LANES), const),
        pl.BlockSpec((1, d_inner), const),
        pl.BlockSpec((1, d_inner), const),
    ]
    return pl.pallas_call(
        _ssd_kernel,
        grid=(batch, nc),
        in_specs=in_specs,
        out_specs=pl.BlockSpec((n, d_inner), lambda b, c: (rows(b, c), 0)),
        out_shape=jax.ShapeDtypeStruct((t, d_inner), BF16),
        scratch_shapes=[pltpu.VMEM((SSM_GROUPS, D_STATE, gw), F32),
                        pltpu.VMEM((SUBLANES + n, d_inner), F32),
                        pltpu.VMEM((SUBLANES + n, gn), F32),
                        pltpu.VMEM((SUBLANES + n, gn), F32)],
        compiler_params=_cparams(("arbitrary", "arbitrary")),
        name="ssd_mixer",
    )(proj, proj, proj, proj, proj, proj, proj, proj,
      conv_wx, conv_wb, conv_wc, conv_bx, conv_bb, conv_bc, dt_bias, a_log, d_skip, norm_w)


def _swa_kernel(sink_ref, q_ref, kc_ref, vc_ref, kp_ref, vp_ref, o_ref):
    i = pl.program_id(1)
    n = WINDOW
    n_heads = q_ref.shape[1] // HEAD_DIM
    grp = n_heads // KV_HEADS
    qpos = lax.broadcasted_iota(jnp.int32, (n, 2 * n), 0) + n
    kpos = lax.broadcasted_iota(jnp.int32, (n, 2 * n), 1)
    diff = qpos - kpos
    mask = (diff >= 0) & (diff < n) & ((kpos >= n) | (i > 0))
    lo_q = lax.broadcasted_iota(jnp.int32, (n, LANES), 1) < HEAD_DIM
    lo_k = lax.broadcasted_iota(jnp.int32, (2 * n, LANES), 1) < HEAD_DIM
    scale = HEAD_DIM ** -0.5
    for p in range(KV_HEADS // 2):
        lanes = slice(p * LANES, (p + 1) * LANES)
        kk = jnp.concatenate([kp_ref[:, lanes], kc_ref[:, lanes]], axis=0).astype(F32)
        vv = jnp.concatenate([vp_ref[:, lanes], vc_ref[:, lanes]], axis=0).astype(F32)
        kk_sw = pltpu.roll(kk, HEAD_DIM, axis=1)
        vv_sw = pltpu.roll(vv, HEAD_DIM, axis=1)
        for par in range(2):
            g = 2 * p + par
            k2 = (jnp.where(lo_k, kk, kk_sw) if par == 0 else jnp.where(lo_k, kk_sw, kk)).astype(BF16)
            v2 = (jnp.where(lo_k, vv, vv_sw) if par == 0 else jnp.where(lo_k, vv_sw, vv)).astype(BF16)
            for qp in range(grp // 2):
                h0 = g * grp + 2 * qp
                qlanes = slice((h0 // 2) * LANES, (h0 // 2 + 1) * LANES)
                q_pair = q_ref[:, qlanes]
                outs = []
                for hh in range(2):
                    keep = lo_q if hh == 0 else jnp.logical_not(lo_q)
                    qm = jnp.where(keep, q_pair, jnp.zeros_like(q_pair))
                    s = lax.dot_general(qm, k2, (((1,), (1,)), ((), ())),
                                        preferred_element_type=F32) * scale
                    s = jnp.where(mask, s, NEG_BIG)
                    sink = sink_ref[h0 + hh]
                    mx = jnp.maximum(jnp.max(s, axis=-1, keepdims=True), sink)
                    pr = jnp.exp(s - mx)
                    den = jnp.sum(pr, axis=-1, keepdims=True) + jnp.exp(sink - mx)
                    o = jnp.dot(pr.astype(BF16), v2, preferred_element_type=F32)
                    outs.append(o / den)
                o_ref[:, qlanes] = jnp.where(lo_q, outs[0], outs[1]).astype(o_ref.dtype)


def _swa(qkv, sinks, batch, seq, n_heads):
    t = qkv.shape[0]
    n = WINDOW
    nb = seq // n
    qw = n_heads * HEAD_DIM
    kw = KV_HEADS * HEAD_DIM
    k_blk = qw // kw
    v_blk = k_blk + 1

    def cur(b, i):
        return b * nb + i

    def prev(b, i):
        return b * nb + jnp.maximum(i - 1, 0)

    return pl.pallas_call(
        _swa_kernel,
        grid=(batch, nb),
        in_specs=[pl.BlockSpec(memory_space=pltpu.SMEM),
                  pl.BlockSpec((n, qw), lambda b, i: (cur(b, i), 0)),
                  pl.BlockSpec((n, kw), lambda b, i: (cur(b, i), k_blk)),
                  pl.BlockSpec((n, kw), lambda b, i: (cur(b, i), v_blk)),
                  pl.BlockSpec((n, kw), lambda b, i: (prev(b, i), k_blk)),
                  pl.BlockSpec((n, kw), lambda b, i: (prev(b, i), v_blk))],
        out_specs=pl.BlockSpec((n, qw), lambda b, i: (cur(b, i), 0)),
        out_shape=jax.ShapeDtypeStruct((t, qw), BF16),
        compiler_params=_cparams(("parallel", "parallel")),
        name="swa",
    )(sinks, qkv, qkv, qkv, qkv, qkv)


def _layer_norm(v, g, b):
    mu = jnp.mean(v, axis=-1, keepdims=True)
    d = v - mu
    var = jnp.mean(d * d, axis=-1, keepdims=True)
    return d * lax.rsqrt(var + LN_EPS) * g + b


def _merge_kernel(alpha, ys_ref, ya_ref, gs_ref, ga_ref, x_ref, wso_ref, wao_ref, wmix_ref,
                  bgs_ref, bga_ref, lg_ref, lb_ref, h_ref):
    y_ssm = jnp.dot(ys_ref[...], wso_ref[...], preferred_element_type=F32)
    y_att = jnp.dot(ya_ref[...], wao_ref[...], preferred_element_type=F32)
    merged = (_sigmoid(gs_ref[...] + bgs_ref[...]) * y_ssm
              + _sigmoid(ga_ref[...] + bga_ref[...]) * y_att)
    mix = jnp.dot(merged.astype(BF16), wmix_ref[...], preferred_element_type=F32)
    h_ref[...] = _layer_norm(alpha * x_ref[...] + mix, lg_ref[...], lb_ref[...])


def _merge(alpha, y_ssm, y_att, proj, col, x, wso, wao, wmix, bgs, bga, lg, lb, tm):
    t, d = x.shape
    tm = min(tm, t)
    di = y_ssm.shape[1]
    da = y_att.shape[1]

    def const(i):
        return (0, 0)

    return pl.pallas_call(
        functools.partial(_merge_kernel, alpha),
        grid=(t // tm,),
        in_specs=[pl.BlockSpec((tm, di), lambda i: (i, 0)),
                  pl.BlockSpec((tm, da), lambda i: (i, 0)),
                  pl.BlockSpec((tm, d), lambda i: (i, col["gs"] // d)),
                  pl.BlockSpec((tm, d), lambda i: (i, col["ga"] // d)),
                  pl.BlockSpec((tm, d), lambda i: (i, 0)),
                  pl.BlockSpec((di, d), const),
                  pl.BlockSpec((da, d), const),
                  pl.BlockSpec((d, d), const),
                  pl.BlockSpec((1, d), const),
                  pl.BlockSpec((1, d), const),
                  pl.BlockSpec((1, d), const),
                  pl.BlockSpec((1, d), const)],
        out_specs=pl.BlockSpec((tm, d), lambda i: (i, 0)),
        out_shape=jax.ShapeDtypeStruct((t, d), F32),
        compiler_params=_cparams(("parallel",)),
        name="merge_ln1",
    )(y_ssm, y_att, proj, proj, x, wso, wao, wmix, bgs, bga, lg, lb)


def _router_kernel(h_ref, wr_ref, br_ref, idx_ref, gw_ref, rank_ref, cnt_ref, tri_ref, base_ref):
    i = pl.program_id(0)
    tm = h_ref.shape[0]
    n_exp = wr_ref.shape[0]

    @pl.when(i == 0)
    def _():
        r = lax.broadcasted_iota(jnp.int32, (tm, tm), 0)
        c = lax.broadcasted_iota(jnp.int32, (tm, tm), 1)
        tri_ref[...] = jnp.where(r < c, 1.0, 0.0).astype(BF16)
        base_ref[...] = jnp.zeros_like(base_ref)

    logits = lax.dot_general(wr_ref[...], h_ref[...], (((1,), (1,)), ((), ())),
                             preferred_element_type=F32,
                             precision=lax.Precision.HIGHEST) + br_ref[...]
    eid = lax.broadcasted_iota(jnp.int32, (n_exp, tm), 0)
    work = logits
    vals, idxs = [], []
    for _ in range(TOP_K):
        mx = jnp.max(work, axis=0, keepdims=True)
        sel = jnp.min(jnp.where(work == mx, eid, n_exp), axis=0, keepdims=True)
        vals.append(mx)
        idxs.append(sel)
        work = jnp.where(eid == sel, -jnp.inf, work)
    exps = [jnp.exp(v - vals[0]) for v in vals]
    den = exps[0]
    for e in exps[1:]:
        den = den + e
    base = base_ref[...]
    for k in range(TOP_K):
        onehot = eid == idxs[k]
        oh = jnp.where(onehot, 1.0, 0.0)
        before = jnp.dot(oh.astype(BF16), tri_ref[...], preferred_element_type=F32)
        rank = jnp.sum(jnp.where(onehot, before + base, 0.0), axis=0, keepdims=True)
        idx_ref[k:k + 1, :] = idxs[k]
        gw_ref[k:k + 1, :] = exps[k] / den
        rank_ref[k:k + 1, :] = rank.astype(jnp.int32)
        base = base + jnp.sum(oh, axis=1, keepdims=True)
    base_ref[...] = base
    cnt_ref[...] = jnp.broadcast_to(base, cnt_ref.shape).astype(jnp.int32)


def _router(h, w_router_t, b_router, tm):
    t, d = h.shape
    n_exp = w_router_t.shape[0]
    tm = min(tm, t)
    outs = pl.pallas_call(
        _router_kernel,
        grid=(t // tm,),
        in_specs=[pl.BlockSpec((tm, d), lambda i: (i, 0)),
                  pl.BlockSpec((n_exp, d), lambda i: (0, 0)),
                  pl.BlockSpec((n_exp, 1), lambda i: (0, 0))],
        out_specs=[pl.BlockSpec((TOP_K, tm), lambda i: (0, i)),
                   pl.BlockSpec((TOP_K, tm), lambda i: (0, i)),
                   pl.BlockSpec((TOP_K, tm), lambda i: (0, i)),
                   pl.BlockSpec((n_exp, LANES), lambda i: (0, 0))],
        out_shape=[jax.ShapeDtypeStruct((TOP_K, t), jnp.int32),
                   jax.ShapeDtypeStruct((TOP_K, t), F32),
                   jax.ShapeDtypeStruct((TOP_K, t), jnp.int32),
                   jax.ShapeDtypeStruct((n_exp, LANES), jnp.int32)],
        scratch_shapes=[pltpu.VMEM((tm, tm), BF16), pltpu.VMEM((n_exp, 1), F32)],
        compiler_params=_cparams(("arbitrary",)),
        name="router",
    )(h, w_router_t, b_router)
    return outs


SC_CORES = 2
SC_SUBCORES = 16
SC_GATHER_CHUNK = 64


def _sc_gather_rows(table, idx):
    b = idx.shape[0]
    d = table.shape[1]
    n_workers = SC_CORES * SC_SUBCORES
    chunk = SC_GATHER_CHUNK
    assert b % (n_workers * chunk) == 0
    per_worker = b // n_workers
    n_chunks = per_worker // chunk
    mesh = plsc.VectorSubcoreMesh(core_axis_name="c", subcore_axis_name="s",
                                  num_cores=SC_CORES, num_subcores=SC_SUBCORES)

    def body(table_hbm, idx_hbm, out_hbm, idx_v, rows_v, sem):
        wid = lax.axis_index("s") * SC_CORES + lax.axis_index("c")
        base = wid * per_worker

        @pl.loop(0, n_chunks)
        def _(c):
            off = pl.multiple_of(base + c * chunk, SUBLANES)
            pltpu.sync_copy(idx_hbm.at[pl.ds(off, chunk)], idx_v)
            pltpu.async_copy(table_hbm.at[idx_v], rows_v, sem).wait()
            pltpu.sync_copy(rows_v, out_hbm.at[pl.ds(off, chunk)])

    return pl.kernel(
        body,
        out_type=jax.ShapeDtypeStruct((b, d), table.dtype),
        mesh=mesh,
        scratch_types=[pltpu.VMEM((chunk,), jnp.int32),
                       pltpu.VMEM((chunk, d), table.dtype),
                       pltpu.SemaphoreType.DMA],
        name="sc_gather_rows",
    )(table, idx)


def _sc_scatter_rows(src, dest_kt, fill_idx, n_out_rows):
    t, d = src.shape
    n_k = dest_kt.shape[0]
    n_fill = fill_idx.shape[0]
    n_workers = SC_CORES * SC_SUBCORES
    chunk = SC_GATHER_CHUNK
    assert t % (n_workers * chunk) == 0 and n_fill % (n_workers * chunk) == 0
    tok_per_worker = t // n_workers
    fill_per_worker = n_fill // n_workers
    mesh = plsc.VectorSubcoreMesh(core_axis_name="c", subcore_axis_name="s",
                                  num_cores=SC_CORES, num_subcores=SC_SUBCORES)

    def body(src_hbm, dest_hbm, fill_hbm, zeros_hbm, out_hbm, idx_v, rows_v):
        wid = lax.axis_index("s") * SC_CORES + lax.axis_index("c")

        pltpu.sync_copy(zeros_hbm, rows_v)

        @pl.loop(0, fill_per_worker // chunk)
        def _(c):
            off = pl.multiple_of(wid * fill_per_worker + c * chunk, SUBLANES)
            pltpu.sync_copy(fill_hbm.at[pl.ds(off, chunk)], idx_v)
            pltpu.sync_copy(rows_v, out_hbm.at[idx_v])

        @pl.loop(0, tok_per_worker // chunk)
        def _(c):
            off = pl.multiple_of(wid * tok_per_worker + c * chunk, SUBLANES)
            pltpu.sync_copy(src_hbm.at[pl.ds(off, chunk)], rows_v)
            for k in range(n_k):
                pltpu.sync_copy(dest_hbm.at[pl.ds(k * t + off, chunk)], idx_v)
                pltpu.sync_copy(rows_v, out_hbm.at[idx_v])

    return pl.kernel(
        body,
        out_type=jax.ShapeDtypeStruct((n_out_rows, d), src.dtype),
        mesh=mesh,
        scratch_types=[pltpu.VMEM((chunk,), jnp.int32),
                       pltpu.VMEM((chunk, d), src.dtype)],
        name="sc_scatter_rows",
    )(src, dest_kt.reshape(-1), fill_idx, jnp.zeros((chunk, d), src.dtype))


def _expert_kernel(be_ref, nu_ref, x_ref, wgu_ref, bgu_ref, wd_ref, bd_ref, y_ref, wgu_b, wd_b):
    i = pl.program_id(0)
    n_used = nu_ref[0]
    changed = jnp.logical_or(i == 0, be_ref[i] != be_ref[jnp.maximum(i - 1, 0)])

    @pl.when(jnp.logical_and(changed, i < n_used))
    def _():
        wgu_b[...] = wgu_ref[0].astype(BF16)
        wd_b[...] = wd_ref[0].astype(BF16)

    @pl.when(i < n_used)
    def _():
        d_ff = wd_b.shape[0]
        xb = x_ref[...].astype(BF16)
        hgu = jnp.dot(xb, wgu_b[...], preferred_element_type=F32) + bgu_ref[0]
        glu = jnp.minimum(hgu[:, :d_ff], SWIGLU_LIMIT)
        lin = jnp.clip(hgu[:, d_ff:], -SWIGLU_LIMIT, SWIGLU_LIMIT)
        act = glu * _sigmoid(SWIGLU_ALPHA * glu) * (lin + 1.0)
        y_ref[...] = jnp.dot(act.astype(BF16), wd_b[...], preferred_element_type=F32) + bd_ref[0]

    @pl.when(i >= n_used)
    def _():
        y_ref[...] = jnp.zeros_like(y_ref)


def _experts(xs, block_expert, n_used, wgu, bgu, wd, bd):
    d = xs.shape[1]
    n_blocks = block_expert.shape[0]
    n_exp, _, two_f = wgu.shape
    d_ff = wd.shape[1]
    rows = MOE_BLOCK
    n_rows = n_blocks * rows
    grid_spec = pltpu.PrefetchScalarGridSpec(
        num_scalar_prefetch=2,
        grid=(n_blocks,),
        in_specs=[
            pl.BlockSpec((rows, d), lambda i, be, nu: (jnp.minimum(i, jnp.maximum(nu[0] - 1, 0)), 0)),
            pl.BlockSpec((1, d, two_f), lambda i, be, nu: (be[i], 0, 0)),
            pl.BlockSpec((1, 1, two_f), lambda i, be, nu: (be[i], 0, 0)),
            pl.BlockSpec((1, d_ff, d), lambda i, be, nu: (be[i], 0, 0)),
            pl.BlockSpec((1, 1, d), lambda i, be, nu: (be[i], 0, 0)),
        ],
        out_specs=pl.BlockSpec((rows, d), lambda i, be, nu: (i, 0)),
        scratch_shapes=[pltpu.VMEM((d, two_f), BF16), pltpu.VMEM((d_ff, d), BF16)],
    )
    return pl.pallas_call(
        _expert_kernel,
        grid_spec=grid_spec,
        out_shape=jax.ShapeDtypeStruct((n_rows, d), F32),
        compiler_params=_cparams(("arbitrary",)),
        name="experts",
    )(block_expert, n_used, xs, wgu, bgu.reshape(n_exp, 1, two_f), wd, bd.reshape(n_exp, 1, d))


def _combine_kernel(alpha, ys_ref, gw_ref, h_ref, lg_ref, lb_ref, o_ref):
    d = h_ref.shape[1]
    gw = gw_ref[...]
    acc = alpha * h_ref[...]
    for k in range(TOP_K):
        acc = acc + gw[:, k:k + 1] * ys_ref[:, k * d:(k + 1) * d]
    o_ref[...] = _layer_norm(acc, lg_ref[...], lb_ref[...])


def _combine(alpha, y_slots, gw_rows, h, lg, lb, tm):
    t, d = h.shape
    tm = min(tm, t)
    return pl.pallas_call(
        functools.partial(_combine_kernel, alpha),
        grid=(t // tm,),
        in_specs=[pl.BlockSpec((tm, TOP_K * d), lambda i: (i, 0)),
                  pl.BlockSpec((tm, LANES), lambda i: (i, 0)),
                  pl.BlockSpec((tm, d), lambda i: (i, 0)),
                  pl.BlockSpec((1, d), lambda i: (0, 0)),
                  pl.BlockSpec((1, d), lambda i: (0, 0))],
        out_specs=pl.BlockSpec((tm, d), lambda i: (i, 0)),
        out_shape=jax.ShapeDtypeStruct((t, d), F32),
        compiler_params=_cparams(("parallel",)),
        name="combine_ln2",
    )(y_slots.reshape(t, TOP_K * d), gw_rows, h, lg, lb)


def _pack_in_proj(w_in, d_inner, n_ssm_heads, attn_dim):
    gn = SSM_GROUPS * D_STATE
    kv_dim = KV_HEADS * HEAD_DIM
    d = w_in.shape[0]
    o = 0
    seg = {}
    for name, width in (("z", d_inner), ("x", d_inner), ("B", gn), ("C", gn), ("dt", n_ssm_heads),
                        ("q", attn_dim), ("k", kv_dim), ("v", kv_dim), ("gs", d), ("ga", d)):
        seg[name] = w_in[:, o:o + width]
        o += width
    assert o == w_in.shape[1]
    dt_pad = jnp.pad(seg["dt"], ((0, 0), (0, LANES - n_ssm_heads)))
    order = (("z", seg["z"]), ("x", seg["x"]), ("gs", seg["gs"]), ("ga", seg["ga"]),
             ("B", seg["B"]), ("C", seg["C"]), ("dt", dt_pad))
    col, off = {}, 0
    for name, w in order:
        col[name] = off
        assert off % w.shape[1] == 0
        off += w.shape[1]
    w_f32grp = jnp.concatenate([w for _, w in order], axis=1).astype(BF16)
    w_qkv = jnp.concatenate([seg["q"], seg["k"], seg["v"]], axis=1).astype(BF16)
    return w_f32grp, w_qkv, col


def _largest_tile(n, cap):
    best = LANES
    for k in range(1, n // LANES + 1):
        if n % (k * LANES) == 0 and k * LANES <= cap:
            best = k * LANES
    return best


def _layer(h_in, batch, seq, alpha, w_in, conv_w, conv_b, dt_bias, a_log, d_skip, ssm_norm_w,
           w_ssm_out, attn_sinks, w_attn_out, b_gates, w_mix_out, ln1_g, ln1_b, w_router, b_router,
           w_gate_up, b_gate_up, w_down, b_down, ln2_g, ln2_b):
    t, d = h_in.shape
    d_inner = ssm_norm_w.shape[0]
    n_ssm_heads = dt_bias.shape[0]
    attn_dim = w_attn_out.shape[0]
    n_heads = attn_sinks.shape[0]
    n_exp = w_router.shape[1]
    gn = SSM_GROUPS * D_STATE

    w_f32grp, w_qkv, col = _pack_in_proj(w_in, d_inner, n_ssm_heads, attn_dim)
    proj = _matmul(h_in, w_f32grp, F32, 512, _largest_tile(w_f32grp.shape[1], 2560))
    qkv = _matmul(h_in, w_qkv, BF16, 1024, _largest_tile(w_qkv.shape[1], 1024))

    pad_h = (0, LANES - n_ssm_heads)
    y_ssm = _ssd(proj, batch, seq, col,
                 conv_w[:, :d_inner], conv_w[:, d_inner:d_inner + gn], conv_w[:, d_inner + gn:],
                 conv_b[None, :d_inner], conv_b[None, d_inner:d_inner + gn], conv_b[None, d_inner + gn:],
                 jnp.pad(dt_bias, pad_h)[None, :], jnp.pad(a_log, pad_h)[None, :],
                 jnp.repeat(d_skip, SSM_HEAD_DIM)[None, :], ssm_norm_w[None, :])
    y_att = _swa(qkv, attn_sinks, batch, seq, n_heads)

    h1 = _merge(alpha, y_ssm, y_att, proj, col, h_in,
                w_ssm_out.astype(BF16), w_attn_out.astype(BF16), w_mix_out.astype(BF16),
                b_gates[None, :d], b_gates[None, d:], ln1_g[None, :], ln1_b[None, :], 512)

    idx_kt, gw_kt, rank_kt, counts = _router(h1, w_router.T, b_router[:, None], 1024)

    counts = counts[:, 0]
    padded = (counts + MOE_BLOCK - 1) // MOE_BLOCK * MOE_BLOCK
    pad_end = jnp.cumsum(padded)
    pad_start = pad_end - padded
    n_slots = t * TOP_K
    n_blocks = -(-n_slots // MOE_BLOCK) + n_exp
    n_rows = n_blocks * MOE_BLOCK
    expert_ids = jnp.arange(n_exp, dtype=jnp.int32)
    pad_start_of_slot = jnp.sum(
        jnp.where(idx_kt[None] == expert_ids[:, None, None], pad_start[:, None, None], 0), axis=0)
    dest_kt = pad_start_of_slot + rank_kt
    fill_rows = (pad_start + counts)[:, None] + jnp.arange(MOE_BLOCK, dtype=jnp.int32)[None, :]
    spare_rows = n_rows + jnp.arange(n_exp * MOE_BLOCK, dtype=jnp.int32).reshape(n_exp, MOE_BLOCK)
    fill_idx = jnp.where(fill_rows < pad_end[:, None], fill_rows, spare_rows).reshape(-1)
    block_row0 = jnp.arange(n_blocks, dtype=jnp.int32) * MOE_BLOCK
    block_expert = jnp.minimum(
        jnp.sum((pad_end[None, :] <= block_row0[:, None]).astype(jnp.int32), axis=1), n_exp - 1)
    n_used = (pad_end[-1:] // MOE_BLOCK).astype(jnp.int32)
    gw_rows = jnp.pad(gw_kt.T, ((0, 0), (0, LANES - TOP_K)))

    xs = _sc_scatter_rows(h1, dest_kt, fill_idx, n_rows + n_exp * MOE_BLOCK)
    ys = _experts(xs, block_expert, n_used, w_gate_up, b_gate_up, w_down, b_down)
    y_slots = _sc_gather_rows(ys, dest_kt.T.reshape(-1))
    return _combine(alpha, y_slots, gw_rows, h1, ln2_g[None, :], ln2_b[None, :], 256)


def kernel(x, w_in, conv_w, conv_b, dt_bias, a_log, d_skip, ssm_norm_w, w_ssm_out, attn_sinks,
           w_attn_out, b_gates, w_mix_out, ln1_g, ln1_b, w_router, b_router, w_gate_up, b_gate_up,
           w_down, b_down, ln2_g, ln2_b):
    batch, seq, d = x.shape
    depth = w_in.shape[0]
    alpha = (2 * depth) ** 0.25
    h = x.reshape(batch * seq, d)
    for i in range(depth):
        h = _layer(h, batch, seq, alpha, w_in[i], conv_w[i], conv_b[i], dt_bias[i], a_log[i],
                   d_skip[i], ssm_norm_w[i], w_ssm_out[i], attn_sinks[i], w_attn_out[i], b_gates[i],
                   w_mix_out[i], ln1_g[i], ln1_b[i], w_router[i], b_router[i], w_gate_up[i],
                   b_gate_up[i], w_down[i], b_down[i], ln2_g[i], ln2_b[i])
    return h.reshape(batch, seq, d)
```

```python
import functools

import jax
import jax.numpy as jnp
from jax import lax
from jax.experimental import pallas as pl
from jax.experimental.pallas import tpu as pltpu
from jax.experimental.pallas import tpu_sc as plsc

SSM_HEAD_DIM = 64
SSM_GROUPS = 4
D_STATE = 128
CONV_WIDTH = 4
SSD_CHUNK = 128
KV_HEADS = 4
HEAD_DIM = 64
WINDOW = 128
TOP_K = 4
SWIGLU_LIMIT = 7.0
SWIGLU_ALPHA = 1.702
MOE_BLOCK = 512
LN_EPS = 1e-5
RMS_EPS = 1e-5

LANES = 128
SUBLANES = 8
NEG_BIG = -1e30
LOG2_E = 1.4426950408889634
F32 = jnp.float32
BF16 = jnp.bfloat16
VMEM_LIMIT = 56 * 1024 * 1024


def _cparams(sem):
    return pltpu.CompilerParams(dimension_semantics=sem, vmem_limit_bytes=VMEM_LIMIT)


def _sigmoid(x):
    return 0.5 + 0.5 * jnp.tanh(0.5 * x)


def _silu(x):
    h = 0.5 * x
    return h + h * jnp.tanh(h)


def _mm_kernel(x_ref, w_ref, o_ref):
    o_ref[...] = jnp.dot(x_ref[...].astype(BF16), w_ref[...],
                         preferred_element_type=F32).astype(o_ref.dtype)


def _matmul(x, w, out_dtype, tm, tn):
    m, k = x.shape
    n = w.shape[1]
    tm = min(tm, m)
    assert m % tm == 0 and n % tn == 0
    return pl.pallas_call(
        _mm_kernel,
        grid=(n // tn, m // tm),
        in_specs=[pl.BlockSpec((tm, k), lambda j, i: (i, 0)),
                  pl.BlockSpec((k, tn), lambda j, i: (0, j))],
        out_specs=pl.BlockSpec((tm, tn), lambda j, i: (i, j)),
        out_shape=jax.ShapeDtypeStruct((m, n), out_dtype),
        compiler_params=_cparams(("parallel", "parallel")),
        name="in_proj",
    )(x, w)


def _conv_silu(u_ref, prev_ref, w_ref, b_ref, ext_ref, first):
    n = u_ref.shape[0]
    prev = prev_ref[...]
    ext_ref[0:SUBLANES, :] = jnp.where(first, jnp.zeros_like(prev), prev)
    ext_ref[SUBLANES:SUBLANES + n, :] = u_ref[...]
    acc = b_ref[...] + w_ref[CONV_WIDTH - 1:CONV_WIDTH, :] * ext_ref[SUBLANES:SUBLANES + n, :]
    for j in range(CONV_WIDTH - 1):
        off = SUBLANES - (CONV_WIDTH - 1) + j
        acc = acc + w_ref[j:j + 1, :] * ext_ref[off:off + n, :]
    return _silu(acc)


def _pair_expand(v, h0, lo_mask):
    n = v.shape[0]
    a = jnp.broadcast_to(v[:, h0:h0 + 1], (n, LANES))
    b = jnp.broadcast_to(v[:, h0 + 1:h0 + 2], (n, LANES))
    return jnp.where(lo_mask, a, b)


def _ssd_kernel(z_ref, x_ref, b_ref, c_ref, dt_ref, xp_ref, bp_ref, cp_ref,
                cwx_ref, cwb_ref, cwc_ref, cbx_ref, cbb_ref, cbc_ref,
                dtb_ref, alog_ref, dsk_ref, nw_ref,
                o_ref, st_ref, extx_ref, extb_ref, extc_ref):
    c = pl.program_id(1)
    first = c == 0
    n = SSD_CHUNK
    heads_per_group = st_ref.shape[2] // SSM_HEAD_DIM
    gw = heads_per_group * SSM_HEAD_DIM

    @pl.when(first)
    def _():
        st_ref[...] = jnp.zeros_like(st_ref)

    xc = _conv_silu(x_ref, xp_ref, cwx_ref, cbx_ref, extx_ref, first)
    bc = _conv_silu(b_ref, bp_ref, cwb_ref, cbb_ref, extb_ref, first)
    cc = _conv_silu(c_ref, cp_ref, cwc_ref, cbc_ref, extc_ref, first)

    dt_in = dt_ref[...] + dtb_ref[...]
    dt = jnp.maximum(dt_in, 0.0) + jnp.log(1.0 + jnp.exp(-jnp.abs(dt_in)))
    a = dt * (-jnp.exp(alog_ref[...]))
    row = lax.broadcasted_iota(jnp.int32, (n, n), 0)
    col = lax.broadcasted_iota(jnp.int32, (n, n), 1)
    causal = row >= col
    tri = jnp.where(causal, 1.0, 0.0).astype(F32)
    acum = jnp.dot(tri, a, preferred_element_type=F32, precision=lax.Precision.HIGHEST)
    acum2 = acum * LOG2_E
    acum2_t = acum2.T
    a_last = acum[n - 1:n, :]
    d2e = jnp.exp(a_last - acum)
    eac = jnp.exp(acum)
    lo_mask = lax.broadcasted_iota(jnp.int32, (n, LANES), 1) < SSM_HEAD_DIM

    for g in range(SSM_GROUPS):
        xg = xc[:, g * gw:(g + 1) * gw]
        bg = bc[:, g * D_STATE:(g + 1) * D_STATE]
        cg = cc[:, g * D_STATE:(g + 1) * D_STATE]
        cg_b = cg.astype(BF16)
        bg_t = bg.T.astype(BF16)
        cb = jnp.dot(cg_b, bg_t, preferred_element_type=F32)
        st_prev = st_ref[g]
        y_off = jnp.dot(cg_b, st_prev.astype(BF16), preferred_element_type=F32)
        y_parts, xw_parts, ea_parts = [], [], []
        for j in range(heads_per_group // 2):
            h0 = g * heads_per_group + 2 * j
            lanes = slice(j * LANES, (j + 1) * LANES)
            xs_pair = xg[:, lanes] * _pair_expand(dt, h0, lo_mask)
            xs_b = xs_pair.astype(BF16)
            ea_pair = _pair_expand(eac, h0, lo_mask)
            xw_parts.append((xs_pair * _pair_expand(d2e, h0, lo_mask)).astype(BF16))
            ea_parts.append(ea_pair)
            ys = []
            for h in (h0, h0 + 1):
                seg2 = acum2[:, h:h + 1] - acum2_t[h:h + 1, :]
                decay = jnp.exp2(jnp.where(causal, seg2, NEG_BIG))
                m = (cb * decay).astype(BF16)
                ys.append(jnp.dot(m, xs_b, preferred_element_type=F32))
            y_parts.append(jnp.where(lo_mask, ys[0], ys[1]))
        y_diag = jnp.concatenate(y_parts, axis=1)
        ea = jnp.concatenate(ea_parts, axis=1)
        xw = jnp.concatenate(xw_parts, axis=1)
        cols = slice(g * gw, (g + 1) * gw)
        y = y_diag + y_off * ea + xg * dsk_ref[:, cols]
        zg = z_ref[:, cols]
        y = y * _silu(zg)
        ms = jnp.mean(y * y, axis=-1, keepdims=True)
        o_ref[:, cols] = (y * lax.rsqrt(ms + RMS_EPS) * nw_ref[:, cols]).astype(o_ref.dtype)
        st_ref[g] = st_prev * ea[n - 1:n, :] + jnp.dot(bg_t, xw, preferred_element_type=F32)


def _ssd(proj, batch, seq, col, conv_wx, conv_wb, conv_wc, conv_bx, conv_bb, conv_bc,
         dt_bias, a_log, d_skip, norm_w):
    t = proj.shape[0]
    d_inner = norm_w.shape[1]
    gn = SSM_GROUPS * D_STATE
    n = SSD_CHUNK
    nc = seq // n
    gw = d_inner // SSM_GROUPS

    def rows(b, c):
        return b * nc + c

    def prev_rows(b, c):
        return jnp.maximum((b * nc + c) * (n // SUBLANES) - 1, 0)

    def const(b, c):
        return (0, 0)

    in_specs = [
        pl.BlockSpec((n, d_inner), lambda b, c: (rows(b, c), col["z"] // d_inner)),
        pl.BlockSpec((n, d_inner), lambda b, c: (rows(b, c), col["x"] // d_inner)),
        pl.BlockSpec((n, gn), lambda b, c: (rows(b, c), col["B"] // gn)),
        pl.BlockSpec((n, gn), lambda b, c: (rows(b, c), col["C"] // gn)),
        pl.BlockSpec((n, LANES), lambda b, c: (rows(b, c), col["dt"] // LANES)),
        pl.BlockSpec((SUBLANES, d_inner), lambda b, c: (prev_rows(b, c), col["x"] // d_inner)),
        pl.BlockSpec((SUBLANES, gn), lambda b, c: (prev_rows(b, c), col["B"] // gn)),
        pl.BlockSpec((SUBLANES, gn), lambda b, c: (prev_rows(b, c), col["C"] // gn)),
        pl.BlockSpec((CONV_WIDTH, d_inner), const),
        pl.BlockSpec((CONV_WIDTH, gn), const),
        pl.BlockSpec((CONV_WIDTH, gn), const),
        pl.BlockSpec((1, d_inner), const),
        pl.BlockSpec((1, gn), const),
        pl.BlockSpec((1, gn), const),
        pl.BlockSpec((1, LANES), const),
        pl.BlockSpec((1, LANES), const),
        pl.BlockSpec((1, d_inner), const),
        pl.BlockSpec((1, d_inner), const),
    ]
    return pl.pallas_call(
        _ssd_kernel,
        grid=(batch, nc),
        in_specs=in_specs,
        out_specs=pl.BlockSpec((n, d_inner), lambda b, c: (rows(b, c), 0)),
        out_shape=jax.ShapeDtypeStruct((t, d_inner), BF16),
        scratch_shapes=[pltpu.VMEM((SSM_GROUPS, D_STATE, gw), F32),
                        pltpu.VMEM((SUBLANES + n, d_inner), F32),
                        pltpu.VMEM((SUBLANES + n, gn), F32),
                        pltpu.VMEM((SUBLANES + n, gn), F32)],
        compiler_params=_cparams(("arbitrary", "arbitrary")),
        name="ssd_mixer",
    )(proj, proj, proj, proj, proj, proj, proj, proj,
      conv_wx, conv_wb, conv_wc, conv_bx, conv_bb, conv_bc, dt_bias, a_log, d_skip, norm_w)


def _swa_kernel(sink_ref, q_ref, kc_ref, vc_ref, kp_ref, vp_ref, o_ref):
    i = pl.program_id(1)
    n = WINDOW
    n_heads = q_ref.shape[1] // HEAD_DIM
    grp = n_heads // KV_HEADS
    qpos = lax.broadcasted_iota(jnp.int32, (n, 2 * n), 0) + n
    kpos = lax.broadcasted_iota(jnp.int32, (n, 2 * n), 1)
    diff = qpos - kpos
    mask = (diff >= 0) & (diff < n) & ((kpos >= n) | (i > 0))
    lo_q = lax.broadcasted_iota(jnp.int32, (n, LANES), 1) < HEAD_DIM
    lo_k = lax.broadcasted_iota(jnp.int32, (2 * n, LANES), 1) < HEAD_DIM
    scale = HEAD_DIM ** -0.5
    for p in range(KV_HEADS // 2):
        lanes = slice(p * LANES, (p + 1) * LANES)
        kk = jnp.concatenate([kp_ref[:, lanes], kc_ref[:, lanes]], axis=0).astype(F32)
        vv = jnp.concatenate([vp_ref[:, lanes], vc_ref[:, lanes]], axis=0).astype(F32)
        kk_sw = pltpu.roll(kk, HEAD_DIM, axis=1)
        vv_sw = pltpu.roll(vv, HEAD_DIM, axis=1)
        for par in range(2):
            g = 2 * p + par
            k2 = (jnp.where(lo_k, kk, kk_sw) if par == 0 else jnp.where(lo_k, kk_sw, kk)).astype(BF16)
            v2 = (jnp.where(lo_k, vv, vv_sw) if par == 0 else jnp.where(lo_k, vv_sw, vv)).astype(BF16)
            for qp in range(grp // 2):
                h0 = g * grp + 2 * qp
                qlanes = slice((h0 // 2) * LANES, (h0 // 2 + 1) * LANES)
                q_pair = q_ref[:, qlanes]
                outs = []
                for hh in range(2):
                    keep = lo_q if hh == 0 else jnp.logical_not(lo_q)
                    qm = jnp.where(keep, q_pair, jnp.zeros_like(q_pair))
                    s = lax.dot_general(qm, k2, (((1,), (1,)), ((), ())),
                                        preferred_element_type=F32) * scale
                    s = jnp.where(mask, s, NEG_BIG)
                    sink = sink_ref[h0 + hh]
                    mx = jnp.maximum(jnp.max(s, axis=-1, keepdims=True), sink)
                    pr = jnp.exp(s - mx)
                    den = jnp.sum(pr, axis=-1, keepdims=True) + jnp.exp(sink - mx)
                    o = jnp.dot(pr.astype(BF16), v2, preferred_element_type=F32)
                    outs.append(o / den)
                o_ref[:, qlanes] = jnp.where(lo_q, outs[0], outs[1]).astype(o_ref.dtype)


def _swa(qkv, sinks, batch, seq, n_heads):
    t = qkv.shape[0]
    n = WINDOW
    nb = seq // n
    qw = n_heads * HEAD_DIM
    kw = KV_HEADS * HEAD_DIM
    k_blk = qw // kw
    v_blk = k_blk + 1

    def cur(b, i):
        return b * nb + i

    def prev(b, i):
        return b * nb + jnp.maximum(i - 1, 0)

    return pl.pallas_call(
        _swa_kernel,
        grid=(batch, nb),
        in_specs=[pl.BlockSpec(memory_space=pltpu.SMEM),
                  pl.BlockSpec((n, qw), lambda b, i: (cur(b, i), 0)),
                  pl.BlockSpec((n, kw), lambda b, i: (cur(b, i), k_blk)),
                  pl.BlockSpec((n, kw), lambda b, i: (cur(b, i), v_blk)),
                  pl.BlockSpec((n, kw), lambda b, i: (prev(b, i), k_blk)),
                  pl.BlockSpec((n, kw), lambda b, i: (prev(b, i), v_blk))],
        out_specs=pl.BlockSpec((n, qw), lambda b, i: (cur(b, i), 0)),
        out_shape=jax.ShapeDtypeStruct((t, qw), BF16),
        compiler_params=_cparams(("parallel", "parallel")),
        name="swa",
    )(sinks, qkv, qkv, qkv, qkv, qkv)


def _layer_norm(v, g, b):
    mu = jnp.mean(v, axis=-1, keepdims=True)
    d = v - mu
    var = jnp.mean(d * d, axis=-1, keepdims=True)
    return d * lax.rsqrt(var + LN_EPS) * g + b


def _merge_kernel(alpha, ys_ref, ya_ref, gs_ref, ga_ref, x_ref, wso_ref, wao_ref, wmix_ref,
                  bgs_ref, bga_ref, lg_ref, lb_ref, h_ref):
    y_ssm = jnp.dot(ys_ref[...], wso_ref[...], preferred_element_type=F32)
    y_att = jnp.dot(ya_ref[...], wao_ref[...], preferred_element_type=F32)
    merged = (_sigmoid(gs_ref[...] + bgs_ref[...]) * y_ssm
              + _sigmoid(ga_ref[...] + bga_ref[...]) * y_att)
    mix = jnp.dot(merged.astype(BF16), wmix_ref[...], preferred_element_type=F32)
    h_ref[...] = _layer_norm(alpha * x_ref[...] + mix, lg_ref[...], lb_ref[...])


def _merge(alpha, y_ssm, y_att, proj, col, x, wso, wao, wmix, bgs, bga, lg, lb, tm):
    t, d = x.shape
    tm = min(tm, t)
    di = y_ssm.shape[1]
    da = y_att.shape[1]

    def const(i):
        return (0, 0)

    return pl.pallas_call(
        functools.partial(_merge_kernel, alpha),
        grid=(t // tm,),
        in_specs=[pl.BlockSpec((tm, di), lambda i: (i, 0)),
                  pl.BlockSpec((tm, da), lambda i: (i, 0)),
                  pl.BlockSpec((tm, d), lambda i: (i, col["gs"] // d)),
                  pl.BlockSpec((tm, d), lambda i: (i, col["ga"] // d)),
                  pl.BlockSpec((tm, d), lambda i: (i, 0)),
                  pl.BlockSpec((di, d), const),
                  pl.BlockSpec((da, d), const),
                  pl.BlockSpec((d, d), const),
                  pl.BlockSpec((1, d), const),
                  pl.BlockSpec((1, d), const),
                  pl.BlockSpec((1, d), const),
                  pl.BlockSpec((1, d), const)],
        out_specs=pl.BlockSpec((tm, d), lambda i: (i, 0)),
        out_shape=jax.ShapeDtypeStruct((t, d), F32),
        compiler_params=_cparams(("parallel",)),
        name="merge_ln1",
    )(y_ssm, y_att, proj, proj, x, wso, wao, wmix, bgs, bga, lg, lb)


def _router_kernel(h_ref, wr_ref, br_ref, idx_ref, gw_ref, rank_ref, cnt_ref, tri_ref, base_ref):
    i = pl.program_id(0)
    tm = h_ref.shape[0]
    n_exp = wr_ref.shape[0]

    @pl.when(i == 0)
    def _():
        r = lax.broadcasted_iota(jnp.int32, (tm, tm), 0)
        c = lax.broadcasted_iota(jnp.int32, (tm, tm), 1)
        tri_ref[...] = jnp.where(r < c, 1.0, 0.0).astype(BF16)
        base_ref[...] = jnp.zeros_like(base_ref)

    logits = lax.dot_general(wr_ref[...], h_ref[...], (((1,), (1,)), ((), ())),
                             preferred_element_type=F32,
                             precision=lax.Precision.HIGHEST) + br_ref[...]
    eid = lax.broadcasted_iota(jnp.int32, (n_exp, tm), 0)
    work = logits
    vals, idxs = [], []
    for _ in range(TOP_K):
        mx = jnp.max(work, axis=0, keepdims=True)
        sel = jnp.min(jnp.where(work == mx, eid, n_exp), axis=0, keepdims=True)
        vals.append(mx)
        idxs.append(sel)
        work = jnp.where(eid == sel, -jnp.inf, work)
    exps = [jnp.exp(v - vals[0]) for v in vals]
    den = exps[0]
    for e in exps[1:]:
        den = den + e
    base = base_ref[...]
    for k in range(TOP_K):
        onehot = eid == idxs[k]
        oh = jnp.where(onehot, 1.0, 0.0)
        before = jnp.dot(oh.astype(BF16), tri_ref[...], preferred_element_type=F32)
        rank = jnp.sum(jnp.where(onehot, before + base, 0.0), axis=0, keepdims=True)
        idx_ref[k:k + 1, :] = idxs[k]
        gw_ref[k:k + 1, :] = exps[k] / den
        rank_ref[k:k + 1, :] = rank.astype(jnp.int32)
        base = base + jnp.sum(oh, axis=1, keepdims=True)
    base_ref[...] = base
    cnt_ref[...] = jnp.broadcast_to(base, cnt_ref.shape).astype(jnp.int32)


def _router(h, w_router_t, b_router, tm):
    t, d = h.shape
    n_exp = w_router_t.shape[0]
    tm = min(tm, t)
    outs = pl.pallas_call(
        _router_kernel,
        grid=(t // tm,),
        in_specs=[pl.BlockSpec((tm, d), lambda i: (i, 0)),
                  pl.BlockSpec((n_exp, d), lambda i: (0, 0)),
                  pl.BlockSpec((n_exp, 1), lambda i: (0, 0))],
        out_specs=[pl.BlockSpec((TOP_K, tm), lambda i: (0, i)),
                   pl.BlockSpec((TOP_K, tm), lambda i: (0, i)),
                   pl.BlockSpec((TOP_K, tm), lambda i: (0, i)),
                   pl.BlockSpec((n_exp, LANES), lambda i: (0, 0))],
        out_shape=[jax.ShapeDtypeStruct((TOP_K, t), jnp.int32),
                   jax.ShapeDtypeStruct((TOP_K, t), F32),
                   jax.ShapeDtypeStruct((TOP_K, t), jnp.int32),
                   jax.ShapeDtypeStruct((n_exp, LANES), jnp.int32)],
        scratch_shapes=[pltpu.VMEM((tm, tm), BF16), pltpu.VMEM((n_exp, 1), F32)],
        compiler_params=_cparams(("arbitrary",)),
        name="router",
    )(h, w_router_t, b_router)
    return outs


SC_CORES = 2
SC_SUBCORES = 16
SC_GATHER_CHUNK = 64


def _sc_gather_rows(table, idx):
    b = idx.shape[0]
    d = table.shape[1]
    n_workers = SC_CORES * SC_SUBCORES
    chunk = SC_GATHER_CHUNK
    assert b % (n_workers * chunk) == 0
    per_worker = b // n_workers
    n_chunks = per_worker // chunk
    mesh = plsc.VectorSubcoreMesh(core_axis_name="c", subcore_axis_name="s",
                                  num_cores=SC_CORES, num_subcores=SC_SUBCORES)

    def body(table_hbm, idx_hbm, out_hbm, idx_v, rows_v, sem):
        wid = lax.axis_index("s") * SC_CORES + lax.axis_index("c")
        base = wid * per_worker

        @pl.loop(0, n_chunks)
        def _(c):
            off = pl.multiple_of(base + c * chunk, SUBLANES)
            pltpu.sync_copy(idx_hbm.at[pl.ds(off, chunk)], idx_v)
            pltpu.async_copy(table_hbm.at[idx_v], rows_v, sem).wait()
            pltpu.sync_copy(rows_v, out_hbm.at[pl.ds(off, chunk)])

    return pl.kernel(
        body,
        out_type=jax.ShapeDtypeStruct((b, d), table.dtype),
        mesh=mesh,
        scratch_types=[pltpu.VMEM((chunk,), jnp.int32),
                       pltpu.VMEM((chunk, d), table.dtype),
                       pltpu.SemaphoreType.DMA],
        name="sc_gather_rows",
    )(table, idx)


def _sc_scatter_rows(src, dest_kt, fill_idx, n_out_rows):
    t, d = src.shape
    n_k = dest_kt.shape[0]
    n_fill = fill_idx.shape[0]
    n_workers = SC_CORES * SC_SUBCORES
    chunk = SC_GATHER_CHUNK
    assert t % (n_workers * chunk) == 0 and n_fill % (n_workers * chunk) == 0
    tok_per_worker = t // n_workers
    fill_per_worker = n_fill // n_workers
    mesh = plsc.VectorSubcoreMesh(core_axis_name="c", subcore_axis_name="s",
                                  num_cores=SC_CORES, num_subcores=SC_SUBCORES)

    def body(src_hbm, dest_hbm, fill_hbm, zeros_hbm, out_hbm, idx_v, rows_v):
        wid = lax.axis_index("s") * SC_CORES + lax.axis_index("c")

        pltpu.sync_copy(zeros_hbm, rows_v)

        @pl.loop(0, fill_per_worker // chunk)
        def _(c):
            off = pl.multiple_of(wid * fill_per_worker + c * chunk, SUBLANES)
            pltpu.sync_copy(fill_hbm.at[pl.ds(off, chunk)], idx_v)
            pltpu.sync_copy(rows_v, out_hbm.at[idx_v])

        @pl.loop(0, tok_per_worker // chunk)
        def _(c):
            off = pl.multiple_of(wid * tok_per_worker + c * chunk, SUBLANES)
            pltpu.sync_copy(src_hbm.at[pl.ds(off, chunk)], rows_v)
            for k in range(n_k):
                pltpu.sync_copy(dest_hbm.at[pl.ds(k * t + off, chunk)], idx_v)
                pltpu.sync_copy(rows_v, out_hbm.at[idx_v])

    return pl.kernel(
        body,
        out_type=jax.ShapeDtypeStruct((n_out_rows, d), src.dtype),
        mesh=mesh,
        scratch_types=[pltpu.VMEM((chunk,), jnp.int32),
                       pltpu.VMEM((chunk, d), src.dtype)],
        name="sc_scatter_rows",
    )(src, dest_kt.reshape(-1), fill_idx, jnp.zeros((chunk, d), src.dtype))


def _expert_kernel(be_ref, nu_ref, x_ref, wgu_ref, bgu_ref, wd_ref, bd_ref, y_ref, wgu_b, wd_b):
    i = pl.program_id(0)
    n_used = nu_ref[0]
    changed = jnp.logical_or(i == 0, be_ref[i] != be_ref[jnp.maximum(i - 1, 0)])

    @pl.when(jnp.logical_and(changed, i < n_used))
    def _():
        wgu_b[...] = wgu_ref[0].astype(BF16)
        wd_b[...] = wd_ref[0].astype(BF16)

    @pl.when(i < n_used)
    def _():
        d_ff = wd_b.shape[0]
        xb = x_ref[...].astype(BF16)
        hgu = jnp.dot(xb, wgu_b[...], preferred_element_type=F32) + bgu_ref[0]
        glu = jnp.minimum(hgu[:, :d_ff], SWIGLU_LIMIT)
        lin = jnp.clip(hgu[:, d_ff:], -SWIGLU_LIMIT, SWIGLU_LIMIT)
        act = glu * _sigmoid(SWIGLU_ALPHA * glu) * (lin + 1.0)
        y_ref[...] = jnp.dot(act.astype(BF16), wd_b[...], preferred_element_type=F32) + bd_ref[0]

    @pl.when(i >= n_used)
    def _():
        y_ref[...] = jnp.zeros_like(y_ref)


def _experts(xs, block_expert, n_used, wgu, bgu, wd, bd):
    d = xs.shape[1]
    n_blocks = block_expert.shape[0]
    n_exp, _, two_f = wgu.shape
    d_ff = wd.shape[1]
    rows = MOE_BLOCK
    n_rows = n_blocks * rows
    grid_spec = pltpu.PrefetchScalarGridSpec(
        num_scalar_prefetch=2,
        grid=(n_blocks,),
        in_specs=[
            pl.BlockSpec((rows, d), lambda i, be, nu: (jnp.minimum(i, jnp.maximum(nu[0] - 1, 0)), 0)),
            pl.BlockSpec((1, d, two_f), lambda i, be, nu: (be[i], 0, 0)),
            pl.BlockSpec((1, 1, two_f), lambda i, be, nu: (be[i], 0, 0)),
            pl.BlockSpec((1, d_ff, d), lambda i, be, nu: (be[i], 0, 0)),
            pl.BlockSpec((1, 1, d), lambda i, be, nu: (be[i], 0, 0)),
        ],
        out_specs=pl.BlockSpec((rows, d), lambda i, be, nu: (i, 0)),
        scratch_shapes=[pltpu.VMEM((d, two_f), BF16), pltpu.VMEM((d_ff, d), BF16)],
    )
    return pl.pallas_call(
        _expert_kernel,
        grid_spec=grid_spec,
        out_shape=jax.ShapeDtypeStruct((n_rows, d), F32),
        compiler_params=_cparams(("arbitrary",)),
        name="experts",
    )(block_expert, n_used, xs, wgu, bgu.reshape(n_exp, 1, two_f), wd, bd.reshape(n_exp, 1, d))


def _combine_kernel(alpha, *refs):
    ys_refs = refs[:TOP_K]
    gw_ref, h_ref, lg_ref, lb_ref, o_ref = refs[TOP_K:]
    gw = gw_ref[...]
    acc = alpha * h_ref[...]
    for k in range(TOP_K):
        acc = acc + gw[:, k:k + 1] * ys_refs[k][...]
    o_ref[...] = _layer_norm(acc, lg_ref[...], lb_ref[...])


def _combine(alpha, y_slots, gw_rows, h, lg, lb, tm):
    t, d = h.shape
    tm = min(tm, t)
    nsteps = t // tm
    ys_specs = [pl.BlockSpec((tm, d), functools.partial(lambda i, k: (k * nsteps + i, 0), k=k))
                for k in range(TOP_K)]
    return pl.pallas_call(
        functools.partial(_combine_kernel, alpha),
        grid=(nsteps,),
        in_specs=ys_specs + [pl.BlockSpec((tm, LANES), lambda i: (i, 0)),
                             pl.BlockSpec((tm, d), lambda i: (i, 0)),
                             pl.BlockSpec((1, d), lambda i: (0, 0)),
                             pl.BlockSpec((1, d), lambda i: (0, 0))],
        out_specs=pl.BlockSpec((tm, d), lambda i: (i, 0)),
        out_shape=jax.ShapeDtypeStruct((t, d), F32),
        compiler_params=_cparams(("parallel",)),
        name="combine_ln2",
    )(*([y_slots] * TOP_K), gw_rows, h, lg, lb)


def _pack_in_proj(w_in, d_inner, n_ssm_heads, attn_dim):
    gn = SSM_GROUPS * D_STATE
    kv_dim = KV_HEADS * HEAD_DIM
    d = w_in.shape[0]
    o = 0
    seg = {}
    for name, width in (("z", d_inner), ("x", d_inner), ("B", gn), ("C", gn), ("dt", n_ssm_heads),
                        ("q", attn_dim), ("k", kv_dim), ("v", kv_dim), ("gs", d), ("ga", d)):
        seg[name] = w_in[:, o:o + width]
        o += width
    assert o == w_in.shape[1]
    dt_pad = jnp.pad(seg["dt"], ((0, 0), (0, LANES - n_ssm_heads)))
    order = (("z", seg["z"]), ("x", seg["x"]), ("gs", seg["gs"]), ("ga", seg["ga"]),
             ("B", seg["B"]), ("C", seg["C"]), ("dt", dt_pad))
    col, off = {}, 0
    for name, w in order:
        col[name] = off
        assert off % w.shape[1] == 0
        off += w.shape[1]
    w_f32grp = jnp.concatenate([w for _, w in order], axis=1).astype(BF16)
    w_qkv = jnp.concatenate([seg["q"], seg["k"], seg["v"]], axis=1).astype(BF16)
    return w_f32grp, w_qkv, col


def _largest_tile(n, cap):
    best = LANES
    for k in range(1, n // LANES + 1):
        if n % (k * LANES) == 0 and k * LANES <= cap:
            best = k * LANES
    return best


def _layer(h_in, batch, seq, alpha, w_in, conv_w, conv_b, dt_bias, a_log, d_skip, ssm_norm_w,
           w_ssm_out, attn_sinks, w_attn_out, b_gates, w_mix_out, ln1_g, ln1_b, w_router, b_router,
           w_gate_up, b_gate_up, w_down, b_down, ln2_g, ln2_b):
    t, d = h_in.shape
    d_inner = ssm_norm_w.shape[0]
    n_ssm_heads = dt_bias.shape[0]
    attn_dim = w_attn_out.shape[0]
    n_heads = attn_sinks.shape[0]
    n_exp = w_router.shape[1]
    gn = SSM_GROUPS * D_STATE

    w_f32grp, w_qkv, col = _pack_in_proj(w_in, d_inner, n_ssm_heads, attn_dim)
    proj = _matmul(h_in, w_f32grp, F32, 1024, _largest_tile(w_f32grp.shape[1], 2560))
    qkv = _matmul(h_in, w_qkv, BF16, 1024, _largest_tile(w_qkv.shape[1], 1536))

    pad_h = (0, LANES - n_ssm_heads)
    y_ssm = _ssd(proj, batch, seq, col,
                 conv_w[:, :d_inner], conv_w[:, d_inner:d_inner + gn], conv_w[:, d_inner + gn:],
                 conv_b[None, :d_inner], conv_b[None, d_inner:d_inner + gn], conv_b[None, d_inner + gn:],
                 jnp.pad(dt_bias, pad_h)[None, :], jnp.pad(a_log, pad_h)[None, :],
                 jnp.repeat(d_skip, SSM_HEAD_DIM)[None, :], ssm_norm_w[None, :])
    y_att = _swa(qkv, attn_sinks, batch, seq, n_heads)

    h1 = _merge(alpha, y_ssm, y_att, proj, col, h_in,
                w_ssm_out.astype(BF16), w_attn_out.astype(BF16), w_mix_out.astype(BF16),
                b_gates[None, :d], b_gates[None, d:], ln1_g[None, :], ln1_b[None, :], 512)

    idx_kt, gw_kt, rank_kt, counts = _router(h1, w_router.T, b_router[:, None], 1024)

    counts = counts[:, 0]
    padded = (counts + MOE_BLOCK - 1) // MOE_BLOCK * MOE_BLOCK
    pad_end = jnp.cumsum(padded)
    pad_start = pad_end - padded
    n_slots = t * TOP_K
    n_blocks = -(-n_slots // MOE_BLOCK) + n_exp
    n_rows = n_blocks * MOE_BLOCK
    expert_ids = jnp.arange(n_exp, dtype=jnp.int32)
    pad_start_of_slot = jnp.sum(
        jnp.where(idx_kt[None] == expert_ids[:, None, None], pad_start[:, None, None], 0), axis=0)
    dest_kt = pad_start_of_slot + rank_kt
    fill_rows = (pad_start + counts)[:, None] + jnp.arange(MOE_BLOCK, dtype=jnp.int32)[None, :]
    spare_rows = n_rows + jnp.arange(n_exp * MOE_BLOCK, dtype=jnp.int32).reshape(n_exp, MOE_BLOCK)
    fill_idx = jnp.where(fill_rows < pad_end[:, None], fill_rows, spare_rows).reshape(-1)
    block_row0 = jnp.arange(n_blocks, dtype=jnp.int32) * MOE_BLOCK
    block_expert = jnp.minimum(
        jnp.sum((pad_end[None, :] <= block_row0[:, None]).astype(jnp.int32), axis=1), n_exp - 1)
    n_used = (pad_end[-1:] // MOE_BLOCK).astype(jnp.int32)
    gw_rows = jnp.pad(gw_kt.T, ((0, 0), (0, LANES - TOP_K)))

    xs = _sc_scatter_rows(h1, dest_kt, fill_idx, n_rows + n_exp * MOE_BLOCK)
    ys = _experts(xs, block_expert, n_used, w_gate_up, b_gate_up, w_down, b_down)
    y_slots = _sc_gather_rows(ys, dest_kt.reshape(-1))
    return _combine(alpha, y_slots, gw_rows, h1, ln2_g[None, :], ln2_b[None, :], 512)


def kernel(x, w_in, conv_w, conv_b, dt_bias, a_log, d_skip, ssm_norm_w, w_ssm_out, attn_sinks,
           w_attn_out, b_gates, w_mix_out, ln1_g, ln1_b, w_router, b_router, w_gate_up, b_gate_up,
           w_down, b_down, ln2_g, ln2_b):
    batch, seq, d = x.shape
    depth = w_in.shape[0]
    alpha = (2 * depth) ** 0.25
    h = x.reshape(batch * seq, d)
    for i in range(depth):
        h = _layer(h, batch, seq, alpha, w_in[i], conv_w[i], conv_b[i], dt_bias[i], a_log[i],
                   d_skip[i], ssm_norm_w[i], w_ssm_out[i], attn_sinks[i], w_attn_out[i], b_gates[i],
                   w_mix_out[i], ln1_g[i], ln1_b[i], w_router[i], b_router[i], w_gate_up[i],
                   b_gate_up[i], w_down[i], b_down[i], ln2_g[i], ln2_b[i])
    return h.reshape(batch, seq, d)
```

```python
import functools

import jax
import jax.numpy as jnp
from jax import lax
from jax.experimental import pallas as pl
from jax.experimental.pallas import tpu as pltpu
from jax.experimental.pallas import tpu_sc as plsc

SSM_HEAD_DIM = 64
SSM_GROUPS = 4
D_STATE = 128
CONV_WIDTH = 4
SSD_CHUNK = 128
KV_HEADS = 4
HEAD_DIM = 64
WINDOW = 128
TOP_K = 4
SWIGLU_LIMIT = 7.0
SWIGLU_ALPHA = 1.702
MOE_BLOCK = 512
LN_EPS = 1e-5
RMS_EPS = 1e-5

LANES = 128
SUBLANES = 8
NEG_BIG = -1e30
LOG2_E = 1.4426950408889634
F32 = jnp.float32
BF16 = jnp.bfloat16
VMEM_LIMIT = 56 * 1024 * 1024


def _cparams(sem):
    return pltpu.CompilerParams(dimension_semantics=sem, vmem_limit_bytes=VMEM_LIMIT)


def _sigmoid(x):
    return 0.5 + 0.5 * jnp.tanh(0.5 * x)


def _silu(x):
    h = 0.5 * x
    return h + h * jnp.tanh(h)


def _mm_kernel(x_ref, w_ref, o_ref):
    o_ref[...] = jnp.dot(x_ref[...].astype(BF16), w_ref[...],
                         preferred_element_type=F32).astype(o_ref.dtype)


def _matmul(x, w, out_dtype, tm, tn):
    m, k = x.shape
    n = w.shape[1]
    tm = min(tm, m)
    assert m % tm == 0 and n % tn == 0
    return pl.pallas_call(
        _mm_kernel,
        grid=(n // tn, m // tm),
        in_specs=[pl.BlockSpec((tm, k), lambda j, i: (i, 0)),
                  pl.BlockSpec((k, tn), lambda j, i: (0, j))],
        out_specs=pl.BlockSpec((tm, tn), lambda j, i: (i, j)),
        out_shape=jax.ShapeDtypeStruct((m, n), out_dtype),
        compiler_params=_cparams(("parallel", "parallel")),
        name="in_proj",
    )(x, w)


def _conv_silu(u_ref, prev_ref, w_ref, b_ref, ext_ref, first):
    n = u_ref.shape[0]
    prev = prev_ref[...]
    ext_ref[0:SUBLANES, :] = jnp.where(first, jnp.zeros_like(prev), prev)
    ext_ref[SUBLANES:SUBLANES + n, :] = u_ref[...]
    acc = b_ref[...] + w_ref[CONV_WIDTH - 1:CONV_WIDTH, :] * ext_ref[SUBLANES:SUBLANES + n, :]
    for j in range(CONV_WIDTH - 1):
        off = SUBLANES - (CONV_WIDTH - 1) + j
        acc = acc + w_ref[j:j + 1, :] * ext_ref[off:off + n, :]
    return _silu(acc)


def _pair_expand(v, h0, lo_mask):
    n = v.shape[0]
    a = jnp.broadcast_to(v[:, h0:h0 + 1], (n, LANES))
    b = jnp.broadcast_to(v[:, h0 + 1:h0 + 2], (n, LANES))
    return jnp.where(lo_mask, a, b)


def _ssd_kernel(z_ref, x_ref, b_ref, c_ref, dt_ref, xp_ref, bp_ref, cp_ref,
                cwx_ref, cwb_ref, cwc_ref, cbx_ref, cbb_ref, cbc_ref,
                dtb_ref, alog_ref, dsk_ref, nw_ref,
                o_ref, st_ref, extx_ref, extb_ref, extc_ref):
    c = pl.program_id(1)
    first = c == 0
    n = SSD_CHUNK
    heads_per_group = st_ref.shape[2] // SSM_HEAD_DIM
    gw = heads_per_group * SSM_HEAD_DIM

    @pl.when(first)
    def _():
        st_ref[...] = jnp.zeros_like(st_ref)

    xc = _conv_silu(x_ref, xp_ref, cwx_ref, cbx_ref, extx_ref, first)
    bc = _conv_silu(b_ref, bp_ref, cwb_ref, cbb_ref, extb_ref, first)
    cc = _conv_silu(c_ref, cp_ref, cwc_ref, cbc_ref, extc_ref, first)

    dt_in = dt_ref[...] + dtb_ref[...]
    dt = jnp.maximum(dt_in, 0.0) + jnp.log(1.0 + jnp.exp(-jnp.abs(dt_in)))
    a = dt * (-jnp.exp(alog_ref[...]))
    row = lax.broadcasted_iota(jnp.int32, (n, n), 0)
    col = lax.broadcasted_iota(jnp.int32, (n, n), 1)
    causal = row >= col
    tri = jnp.where(causal, 1.0, 0.0).astype(F32)
    acum = jnp.dot(tri, a, preferred_element_type=F32, precision=lax.Precision.HIGHEST)
    acum2 = acum * LOG2_E
    acum2_t = acum2.T
    a_last = acum[n - 1:n, :]
    d2e = jnp.exp(a_last - acum)
    eac = jnp.exp(acum)
    lo_mask = lax.broadcasted_iota(jnp.int32, (n, LANES), 1) < SSM_HEAD_DIM

    for g in range(SSM_GROUPS):
        xg = xc[:, g * gw:(g + 1) * gw]
        bg = bc[:, g * D_STATE:(g + 1) * D_STATE]
        cg = cc[:, g * D_STATE:(g + 1) * D_STATE]
        cg_b = cg.astype(BF16)
        bg_t = bg.T.astype(BF16)
        cb = jnp.dot(cg_b, bg_t, preferred_element_type=F32)
        st_prev = st_ref[g]
        y_off = jnp.dot(cg_b, st_prev.astype(BF16), preferred_element_type=F32)
        y_parts, xw_parts, ea_parts = [], [], []
        for j in range(heads_per_group // 2):
            h0 = g * heads_per_group + 2 * j
            lanes = slice(j * LANES, (j + 1) * LANES)
            xs_pair = xg[:, lanes] * _pair_expand(dt, h0, lo_mask)
            xs_b = xs_pair.astype(BF16)
            ea_pair = _pair_expand(eac, h0, lo_mask)
            xw_parts.append((xs_pair * _pair_expand(d2e, h0, lo_mask)).astype(BF16))
            ea_parts.append(ea_pair)
            ys = []
            for h in (h0, h0 + 1):
                seg2 = acum2[:, h:h + 1] - acum2_t[h:h + 1, :]
                decay = jnp.exp2(jnp.where(causal, seg2, NEG_BIG))
                m = (cb * decay).astype(BF16)
                ys.append(jnp.dot(m, xs_b, preferred_element_type=F32))
            y_parts.append(jnp.where(lo_mask, ys[0], ys[1]))
        y_diag = jnp.concatenate(y_parts, axis=1)
        ea = jnp.concatenate(ea_parts, axis=1)
        xw = jnp.concatenate(xw_parts, axis=1)
        cols = slice(g * gw, (g + 1) * gw)
        y = y_diag + y_off * ea + xg * dsk_ref[:, cols]
        zg = z_ref[:, cols]
        y = y * _silu(zg)
        ms = jnp.mean(y * y, axis=-1, keepdims=True)
        o_ref[:, cols] = (y * lax.rsqrt(ms + RMS_EPS) * nw_ref[:, cols]).astype(o_ref.dtype)
        st_ref[g] = st_prev * ea[n - 1:n, :] + jnp.dot(bg_t, xw, preferred_element_type=F32)


def _ssd(proj, batch, seq, col, conv_wx, conv_wb, conv_wc, conv_bx, conv_bb, conv_bc,
         dt_bias, a_log, d_skip, norm_w):
    t = proj.shape[0]
    d_inner = norm_w.shape[1]
    gn = SSM_GROUPS * D_STATE
    n = SSD_CHUNK
    nc = seq // n
    gw = d_inner // SSM_GROUPS

    def rows(b, c):
        return b * nc + c

    def prev_rows(b, c):
        return jnp.maximum((b * nc + c) * (n // SUBLANES) - 1, 0)

    def const(b, c):
        return (0, 0)

    in_specs = [
        pl.BlockSpec((n, d_inner), lambda b, c: (rows(b, c), col["z"] // d_inner)),
        pl.BlockSpec((n, d_inner), lambda b, c: (rows(b, c), col["x"] // d_inner)),
        pl.BlockSpec((n, gn), lambda b, c: (rows(b, c), col["B"] // gn)),
        pl.BlockSpec((n, gn), lambda b, c: (rows(b, c), col["C"] // gn)),
        pl.BlockSpec((n, LANES), lambda b, c: (rows(b, c), col["dt"] // LANES)),
        pl.BlockSpec((SUBLANES, d_inner), lambda b, c: (prev_rows(b, c), col["x"] // d_inner)),
        pl.BlockSpec((SUBLANES, gn), lambda b, c: (prev_rows(b, c), col["B"] // gn)),
        pl.BlockSpec((SUBLANES, gn), lambda b, c: (prev_rows(b, c), col["C"] // gn)),
        pl.BlockSpec((CONV_WIDTH, d_inner), const),
        pl.BlockSpec((CONV_WIDTH, gn), const),
        pl.BlockSpec((CONV_WIDTH, gn), const),
        pl.BlockSpec((1, d_inner), const),
        pl.BlockSpec((1, gn), const),
        pl.BlockSpec((1, gn), const),
        pl.BlockSpec((1, LANES), const),
        pl.BlockSpec((1, LANES), const),
        pl.BlockSpec((1, d_inner), const),
        pl.BlockSpec((1, d_inner), const),
    ]
    return pl.pallas_call(
        _ssd_kernel,
        grid=(batch, nc),
        in_specs=in_specs,
        out_specs=pl.BlockSpec((n, d_inner), lambda b, c: (rows(b, c), 0)),
        out_shape=jax.ShapeDtypeStruct((t, d_inner), BF16),
        scratch_shapes=[pltpu.VMEM((SSM_GROUPS, D_STATE, gw), F32),
                        pltpu.VMEM((SUBLANES + n, d_inner), F32),
                        pltpu.VMEM((SUBLANES + n, gn), F32),
                        pltpu.VMEM((SUBLANES + n, gn), F32)],
        compiler_params=_cparams(("arbitrary", "arbitrary")),
        name="ssd_mixer",
    )(proj, proj, proj, proj, proj, proj, proj, proj,
      conv_wx, conv_wb, conv_wc, conv_bx, conv_bb, conv_bc, dt_bias, a_log, d_skip, norm_w)


def _swa_kernel(sink_ref, q_ref, kc_ref, vc_ref, kp_ref, vp_ref, o_ref):
    i = pl.program_id(1)
    n = WINDOW
    n_heads = q_ref.shape[1] // HEAD_DIM
    grp = n_heads // KV_HEADS
    qpos = lax.broadcasted_iota(jnp.int32, (n, 2 * n), 0) + n
    kpos = lax.broadcasted_iota(jnp.int32, (n, 2 * n), 1)
    diff = qpos - kpos
    mask = (diff >= 0) & (diff < n) & ((kpos >= n) | (i > 0))
    lo_q = lax.broadcasted_iota(jnp.int32, (n, LANES), 1) < HEAD_DIM
    lo_k = lax.broadcasted_iota(jnp.int32, (2 * n, LANES), 1) < HEAD_DIM
    scale = HEAD_DIM ** -0.5
    for p in range(KV_HEADS // 2):
        lanes = slice(p * LANES, (p + 1) * LANES)
        kk = jnp.concatenate([kp_ref[:, lanes], kc_ref[:, lanes]], axis=0).astype(F32)
        vv = jnp.concatenate([vp_ref[:, lanes], vc_ref[:, lanes]], axis=0).astype(F32)
        kk_sw = pltpu.roll(kk, HEAD_DIM, axis=1)
        vv_sw = pltpu.roll(vv, HEAD_DIM, axis=1)
        for par in range(2):
            g = 2 * p + par
            k2 = (jnp.where(lo_k, kk, kk_sw) if par == 0 else jnp.where(lo_k, kk_sw, kk)).astype(BF16)
            v2 = (jnp.where(lo_k, vv, vv_sw) if par == 0 else jnp.where(lo_k, vv_sw, vv)).astype(BF16)
            for qp in range(grp // 2):
                h0 = g * grp + 2 * qp
                qlanes = slice((h0 // 2) * LANES, (h0 // 2 + 1) * LANES)
                q_pair = q_ref[:, qlanes]
                outs = []
                for hh in range(2):
                    keep = lo_q if hh == 0 else jnp.logical_not(lo_q)
                    qm = jnp.where(keep, q_pair, jnp.zeros_like(q_pair))
                    s = lax.dot_general(qm, k2, (((1,), (1,)), ((), ())),
                                        preferred_element_type=F32) * scale
                    s = jnp.where(mask, s, NEG_BIG)
                    sink = sink_ref[h0 + hh]
                    mx = jnp.maximum(jnp.max(s, axis=-1, keepdims=True), sink)
                    pr = jnp.exp(s - mx)
                    den = jnp.sum(pr, axis=-1, keepdims=True) + jnp.exp(sink - mx)
                    o = jnp.dot(pr.astype(BF16), v2, preferred_element_type=F32)
                    outs.append(o / den)
                o_ref[:, qlanes] = jnp.where(lo_q, outs[0], outs[1]).astype(o_ref.dtype)


def _swa(qkv, sinks, batch, seq, n_heads):
    t = qkv.shape[0]
    n = WINDOW
    nb = seq // n
    qw = n_heads * HEAD_DIM
    kw = KV_HEADS * HEAD_DIM
    k_blk = qw // kw
    v_blk = k_blk + 1

    def cur(b, i):
        return b * nb + i

    def prev(b, i):
        return b * nb + jnp.maximum(i - 1, 0)

    return pl.pallas_call(
        _swa_kernel,
        grid=(batch, nb),
        in_specs=[pl.BlockSpec(memory_space=pltpu.SMEM),
                  pl.BlockSpec((n, qw), lambda b, i: (cur(b, i), 0)),
                  pl.BlockSpec((n, kw), lambda b, i: (cur(b, i), k_blk)),
                  pl.BlockSpec((n, kw), lambda b, i: (cur(b, i), v_blk)),
                  pl.BlockSpec((n, kw), lambda b, i: (prev(b, i), k_blk)),
                  pl.BlockSpec((n, kw), lambda b, i: (prev(b, i), v_blk))],
        out_specs=pl.BlockSpec((n, qw), lambda b, i: (cur(b, i), 0)),
        out_shape=jax.ShapeDtypeStruct((t, qw), BF16),
        compiler_params=_cparams(("parallel", "parallel")),
        name="swa",
    )(sinks, qkv, qkv, qkv, qkv, qkv)


def _layer_norm(v, g, b):
    mu = jnp.mean(v, axis=-1, keepdims=True)
    d = v - mu
    var = jnp.mean(d * d, axis=-1, keepdims=True)
    return d * lax.rsqrt(var + LN_EPS) * g + b


def _merge_kernel(alpha, ys_ref, ya_ref, gs_ref, ga_ref, x_ref, wso_ref, wao_ref, wmix_ref,
                  bgs_ref, bga_ref, lg_ref, lb_ref, h_ref):
    y_ssm = jnp.dot(ys_ref[...], wso_ref[...], preferred_element_type=F32)
    y_att = jnp.dot(ya_ref[...], wao_ref[...], preferred_element_type=F32)
    merged = (_sigmoid(gs_ref[...] + bgs_ref[...]) * y_ssm
              + _sigmoid(ga_ref[...] + bga_ref[...]) * y_att)
    mix = jnp.dot(merged.astype(BF16), wmix_ref[...], preferred_element_type=F32)
    h_ref[...] = _layer_norm(alpha * x_ref[...] + mix, lg_ref[...], lb_ref[...])


def _merge(alpha, y_ssm, y_att, proj, col, x, wso, wao, wmix, bgs, bga, lg, lb, tm, row0, rows):
    d = x.shape[1]
    tm = min(tm, rows)
    assert rows % tm == 0 and row0 % tm == 0
    blk0 = row0 // tm
    di = y_ssm.shape[1]
    da = y_att.shape[1]

    def const(i):
        return (0, 0)

    return pl.pallas_call(
        functools.partial(_merge_kernel, alpha),
        grid=(rows // tm,),
        in_specs=[pl.BlockSpec((tm, di), lambda i: (i + blk0, 0)),
                  pl.BlockSpec((tm, da), lambda i: (i + blk0, 0)),
                  pl.BlockSpec((tm, d), lambda i: (i + blk0, col["gs"] // d)),
                  pl.BlockSpec((tm, d), lambda i: (i + blk0, col["ga"] // d)),
                  pl.BlockSpec((tm, d), lambda i: (i + blk0, 0)),
                  pl.BlockSpec((di, d), const),
                  pl.BlockSpec((da, d), const),
                  pl.BlockSpec((d, d), const),
                  pl.BlockSpec((1, d), const),
                  pl.BlockSpec((1, d), const),
                  pl.BlockSpec((1, d), const),
                  pl.BlockSpec((1, d), const)],
        out_specs=pl.BlockSpec((tm, d), lambda i: (i, 0)),
        out_shape=jax.ShapeDtypeStruct((rows, d), F32),
        compiler_params=_cparams(("parallel",)),
        name="merge_ln1",
    )(y_ssm, y_att, proj, proj, x, wso, wao, wmix, bgs, bga, lg, lb)


def _router_kernel(h_ref, wr_ref, br_ref, idx_ref, gw_ref, rank_ref, cnt_ref, tri_ref, base_ref):
    i = pl.program_id(0)
    tm = h_ref.shape[0]
    n_exp = wr_ref.shape[0]

    @pl.when(i == 0)
    def _():
        r = lax.broadcasted_iota(jnp.int32, (tm, tm), 0)
        c = lax.broadcasted_iota(jnp.int32, (tm, tm), 1)
        tri_ref[...] = jnp.where(r < c, 1.0, 0.0).astype(BF16)
        base_ref[...] = jnp.zeros_like(base_ref)

    logits = lax.dot_general(wr_ref[...], h_ref[...], (((1,), (1,)), ((), ())),
                             preferred_element_type=F32,
                             precision=lax.Precision.HIGHEST) + br_ref[...]
    eid = lax.broadcasted_iota(jnp.int32, (n_exp, tm), 0)
    work = logits
    vals, idxs = [], []
    for _ in range(TOP_K):
        mx = jnp.max(work, axis=0, keepdims=True)
        sel = jnp.min(jnp.where(work == mx, eid, n_exp), axis=0, keepdims=True)
        vals.append(mx)
        idxs.append(sel)
        work = jnp.where(eid == sel, -jnp.inf, work)
    exps = [jnp.exp(v - vals[0]) for v in vals]
    den = exps[0]
    for e in exps[1:]:
        den = den + e
    base = base_ref[...]
    for k in range(TOP_K):
        onehot = eid == idxs[k]
        oh = jnp.where(onehot, 1.0, 0.0)
        before = jnp.dot(oh.astype(BF16), tri_ref[...], preferred_element_type=F32)
        rank = jnp.sum(jnp.where(onehot, before + base, 0.0), axis=0, keepdims=True)
        idx_ref[k:k + 1, :] = idxs[k]
        gw_ref[k:k + 1, :] = exps[k] / den
        rank_ref[k:k + 1, :] = rank.astype(jnp.int32)
        base = base + jnp.sum(oh, axis=1, keepdims=True)
    base_ref[...] = base
    cnt_ref[...] = jnp.broadcast_to(base, cnt_ref.shape).astype(jnp.int32)


def _router(h, w_router_t, b_router, tm):
    t, d = h.shape
    n_exp = w_router_t.shape[0]
    tm = min(tm, t)
    outs = pl.pallas_call(
        _router_kernel,
        grid=(t // tm,),
        in_specs=[pl.BlockSpec((tm, d), lambda i: (i, 0)),
                  pl.BlockSpec((n_exp, d), lambda i: (0, 0)),
                  pl.BlockSpec((n_exp, 1), lambda i: (0, 0))],
        out_specs=[pl.BlockSpec((TOP_K, tm), lambda i: (0, i)),
                   pl.BlockSpec((TOP_K, tm), lambda i: (0, i)),
                   pl.BlockSpec((TOP_K, tm), lambda i: (0, i)),
                   pl.BlockSpec((n_exp, LANES), lambda i: (0, 0))],
        out_shape=[jax.ShapeDtypeStruct((TOP_K, t), jnp.int32),
                   jax.ShapeDtypeStruct((TOP_K, t), F32),
                   jax.ShapeDtypeStruct((TOP_K, t), jnp.int32),
                   jax.ShapeDtypeStruct((n_exp, LANES), jnp.int32)],
        scratch_shapes=[pltpu.VMEM((tm, tm), BF16), pltpu.VMEM((n_exp, 1), F32)],
        compiler_params=_cparams(("arbitrary",)),
        name="router",
    )(h, w_router_t, b_router)
    return outs


SC_CORES = 2
SC_SUBCORES = 16
SC_GATHER_CHUNK = 64
MOE_SPLITS = 2


def _sc_gather_rows(table, idx):
    b = idx.shape[0]
    d = table.shape[1]
    n_workers = SC_CORES * SC_SUBCORES
    chunk = SC_GATHER_CHUNK
    assert b % (n_workers * chunk) == 0
    per_worker = b // n_workers
    n_chunks = per_worker // chunk
    mesh = plsc.VectorSubcoreMesh(core_axis_name="c", subcore_axis_name="s",
                                  num_cores=SC_CORES, num_subcores=SC_SUBCORES)

    def body(table_hbm, idx_hbm, out_hbm, idx_v, rows_v, sem):
        wid = lax.axis_index("s") * SC_CORES + lax.axis_index("c")
        base = wid * per_worker

        @pl.loop(0, n_chunks)
        def _(c):
            off = pl.multiple_of(base + c * chunk, SUBLANES)
            pltpu.sync_copy(idx_hbm.at[pl.ds(off, chunk)], idx_v)
            pltpu.async_copy(table_hbm.at[idx_v], rows_v, sem).wait()
            pltpu.sync_copy(rows_v, out_hbm.at[pl.ds(off, chunk)])

    return pl.kernel(
        body,
        out_type=jax.ShapeDtypeStruct((b, d), table.dtype),
        mesh=mesh,
        scratch_types=[pltpu.VMEM((chunk,), jnp.int32),
                       pltpu.VMEM((chunk, d), table.dtype),
                       pltpu.SemaphoreType.DMA],
        name="sc_gather_rows",
    )(table, idx)


def _sc_scatter_rows(src, dest_kt, fill_idx, n_out_rows):
    t, d = src.shape
    n_k = dest_kt.shape[0]
    n_fill = fill_idx.shape[0]
    n_workers = SC_CORES * SC_SUBCORES
    chunk = SC_GATHER_CHUNK
    assert t % (n_workers * chunk) == 0 and n_fill % (n_workers * chunk) == 0
    tok_per_worker = t // n_workers
    fill_per_worker = n_fill // n_workers
    mesh = plsc.VectorSubcoreMesh(core_axis_name="c", subcore_axis_name="s",
                                  num_cores=SC_CORES, num_subcores=SC_SUBCORES)

    def body(src_hbm, dest_hbm, fill_hbm, zeros_hbm, out_hbm, idx_v, rows_v):
        wid = lax.axis_index("s") * SC_CORES + lax.axis_index("c")

        pltpu.sync_copy(zeros_hbm, rows_v)

        @pl.loop(0, fill_per_worker // chunk)
        def _(c):
            off = pl.multiple_of(wid * fill_per_worker + c * chunk, SUBLANES)
            pltpu.sync_copy(fill_hbm.at[pl.ds(off, chunk)], idx_v)
            pltpu.sync_copy(rows_v, out_hbm.at[idx_v])

        @pl.loop(0, tok_per_worker // chunk)
        def _(c):
            off = pl.multiple_of(wid * tok_per_worker + c * chunk, SUBLANES)
            pltpu.sync_copy(src_hbm.at[pl.ds(off, chunk)], rows_v)
            for k in range(n_k):
                pltpu.sync_copy(dest_hbm.at[pl.ds(k * t + off, chunk)], idx_v)
                pltpu.sync_copy(rows_v, out_hbm.at[idx_v])

    return pl.kernel(
        body,
        out_type=jax.ShapeDtypeStruct((n_out_rows, d), src.dtype),
        mesh=mesh,
        scratch_types=[pltpu.VMEM((chunk,), jnp.int32),
                       pltpu.VMEM((chunk, d), src.dtype)],
        name="sc_scatter_rows",
    )(src, dest_kt.reshape(-1), fill_idx, jnp.zeros((chunk, d), src.dtype))


def _expert_kernel(be_ref, nu_ref, x_ref, wgu_ref, bgu_ref, wd_ref, bd_ref, y_ref, wgu_b, wd_b):
    i = pl.program_id(0)
    n_used = nu_ref[0]
    changed = jnp.logical_or(i == 0, be_ref[i] != be_ref[jnp.maximum(i - 1, 0)])

    @pl.when(jnp.logical_and(changed, i < n_used))
    def _():
        wgu_b[...] = wgu_ref[0].astype(BF16)
        wd_b[...] = wd_ref[0].astype(BF16)

    @pl.when(i < n_used)
    def _():
        d_ff = wd_b.shape[0]
        xb = x_ref[...].astype(BF16)
        hgu = jnp.dot(xb, wgu_b[...], preferred_element_type=F32) + bgu_ref[0]
        glu = jnp.minimum(hgu[:, :d_ff], SWIGLU_LIMIT)
        lin = jnp.clip(hgu[:, d_ff:], -SWIGLU_LIMIT, SWIGLU_LIMIT)
        act = glu * _sigmoid(SWIGLU_ALPHA * glu) * (lin + 1.0)
        y_ref[...] = jnp.dot(act.astype(BF16), wd_b[...], preferred_element_type=F32) + bd_ref[0]

    @pl.when(i >= n_used)
    def _():
        y_ref[...] = jnp.zeros_like(y_ref)


def _experts(xs, block_expert, n_used, wgu, bgu, wd, bd):
    d = xs.shape[1]
    n_blocks = block_expert.shape[0]
    n_exp, _, two_f = wgu.shape
    d_ff = wd.shape[1]
    rows = MOE_BLOCK
    n_rows = n_blocks * rows
    grid_spec = pltpu.PrefetchScalarGridSpec(
        num_scalar_prefetch=2,
        grid=(n_blocks,),
        in_specs=[
            pl.BlockSpec((rows, d), lambda i, be, nu: (jnp.minimum(i, jnp.maximum(nu[0] - 1, 0)), 0)),
            pl.BlockSpec((1, d, two_f), lambda i, be, nu: (be[i], 0, 0)),
            pl.BlockSpec((1, 1, two_f), lambda i, be, nu: (be[i], 0, 0)),
            pl.BlockSpec((1, d_ff, d), lambda i, be, nu: (be[i], 0, 0)),
            pl.BlockSpec((1, 1, d), lambda i, be, nu: (be[i], 0, 0)),
        ],
        out_specs=pl.BlockSpec((rows, d), lambda i, be, nu: (i, 0)),
        scratch_shapes=[pltpu.VMEM((d, two_f), BF16), pltpu.VMEM((d_ff, d), BF16)],
    )
    return pl.pallas_call(
        _expert_kernel,
        grid_spec=grid_spec,
        out_shape=jax.ShapeDtypeStruct((n_rows, d), F32),
        compiler_params=_cparams(("arbitrary",)),
        name="experts",
    )(block_expert, n_used, xs, wgu, bgu.reshape(n_exp, 1, two_f), wd, bd.reshape(n_exp, 1, d))


def _combine_kernel(alpha, *refs):
    ys_refs = refs[:TOP_K]
    gw_ref, h_ref, lg_ref, lb_ref = refs[TOP_K:TOP_K + 4]
    o_ref = refs[-1]
    gw = gw_ref[...]
    acc = alpha * h_ref[...]
    for k in range(TOP_K):
        acc = acc + gw[:, k:k + 1] * ys_refs[k][...]
    o_ref[...] = _layer_norm(acc, lg_ref[...], lb_ref[...])


def _combine(alpha, y_slots, gw_rows, h, lg, lb, tm, out_prev, row0, total_rows):
    rows, d = h.shape
    tm = min(tm, rows)
    assert rows % tm == 0 and row0 % tm == 0
    nsteps = rows // tm
    blk0 = row0 // tm
    ys_specs = [pl.BlockSpec((tm, d), functools.partial(lambda i, k: (k * nsteps + i, 0), k=k))
                for k in range(TOP_K)]
    in_specs = ys_specs + [pl.BlockSpec((tm, LANES), lambda i: (i, 0)),
                           pl.BlockSpec((tm, d), lambda i: (i, 0)),
                           pl.BlockSpec((1, d), lambda i: (0, 0)),
                           pl.BlockSpec((1, d), lambda i: (0, 0))]
    args = [y_slots] * TOP_K + [gw_rows, h, lg, lb]
    aliases = {}
    if out_prev is not None:
        in_specs.append(pl.BlockSpec(memory_space=pl.ANY))
        aliases = {len(args): 0}
        args.append(out_prev)
    return pl.pallas_call(
        functools.partial(_combine_kernel, alpha),
        grid=(nsteps,),
        in_specs=in_specs,
        out_specs=pl.BlockSpec((tm, d), lambda i: (i + blk0, 0)),
        out_shape=jax.ShapeDtypeStruct((total_rows, d), F32),
        input_output_aliases=aliases,
        compiler_params=_cparams(("parallel",)),
        name="combine_ln2",
    )(*args)


def _pack_in_proj(w_in, d_inner, n_ssm_heads, attn_dim):
    gn = SSM_GROUPS * D_STATE
    kv_dim = KV_HEADS * HEAD_DIM
    d = w_in.shape[0]
    o = 0
    seg = {}
    for name, width in (("z", d_inner), ("x", d_inner), ("B", gn), ("C", gn), ("dt", n_ssm_heads),
                        ("q", attn_dim), ("k", kv_dim), ("v", kv_dim), ("gs", d), ("ga", d)):
        seg[name] = w_in[:, o:o + width]
        o += width
    assert o == w_in.shape[1]
    dt_pad = jnp.pad(seg["dt"], ((0, 0), (0, LANES - n_ssm_heads)))
    order = (("z", seg["z"]), ("x", seg["x"]), ("gs", seg["gs"]), ("ga", seg["ga"]),
             ("B", seg["B"]), ("C", seg["C"]), ("dt", dt_pad))
    col, off = {}, 0
    for name, w in order:
        col[name] = off
        assert off % w.shape[1] == 0
        off += w.shape[1]
    w_f32grp = jnp.concatenate([w for _, w in order], axis=1).astype(BF16)
    w_qkv = jnp.concatenate([seg["q"], seg["k"], seg["v"]], axis=1).astype(BF16)
    return w_f32grp, w_qkv, col


def _largest_tile(n, cap):
    best = LANES
    for k in range(1, n // LANES + 1):
        if n % (k * LANES) == 0 and k * LANES <= cap:
            best = k * LANES
    return best


def _layer(h_in, batch, seq, alpha, w_in, conv_w, conv_b, dt_bias, a_log, d_skip, ssm_norm_w,
           w_ssm_out, attn_sinks, w_attn_out, b_gates, w_mix_out, ln1_g, ln1_b, w_router, b_router,
           w_gate_up, b_gate_up, w_down, b_down, ln2_g, ln2_b):
    t, d = h_in.shape
    d_inner = ssm_norm_w.shape[0]
    n_ssm_heads = dt_bias.shape[0]
    attn_dim = w_attn_out.shape[0]
    n_heads = attn_sinks.shape[0]
    n_exp = w_router.shape[1]
    gn = SSM_GROUPS * D_STATE

    w_f32grp, w_qkv, col = _pack_in_proj(w_in, d_inner, n_ssm_heads, attn_dim)
    proj = _matmul(h_in, w_f32grp, F32, 1024, _largest_tile(w_f32grp.shape[1], 2560))
    qkv = _matmul(h_in, w_qkv, BF16, 1024, _largest_tile(w_qkv.shape[1], 1536))

    pad_h = (0, LANES - n_ssm_heads)
    y_ssm = _ssd(proj, batch, seq, col,
                 conv_w[:, :d_inner], conv_w[:, d_inner:d_inner + gn], conv_w[:, d_inner + gn:],
                 conv_b[None, :d_inner], conv_b[None, d_inner:d_inner + gn], conv_b[None, d_inner + gn:],
                 jnp.pad(dt_bias, pad_h)[None, :], jnp.pad(a_log, pad_h)[None, :],
                 jnp.repeat(d_skip, SSM_HEAD_DIM)[None, :], ssm_norm_w[None, :])
    y_att = _swa(qkv, attn_sinks, batch, seq, n_heads)

    wso, wao, wmix = w_ssm_out.astype(BF16), w_attn_out.astype(BF16), w_mix_out.astype(BF16)
    w_router_t = w_router.T

    n_parts = MOE_SPLITS if t % (MOE_SPLITS * SC_CORES * SC_SUBCORES * SC_GATHER_CHUNK) == 0 else 1
    rows = t // n_parts
    out = None
    for part in range(n_parts):
        row0 = part * rows
        h1 = _merge(alpha, y_ssm, y_att, proj, col, h_in, wso, wao, wmix,
                    b_gates[None, :d], b_gates[None, d:], ln1_g[None, :], ln1_b[None, :], 512, row0, rows)
        idx_kt, gw_kt, rank_kt, counts = _router(h1, w_router_t, b_router[:, None], 1024)

        counts = counts[:, 0]
        padded = (counts + MOE_BLOCK - 1) // MOE_BLOCK * MOE_BLOCK
        pad_end = jnp.cumsum(padded)
        pad_start = pad_end - padded
        n_blocks = -(-(rows * TOP_K) // MOE_BLOCK) + n_exp
        n_rows = n_blocks * MOE_BLOCK
        expert_ids = jnp.arange(n_exp, dtype=jnp.int32)
        pad_start_of_slot = jnp.sum(
            jnp.where(idx_kt[None] == expert_ids[:, None, None], pad_start[:, None, None], 0), axis=0)
        dest_kt = pad_start_of_slot + rank_kt
        fill_rows = (pad_start + counts)[:, None] + jnp.arange(MOE_BLOCK, dtype=jnp.int32)[None, :]
        spare_rows = n_rows + jnp.arange(n_exp * MOE_BLOCK, dtype=jnp.int32).reshape(n_exp, MOE_BLOCK)
        fill_idx = jnp.where(fill_rows < pad_end[:, None], fill_rows, spare_rows).reshape(-1)
        block_row0 = jnp.arange(n_blocks, dtype=jnp.int32) * MOE_BLOCK
        block_expert = jnp.minimum(
            jnp.sum((pad_end[None, :] <= block_row0[:, None]).astype(jnp.int32), axis=1), n_exp - 1)
        n_used = (pad_end[-1:] // MOE_BLOCK).astype(jnp.int32)
        gw_rows = jnp.pad(gw_kt.T, ((0, 0), (0, LANES - TOP_K)))

        xs = _sc_scatter_rows(h1, dest_kt, fill_idx, n_rows + n_exp * MOE_BLOCK)
        ys = _experts(xs, block_expert, n_used, w_gate_up, b_gate_up, w_down, b_down)
        y_slots = _sc_gather_rows(ys, dest_kt.reshape(-1))
        out = _combine(alpha, y_slots, gw_rows, h1, ln2_g[None, :], ln2_b[None, :], 512, out, row0, t)
    return out


def kernel(x, w_in, conv_w, conv_b, dt_bias, a_log, d_skip, ssm_norm_w, w_ssm_out, attn_sinks,
           w_attn_out, b_gates, w_mix_out, ln1_g, ln1_b, w_router, b_router, w_gate_up, b_gate_up,
           w_down, b_down, ln2_g, ln2_b):
    batch, seq, d = x.shape
    depth = w_in.shape[0]
    alpha = (2 * depth) ** 0.25
    h = x.reshape(batch * seq, d)
    for i in range(depth):
        h = _layer(h, batch, seq, alpha, w_in[i], conv_w[i], conv_b[i], dt_bias[i], a_log[i],
                   d_skip[i], ssm_norm_w[i], w_ssm_out[i], attn_sinks[i], w_attn_out[i], b_gates[i],
                   w_mix_out[i], ln1_g[i], ln1_b[i], w_router[i], b_router[i], w_gate_up[i],
                   b_gate_up[i], w_down[i], b_down[i], ln2_g[i], ln2_b[i])
    return h.reshape(batch, seq, d)
```

```python
import functools

import jax
import jax.numpy as jnp
from jax import lax
from jax.experimental import pallas as pl
from jax.experimental.pallas import tpu as pltpu
from jax.experimental.pallas import tpu_sc as plsc

SSM_HEAD_DIM = 64
SSM_GROUPS = 4
D_STATE = 128
CONV_WIDTH = 4
SSD_CHUNK = 128
KV_HEADS = 4
HEAD_DIM = 64
WINDOW = 128
TOP_K = 4
SWIGLU_LIMIT = 7.0
SWIGLU_ALPHA = 1.702
MOE_BLOCK = 512
LN_EPS = 1e-5
RMS_EPS = 1e-5

LANES = 128
SUBLANES = 8
NEG_BIG = -1e30
LOG2_E = 1.4426950408889634
F32 = jnp.float32
BF16 = jnp.bfloat16
VMEM_LIMIT = 56 * 1024 * 1024


def _cparams(sem):
    return pltpu.CompilerParams(dimension_semantics=sem, vmem_limit_bytes=VMEM_LIMIT)


def _sigmoid(x):
    return 0.5 + 0.5 * jnp.tanh(0.5 * x)


def _silu(x):
    h = 0.5 * x
    return h + h * jnp.tanh(h)


def _mm_kernel(x_ref, w_ref, o_ref):
    o_ref[...] = jnp.dot(x_ref[...].astype(BF16), w_ref[...],
                         preferred_element_type=F32).astype(o_ref.dtype)


def _matmul(x, w, out_dtype, tm, tn):
    m, k = x.shape
    n = w.shape[1]
    tm = min(tm, m)
    assert m % tm == 0 and n % tn == 0
    return pl.pallas_call(
        _mm_kernel,
        grid=(n // tn, m // tm),
        in_specs=[pl.BlockSpec((tm, k), lambda j, i: (i, 0)),
                  pl.BlockSpec((k, tn), lambda j, i: (0, j))],
        out_specs=pl.BlockSpec((tm, tn), lambda j, i: (i, j)),
        out_shape=jax.ShapeDtypeStruct((m, n), out_dtype),
        compiler_params=_cparams(("parallel", "parallel")),
        name="in_proj",
    )(x, w)


def _conv_silu(u_ref, prev_ref, w_ref, b_ref, ext_ref, first):
    n = u_ref.shape[0]
    prev = prev_ref[...]
    ext_ref[0:SUBLANES, :] = jnp.where(first, jnp.zeros_like(prev), prev)
    ext_ref[SUBLANES:SUBLANES + n, :] = u_ref[...]
    acc = b_ref[...] + w_ref[CONV_WIDTH - 1:CONV_WIDTH, :] * ext_ref[SUBLANES:SUBLANES + n, :]
    for j in range(CONV_WIDTH - 1):
        off = SUBLANES - (CONV_WIDTH - 1) + j
        acc = acc + w_ref[j:j + 1, :] * ext_ref[off:off + n, :]
    return _silu(acc)


def _pair_expand(v, h0, lo_mask):
    n = v.shape[0]
    a = jnp.broadcast_to(v[:, h0:h0 + 1], (n, LANES))
    b = jnp.broadcast_to(v[:, h0 + 1:h0 + 2], (n, LANES))
    return jnp.where(lo_mask, a, b)


def _ssd_kernel(z_ref, x_ref, b_ref, c_ref, dt_ref, xp_ref, bp_ref, cp_ref,
                cwx_ref, cwb_ref, cwc_ref, cbx_ref, cbb_ref, cbc_ref,
                dtb_ref, alog_ref, dsk_ref, nw_ref,
                o_ref, st_ref, extx_ref, extb_ref, extc_ref):
    c = pl.program_id(1)
    first = c == 0
    n = SSD_CHUNK
    heads_per_group = st_ref.shape[2] // SSM_HEAD_DIM
    gw = heads_per_group * SSM_HEAD_DIM

    @pl.when(first)
    def _():
        st_ref[...] = jnp.zeros_like(st_ref)

    xc = _conv_silu(x_ref, xp_ref, cwx_ref, cbx_ref, extx_ref, first)
    bc = _conv_silu(b_ref, bp_ref, cwb_ref, cbb_ref, extb_ref, first)
    cc = _conv_silu(c_ref, cp_ref, cwc_ref, cbc_ref, extc_ref, first)

    dt_in = dt_ref[...] + dtb_ref[...]
    dt = jnp.maximum(dt_in, 0.0) + jnp.log(1.0 + jnp.exp(-jnp.abs(dt_in)))
    a = dt * (-jnp.exp(alog_ref[...]))
    row = lax.broadcasted_iota(jnp.int32, (n, n), 0)
    col = lax.broadcasted_iota(jnp.int32, (n, n), 1)
    causal = row >= col
    tri = jnp.where(causal, 1.0, 0.0).astype(F32)
    acum = jnp.dot(tri, a, preferred_element_type=F32, precision=lax.Precision.HIGHEST)
    acum2 = acum * LOG2_E
    acum2_t = acum2.T
    a_last = acum[n - 1:n, :]
    d2e = jnp.exp(a_last - acum)
    eac = jnp.exp(acum)
    lo_mask = lax.broadcasted_iota(jnp.int32, (n, LANES), 1) < SSM_HEAD_DIM

    for g in range(SSM_GROUPS):
        xg = xc[:, g * gw:(g + 1) * gw]
        bg = bc[:, g * D_STATE:(g + 1) * D_STATE]
        cg = cc[:, g * D_STATE:(g + 1) * D_STATE]
        cg_b = cg.astype(BF16)
        bg_t = bg.T.astype(BF16)
        cb = jnp.dot(cg_b, bg_t, preferred_element_type=F32)
        st_prev = st_ref[g]
        y_off = jnp.dot(cg_b, st_prev.astype(BF16), preferred_element_type=F32)
        y_parts, xw_parts, ea_parts = [], [], []
        for j in range(heads_per_group // 2):
            h0 = g * heads_per_group + 2 * j
            lanes = slice(j * LANES, (j + 1) * LANES)
            xs_pair = xg[:, lanes] * _pair_expand(dt, h0, lo_mask)
            xs_b = xs_pair.astype(BF16)
            ea_pair = _pair_expand(eac, h0, lo_mask)
            xw_parts.append((xs_pair * _pair_expand(d2e, h0, lo_mask)).astype(BF16))
            ea_parts.append(ea_pair)
            ys = []
            for h in (h0, h0 + 1):
                seg2 = acum2[:, h:h + 1] - acum2_t[h:h + 1, :]
                decay = jnp.exp2(jnp.where(causal, seg2, NEG_BIG))
                m = (cb * decay).astype(BF16)
                ys.append(jnp.dot(m, xs_b, preferred_element_type=F32))
            y_parts.append(jnp.where(lo_mask, ys[0], ys[1]))
        y_diag = jnp.concatenate(y_parts, axis=1)
        ea = jnp.concatenate(ea_parts, axis=1)
        xw = jnp.concatenate(xw_parts, axis=1)
        cols = slice(g * gw, (g + 1) * gw)
        y = y_diag + y_off * ea + xg * dsk_ref[:, cols]
        zg = z_ref[:, cols]
        y = y * _silu(zg)
        ms = jnp.mean(y * y, axis=-1, keepdims=True)
        o_ref[:, cols] = (y * lax.rsqrt(ms + RMS_EPS) * nw_ref[:, cols]).astype(o_ref.dtype)
        st_ref[g] = st_prev * ea[n - 1:n, :] + jnp.dot(bg_t, xw, preferred_element_type=F32)


def _ssd(proj, batch, seq, col, conv_wx, conv_wb, conv_wc, conv_bx, conv_bb, conv_bc,
         dt_bias, a_log, d_skip, norm_w):
    t = proj.shape[0]
    d_inner = norm_w.shape[1]
    gn = SSM_GROUPS * D_STATE
    n = SSD_CHUNK
    nc = seq // n
    gw = d_inner // SSM_GROUPS

    def rows(b, c):
        return b * nc + c

    def prev_rows(b, c):
        return jnp.maximum((b * nc + c) * (n // SUBLANES) - 1, 0)

    def const(b, c):
        return (0, 0)

    in_specs = [
        pl.BlockSpec((n, d_inner), lambda b, c: (rows(b, c), col["z"] // d_inner)),
        pl.BlockSpec((n, d_inner), lambda b, c: (rows(b, c), col["x"] // d_inner)),
        pl.BlockSpec((n, gn), lambda b, c: (rows(b, c), col["B"] // gn)),
        pl.BlockSpec((n, gn), lambda b, c: (rows(b, c), col["C"] // gn)),
        pl.BlockSpec((n, LANES), lambda b, c: (rows(b, c), col["dt"] // LANES)),
        pl.BlockSpec((SUBLANES, d_inner), lambda b, c: (prev_rows(b, c), col["x"] // d_inner)),
        pl.BlockSpec((SUBLANES, gn), lambda b, c: (prev_rows(b, c), col["B"] // gn)),
        pl.BlockSpec((SUBLANES, gn), lambda b, c: (prev_rows(b, c), col["C"] // gn)),
        pl.BlockSpec((CONV_WIDTH, d_inner), const),
        pl.BlockSpec((CONV_WIDTH, gn), const),
        pl.BlockSpec((CONV_WIDTH, gn), const),
        pl.BlockSpec((1, d_inner), const),
        pl.BlockSpec((1, gn), const),
        pl.BlockSpec((1, gn), const),
        pl.BlockSpec((1, LANES), const),
        pl.BlockSpec((1, LANES), const),
        pl.BlockSpec((1, d_inner), const),
        pl.BlockSpec((1, d_inner), const),
    ]
    return pl.pallas_call(
        _ssd_kernel,
        grid=(batch, nc),
        in_specs=in_specs,
        out_specs=pl.BlockSpec((n, d_inner), lambda b, c: (rows(b, c), 0)),
        out_shape=jax.ShapeDtypeStruct((t, d_inner), BF16),
        scratch_shapes=[pltpu.VMEM((SSM_GROUPS, D_STATE, gw), F32),
                        pltpu.VMEM((SUBLANES + n, d_inner), F32),
                        pltpu.VMEM((SUBLANES + n, gn), F32),
                        pltpu.VMEM((SUBLANES + n, gn), F32)],
        compiler_params=_cparams(("arbitrary", "arbitrary")),
        name="ssd_mixer",
    )(proj, proj, proj, proj, proj, proj, proj, proj,
      conv_wx, conv_wb, conv_wc, conv_bx, conv_bb, conv_bc, dt_bias, a_log, d_skip, norm_w)


def _swa_kernel(sink_ref, q_ref, kc_ref, vc_ref, kp_ref, vp_ref, o_ref):
    i = pl.program_id(1)
    n = WINDOW
    n_heads = q_ref.shape[1] // HEAD_DIM
    grp = n_heads // KV_HEADS
    qpos = lax.broadcasted_iota(jnp.int32, (n, 2 * n), 0) + n
    kpos = lax.broadcasted_iota(jnp.int32, (n, 2 * n), 1)
    diff = qpos - kpos
    mask = (diff >= 0) & (diff < n) & ((kpos >= n) | (i > 0))
    lo_q = lax.broadcasted_iota(jnp.int32, (n, LANES), 1) < HEAD_DIM
    lo_k = lax.broadcasted_iota(jnp.int32, (2 * n, LANES), 1) < HEAD_DIM
    scale = HEAD_DIM ** -0.5
    for p in range(KV_HEADS // 2):
        lanes = slice(p * LANES, (p + 1) * LANES)
        kk = jnp.concatenate([kp_ref[:, lanes], kc_ref[:, lanes]], axis=0).astype(F32)
        vv = jnp.concatenate([vp_ref[:, lanes], vc_ref[:, lanes]], axis=0).astype(F32)
        kk_sw = pltpu.roll(kk, HEAD_DIM, axis=1)
        vv_sw = pltpu.roll(vv, HEAD_DIM, axis=1)
        for par in range(2):
            g = 2 * p + par
            k2 = (jnp.where(lo_k, kk, kk_sw) if par == 0 else jnp.where(lo_k, kk_sw, kk)).astype(BF16)
            v2 = (jnp.where(lo_k, vv, vv_sw) if par == 0 else jnp.where(lo_k, vv_sw, vv)).astype(BF16)
            for qp in range(grp // 2):
                h0 = g * grp + 2 * qp
                qlanes = slice((h0 // 2) * LANES, (h0 // 2 + 1) * LANES)
                q_pair = q_ref[:, qlanes]
                outs = []
                for hh in range(2):
                    keep = lo_q if hh == 0 else jnp.logical_not(lo_q)
                    qm = jnp.where(keep, q_pair, jnp.zeros_like(q_pair))
                    s = lax.dot_general(qm, k2, (((1,), (1,)), ((), ())),
                                        preferred_element_type=F32) * scale
                    s = jnp.where(mask, s, NEG_BIG)
                    sink = sink_ref[h0 + hh]
                    mx = jnp.maximum(jnp.max(s, axis=-1, keepdims=True), sink)
                    pr = jnp.exp(s - mx)
                    den = jnp.sum(pr, axis=-1, keepdims=True) + jnp.exp(sink - mx)
                    o = jnp.dot(pr.astype(BF16), v2, preferred_element_type=F32)
                    outs.append(o / den)
                o_ref[:, qlanes] = jnp.where(lo_q, outs[0], outs[1]).astype(o_ref.dtype)


def _swa(qkv, sinks, batch, seq, n_heads):
    t = qkv.shape[0]
    n = WINDOW
    nb = seq // n
    qw = n_heads * HEAD_DIM
    kw = KV_HEADS * HEAD_DIM
    k_blk = qw // kw
    v_blk = k_blk + 1

    def cur(b, i):
        return b * nb + i

    def prev(b, i):
        return b * nb + jnp.maximum(i - 1, 0)

    return pl.pallas_call(
        _swa_kernel,
        grid=(batch, nb),
        in_specs=[pl.BlockSpec(memory_space=pltpu.SMEM),
                  pl.BlockSpec((n, qw), lambda b, i: (cur(b, i), 0)),
                  pl.BlockSpec((n, kw), lambda b, i: (cur(b, i), k_blk)),
                  pl.BlockSpec((n, kw), lambda b, i: (cur(b, i), v_blk)),
                  pl.BlockSpec((n, kw), lambda b, i: (prev(b, i), k_blk)),
                  pl.BlockSpec((n, kw), lambda b, i: (prev(b, i), v_blk))],
        out_specs=pl.BlockSpec((n, qw), lambda b, i: (cur(b, i), 0)),
        out_shape=jax.ShapeDtypeStruct((t, qw), BF16),
        compiler_params=_cparams(("parallel", "parallel")),
        name="swa",
    )(sinks, qkv, qkv, qkv, qkv, qkv)


def _layer_norm(v, g, b):
    mu = jnp.mean(v, axis=-1, keepdims=True)
    d = v - mu
    var = jnp.mean(d * d, axis=-1, keepdims=True)
    return d * lax.rsqrt(var + LN_EPS) * g + b


def _merge_kernel(alpha, ys_ref, ya_ref, gs_ref, ga_ref, x_ref, wso_ref, wao_ref, wmix_ref,
                  bgs_ref, bga_ref, lg_ref, lb_ref, h_ref):
    y_ssm = jnp.dot(ys_ref[...], wso_ref[...], preferred_element_type=F32)
    y_att = jnp.dot(ya_ref[...], wao_ref[...], preferred_element_type=F32)
    merged = (_sigmoid(gs_ref[...] + bgs_ref[...]) * y_ssm
              + _sigmoid(ga_ref[...] + bga_ref[...]) * y_att)
    mix = jnp.dot(merged.astype(BF16), wmix_ref[...], preferred_element_type=F32)
    h_ref[...] = _layer_norm(alpha * x_ref[...] + mix, lg_ref[...], lb_ref[...])


def _merge(alpha, y_ssm, y_att, proj, col, x, wso, wao, wmix, bgs, bga, lg, lb, tm, row0, rows):
    d = x.shape[1]
    tm = min(tm, rows)
    assert rows % tm == 0 and row0 % tm == 0
    blk0 = row0 // tm
    di = y_ssm.shape[1]
    da = y_att.shape[1]

    def const(i):
        return (0, 0)

    return pl.pallas_call(
        functools.partial(_merge_kernel, alpha),
        grid=(rows // tm,),
        in_specs=[pl.BlockSpec((tm, di), lambda i: (i + blk0, 0)),
                  pl.BlockSpec((tm, da), lambda i: (i + blk0, 0)),
                  pl.BlockSpec((tm, d), lambda i: (i + blk0, col["gs"] // d)),
                  pl.BlockSpec((tm, d), lambda i: (i + blk0, col["ga"] // d)),
                  pl.BlockSpec((tm, d), lambda i: (i + blk0, 0)),
                  pl.BlockSpec((di, d), const),
                  pl.BlockSpec((da, d), const),
                  pl.BlockSpec((d, d), const),
                  pl.BlockSpec((1, d), const),
                  pl.BlockSpec((1, d), const),
                  pl.BlockSpec((1, d), const),
                  pl.BlockSpec((1, d), const)],
        out_specs=pl.BlockSpec((tm, d), lambda i: (i, 0)),
        out_shape=jax.ShapeDtypeStruct((rows, d), F32),
        compiler_params=_cparams(("parallel",)),
        name="merge_ln1",
    )(y_ssm, y_att, proj, proj, x, wso, wao, wmix, bgs, bga, lg, lb)


def _router_kernel(h_ref, wr_ref, br_ref, idx_ref, gw_ref, rank_ref, cnt_ref, tri_ref, base_ref):
    i = pl.program_id(0)
    tm = h_ref.shape[0]
    n_exp = wr_ref.shape[0]

    @pl.when(i == 0)
    def _():
        r = lax.broadcasted_iota(jnp.int32, (tm, tm), 0)
        c = lax.broadcasted_iota(jnp.int32, (tm, tm), 1)
        tri_ref[...] = jnp.where(r < c, 1.0, 0.0).astype(BF16)
        base_ref[...] = jnp.zeros_like(base_ref)

    logits = lax.dot_general(wr_ref[...], h_ref[...], (((1,), (1,)), ((), ())),
                             preferred_element_type=F32,
                             precision=lax.Precision.HIGHEST) + br_ref[...]
    eid = lax.broadcasted_iota(jnp.int32, (n_exp, tm), 0)
    work = logits
    vals, idxs = [], []
    for _ in range(TOP_K):
        mx = jnp.max(work, axis=0, keepdims=True)
        sel = jnp.min(jnp.where(work == mx, eid, n_exp), axis=0, keepdims=True)
        vals.append(mx)
        idxs.append(sel)
        work = jnp.where(eid == sel, -jnp.inf, work)
    exps = [jnp.exp(v - vals[0]) for v in vals]
    den = exps[0]
    for e in exps[1:]:
        den = den + e
    base = base_ref[...]
    for k in range(TOP_K):
        onehot = eid == idxs[k]
        oh = jnp.where(onehot, 1.0, 0.0)
        before = jnp.dot(oh.astype(BF16), tri_ref[...], preferred_element_type=F32)
        rank = jnp.sum(jnp.where(onehot, before + base, 0.0), axis=0, keepdims=True)
        idx_ref[k:k + 1, :] = idxs[k]
        gw_ref[k:k + 1, :] = exps[k] / den
        rank_ref[k:k + 1, :] = rank.astype(jnp.int32)
        base = base + jnp.sum(oh, axis=1, keepdims=True)
    base_ref[...] = base
    cnt_ref[...] = jnp.broadcast_to(base, cnt_ref.shape).astype(jnp.int32)


def _router(h, w_router_t, b_router, tm):
    t, d = h.shape
    n_exp = w_router_t.shape[0]
    tm = min(tm, t)
    outs = pl.pallas_call(
        _router_kernel,
        grid=(t // tm,),
        in_specs=[pl.BlockSpec((tm, d), lambda i: (i, 0)),
                  pl.BlockSpec((n_exp, d), lambda i: (0, 0)),
                  pl.BlockSpec((n_exp, 1), lambda i: (0, 0))],
        out_specs=[pl.BlockSpec((TOP_K, tm), lambda i: (0, i)),
                   pl.BlockSpec((TOP_K, tm), lambda i: (0, i)),
                   pl.BlockSpec((TOP_K, tm), lambda i: (0, i)),
                   pl.BlockSpec((n_exp, LANES), lambda i: (0, 0))],
        out_shape=[jax.ShapeDtypeStruct((TOP_K, t), jnp.int32),
                   jax.ShapeDtypeStruct((TOP_K, t), F32),
                   jax.ShapeDtypeStruct((TOP_K, t), jnp.int32),
                   jax.ShapeDtypeStruct((n_exp, LANES), jnp.int32)],
        scratch_shapes=[pltpu.VMEM((tm, tm), BF16), pltpu.VMEM((n_exp, 1), F32)],
        compiler_params=_cparams(("arbitrary",)),
        name="router",
    )(h, w_router_t, b_router)
    return outs


SC_CORES = 2
SC_SUBCORES = 16
SC_GATHER_CHUNK = 64
MOE_SPLITS = 2


def _sc_gather_rows(table, idx):
    b = idx.shape[0]
    d = table.shape[1]
    n_workers = SC_CORES * SC_SUBCORES
    chunk = SC_GATHER_CHUNK
    assert b % (n_workers * chunk) == 0
    per_worker = b // n_workers
    n_chunks = per_worker // chunk
    mesh = plsc.VectorSubcoreMesh(core_axis_name="c", subcore_axis_name="s",
                                  num_cores=SC_CORES, num_subcores=SC_SUBCORES)

    def body(table_hbm, idx_hbm, out_hbm, idx_v, rows_v, sem):
        wid = lax.axis_index("s") * SC_CORES + lax.axis_index("c")
        base = wid * per_worker

        @pl.loop(0, n_chunks)
        def _(c):
            off = pl.multiple_of(base + c * chunk, SUBLANES)
            pltpu.sync_copy(idx_hbm.at[pl.ds(off, chunk)], idx_v)
            pltpu.async_copy(table_hbm.at[idx_v], rows_v, sem).wait()
            pltpu.sync_copy(rows_v, out_hbm.at[pl.ds(off, chunk)])

    return pl.kernel(
        body,
        out_type=jax.ShapeDtypeStruct((b, d), table.dtype),
        mesh=mesh,
        scratch_types=[pltpu.VMEM((chunk,), jnp.int32),
                       pltpu.VMEM((chunk, d), table.dtype),
                       pltpu.SemaphoreType.DMA],
        name="sc_gather_rows",
    )(table, idx)


def _sc_scatter_rows(src, dest_kt, fill_idx, n_out_rows):
    t, d = src.shape
    n_k = dest_kt.shape[0]
    n_fill = fill_idx.shape[0]
    n_workers = SC_CORES * SC_SUBCORES
    chunk = SC_GATHER_CHUNK
    assert t % (n_workers * chunk) == 0 and n_fill % (n_workers * chunk) == 0
    tok_per_worker = t // n_workers
    fill_per_worker = n_fill // n_workers
    mesh = plsc.VectorSubcoreMesh(core_axis_name="c", subcore_axis_name="s",
                                  num_cores=SC_CORES, num_subcores=SC_SUBCORES)

    def body(src_hbm, dest_hbm, fill_hbm, zeros_hbm, out_hbm, idx_v, rows_v):
        wid = lax.axis_index("s") * SC_CORES + lax.axis_index("c")

        pltpu.sync_copy(zeros_hbm, rows_v)

        @pl.loop(0, fill_per_worker // chunk)
        def _(c):
            off = pl.multiple_of(wid * fill_per_worker + c * chunk, SUBLANES)
            pltpu.sync_copy(fill_hbm.at[pl.ds(off, chunk)], idx_v)
            pltpu.sync_copy(rows_v, out_hbm.at[idx_v])

        @pl.loop(0, tok_per_worker // chunk)
        def _(c):
            off = pl.multiple_of(wid * tok_per_worker + c * chunk, SUBLANES)
            pltpu.sync_copy(src_hbm.at[pl.ds(off, chunk)], rows_v)
            for k in range(n_k):
                pltpu.sync_copy(dest_hbm.at[pl.ds(k * t + off, chunk)], idx_v)
                pltpu.sync_copy(rows_v, out_hbm.at[idx_v])

    return pl.kernel(
        body,
        out_type=jax.ShapeDtypeStruct((n_out_rows, d), src.dtype),
        mesh=mesh,
        scratch_types=[pltpu.VMEM((chunk,), jnp.int32),
                       pltpu.VMEM((chunk, d), src.dtype)],
        name="sc_scatter_rows",
    )(src, dest_kt.reshape(-1), fill_idx, jnp.zeros((chunk, d), src.dtype))


def _expert_kernel(be_ref, nxt_ref, eslot_ref, nu_ref, x_ref, wgu_hbm, bgu_ref, wd_hbm, bd_ref,
                   y_ref, wgu_f, wd_f, wsem, wgu_b, wd_b):
    i = pl.program_id(0)
    n_used = nu_ref[0]
    expert = be_ref[i]
    changed = jnp.logical_or(i == 0, expert != be_ref[jnp.maximum(i - 1, 0)])

    def weight_copies(e, slot):
        return (pltpu.make_async_copy(wgu_hbm.at[e], wgu_f.at[slot], wsem.at[0, slot]),
                pltpu.make_async_copy(wd_hbm.at[e], wd_f.at[slot], wsem.at[1, slot]))

    @pl.when(jnp.logical_and(changed, i < n_used))
    def _():
        slot = eslot_ref[i]

        @pl.when(i == 0)
        def _():
            for cp in weight_copies(expert, slot):
                cp.start()

        for cp in weight_copies(expert, slot):
            cp.wait()
        nxt = nxt_ref[i]

        @pl.when(nxt >= 0)
        def _():
            for cp in weight_copies(nxt, 1 - slot):
                cp.start()

        wgu_b[...] = wgu_f[slot].astype(BF16)
        wd_b[...] = wd_f[slot].astype(BF16)

    @pl.when(i < n_used)
    def _():
        d_ff = wd_b.shape[0]
        xb = x_ref[...].astype(BF16)
        hgu = jnp.dot(xb, wgu_b[...], preferred_element_type=F32) + bgu_ref[0]
        glu = jnp.minimum(hgu[:, :d_ff], SWIGLU_LIMIT)
        lin = jnp.clip(hgu[:, d_ff:], -SWIGLU_LIMIT, SWIGLU_LIMIT)
        act = glu * _sigmoid(SWIGLU_ALPHA * glu) * (lin + 1.0)
        y_ref[...] = jnp.dot(act.astype(BF16), wd_b[...], preferred_element_type=F32) + bd_ref[0]

    @pl.when(i >= n_used)
    def _():
        y_ref[...] = jnp.zeros_like(y_ref)


def _experts(xs, block_expert, n_used, wgu, bgu, wd, bd):
    d = xs.shape[1]
    n_blocks = block_expert.shape[0]
    n_exp, _, two_f = wgu.shape
    d_ff = wd.shape[1]
    rows = MOE_BLOCK
    n_rows = n_blocks * rows
    blk = jnp.arange(n_blocks, dtype=jnp.int32)
    is_first = jnp.concatenate([jnp.ones((1,), bool), block_expert[1:] != block_expert[:-1]])
    first_used = jnp.logical_and(is_first, blk < n_used[0])
    first_pos = jnp.where(first_used, blk, n_blocks)
    next_first = jnp.concatenate([lax.cummin(first_pos[::-1])[::-1][1:],
                                  jnp.full((1,), n_blocks, jnp.int32)])
    next_expert = jnp.where(next_first < n_blocks,
                            block_expert[jnp.minimum(next_first, n_blocks - 1)], -1).astype(jnp.int32)
    expert_slot = ((jnp.cumsum(is_first.astype(jnp.int32)) - 1) % 2).astype(jnp.int32)

    def last_used(i, be, nxt, es, nu):
        return (jnp.minimum(i, jnp.maximum(nu[0] - 1, 0)), 0)

    grid_spec = pltpu.PrefetchScalarGridSpec(
        num_scalar_prefetch=4,
        grid=(n_blocks,),
        in_specs=[
            pl.BlockSpec((rows, d), last_used),
            pl.BlockSpec(memory_space=pl.ANY),
            pl.BlockSpec((1, 1, two_f), lambda i, be, nxt, es, nu: (be[i], 0, 0)),
            pl.BlockSpec(memory_space=pl.ANY),
            pl.BlockSpec((1, 1, d), lambda i, be, nxt, es, nu: (be[i], 0, 0)),
        ],
        out_specs=pl.BlockSpec((rows, d), lambda i, be, nxt, es, nu: (i, 0)),
        scratch_shapes=[pltpu.VMEM((2, d, two_f), F32),
                        pltpu.VMEM((2, d_ff, d), F32),
                        pltpu.SemaphoreType.DMA((2, 2)),
                        pltpu.VMEM((d, two_f), BF16),
                        pltpu.VMEM((d_ff, d), BF16)],
    )
    return pl.pallas_call(
        _expert_kernel,
        grid_spec=grid_spec,
        out_shape=jax.ShapeDtypeStruct((n_rows, d), F32),
        compiler_params=_cparams(("arbitrary",)),
        name="experts",
    )(block_expert, next_expert, expert_slot, n_used, xs, wgu, bgu.reshape(n_exp, 1, two_f), wd,
      bd.reshape(n_exp, 1, d))


def _combine_kernel(alpha, *refs):
    ys_refs = refs[:TOP_K]
    gw_ref, h_ref, lg_ref, lb_ref = refs[TOP_K:TOP_K + 4]
    o_ref = refs[-1]
    gw = gw_ref[...]
    acc = alpha * h_ref[...]
    for k in range(TOP_K):
        acc = acc + gw[:, k:k + 1] * ys_refs[k][...]
    o_ref[...] = _layer_norm(acc, lg_ref[...], lb_ref[...])


def _combine(alpha, y_slots, gw_rows, h, lg, lb, tm, out_prev, row0, total_rows):
    rows, d = h.shape
    tm = min(tm, rows)
    assert rows % tm == 0 and row0 % tm == 0
    nsteps = rows // tm
    blk0 = row0 // tm
    ys_specs = [pl.BlockSpec((tm, d), functools.partial(lambda i, k: (k * nsteps + i, 0), k=k))
                for k in range(TOP_K)]
    in_specs = ys_specs + [pl.BlockSpec((tm, LANES), lambda i: (i, 0)),
                           pl.BlockSpec((tm, d), lambda i: (i, 0)),
                           pl.BlockSpec((1, d), lambda i: (0, 0)),
                           pl.BlockSpec((1, d), lambda i: (0, 0))]
    args = [y_slots] * TOP_K + [gw_rows, h, lg, lb]
    aliases = {}
    if out_prev is not None:
        in_specs.append(pl.BlockSpec(memory_space=pl.ANY))
        aliases = {len(args): 0}
        args.append(out_prev)
    return pl.pallas_call(
        functools.partial(_combine_kernel, alpha),
        grid=(nsteps,),
        in_specs=in_specs,
        out_specs=pl.BlockSpec((tm, d), lambda i: (i + blk0, 0)),
        out_shape=jax.ShapeDtypeStruct((total_rows, d), F32),
        input_output_aliases=aliases,
        compiler_params=_cparams(("parallel",)),
        name="combine_ln2",
    )(*args)


def _pack_in_proj(w_in, d_inner, n_ssm_heads, attn_dim):
    gn = SSM_GROUPS * D_STATE
    kv_dim = KV_HEADS * HEAD_DIM
    d = w_in.shape[0]
    o = 0
    seg = {}
    for name, width in (("z", d_inner), ("x", d_inner), ("B", gn), ("C", gn), ("dt", n_ssm_heads),
                        ("q", attn_dim), ("k", kv_dim), ("v", kv_dim), ("gs", d), ("ga", d)):
        seg[name] = w_in[:, o:o + width]
        o += width
    assert o == w_in.shape[1]
    dt_pad = jnp.pad(seg["dt"], ((0, 0), (0, LANES - n_ssm_heads)))
    order = (("z", seg["z"]), ("x", seg["x"]), ("gs", seg["gs"]), ("ga", seg["ga"]),
             ("B", seg["B"]), ("C", seg["C"]), ("dt", dt_pad))
    col, off = {}, 0
    for name, w in order:
        col[name] = off
        assert off % w.shape[1] == 0
        off += w.shape[1]
    w_f32grp = jnp.concatenate([w for _, w in order], axis=1).astype(BF16)
    w_qkv = jnp.concatenate([seg["q"], seg["k"], seg["v"]], axis=1).astype(BF16)
    return w_f32grp, w_qkv, col


def _largest_tile(n, cap):
    best = LANES
    for k in range(1, n // LANES + 1):
        if n % (k * LANES) == 0 and k * LANES <= cap:
            best = k * LANES
    return best


def _layer(h_in, batch, seq, alpha, w_in, conv_w, conv_b, dt_bias, a_log, d_skip, ssm_norm_w,
           w_ssm_out, attn_sinks, w_attn_out, b_gates, w_mix_out, ln1_g, ln1_b, w_router, b_router,
           w_gate_up, b_gate_up, w_down, b_down, ln2_g, ln2_b):
    t, d = h_in.shape
    d_inner = ssm_norm_w.shape[0]
    n_ssm_heads = dt_bias.shape[0]
    attn_dim = w_attn_out.shape[0]
    n_heads = attn_sinks.shape[0]
    n_exp = w_router.shape[1]
    gn = SSM_GROUPS * D_STATE

    w_f32grp, w_qkv, col = _pack_in_proj(w_in, d_inner, n_ssm_heads, attn_dim)
    proj = _matmul(h_in, w_f32grp, F32, 1024, _largest_tile(w_f32grp.shape[1], 2560))
    qkv = _matmul(h_in, w_qkv, BF16, 1024, _largest_tile(w_qkv.shape[1], 1536))

    pad_h = (0, LANES - n_ssm_heads)
    y_ssm = _ssd(proj, batch, seq, col,
                 conv_w[:, :d_inner], conv_w[:, d_inner:d_inner + gn], conv_w[:, d_inner + gn:],
                 conv_b[None, :d_inner], conv_b[None, d_inner:d_inner + gn], conv_b[None, d_inner + gn:],
                 jnp.pad(dt_bias, pad_h)[None, :], jnp.pad(a_log, pad_h)[None, :],
                 jnp.repeat(d_skip, SSM_HEAD_DIM)[None, :], ssm_norm_w[None, :])
    y_att = _swa(qkv, attn_sinks, batch, seq, n_heads)

    wso, wao, wmix = w_ssm_out.astype(BF16), w_attn_out.astype(BF16), w_mix_out.astype(BF16)
    w_router_t = w_router.T

    n_parts = MOE_SPLITS if t % (MOE_SPLITS * SC_CORES * SC_SUBCORES * SC_GATHER_CHUNK) == 0 else 1
    rows = t // n_parts
    out = None
    for part in range(n_parts):
        row0 = part * rows
        h1 = _merge(alpha, y_ssm, y_att, proj, col, h_in, wso, wao, wmix,
                    b_gates[None, :d], b_gates[None, d:], ln1_g[None, :], ln1_b[None, :], 512, row0, rows)
        idx_kt, gw_kt, rank_kt, counts = _router(h1, w_router_t, b_router[:, None], 1024)

        counts = counts[:, 0]
        padded = (counts + MOE_BLOCK - 1) // MOE_BLOCK * MOE_BLOCK
        pad_end = jnp.cumsum(padded)
        pad_start = pad_end - padded
        n_blocks = -(-(rows * TOP_K) // MOE_BLOCK) + n_exp
        n_rows = n_blocks * MOE_BLOCK
        expert_ids = jnp.arange(n_exp, dtype=jnp.int32)
        pad_start_of_slot = jnp.sum(
            jnp.where(idx_kt[None] == expert_ids[:, None, None], pad_start[:, None, None], 0), axis=0)
        dest_kt = pad_start_of_slot + rank_kt
        fill_rows = (pad_start + counts)[:, None] + jnp.arange(MOE_BLOCK, dtype=jnp.int32)[None, :]
        spare_rows = n_rows + jnp.arange(n_exp * MOE_BLOCK, dtype=jnp.int32).reshape(n_exp, MOE_BLOCK)
        fill_idx = jnp.where(fill_rows < pad_end[:, None], fill_rows, spare_rows).reshape(-1)
        block_row0 = jnp.arange(n_blocks, dtype=jnp.int32) * MOE_BLOCK
        block_expert = jnp.minimum(
            jnp.sum((pad_end[None, :] <= block_row0[:, None]).astype(jnp.int32), axis=1), n_exp - 1)
        n_used = (pad_end[-1:] // MOE_BLOCK).astype(jnp.int32)
        gw_rows = jnp.pad(gw_kt.T, ((0, 0), (0, LANES - TOP_K)))

        xs = _sc_scatter_rows(h1, dest_kt, fill_idx, n_rows + n_exp * MOE_BLOCK)
        ys = _experts(xs, block_expert, n_used, w_gate_up, b_gate_up, w_down, b_down)
        y_slots = _sc_gather_rows(ys, dest_kt.reshape(-1))
        out = _combine(alpha, y_slots, gw_rows, h1, ln2_g[None, :], ln2_b[None, :], 512, out, row0, t)
    return out


def kernel(x, w_in, conv_w, conv_b, dt_bias, a_log, d_skip, ssm_norm_w, w_ssm_out, attn_sinks,
           w_attn_out, b_gates, w_mix_out, ln1_g, ln1_b, w_router, b_router, w_gate_up, b_gate_up,
           w_down, b_down, ln2_g, ln2_b):
    batch, seq, d = x.shape
    depth = w_in.shape[0]
    alpha = (2 * depth) ** 0.25
    h = x.reshape(batch * seq, d)
    for i in range(depth):
        h = _layer(h, batch, seq, alpha, w_in[i], conv_w[i], conv_b[i], dt_bias[i], a_log[i],
                   d_skip[i], ssm_norm_w[i], w_ssm_out[i], attn_sinks[i], w_attn_out[i], b_gates[i],
                   w_mix_out[i], ln1_g[i], ln1_b[i], w_router[i], b_router[i], w_gate_up[i],
                   b_gate_up[i], w_down[i], b_down[i], ln2_g[i], ln2_b[i])
    return h.reshape(batch, seq, d)
```

```python
import functools

import jax
import jax.numpy as jnp
from jax import lax
from jax.experimental import pallas as pl
from jax.experimental.pallas import tpu as pltpu
from jax.experimental.pallas import tpu_sc as plsc

SSM_HEAD_DIM = 64
SSM_GROUPS = 4
D_STATE = 128
CONV_WIDTH = 4
SSD_CHUNK = 128
KV_HEADS = 4
HEAD_DIM = 64
WINDOW = 128
TOP_K = 4
SWIGLU_LIMIT = 7.0
SWIGLU_ALPHA = 1.702
MOE_BLOCK = 512
LN_EPS = 1e-5
RMS_EPS = 1e-5

LANES = 128
SUBLANES = 8
NEG_BIG = -1e30
LOG2_E = 1.4426950408889634
F32 = jnp.float32
BF16 = jnp.bfloat16
VMEM_LIMIT = 56 * 1024 * 1024


def _cparams(sem):
    return pltpu.CompilerParams(dimension_semantics=sem, vmem_limit_bytes=VMEM_LIMIT)


def _sigmoid(x):
    return 0.5 + 0.5 * jnp.tanh(0.5 * x)


def _pack_bf16_pair(v):
    w = v.shape[1] // 2
    bits = lax.bitcast_convert_type(v.astype(BF16).astype(F32), jnp.uint32)
    return (bits[:, :w] >> 16) | (bits[:, w:] & jnp.uint32(0xFFFF0000))


def _unpack_bf16_pair(p):
    lo = lax.bitcast_convert_type(p << 16, F32)
    hi = lax.bitcast_convert_type(p & jnp.uint32(0xFFFF0000), F32)
    return jnp.concatenate([lo, hi], axis=1)


def _silu(x):
    h = 0.5 * x
    return h + h * jnp.tanh(h)


def _mm_kernel(x_ref, w_ref, o_ref):
    o_ref[...] = jnp.dot(x_ref[...].astype(BF16), w_ref[...],
                         preferred_element_type=F32).astype(o_ref.dtype)


def _matmul(x, w, out_dtype, tm, tn):
    m, k = x.shape
    n = w.shape[1]
    tm = min(tm, m)
    assert m % tm == 0 and n % tn == 0
    return pl.pallas_call(
        _mm_kernel,
        grid=(n // tn, m // tm),
        in_specs=[pl.BlockSpec((tm, k), lambda j, i: (i, 0)),
                  pl.BlockSpec((k, tn), lambda j, i: (0, j))],
        out_specs=pl.BlockSpec((tm, tn), lambda j, i: (i, j)),
        out_shape=jax.ShapeDtypeStruct((m, n), out_dtype),
        compiler_params=_cparams(("parallel", "parallel")),
        name="in_proj",
    )(x, w)


def _conv_silu(u_ref, prev_ref, w_ref, b_ref, ext_ref, first):
    n = u_ref.shape[0]
    prev = prev_ref[...]
    ext_ref[0:SUBLANES, :] = jnp.where(first, jnp.zeros_like(prev), prev)
    ext_ref[SUBLANES:SUBLANES + n, :] = u_ref[...]
    acc = b_ref[...] + w_ref[CONV_WIDTH - 1:CONV_WIDTH, :] * ext_ref[SUBLANES:SUBLANES + n, :]
    for j in range(CONV_WIDTH - 1):
        off = SUBLANES - (CONV_WIDTH - 1) + j
        acc = acc + w_ref[j:j + 1, :] * ext_ref[off:off + n, :]
    return _silu(acc)


def _pair_expand(v, h0, lo_mask):
    n = v.shape[0]
    a = jnp.broadcast_to(v[:, h0:h0 + 1], (n, LANES))
    b = jnp.broadcast_to(v[:, h0 + 1:h0 + 2], (n, LANES))
    return jnp.where(lo_mask, a, b)


def _ssd_kernel(z_ref, x_ref, b_ref, c_ref, dt_ref, xp_ref, bp_ref, cp_ref,
                cwx_ref, cwb_ref, cwc_ref, cbx_ref, cbb_ref, cbc_ref,
                dtb_ref, alog_ref, dsk_ref, nw_ref,
                o_ref, st_ref, extx_ref, extb_ref, extc_ref):
    c = pl.program_id(1)
    first = c == 0
    n = SSD_CHUNK
    heads_per_group = st_ref.shape[2] // SSM_HEAD_DIM
    gw = heads_per_group * SSM_HEAD_DIM

    @pl.when(first)
    def _():
        st_ref[...] = jnp.zeros_like(st_ref)

    xc = _conv_silu(x_ref, xp_ref, cwx_ref, cbx_ref, extx_ref, first)
    bc = _conv_silu(b_ref, bp_ref, cwb_ref, cbb_ref, extb_ref, first)
    cc = _conv_silu(c_ref, cp_ref, cwc_ref, cbc_ref, extc_ref, first)

    dt_in = dt_ref[...] + dtb_ref[...]
    dt = jnp.maximum(dt_in, 0.0) + jnp.log(1.0 + jnp.exp(-jnp.abs(dt_in)))
    a = dt * (-jnp.exp(alog_ref[...]))
    row = lax.broadcasted_iota(jnp.int32, (n, n), 0)
    col = lax.broadcasted_iota(jnp.int32, (n, n), 1)
    causal = row >= col
    tri = jnp.where(causal, 1.0, 0.0).astype(F32)
    acum = jnp.dot(tri, a, preferred_element_type=F32, precision=lax.Precision.HIGHEST)
    acum2 = acum * LOG2_E
    acum2_t = acum2.T
    a_last = acum[n - 1:n, :]
    d2e = jnp.exp(a_last - acum)
    eac = jnp.exp(acum)
    lo_mask = lax.broadcasted_iota(jnp.int32, (n, LANES), 1) < SSM_HEAD_DIM

    for g in range(SSM_GROUPS):
        xg = xc[:, g * gw:(g + 1) * gw]
        bg = bc[:, g * D_STATE:(g + 1) * D_STATE]
        cg = cc[:, g * D_STATE:(g + 1) * D_STATE]
        cg_b = cg.astype(BF16)
        bg_t = bg.T.astype(BF16)
        cb = jnp.dot(cg_b, bg_t, preferred_element_type=F32)
        st_prev = st_ref[g]
        y_off = jnp.dot(cg_b, st_prev.astype(BF16), preferred_element_type=F32)
        y_parts, xw_parts, ea_parts = [], [], []
        for j in range(heads_per_group // 2):
            h0 = g * heads_per_group + 2 * j
            lanes = slice(j * LANES, (j + 1) * LANES)
            xs_pair = xg[:, lanes] * _pair_expand(dt, h0, lo_mask)
            xs_b = xs_pair.astype(BF16)
            ea_pair = _pair_expand(eac, h0, lo_mask)
            xw_parts.append((xs_pair * _pair_expand(d2e, h0, lo_mask)).astype(BF16))
            ea_parts.append(ea_pair)
            ys = []
            for h in (h0, h0 + 1):
                seg2 = acum2[:, h:h + 1] - acum2_t[h:h + 1, :]
                decay = jnp.exp2(jnp.where(causal, seg2, NEG_BIG))
                m = (cb * decay).astype(BF16)
                ys.append(jnp.dot(m, xs_b, preferred_element_type=F32))
            y_parts.append(jnp.where(lo_mask, ys[0], ys[1]))
        y_diag = jnp.concatenate(y_parts, axis=1)
        ea = jnp.concatenate(ea_parts, axis=1)
        xw = jnp.concatenate(xw_parts, axis=1)
        cols = slice(g * gw, (g + 1) * gw)
        y = y_diag + y_off * ea + xg * dsk_ref[:, cols]
        zg = z_ref[:, cols]
        y = y * _silu(zg)
        ms = jnp.mean(y * y, axis=-1, keepdims=True)
        o_ref[:, cols] = (y * lax.rsqrt(ms + RMS_EPS) * nw_ref[:, cols]).astype(o_ref.dtype)
        st_ref[g] = st_prev * ea[n - 1:n, :] + jnp.dot(bg_t, xw, preferred_element_type=F32)


def _ssd(proj, batch, seq, col, conv_wx, conv_wb, conv_wc, conv_bx, conv_bb, conv_bc,
         dt_bias, a_log, d_skip, norm_w):
    t = proj.shape[0]
    d_inner = norm_w.shape[1]
    gn = SSM_GROUPS * D_STATE
    n = SSD_CHUNK
    nc = seq // n
    gw = d_inner // SSM_GROUPS

    def rows(b, c):
        return b * nc + c

    def prev_rows(b, c):
        return jnp.maximum((b * nc + c) * (n // SUBLANES) - 1, 0)

    def const(b, c):
        return (0, 0)

    in_specs = [
        pl.BlockSpec((n, d_inner), lambda b, c: (rows(b, c), col["z"] // d_inner)),
        pl.BlockSpec((n, d_inner), lambda b, c: (rows(b, c), col["x"] // d_inner)),
        pl.BlockSpec((n, gn), lambda b, c: (rows(b, c), col["B"] // gn)),
        pl.BlockSpec((n, gn), lambda b, c: (rows(b, c), col["C"] // gn)),
        pl.BlockSpec((n, LANES), lambda b, c: (rows(b, c), col["dt"] // LANES)),
        pl.BlockSpec((SUBLANES, d_inner), lambda b, c: (prev_rows(b, c), col["x"] // d_inner)),
        pl.BlockSpec((SUBLANES, gn), lambda b, c: (prev_rows(b, c), col["B"] // gn)),
        pl.BlockSpec((SUBLANES, gn), lambda b, c: (prev_rows(b, c), col["C"] // gn)),
        pl.BlockSpec((CONV_WIDTH, d_inner), const),
        pl.BlockSpec((CONV_WIDTH, gn), const),
        pl.BlockSpec((CONV_WIDTH, gn), const),
        pl.BlockSpec((1, d_inner), const),
        pl.BlockSpec((1, gn), const),
        pl.BlockSpec((1, gn), const),
        pl.BlockSpec((1, LANES), const),
        pl.BlockSpec((1, LANES), const),
        pl.BlockSpec((1, d_inner), const),
        pl.BlockSpec((1, d_inner), const),
    ]
    return pl.pallas_call(
        _ssd_kernel,
        grid=(batch, nc),
        in_specs=in_specs,
        out_specs=pl.BlockSpec((n, d_inner), lambda b, c: (rows(b, c), 0)),
        out_shape=jax.ShapeDtypeStruct((t, d_inner), BF16),
        scratch_shapes=[pltpu.VMEM((SSM_GROUPS, D_STATE, gw), F32),
                        pltpu.VMEM((SUBLANES + n, d_inner), F32),
                        pltpu.VMEM((SUBLANES + n, gn), F32),
                        pltpu.VMEM((SUBLANES + n, gn), F32)],
        compiler_params=_cparams(("arbitrary", "arbitrary")),
        name="ssd_mixer",
    )(proj, proj, proj, proj, proj, proj, proj, proj,
      conv_wx, conv_wb, conv_wc, conv_bx, conv_bb, conv_bc, dt_bias, a_log, d_skip, norm_w)


def _swa_kernel(sink_ref, q_ref, kc_ref, vc_ref, kp_ref, vp_ref, o_ref):
    i = pl.program_id(1)
    n = WINDOW
    n_heads = q_ref.shape[1] // HEAD_DIM
    grp = n_heads // KV_HEADS
    qpos = lax.broadcasted_iota(jnp.int32, (n, 2 * n), 0) + n
    kpos = lax.broadcasted_iota(jnp.int32, (n, 2 * n), 1)
    diff = qpos - kpos
    mask = (diff >= 0) & (diff < n) & ((kpos >= n) | (i > 0))
    lo_q = lax.broadcasted_iota(jnp.int32, (n, LANES), 1) < HEAD_DIM
    lo_k = lax.broadcasted_iota(jnp.int32, (2 * n, LANES), 1) < HEAD_DIM
    scale = HEAD_DIM ** -0.5
    for p in range(KV_HEADS // 2):
        lanes = slice(p * LANES, (p + 1) * LANES)
        kk = jnp.concatenate([kp_ref[:, lanes], kc_ref[:, lanes]], axis=0).astype(F32)
        vv = jnp.concatenate([vp_ref[:, lanes], vc_ref[:, lanes]], axis=0).astype(F32)
        kk_sw = pltpu.roll(kk, HEAD_DIM, axis=1)
        vv_sw = pltpu.roll(vv, HEAD_DIM, axis=1)
        for par in range(2):
            g = 2 * p + par
            k2 = (jnp.where(lo_k, kk, kk_sw) if par == 0 else jnp.where(lo_k, kk_sw, kk)).astype(BF16)
            v2 = (jnp.where(lo_k, vv, vv_sw) if par == 0 else jnp.where(lo_k, vv_sw, vv)).astype(BF16)
            for qp in range(grp // 2):
                h0 = g * grp + 2 * qp
                qlanes = slice((h0 // 2) * LANES, (h0 // 2 + 1) * LANES)
                q_pair = q_ref[:, qlanes]
                outs = []
                for hh in range(2):
                    keep = lo_q if hh == 0 else jnp.logical_not(lo_q)
                    qm = jnp.where(keep, q_pair, jnp.zeros_like(q_pair))
                    s = lax.dot_general(qm, k2, (((1,), (1,)), ((), ())),
                                        preferred_element_type=F32) * scale
                    s = jnp.where(mask, s, NEG_BIG)
                    sink = sink_ref[h0 + hh]
                    mx = jnp.maximum(jnp.max(s, axis=-1, keepdims=True), sink)
                    pr = jnp.exp(s - mx)
                    den = jnp.sum(pr, axis=-1, keepdims=True) + jnp.exp(sink - mx)
                    o = jnp.dot(pr.astype(BF16), v2, preferred_element_type=F32)
                    outs.append(o / den)
                o_ref[:, qlanes] = jnp.where(lo_q, outs[0], outs[1]).astype(o_ref.dtype)


def _swa(qkv, sinks, batch, seq, n_heads):
    t = qkv.shape[0]
    n = WINDOW
    nb = seq // n
    qw = n_heads * HEAD_DIM
    kw = KV_HEADS * HEAD_DIM
    k_blk = qw // kw
    v_blk = k_blk + 1

    def cur(b, i):
        return b * nb + i

    def prev(b, i):
        return b * nb + jnp.maximum(i - 1, 0)

    return pl.pallas_call(
        _swa_kernel,
        grid=(batch, nb),
        in_specs=[pl.BlockSpec(memory_space=pltpu.SMEM),
                  pl.BlockSpec((n, qw), lambda b, i: (cur(b, i), 0)),
                  pl.BlockSpec((n, kw), lambda b, i: (cur(b, i), k_blk)),
                  pl.BlockSpec((n, kw), lambda b, i: (cur(b, i), v_blk)),
                  pl.BlockSpec((n, kw), lambda b, i: (prev(b, i), k_blk)),
                  pl.BlockSpec((n, kw), lambda b, i: (prev(b, i), v_blk))],
        out_specs=pl.BlockSpec((n, qw), lambda b, i: (cur(b, i), 0)),
        out_shape=jax.ShapeDtypeStruct((t, qw), BF16),
        compiler_params=_cparams(("parallel", "parallel")),
        name="swa",
    )(sinks, qkv, qkv, qkv, qkv, qkv)


def _layer_norm(v, g, b):
    mu = jnp.mean(v, axis=-1, keepdims=True)
    d = v - mu
    var = jnp.mean(d * d, axis=-1, keepdims=True)
    return d * lax.rsqrt(var + LN_EPS) * g + b


def _merge_kernel(alpha, ys_ref, ya_ref, gs_ref, ga_ref, x_ref, wso_ref, wao_ref, wmix_ref,
                  bgs_ref, bga_ref, lg_ref, lb_ref, h_ref, hp_ref):
    y_ssm = jnp.dot(ys_ref[...], wso_ref[...], preferred_element_type=F32)
    y_att = jnp.dot(ya_ref[...], wao_ref[...], preferred_element_type=F32)
    merged = (_sigmoid(gs_ref[...] + bgs_ref[...]) * y_ssm
              + _sigmoid(ga_ref[...] + bga_ref[...]) * y_att)
    mix = jnp.dot(merged.astype(BF16), wmix_ref[...], preferred_element_type=F32)
    h = _layer_norm(alpha * x_ref[...] + mix, lg_ref[...], lb_ref[...])
    h_ref[...] = h
    hp_ref[...] = _pack_bf16_pair(h)


def _merge(alpha, y_ssm, y_att, proj, col, x, wso, wao, wmix, bgs, bga, lg, lb, tm, row0, rows):
    d = x.shape[1]
    tm = min(tm, rows)
    assert rows % tm == 0 and row0 % tm == 0
    blk0 = row0 // tm
    di = y_ssm.shape[1]
    da = y_att.shape[1]

    def const(i):
        return (0, 0)

    return pl.pallas_call(
        functools.partial(_merge_kernel, alpha),
        grid=(rows // tm,),
        in_specs=[pl.BlockSpec((tm, di), lambda i: (i + blk0, 0)),
                  pl.BlockSpec((tm, da), lambda i: (i + blk0, 0)),
                  pl.BlockSpec((tm, d), lambda i: (i + blk0, col["gs"] // d)),
                  pl.BlockSpec((tm, d), lambda i: (i + blk0, col["ga"] // d)),
                  pl.BlockSpec((tm, d), lambda i: (i + blk0, 0)),
                  pl.BlockSpec((di, d), const),
                  pl.BlockSpec((da, d), const),
                  pl.BlockSpec((d, d), const),
                  pl.BlockSpec((1, d), const),
                  pl.BlockSpec((1, d), const),
                  pl.BlockSpec((1, d), const),
                  pl.BlockSpec((1, d), const)],
        out_specs=[pl.BlockSpec((tm, d), lambda i: (i, 0)),
                   pl.BlockSpec((tm, d // 2), lambda i: (i, 0))],
        out_shape=[jax.ShapeDtypeStruct((rows, d), F32),
                   jax.ShapeDtypeStruct((rows, d // 2), jnp.uint32)],
        compiler_params=_cparams(("parallel",)),
        name="merge_ln1",
    )(y_ssm, y_att, proj, proj, x, wso, wao, wmix, bgs, bga, lg, lb)


def _router_kernel(h_ref, wr_ref, br_ref, idx_ref, gw_ref, rank_ref, cnt_ref, tri_ref, base_ref):
    i = pl.program_id(0)
    tm = h_ref.shape[0]
    n_exp = wr_ref.shape[0]

    @pl.when(i == 0)
    def _():
        r = lax.broadcasted_iota(jnp.int32, (tm, tm), 0)
        c = lax.broadcasted_iota(jnp.int32, (tm, tm), 1)
        tri_ref[...] = jnp.where(r < c, 1.0, 0.0).astype(BF16)
        base_ref[...] = jnp.zeros_like(base_ref)

    logits = lax.dot_general(wr_ref[...], h_ref[...], (((1,), (1,)), ((), ())),
                             preferred_element_type=F32,
                             precision=lax.Precision.HIGHEST) + br_ref[...]
    eid = lax.broadcasted_iota(jnp.int32, (n_exp, tm), 0)
    work = logits
    vals, idxs = [], []
    for _ in range(TOP_K):
        mx = jnp.max(work, axis=0, keepdims=True)
        sel = jnp.min(jnp.where(work == mx, eid, n_exp), axis=0, keepdims=True)
        vals.append(mx)
        idxs.append(sel)
        work = jnp.where(eid == sel, -jnp.inf, work)
    exps = [jnp.exp(v - vals[0]) for v in vals]
    den = exps[0]
    for e in exps[1:]:
        den = den + e
    base = base_ref[...]
    for k in range(TOP_K):
        onehot = eid == idxs[k]
        oh = jnp.where(onehot, 1.0, 0.0)
        before = jnp.dot(oh.astype(BF16), tri_ref[...], preferred_element_type=F32)
        rank = jnp.sum(jnp.where(onehot, before + base, 0.0), axis=0, keepdims=True)
        idx_ref[k:k + 1, :] = idxs[k]
        gw_ref[k:k + 1, :] = exps[k] / den
        rank_ref[k:k + 1, :] = rank.astype(jnp.int32)
        base = base + jnp.sum(oh, axis=1, keepdims=True)
    base_ref[...] = base
    cnt_ref[...] = jnp.broadcast_to(base, cnt_ref.shape).astype(jnp.int32)


def _router(h, w_router_t, b_router, tm):
    t, d = h.shape
    n_exp = w_router_t.shape[0]
    tm = min(tm, t)
    outs = pl.pallas_call(
        _router_kernel,
        grid=(t // tm,),
        in_specs=[pl.BlockSpec((tm, d), lambda i: (i, 0)),
                  pl.BlockSpec((n_exp, d), lambda i: (0, 0)),
                  pl.BlockSpec((n_exp, 1), lambda i: (0, 0))],
        out_specs=[pl.BlockSpec((TOP_K, tm), lambda i: (0, i)),
                   pl.BlockSpec((TOP_K, tm), lambda i: (0, i)),
                   pl.BlockSpec((TOP_K, tm), lambda i: (0, i)),
                   pl.BlockSpec((n_exp, LANES), lambda i: (0, 0))],
        out_shape=[jax.ShapeDtypeStruct((TOP_K, t), jnp.int32),
                   jax.ShapeDtypeStruct((TOP_K, t), F32),
                   jax.ShapeDtypeStruct((TOP_K, t), jnp.int32),
                   jax.ShapeDtypeStruct((n_exp, LANES), jnp.int32)],
        scratch_shapes=[pltpu.VMEM((tm, tm), BF16), pltpu.VMEM((n_exp, 1), F32)],
        compiler_params=_cparams(("arbitrary",)),
        name="router",
    )(h, w_router_t, b_router)
    return outs


SC_CORES = 2
SC_SUBCORES = 16
SC_GATHER_CHUNK = 128
MOE_SPLITS = 2


def _sc_gather_rows(table, idx):
    b = idx.shape[0]
    d = table.shape[1]
    n_workers = SC_CORES * SC_SUBCORES
    chunk = SC_GATHER_CHUNK
    assert b % (n_workers * chunk) == 0
    per_worker = b // n_workers
    n_chunks = per_worker // chunk
    mesh = plsc.VectorSubcoreMesh(core_axis_name="c", subcore_axis_name="s",
                                  num_cores=SC_CORES, num_subcores=SC_SUBCORES)

    def body(table_hbm, idx_hbm, out_hbm, idx_v, rows_v, sem):
        wid = lax.axis_index("s") * SC_CORES + lax.axis_index("c")
        base = wid * per_worker

        @pl.loop(0, n_chunks)
        def _(c):
            off = pl.multiple_of(base + c * chunk, SUBLANES)
            pltpu.sync_copy(idx_hbm.at[pl.ds(off, chunk)], idx_v)
            pltpu.async_copy(table_hbm.at[idx_v], rows_v, sem).wait()
            pltpu.sync_copy(rows_v, out_hbm.at[pl.ds(off, chunk)])

    return pl.kernel(
        body,
        out_type=jax.ShapeDtypeStruct((b, d), table.dtype),
        mesh=mesh,
        scratch_types=[pltpu.VMEM((chunk,), jnp.int32),
                       pltpu.VMEM((chunk, d), table.dtype),
                       pltpu.SemaphoreType.DMA],
        name="sc_gather_rows",
    )(table, idx)


def _sc_scatter_rows(src, dest_kt, fill_idx, n_out_rows):
    t, d = src.shape
    n_k = dest_kt.shape[0]
    n_fill = fill_idx.shape[0]
    n_workers = SC_CORES * SC_SUBCORES
    chunk = SC_GATHER_CHUNK
    assert t % (n_workers * chunk) == 0 and n_fill % (n_workers * chunk) == 0
    tok_per_worker = t // n_workers
    fill_per_worker = n_fill // n_workers
    mesh = plsc.VectorSubcoreMesh(core_axis_name="c", subcore_axis_name="s",
                                  num_cores=SC_CORES, num_subcores=SC_SUBCORES)

    def body(src_hbm, dest_hbm, fill_hbm, zeros_hbm, out_hbm, idx_v, rows_v):
        wid = lax.axis_index("s") * SC_CORES + lax.axis_index("c")

        pltpu.sync_copy(zeros_hbm, rows_v)

        @pl.loop(0, fill_per_worker // chunk)
        def _(c):
            off = pl.multiple_of(wid * fill_per_worker + c * chunk, SUBLANES)
            pltpu.sync_copy(fill_hbm.at[pl.ds(off, chunk)], idx_v)
            pltpu.sync_copy(rows_v, out_hbm.at[idx_v])

        @pl.loop(0, tok_per_worker // chunk)
        def _(c):
            off = pl.multiple_of(wid * tok_per_worker + c * chunk, SUBLANES)
            pltpu.sync_copy(src_hbm.at[pl.ds(off, chunk)], rows_v)
            for k in range(n_k):
                pltpu.sync_copy(dest_hbm.at[pl.ds(k * t + off, chunk)], idx_v)
                pltpu.sync_copy(rows_v, out_hbm.at[idx_v])

    return pl.kernel(
        body,
        out_type=jax.ShapeDtypeStruct((n_out_rows, d), src.dtype),
        mesh=mesh,
        scratch_types=[pltpu.VMEM((chunk,), jnp.int32),
                       pltpu.VMEM((chunk, d), src.dtype)],
        name="sc_scatter_rows",
    )(src, dest_kt.reshape(-1), fill_idx, jnp.zeros((chunk, d), src.dtype))


def _expert_kernel(be_ref, nxt_ref, eslot_ref, nu_ref, x_ref, wgu_hbm, bgu_ref, wd_hbm, bd_ref,
                   y_ref, wgu_f, wd_f, wsem, wgu_b, wd_b):
    i = pl.program_id(0)
    n_used = nu_ref[0]
    expert = be_ref[i]
    changed = jnp.logical_or(i == 0, expert != be_ref[jnp.maximum(i - 1, 0)])

    def weight_copies(e, slot):
        return (pltpu.make_async_copy(wgu_hbm.at[e], wgu_f.at[slot], wsem.at[0, slot]),
                pltpu.make_async_copy(wd_hbm.at[e], wd_f.at[slot], wsem.at[1, slot]))

    @pl.when(jnp.logical_and(changed, i < n_used))
    def _():
        slot = eslot_ref[i]

        @pl.when(i == 0)
        def _():
            for cp in weight_copies(expert, slot):
                cp.start()

        for cp in weight_copies(expert, slot):
            cp.wait()
        nxt = nxt_ref[i]

        @pl.when(nxt >= 0)
        def _():
            for cp in weight_copies(nxt, 1 - slot):
                cp.start()

        wgu_b[...] = wgu_f[slot].astype(BF16)
        wd_b[...] = wd_f[slot].astype(BF16)

    @pl.when(i < n_used)
    def _():
        d_ff = wd_b.shape[0]
        xb = _unpack_bf16_pair(x_ref[...]).astype(BF16)
        hgu = jnp.dot(xb, wgu_b[...], preferred_element_type=F32) + bgu_ref[0]
        glu = jnp.minimum(hgu[:, :d_ff], SWIGLU_LIMIT)
        lin = jnp.clip(hgu[:, d_ff:], -SWIGLU_LIMIT, SWIGLU_LIMIT)
        act = glu * _sigmoid(SWIGLU_ALPHA * glu) * (lin + 1.0)
        y = jnp.dot(act.astype(BF16), wd_b[...], preferred_element_type=F32) + bd_ref[0]
        y_ref[...] = _pack_bf16_pair(y)

    @pl.when(i >= n_used)
    def _():
        y_ref[...] = jnp.zeros_like(y_ref)


def _experts(xs, block_expert, n_used, wgu, bgu, wd, bd):
    d = wgu.shape[1]
    dp = xs.shape[1]
    assert 2 * dp == d
    n_blocks = block_expert.shape[0]
    n_exp, _, two_f = wgu.shape
    d_ff = wd.shape[1]
    rows = MOE_BLOCK
    n_rows = n_blocks * rows
    blk = jnp.arange(n_blocks, dtype=jnp.int32)
    is_first = jnp.concatenate([jnp.ones((1,), bool), block_expert[1:] != block_expert[:-1]])
    first_used = jnp.logical_and(is_first, blk < n_used[0])
    first_pos = jnp.where(first_used, blk, n_blocks)
    next_first = jnp.concatenate([lax.cummin(first_pos[::-1])[::-1][1:],
                                  jnp.full((1,), n_blocks, jnp.int32)])
    next_expert = jnp.where(next_first < n_blocks,
                            block_expert[jnp.minimum(next_first, n_blocks - 1)], -1).astype(jnp.int32)
    expert_slot = ((jnp.cumsum(is_first.astype(jnp.int32)) - 1) % 2).astype(jnp.int32)

    def last_used(i, be, nxt, es, nu):
        return (jnp.minimum(i, jnp.maximum(nu[0] - 1, 0)), 0)

    grid_spec = pltpu.PrefetchScalarGridSpec(
        num_scalar_prefetch=4,
        grid=(n_blocks,),
        in_specs=[
            pl.BlockSpec((rows, dp), last_used),
            pl.BlockSpec(memory_space=pl.ANY),
            pl.BlockSpec((1, 1, two_f), lambda i, be, nxt, es, nu: (be[i], 0, 0)),
            pl.BlockSpec(memory_space=pl.ANY),
            pl.BlockSpec((1, 1, d), lambda i, be, nxt, es, nu: (be[i], 0, 0)),
        ],
        out_specs=pl.BlockSpec((rows, dp), lambda i, be, nxt, es, nu: (i, 0)),
        scratch_shapes=[pltpu.VMEM((2, d, two_f), F32),
                        pltpu.VMEM((2, d_ff, d), F32),
                        pltpu.SemaphoreType.DMA((2, 2)),
                        pltpu.VMEM((d, two_f), BF16),
                        pltpu.VMEM((d_ff, d), BF16)],
    )
    return pl.pallas_call(
        _expert_kernel,
        grid_spec=grid_spec,
        out_shape=jax.ShapeDtypeStruct((n_rows, dp), jnp.uint32),
        compiler_params=_cparams(("arbitrary",)),
        name="experts",
    )(block_expert, next_expert, expert_slot, n_used, xs, wgu, bgu.reshape(n_exp, 1, two_f), wd,
      bd.reshape(n_exp, 1, d))


def _combine_kernel(alpha, *refs):
    ys_refs = refs[:TOP_K]
    gw_ref, h_ref, lg_ref, lb_ref = refs[TOP_K:TOP_K + 4]
    o_ref = refs[-1]
    gw = gw_ref[...]
    acc = alpha * h_ref[...]
    for k in range(TOP_K):
        acc = acc + gw[:, k:k + 1] * _unpack_bf16_pair(ys_refs[k][...])
    o_ref[...] = _layer_norm(acc, lg_ref[...], lb_ref[...])


def _combine(alpha, y_slots, gw_rows, h, lg, lb, tm, out_prev, row0, total_rows):
    rows, d = h.shape
    tm = min(tm, rows)
    assert rows % tm == 0 and row0 % tm == 0
    nsteps = rows // tm
    blk0 = row0 // tm
    ys_specs = [pl.BlockSpec((tm, d // 2), functools.partial(lambda i, k: (k * nsteps + i, 0), k=k))
                for k in range(TOP_K)]
    in_specs = ys_specs + [pl.BlockSpec((tm, LANES), lambda i: (i, 0)),
                           pl.BlockSpec((tm, d), lambda i: (i, 0)),
                           pl.BlockSpec((1, d), lambda i: (0, 0)),
                           pl.BlockSpec((1, d), lambda i: (0, 0))]
    args = [y_slots] * TOP_K + [gw_rows, h, lg, lb]
    aliases = {}
    if out_prev is not None:
        in_specs.append(pl.BlockSpec(memory_space=pl.ANY))
        aliases = {len(args): 0}
        args.append(out_prev)
    return pl.pallas_call(
        functools.partial(_combine_kernel, alpha),
        grid=(nsteps,),
        in_specs=in_specs,
        out_specs=pl.BlockSpec((tm, d), lambda i: (i + blk0, 0)),
        out_shape=jax.ShapeDtypeStruct((total_rows, d), F32),
        input_output_aliases=aliases,
        compiler_params=_cparams(("parallel",)),
        name="combine_ln2",
    )(*args)


def _pack_in_proj(w_in, d_inner, n_ssm_heads, attn_dim):
    gn = SSM_GROUPS * D_STATE
    kv_dim = KV_HEADS * HEAD_DIM
    d = w_in.shape[0]
    o = 0
    seg = {}
    for name, width in (("z", d_inner), ("x", d_inner), ("B", gn), ("C", gn), ("dt", n_ssm_heads),
                        ("q", attn_dim), ("k", kv_dim), ("v", kv_dim), ("gs", d), ("ga", d)):
        seg[name] = w_in[:, o:o + width]
        o += width
    assert o == w_in.shape[1]
    dt_pad = jnp.pad(seg["dt"], ((0, 0), (0, LANES - n_ssm_heads)))
    order = (("z", seg["z"]), ("x", seg["x"]), ("gs", seg["gs"]), ("ga", seg["ga"]),
             ("B", seg["B"]), ("C", seg["C"]), ("dt", dt_pad))
    col, off = {}, 0
    for name, w in order:
        col[name] = off
        assert off % w.shape[1] == 0
        off += w.shape[1]
    w_f32grp = jnp.concatenate([w for _, w in order], axis=1).astype(BF16)
    w_qkv = jnp.concatenate([seg["q"], seg["k"], seg["v"]], axis=1).astype(BF16)
    return w_f32grp, w_qkv, col


def _largest_tile(n, cap):
    best = LANES
    for k in range(1, n // LANES + 1):
        if n % (k * LANES) == 0 and k * LANES <= cap:
            best = k * LANES
    return best


def _layer(h_in, batch, seq, alpha, w_in, conv_w, conv_b, dt_bias, a_log, d_skip, ssm_norm_w,
           w_ssm_out, attn_sinks, w_attn_out, b_gates, w_mix_out, ln1_g, ln1_b, w_router, b_router,
           w_gate_up, b_gate_up, w_down, b_down, ln2_g, ln2_b):
    t, d = h_in.shape
    d_inner = ssm_norm_w.shape[0]
    n_ssm_heads = dt_bias.shape[0]
    attn_dim = w_attn_out.shape[0]
    n_heads = attn_sinks.shape[0]
    n_exp = w_router.shape[1]
    gn = SSM_GROUPS * D_STATE

    w_f32grp, w_qkv, col = _pack_in_proj(w_in, d_inner, n_ssm_heads, attn_dim)
    proj = _matmul(h_in, w_f32grp, F32, 1024, _largest_tile(w_f32grp.shape[1], 2560))
    qkv = _matmul(h_in, w_qkv, BF16, 1024, _largest_tile(w_qkv.shape[1], 1536))

    pad_h = (0, LANES - n_ssm_heads)
    y_ssm = _ssd(proj, batch, seq, col,
                 conv_w[:, :d_inner], conv_w[:, d_inner:d_inner + gn], conv_w[:, d_inner + gn:],
                 conv_b[None, :d_inner], conv_b[None, d_inner:d_inner + gn], conv_b[None, d_inner + gn:],
                 jnp.pad(dt_bias, pad_h)[None, :], jnp.pad(a_log, pad_h)[None, :],
                 jnp.repeat(d_skip, SSM_HEAD_DIM)[None, :], ssm_norm_w[None, :])
    y_att = _swa(qkv, attn_sinks, batch, seq, n_heads)

    wso, wao, wmix = w_ssm_out.astype(BF16), w_attn_out.astype(BF16), w_mix_out.astype(BF16)
    w_router_t = w_router.T

    n_parts = MOE_SPLITS if t % (MOE_SPLITS * SC_CORES * SC_SUBCORES * SC_GATHER_CHUNK) == 0 else 1
    rows = t // n_parts
    out = None
    for part in range(n_parts):
        row0 = part * rows
        h1, h1_packed = _merge(alpha, y_ssm, y_att, proj, col, h_in, wso, wao, wmix, b_gates[None, :d],
                               b_gates[None, d:], ln1_g[None, :], ln1_b[None, :], 512, row0, rows)
        idx_kt, gw_kt, rank_kt, counts = _router(h1, w_router_t, b_router[:, None], 1024)

        counts = counts[:, 0]
        padded = (counts + MOE_BLOCK - 1) // MOE_BLOCK * MOE_BLOCK
        pad_end = jnp.cumsum(padded)
        pad_start = pad_end - padded
        n_blocks = -(-(rows * TOP_K) // MOE_BLOCK) + n_exp
        n_rows = n_blocks * MOE_BLOCK
        expert_ids = jnp.arange(n_exp, dtype=jnp.int32)
        pad_start_of_slot = jnp.sum(
            jnp.where(idx_kt[None] == expert_ids[:, None, None], pad_start[:, None, None], 0), axis=0)
        dest_kt = pad_start_of_slot + rank_kt
        fill_rows = (pad_start + counts)[:, None] + jnp.arange(MOE_BLOCK, dtype=jnp.int32)[None, :]
        spare_rows = n_rows + jnp.arange(n_exp * MOE_BLOCK, dtype=jnp.int32).reshape(n_exp, MOE_BLOCK)
        fill_idx = jnp.where(fill_rows < pad_end[:, None], fill_rows, spare_rows).reshape(-1)
        block_row0 = jnp.arange(n_blocks, dtype=jnp.int32) * MOE_BLOCK
        block_expert = jnp.minimum(
            jnp.sum((pad_end[None, :] <= block_row0[:, None]).astype(jnp.int32), axis=1), n_exp - 1)
        n_used = (pad_end[-1:] // MOE_BLOCK).astype(jnp.int32)
        gw_rows = jnp.pad(gw_kt.T, ((0, 0), (0, LANES - TOP_K)))

        xs = _sc_scatter_rows(h1_packed, dest_kt, fill_idx, n_rows + n_exp * MOE_BLOCK)
        ys = _experts(xs, block_expert, n_used, w_gate_up, b_gate_up, w_down, b_down)
        y_slots = _sc_gather_rows(ys, dest_kt.reshape(-1))
        out = _combine(alpha, y_slots, gw_rows, h1, ln2_g[None, :], ln2_b[None, :], 512, out, row0, t)
    return out


def kernel(x, w_in, conv_w, conv_b, dt_bias, a_log, d_skip, ssm_norm_w, w_ssm_out, attn_sinks,
           w_attn_out, b_gates, w_mix_out, ln1_g, ln1_b, w_router, b_router, w_gate_up, b_gate_up,
           w_down, b_down, ln2_g, ln2_b):
    batch, seq, d = x.shape
    depth = w_in.shape[0]
    alpha = (2 * depth) ** 0.25
    h = x.reshape(batch * seq, d)
    for i in range(depth):
        h = _layer(h, batch, seq, alpha, w_in[i], conv_w[i], conv_b[i], dt_bias[i], a_log[i],
                   d_skip[i], ssm_norm_w[i], w_ssm_out[i], attn_sinks[i], w_attn_out[i], b_gates[i],
                   w_mix_out[i], ln1_g[i], ln1_b[i], w_router[i], b_router[i], w_gate_up[i],
                   b_gate_up[i], w_down[i], b_down[i], ln2_g[i], ln2_b[i])
    return h.reshape(batch, seq, d)
```

```python
import functools

import jax
import jax.numpy as jnp
from jax import lax
from jax.experimental import pallas as pl
from jax.experimental.pallas import tpu as pltpu
from jax.experimental.pallas import tpu_sc as plsc

SSM_HEAD_DIM = 64
SSM_GROUPS = 4
D_STATE = 128
CONV_WIDTH = 4
SSD_CHUNK = 128
SEQS_PER_STEP = 2
KV_HEADS = 4
HEAD_DIM = 64
WINDOW = 128
TOP_K = 4
SWIGLU_LIMIT = 7.0
SWIGLU_ALPHA = 1.702
MOE_BLOCK = 512
LN_EPS = 1e-5
RMS_EPS = 1e-5

LANES = 128
SUBLANES = 8
NEG_BIG = -1e30
LOG2_E = 1.4426950408889634
F32 = jnp.float32
BF16 = jnp.bfloat16
VMEM_LIMIT = 56 * 1024 * 1024


def _cparams(sem):
    return pltpu.CompilerParams(dimension_semantics=sem, vmem_limit_bytes=VMEM_LIMIT)


def _sigmoid(x):
    return 0.5 + 0.5 * jnp.tanh(0.5 * x)


def _pack_bf16_pair(v):
    w = v.shape[1] // 2
    bits = lax.bitcast_convert_type(v.astype(BF16).astype(F32), jnp.uint32)
    return (bits[:, :w] >> 16) | (bits[:, w:] & jnp.uint32(0xFFFF0000))


def _unpack_bf16_pair(p):
    lo = lax.bitcast_convert_type(p << 16, F32)
    hi = lax.bitcast_convert_type(p & jnp.uint32(0xFFFF0000), F32)
    return jnp.concatenate([lo, hi], axis=1)


def _silu(x):
    h = 0.5 * x
    return h + h * jnp.tanh(h)


def _mm_kernel(x_ref, w_ref, o_ref):
    o_ref[...] = jnp.dot(x_ref[...].astype(BF16), w_ref[...],
                         preferred_element_type=F32).astype(o_ref.dtype)


def _matmul(x, w, out_dtype, tm, tn):
    m, k = x.shape
    n = w.shape[1]
    tm = min(tm, m)
    assert m % tm == 0 and n % tn == 0
    return pl.pallas_call(
        _mm_kernel,
        grid=(n // tn, m // tm),
        in_specs=[pl.BlockSpec((tm, k), lambda j, i: (i, 0)),
                  pl.BlockSpec((k, tn), lambda j, i: (0, j))],
        out_specs=pl.BlockSpec((tm, tn), lambda j, i: (i, j)),
        out_shape=jax.ShapeDtypeStruct((m, n), out_dtype),
        compiler_params=_cparams(("parallel", "parallel")),
        name="in_proj",
    )(x, w)


def _conv_silu(u_ref, prev_ref, w_ref, b_ref, first):
    cur = u_ref[...]
    prev = prev_ref[...]
    prev = jnp.where(first, jnp.zeros_like(prev), prev)
    row8 = lax.broadcasted_iota(jnp.int32, prev.shape, 0)
    acc = b_ref[...] + w_ref[CONV_WIDTH - 1:CONV_WIDTH, :] * cur
    for k in range(1, CONV_WIDTH):
        rolled = pltpu.roll(cur, k, axis=0)
        head = jnp.where(row8 < k, pltpu.roll(prev, k, axis=0), rolled[:SUBLANES])
        shifted = jnp.concatenate([head, rolled[SUBLANES:]], axis=0)
        j = CONV_WIDTH - 1 - k
        acc = acc + w_ref[j:j + 1, :] * shifted
    return _silu(acc)


def _pair_expand(v, h0, lo_mask):
    n = v.shape[0]
    a = jnp.broadcast_to(v[:, h0:h0 + 1], (n, LANES))
    b = jnp.broadcast_to(v[:, h0 + 1:h0 + 2], (n, LANES))
    return jnp.where(lo_mask, a, b)


def _ssd_kernel(z_ref, x_ref, b_ref, c_ref, dt_ref, xp_ref, bp_ref, cp_ref,
                cwx_ref, cwb_ref, cwc_ref, cbx_ref, cbb_ref, cbc_ref,
                dtb_ref, alog_ref, dsk_ref, nw_ref,
                o_ref, st_ref):
    first = pl.program_id(1) == 0

    @pl.when(first)
    def _():
        st_ref[...] = jnp.zeros_like(st_ref)

    for s in range(z_ref.shape[0]):
        _ssd_chunk(z_ref.at[s], x_ref.at[s], b_ref.at[s], c_ref.at[s], dt_ref.at[s],
                   xp_ref.at[s], bp_ref.at[s], cp_ref.at[s],
                   cwx_ref, cwb_ref, cwc_ref, cbx_ref, cbb_ref, cbc_ref,
                   dtb_ref, alog_ref, dsk_ref, nw_ref, o_ref.at[s], st_ref.at[s], first)


def _ssd_chunk(z_ref, x_ref, b_ref, c_ref, dt_ref, xp_ref, bp_ref, cp_ref,
               cwx_ref, cwb_ref, cwc_ref, cbx_ref, cbb_ref, cbc_ref,
               dtb_ref, alog_ref, dsk_ref, nw_ref, o_ref, st_ref, first):
    n = SSD_CHUNK
    heads_per_group = st_ref.shape[2] // SSM_HEAD_DIM
    gw = heads_per_group * SSM_HEAD_DIM

    xc = _conv_silu(x_ref, xp_ref, cwx_ref, cbx_ref, first)
    bc = _conv_silu(b_ref, bp_ref, cwb_ref, cbb_ref, first)
    cc = _conv_silu(c_ref, cp_ref, cwc_ref, cbc_ref, first)

    dt_in = dt_ref[...] + dtb_ref[...]
    dt = jnp.maximum(dt_in, 0.0) + jnp.log(1.0 + jnp.exp(-jnp.abs(dt_in)))
    a = dt * (-jnp.exp(alog_ref[...]))
    row = lax.broadcasted_iota(jnp.int32, (n, n), 0)
    col = lax.broadcasted_iota(jnp.int32, (n, n), 1)
    causal = row >= col
    tri = jnp.where(causal, 1.0, 0.0).astype(F32)
    acum = jnp.dot(tri, a, preferred_element_type=F32, precision=lax.Precision.HIGHEST)
    acum2 = acum * LOG2_E
    acum2_t = acum2.T
    a_last = acum[n - 1:n, :]
    d2e = jnp.exp(a_last - acum)
    eac = jnp.exp(acum)
    lo_mask = lax.broadcasted_iota(jnp.int32, (n, LANES), 1) < SSM_HEAD_DIM

    for g in range(SSM_GROUPS):
        xg = xc[:, g * gw:(g + 1) * gw]
        bg = bc[:, g * D_STATE:(g + 1) * D_STATE]
        cg = cc[:, g * D_STATE:(g + 1) * D_STATE]
        cg_b = cg.astype(BF16)
        bg_t = bg.T.astype(BF16)
        cb = jnp.dot(cg_b, bg_t, preferred_element_type=F32)
        st_prev = st_ref[g]
        y_off = jnp.dot(cg_b, st_prev.astype(BF16), preferred_element_type=F32)
        y_parts, xw_parts, ea_parts = [], [], []
        for j in range(heads_per_group // 2):
            h0 = g * heads_per_group + 2 * j
            lanes = slice(j * LANES, (j + 1) * LANES)
            xs_pair = xg[:, lanes] * _pair_expand(dt, h0, lo_mask)
            xs_b = xs_pair.astype(BF16)
            ea_pair = _pair_expand(eac, h0, lo_mask)
            xw_parts.append((xs_pair * _pair_expand(d2e, h0, lo_mask)).astype(BF16))
            ea_parts.append(ea_pair)
            ys = []
            for h in (h0, h0 + 1):
                seg2 = acum2[:, h:h + 1] - acum2_t[h:h + 1, :]
                decay = jnp.exp2(jnp.where(causal, seg2, NEG_BIG))
                m = (cb * decay).astype(BF16)
                ys.append(jnp.dot(m, xs_b, preferred_element_type=F32))
            y_parts.append(jnp.where(lo_mask, ys[0], ys[1]))
        y_diag = jnp.concatenate(y_parts, axis=1)
        ea = jnp.concatenate(ea_parts, axis=1)
        xw = jnp.concatenate(xw_parts, axis=1)
        cols = slice(g * gw, (g + 1) * gw)
        y = y_diag + y_off * ea + xg * dsk_ref[:, cols]
        zg = z_ref[:, cols]
        y = y * _silu(zg)
        ms = jnp.mean(y * y, axis=-1, keepdims=True)
        o_ref[:, cols] = (y * lax.rsqrt(ms + RMS_EPS) * nw_ref[:, cols]).astype(o_ref.dtype)
        st_ref[g] = st_prev * ea[n - 1:n, :] + jnp.dot(bg_t, xw, preferred_element_type=F32)


def _ssd(proj, batch, seq, col, conv_wx, conv_wb, conv_wc, conv_bx, conv_bb, conv_bc,
         dt_bias, a_log, d_skip, norm_w):
    t, width = proj.shape
    d_inner = norm_w.shape[1]
    gn = SSM_GROUPS * D_STATE
    n = SSD_CHUNK
    nc = seq // n
    gw = d_inner // SSM_GROUPS
    ns = SEQS_PER_STEP if batch % SEQS_PER_STEP == 0 else 1
    proj3 = proj.reshape(batch, seq, width)

    def prev_rows(c):
        return jnp.maximum(c * (n // SUBLANES) - 1, 0)

    def const(b, c):
        return (0, 0)

    in_specs = [
        pl.BlockSpec((ns, n, d_inner), lambda b, c: (b, c, col["z"] // d_inner)),
        pl.BlockSpec((ns, n, d_inner), lambda b, c: (b, c, col["x"] // d_inner)),
        pl.BlockSpec((ns, n, gn), lambda b, c: (b, c, col["B"] // gn)),
        pl.BlockSpec((ns, n, gn), lambda b, c: (b, c, col["C"] // gn)),
        pl.BlockSpec((ns, n, LANES), lambda b, c: (b, c, col["dt"] // LANES)),
        pl.BlockSpec((ns, SUBLANES, d_inner), lambda b, c: (b, prev_rows(c), col["x"] // d_inner)),
        pl.BlockSpec((ns, SUBLANES, gn), lambda b, c: (b, prev_rows(c), col["B"] // gn)),
        pl.BlockSpec((ns, SUBLANES, gn), lambda b, c: (b, prev_rows(c), col["C"] // gn)),
        pl.BlockSpec((CONV_WIDTH, d_inner), const),
        pl.BlockSpec((CONV_WIDTH, gn), const),
        pl.BlockSpec((CONV_WIDTH, gn), const),
        pl.BlockSpec((1, d_inner), const),
        pl.BlockSpec((1, gn), const),
        pl.BlockSpec((1, gn), const),
        pl.BlockSpec((1, LANES), const),
        pl.BlockSpec((1, LANES), const),
        pl.BlockSpec((1, d_inner), const),
        pl.BlockSpec((1, d_inner), const),
    ]
    out = pl.pallas_call(
        _ssd_kernel,
        grid=(batch // ns, nc),
        in_specs=in_specs,
        out_specs=pl.BlockSpec((ns, n, d_inner), lambda b, c: (b, c, 0)),
        out_shape=jax.ShapeDtypeStruct((batch, seq, d_inner), BF16),
        scratch_shapes=[pltpu.VMEM((ns, SSM_GROUPS, D_STATE, gw), F32)],
        compiler_params=_cparams(("arbitrary", "arbitrary")),
        name="ssd_mixer",
    )(proj3, proj3, proj3, proj3, proj3, proj3, proj3, proj3,
      conv_wx, conv_wb, conv_wc, conv_bx, conv_bb, conv_bc, dt_bias, a_log, d_skip, norm_w)
    return out.reshape(t, d_inner)


def _swa_kernel(sink_ref, q_ref, kc_ref, vc_ref, kp_ref, vp_ref, o_ref):
    for s in range(q_ref.shape[0]):
        _swa_block(sink_ref, q_ref.at[s], kc_ref.at[s], vc_ref.at[s], kp_ref.at[s], vp_ref.at[s],
                   o_ref.at[s])


def _swa_block(sink_ref, q_ref, kc_ref, vc_ref, kp_ref, vp_ref, o_ref):
    i = pl.program_id(1)
    n = WINDOW
    n_heads = q_ref.shape[1] // HEAD_DIM
    grp = n_heads // KV_HEADS
    qpos = lax.broadcasted_iota(jnp.int32, (n, 2 * n), 0) + n
    kpos = lax.broadcasted_iota(jnp.int32, (n, 2 * n), 1)
    diff = qpos - kpos
    mask = (diff >= 0) & (diff < n) & ((kpos >= n) | (i > 0))
    lo_q = lax.broadcasted_iota(jnp.int32, (n, LANES), 1) < HEAD_DIM
    lo_k = lax.broadcasted_iota(jnp.int32, (2 * n, LANES), 1) < HEAD_DIM
    scale = HEAD_DIM ** -0.5
    exp2_scale = scale * LOG2_E
    for p in range(KV_HEADS // 2):
        lanes = slice(p * LANES, (p + 1) * LANES)
        kk = jnp.concatenate([kp_ref[:, lanes], kc_ref[:, lanes]], axis=0).astype(F32)
        vv = jnp.concatenate([vp_ref[:, lanes], vc_ref[:, lanes]], axis=0).astype(F32)
        kk_sw = pltpu.roll(kk, HEAD_DIM, axis=1)
        vv_sw = pltpu.roll(vv, HEAD_DIM, axis=1)
        for par in range(2):
            g = 2 * p + par
            k2 = (jnp.where(lo_k, kk, kk_sw) if par == 0 else jnp.where(lo_k, kk_sw, kk)).astype(BF16)
            v2 = (jnp.where(lo_k, vv, vv_sw) if par == 0 else jnp.where(lo_k, vv_sw, vv)).astype(BF16)
            for qp in range(grp // 2):
                h0 = g * grp + 2 * qp
                qlanes = slice((h0 // 2) * LANES, (h0 // 2 + 1) * LANES)
                q_pair = q_ref[:, qlanes]
                outs = []
                for hh in range(2):
                    keep = lo_q if hh == 0 else jnp.logical_not(lo_q)
                    qm = jnp.where(keep, q_pair, jnp.zeros_like(q_pair))
                    s = lax.dot_general(qm, k2, (((1,), (1,)), ((), ())), preferred_element_type=F32)
                    s = jnp.where(mask, s, NEG_BIG)
                    sink = sink_ref[h0 + hh] / scale
                    mx = jnp.maximum(jnp.max(s, axis=-1, keepdims=True), sink)
                    pr = jnp.exp2((s - mx) * exp2_scale)
                    den = jnp.sum(pr, axis=-1, keepdims=True) + jnp.exp2((sink - mx) * exp2_scale)
                    o = jnp.dot(pr.astype(BF16), v2, preferred_element_type=F32)
                    outs.append(o / den)
                o_ref[:, qlanes] = jnp.where(lo_q, outs[0], outs[1]).astype(o_ref.dtype)


def _swa(qkv, sinks, batch, seq, n_heads):
    t, width = qkv.shape
    n = WINDOW
    nb = seq // n
    qw = n_heads * HEAD_DIM
    kw = KV_HEADS * HEAD_DIM
    k_blk = qw // kw
    v_blk = k_blk + 1
    ns = SEQS_PER_STEP if batch % SEQS_PER_STEP == 0 else 1
    qkv3 = qkv.reshape(batch, seq, width)

    def prev(i):
        return jnp.maximum(i - 1, 0)

    out = pl.pallas_call(
        _swa_kernel,
        grid=(batch // ns, nb),
        in_specs=[pl.BlockSpec(memory_space=pltpu.SMEM),
                  pl.BlockSpec((ns, n, qw), lambda b, i: (b, i, 0)),
                  pl.BlockSpec((ns, n, kw), lambda b, i: (b, i, k_blk)),
                  pl.BlockSpec((ns, n, kw), lambda b, i: (b, i, v_blk)),
                  pl.BlockSpec((ns, n, kw), lambda b, i: (b, prev(i), k_blk)),
                  pl.BlockSpec((ns, n, kw), lambda b, i: (b, prev(i), v_blk))],
        out_specs=pl.BlockSpec((ns, n, qw), lambda b, i: (b, i, 0)),
        out_shape=jax.ShapeDtypeStruct((batch, seq, qw), BF16),
        compiler_params=_cparams(("parallel", "parallel")),
        name="swa",
    )(sinks, qkv3, qkv3, qkv3, qkv3, qkv3)
    return out.reshape(t, qw)


def _layer_norm(v, g, b):
    mu = jnp.mean(v, axis=-1, keepdims=True)
    d = v - mu
    var = jnp.mean(d * d, axis=-1, keepdims=True)
    return d * lax.rsqrt(var + LN_EPS) * g + b


def _merge_kernel(alpha, ys_ref, ya_ref, gs_ref, ga_ref, x_ref, wso_ref, wao_ref, wmix_ref,
                  bgs_ref, bga_ref, lg_ref, lb_ref, h_ref, hp_ref):
    y_ssm = jnp.dot(ys_ref[...], wso_ref[...], preferred_element_type=F32)
    y_att = jnp.dot(ya_ref[...], wao_ref[...], preferred_element_type=F32)
    merged = (_sigmoid(gs_ref[...] + bgs_ref[...]) * y_ssm
              + _sigmoid(ga_ref[...] + bga_ref[...]) * y_att)
    mix = jnp.dot(merged.astype(BF16), wmix_ref[...], preferred_element_type=F32)
    h = _layer_norm(alpha * x_ref[...] + mix, lg_ref[...], lb_ref[...])
    h_ref[...] = h
    hp_ref[...] = _pack_bf16_pair(h)


def _merge(alpha, y_ssm, y_att, proj, col, x, wso, wao, wmix, bgs, bga, lg, lb, tm, row0, rows):
    d = x.shape[1]
    tm = min(tm, rows)
    assert rows % tm == 0 and row0 % tm == 0
    blk0 = row0 // tm
    di = y_ssm.shape[1]
    da = y_att.shape[1]

    def const(i):
        return (0, 0)

    return pl.pallas_call(
        functools.partial(_merge_kernel, alpha),
        grid=(rows // tm,),
        in_specs=[pl.BlockSpec((tm, di), lambda i: (i + blk0, 0)),
                  pl.BlockSpec((tm, da), lambda i: (i + blk0, 0)),
                  pl.BlockSpec((tm, d), lambda i: (i + blk0, col["gs"] // d)),
                  pl.BlockSpec((tm, d), lambda i: (i + blk0, col["ga"] // d)),
                  pl.BlockSpec((tm, d), lambda i: (i + blk0, 0)),
                  pl.BlockSpec((di, d), const),
                  pl.BlockSpec((da, d), const),
                  pl.BlockSpec((d, d), const),
                  pl.BlockSpec((1, d), const),
                  pl.BlockSpec((1, d), const),
                  pl.BlockSpec((1, d), const),
                  pl.BlockSpec((1, d), const)],
        out_specs=[pl.BlockSpec((tm, d), lambda i: (i, 0)),
                   pl.BlockSpec((tm, d // 2), lambda i: (i, 0))],
        out_shape=[jax.ShapeDtypeStruct((rows, d), F32),
                   jax.ShapeDtypeStruct((rows, d // 2), jnp.uint32)],
        compiler_params=_cparams(("parallel",)),
        name="merge_ln1",
    )(y_ssm, y_att, proj, proj, x, wso, wao, wmix, bgs, bga, lg, lb)


def _router_kernel(h_ref, wr_ref, br_ref, idx_ref, gw_ref, rank_ref, cnt_ref, tri_ref, base_ref):
    i = pl.program_id(0)
    tm = h_ref.shape[0]
    n_exp = wr_ref.shape[0]

    @pl.when(i == 0)
    def _():
        r = lax.broadcasted_iota(jnp.int32, (tm, tm), 0)
        c = lax.broadcasted_iota(jnp.int32, (tm, tm), 1)
        tri_ref[...] = jnp.where(r < c, 1.0, 0.0).astype(BF16)
        base_ref[...] = jnp.zeros_like(base_ref)

    logits = lax.dot_general(wr_ref[...], h_ref[...], (((1,), (1,)), ((), ())),
                             preferred_element_type=F32,
                             precision=lax.Precision.HIGHEST) + br_ref[...]
    eid = lax.broadcasted_iota(jnp.int32, (n_exp, tm), 0)
    work = logits
    vals, idxs = [], []
    for _ in range(TOP_K):
        mx = jnp.max(work, axis=0, keepdims=True)
        sel = jnp.min(jnp.where(work == mx, eid, n_exp), axis=0, keepdims=True)
        vals.append(mx)
        idxs.append(sel)
        work = jnp.where(eid == sel, -jnp.inf, work)
    exps = [jnp.exp(v - vals[0]) for v in vals]
    den = exps[0]
    for e in exps[1:]:
        den = den + e
    base = base_ref[...]
    for k in range(TOP_K):
        onehot = eid == idxs[k]
        oh = jnp.where(onehot, 1.0, 0.0)
        before = jnp.dot(oh.astype(BF16), tri_ref[...], preferred_element_type=F32)
        rank = jnp.sum(jnp.where(onehot, before + base, 0.0), axis=0, keepdims=True)
        idx_ref[k:k + 1, :] = idxs[k]
        gw_ref[k:k + 1, :] = exps[k] / den
        rank_ref[k:k + 1, :] = rank.astype(jnp.int32)
        base = base + jnp.sum(oh, axis=1, keepdims=True)
    base_ref[...] = base
    cnt_ref[...] = jnp.broadcast_to(base, cnt_ref.shape).astype(jnp.int32)


def _router(h, w_router_t, b_router, tm):
    t, d = h.shape
    n_exp = w_router_t.shape[0]
    tm = min(tm, t)
    outs = pl.pallas_call(
        _router_kernel,
        grid=(t // tm,),
        in_specs=[pl.BlockSpec((tm, d), lambda i: (i, 0)),
                  pl.BlockSpec((n_exp, d), lambda i: (0, 0)),
                  pl.BlockSpec((n_exp, 1), lambda i: (0, 0))],
        out_specs=[pl.BlockSpec((TOP_K, tm), lambda i: (0, i)),
                   pl.BlockSpec((TOP_K, tm), lambda i: (0, i)),
                   pl.BlockSpec((TOP_K, tm), lambda i: (0, i)),
                   pl.BlockSpec((n_exp, LANES), lambda i: (0, 0))],
        out_shape=[jax.ShapeDtypeStruct((TOP_K, t), jnp.int32),
                   jax.ShapeDtypeStruct((TOP_K, t), F32),
                   jax.ShapeDtypeStruct((TOP_K, t), jnp.int32),
                   jax.ShapeDtypeStruct((n_exp, LANES), jnp.int32)],
        scratch_shapes=[pltpu.VMEM((tm, tm), BF16), pltpu.VMEM((n_exp, 1), F32)],
        compiler_params=_cparams(("arbitrary",)),
        name="router",
    )(h, w_router_t, b_router)
    return outs


SC_CORES = 2
SC_SUBCORES = 16
SC_GATHER_CHUNK = 128
MOE_SPLITS = 2


def _sc_gather_rows(table, idx):
    b = idx.shape[0]
    d = table.shape[1]
    n_workers = SC_CORES * SC_SUBCORES
    chunk = SC_GATHER_CHUNK
    assert b % (n_workers * chunk) == 0
    per_worker = b // n_workers
    n_chunks = per_worker // chunk
    mesh = plsc.VectorSubcoreMesh(core_axis_name="c", subcore_axis_name="s",
                                  num_cores=SC_CORES, num_subcores=SC_SUBCORES)

    def body(table_hbm, idx_hbm, out_hbm, idx_v, rows_v, sem):
        wid = lax.axis_index("s") * SC_CORES + lax.axis_index("c")
        base = wid * per_worker

        @pl.loop(0, n_chunks)
        def _(c):
            off = pl.multiple_of(base + c * chunk, SUBLANES)
            pltpu.sync_copy(idx_hbm.at[pl.ds(off, chunk)], idx_v)
            pltpu.async_copy(table_hbm.at[idx_v], rows_v, sem).wait()
            pltpu.sync_copy(rows_v, out_hbm.at[pl.ds(off, chunk)])

    return pl.kernel(
        body,
        out_type=jax.ShapeDtypeStruct((b, d), table.dtype),
        mesh=mesh,
        scratch_types=[pltpu.VMEM((chunk,), jnp.int32),
                       pltpu.VMEM((chunk, d), table.dtype),
                       pltpu.SemaphoreType.DMA],
        name="sc_gather_rows",
    )(table, idx)


def _sc_scatter_rows(src, dest_kt, fill_idx, n_out_rows):
    t, d = src.shape
    n_k = dest_kt.shape[0]
    n_fill = fill_idx.shape[0]
    n_workers = SC_CORES * SC_SUBCORES
    chunk = SC_GATHER_CHUNK
    assert t % (n_workers * chunk) == 0 and n_fill % (n_workers * chunk) == 0
    tok_per_worker = t // n_workers
    fill_per_worker = n_fill // n_workers
    mesh = plsc.VectorSubcoreMesh(core_axis_name="c", subcore_axis_name="s",
                                  num_cores=SC_CORES, num_subcores=SC_SUBCORES)

    def body(src_hbm, dest_hbm, fill_hbm, zeros_hbm, out_hbm, idx_v, rows_v):
        wid = lax.axis_index("s") * SC_CORES + lax.axis_index("c")

        pltpu.sync_copy(zeros_hbm, rows_v)

        @pl.loop(0, fill_per_worker // chunk)
        def _(c):
            off = pl.multiple_of(wid * fill_per_worker + c * chunk, SUBLANES)
            pltpu.sync_copy(fill_hbm.at[pl.ds(off, chunk)], idx_v)
            pltpu.sync_copy(rows_v, out_hbm.at[idx_v])

        @pl.loop(0, tok_per_worker // chunk)
        def _(c):
            off = pl.multiple_of(wid * tok_per_worker + c * chunk, SUBLANES)
            pltpu.sync_copy(src_hbm.at[pl.ds(off, chunk)], rows_v)
            for k in range(n_k):
                pltpu.sync_copy(dest_hbm.at[pl.ds(k * t + off, chunk)], idx_v)
                pltpu.sync_copy(rows_v, out_hbm.at[idx_v])

    return pl.kernel(
        body,
        out_type=jax.ShapeDtypeStruct((n_out_rows, d), src.dtype),
        mesh=mesh,
        scratch_types=[pltpu.VMEM((chunk,), jnp.int32),
                       pltpu.VMEM((chunk, d), src.dtype)],
        name="sc_scatter_rows",
    )(src, dest_kt.reshape(-1), fill_idx, jnp.zeros((chunk, d), src.dtype))


EXPERT_SUBROWS = 128


def _expert_kernel(be_ref, nxt_ref, eslot_ref, nsub_ref, nu_ref, x_ref, wgu_hbm, bgu_ref, wd_hbm,
                   bd_ref, y_ref, wgu_f, wd_f, wsem, wgu_b, wd_b):
    i = pl.program_id(0)
    n_used = nu_ref[0]
    expert = be_ref[i]
    changed = jnp.logical_or(i == 0, expert != be_ref[jnp.maximum(i - 1, 0)])

    def weight_copies(e, slot):
        return (pltpu.make_async_copy(wgu_hbm.at[e], wgu_f.at[slot], wsem.at[0, slot]),
                pltpu.make_async_copy(wd_hbm.at[e], wd_f.at[slot], wsem.at[1, slot]))

    @pl.when(jnp.logical_and(changed, i < n_used))
    def _():
        slot = eslot_ref[i]

        @pl.when(i == 0)
        def _():
            for cp in weight_copies(expert, slot):
                cp.start()

        for cp in weight_copies(expert, slot):
            cp.wait()
        nxt = nxt_ref[i]

        @pl.when(nxt >= 0)
        def _():
            for cp in weight_copies(nxt, 1 - slot):
                cp.start()

        wgu_b[...] = wgu_f[slot].astype(BF16)
        wd_b[...] = wd_f[slot].astype(BF16)

    def ffn(rows):
        d_ff = wd_b.shape[0]
        xb = _unpack_bf16_pair(x_ref[0:rows, :]).astype(BF16)
        hgu = jnp.dot(xb, wgu_b[...], preferred_element_type=F32) + bgu_ref[0]
        glu = jnp.minimum(hgu[:, :d_ff], SWIGLU_LIMIT)
        lin = jnp.clip(hgu[:, d_ff:], -SWIGLU_LIMIT, SWIGLU_LIMIT)
        act = glu * _sigmoid(SWIGLU_ALPHA * glu) * (lin + 1.0)
        y = jnp.dot(act.astype(BF16), wd_b[...], preferred_element_type=F32) + bd_ref[0]
        y_ref[0:rows, :] = _pack_bf16_pair(y)
        if rows < y_ref.shape[0]:
            y_ref[rows:, :] = jnp.zeros((y_ref.shape[0] - rows, y_ref.shape[1]), y_ref.dtype)

    n_sub = nsub_ref[i]
    for sub in range(1, y_ref.shape[0] // EXPERT_SUBROWS + 1):
        @pl.when(jnp.logical_and(i < n_used, n_sub == sub))
        def _(sub=sub):
            ffn(sub * EXPERT_SUBROWS)

    @pl.when(i >= n_used)
    def _():
        y_ref[...] = jnp.zeros_like(y_ref)


def _experts(xs, block_expert, block_valid, n_used, wgu, bgu, wd, bd):
    d = wgu.shape[1]
    dp = xs.shape[1]
    assert 2 * dp == d
    n_blocks = block_expert.shape[0]
    n_exp, _, two_f = wgu.shape
    d_ff = wd.shape[1]
    rows = MOE_BLOCK
    n_rows = n_blocks * rows
    blk = jnp.arange(n_blocks, dtype=jnp.int32)
    is_first = jnp.concatenate([jnp.ones((1,), bool), block_expert[1:] != block_expert[:-1]])
    first_used = jnp.logical_and(is_first, blk < n_used[0])
    first_pos = jnp.where(first_used, blk, n_blocks)
    next_first = jnp.concatenate([lax.cummin(first_pos[::-1])[::-1][1:],
                                  jnp.full((1,), n_blocks, jnp.int32)])
    next_expert = jnp.where(next_first < n_blocks,
                            block_expert[jnp.minimum(next_first, n_blocks - 1)], -1).astype(jnp.int32)
    expert_slot = ((jnp.cumsum(is_first.astype(jnp.int32)) - 1) % 2).astype(jnp.int32)
    n_sub = jnp.clip(-(-block_valid // EXPERT_SUBROWS), 1, rows // EXPERT_SUBROWS).astype(jnp.int32)

    def last_used(i, be, nxt, es, ns, nu):
        return (jnp.minimum(i, jnp.maximum(nu[0] - 1, 0)), 0)

    grid_spec = pltpu.PrefetchScalarGridSpec(
        num_scalar_prefetch=5,
        grid=(n_blocks,),
        in_specs=[
            pl.BlockSpec((rows, dp), last_used),
            pl.BlockSpec(memory_space=pl.ANY),
            pl.BlockSpec((1, 1, two_f), lambda i, be, nxt, es, ns, nu: (be[i], 0, 0)),
            pl.BlockSpec(memory_space=pl.ANY),
            pl.BlockSpec((1, 1, d), lambda i, be, nxt, es, ns, nu: (be[i], 0, 0)),
        ],
        out_specs=pl.BlockSpec((rows, dp), lambda i, be, nxt, es, ns, nu: (i, 0)),
        scratch_shapes=[pltpu.VMEM((2, d, two_f), F32),
                        pltpu.VMEM((2, d_ff, d), F32),
                        pltpu.SemaphoreType.DMA((2, 2)),
                        pltpu.VMEM((d, two_f), BF16),
                        pltpu.VMEM((d_ff, d), BF16)],
    )
    return pl.pallas_call(
        _expert_kernel,
        grid_spec=grid_spec,
        out_shape=jax.ShapeDtypeStruct((n_rows, dp), jnp.uint32),
        compiler_params=_cparams(("arbitrary",)),
        name="experts",
    )(block_expert, next_expert, expert_slot, n_sub, n_used, xs, wgu, bgu.reshape(n_exp, 1, two_f), wd,
      bd.reshape(n_exp, 1, d))


def _combine_kernel(alpha, *refs):
    ys_refs = refs[:TOP_K]
    gw_ref, h_ref, lg_ref, lb_ref = refs[TOP_K:TOP_K + 4]
    o_ref = refs[-1]
    gw = gw_ref[...]
    acc = alpha * h_ref[...]
    for k in range(TOP_K):
        acc = acc + gw[:, k:k + 1] * _unpack_bf16_pair(ys_refs[k][...])
    o_ref[...] = _layer_norm(acc, lg_ref[...], lb_ref[...])


def _combine(alpha, y_slots, gw_rows, h, lg, lb, tm, out_prev, row0, total_rows):
    rows, d = h.shape
    tm = min(tm, rows)
    assert rows % tm == 0 and row0 % tm == 0
    nsteps = rows // tm
    blk0 = row0 // tm
    ys_specs = [pl.BlockSpec((tm, d // 2), functools.partial(lambda i, k: (k * nsteps + i, 0), k=k))
                for k in range(TOP_K)]
    in_specs = ys_specs + [pl.BlockSpec((tm, LANES), lambda i: (i, 0)),
                           pl.BlockSpec((tm, d), lambda i: (i, 0)),
                           pl.BlockSpec((1, d), lambda i: (0, 0)),
                           pl.BlockSpec((1, d), lambda i: (0, 0))]
    args = [y_slots] * TOP_K + [gw_rows, h, lg, lb]
    aliases = {}
    if out_prev is not None:
        in_specs.append(pl.BlockSpec(memory_space=pl.ANY))
        aliases = {len(args): 0}
        args.append(out_prev)
    return pl.pallas_call(
        functools.partial(_combine_kernel, alpha),
        grid=(nsteps,),
        in_specs=in_specs,
        out_specs=pl.BlockSpec((tm, d), lambda i: (i + blk0, 0)),
        out_shape=jax.ShapeDtypeStruct((total_rows, d), F32),
        input_output_aliases=aliases,
        compiler_params=_cparams(("parallel",)),
        name="combine_ln2",
    )(*args)


def _pack_in_proj(w_in, d_inner, n_ssm_heads, attn_dim):
    gn = SSM_GROUPS * D_STATE
    kv_dim = KV_HEADS * HEAD_DIM
    d = w_in.shape[0]
    o = 0
    seg = {}
    for name, width in (("z", d_inner), ("x", d_inner), ("B", gn), ("C", gn), ("dt", n_ssm_heads),
                        ("q", attn_dim), ("k", kv_dim), ("v", kv_dim), ("gs", d), ("ga", d)):
        seg[name] = w_in[:, o:o + width]
        o += width
    assert o == w_in.shape[1]
    dt_pad = jnp.pad(seg["dt"], ((0, 0), (0, LANES - n_ssm_heads)))
    order = (("z", seg["z"]), ("x", seg["x"]), ("gs", seg["gs"]), ("ga", seg["ga"]),
             ("B", seg["B"]), ("C", seg["C"]), ("dt", dt_pad))
    col, off = {}, 0
    for name, w in order:
        col[name] = off
        assert off % w.shape[1] == 0
        off += w.shape[1]
    w_f32grp = jnp.concatenate([w for _, w in order], axis=1).astype(BF16)
    w_qkv = jnp.concatenate([seg["q"], seg["k"], seg["v"]], axis=1).astype(BF16)
    return w_f32grp, w_qkv, col


def _largest_tile(n, cap):
    best = LANES
    for k in range(1, n // LANES + 1):
        if n % (k * LANES) == 0 and k * LANES <= cap:
            best = k * LANES
    return best


def _layer(h_in, batch, seq, alpha, w_in, conv_w, conv_b, dt_bias, a_log, d_skip, ssm_norm_w,
           w_ssm_out, attn_sinks, w_attn_out, b_gates, w_mix_out, ln1_g, ln1_b, w_router, b_router,
           w_gate_up, b_gate_up, w_down, b_down, ln2_g, ln2_b):
    t, d = h_in.shape
    d_inner = ssm_norm_w.shape[0]
    n_ssm_heads = dt_bias.shape[0]
    attn_dim = w_attn_out.shape[0]
    n_heads = attn_sinks.shape[0]
    n_exp = w_router.shape[1]
    gn = SSM_GROUPS * D_STATE

    w_f32grp, w_qkv, col = _pack_in_proj(w_in, d_inner, n_ssm_heads, attn_dim)
    proj = _matmul(h_in, w_f32grp, F32, 1024, _largest_tile(w_f32grp.shape[1], 2560))
    qkv = _matmul(h_in, w_qkv, BF16, 1024, _largest_tile(w_qkv.shape[1], 1536))

    pad_h = (0, LANES - n_ssm_heads)
    y_ssm = _ssd(proj, batch, seq, col,
                 conv_w[:, :d_inner], conv_w[:, d_inner:d_inner + gn], conv_w[:, d_inner + gn:],
                 conv_b[None, :d_inner], conv_b[None, d_inner:d_inner + gn], conv_b[None, d_inner + gn:],
                 jnp.pad(dt_bias, pad_h)[None, :], jnp.pad(a_log, pad_h)[None, :],
                 jnp.repeat(d_skip, SSM_HEAD_DIM)[None, :], ssm_norm_w[None, :])
    y_att = _swa(qkv, attn_sinks, batch, seq, n_heads)

    wso, wao, wmix = w_ssm_out.astype(BF16), w_attn_out.astype(BF16), w_mix_out.astype(BF16)
    w_router_t = w_router.T

    n_parts = MOE_SPLITS if t % (MOE_SPLITS * SC_CORES * SC_SUBCORES * SC_GATHER_CHUNK) == 0 else 1
    rows = t // n_parts
    out = None
    for part in range(n_parts):
        row0 = part * rows
        h1, h1_packed = _merge(alpha, y_ssm, y_att, proj, col, h_in, wso, wao, wmix, b_gates[None, :d],
                               b_gates[None, d:], ln1_g[None, :], ln1_b[None, :], 512, row0, rows)
        idx_kt, gw_kt, rank_kt, counts = _router(h1, w_router_t, b_router[:, None], 1024)

        counts = counts[:, 0]
        padded = (counts + MOE_BLOCK - 1) // MOE_BLOCK * MOE_BLOCK
        pad_end = jnp.cumsum(padded)
        pad_start = pad_end - padded
        n_blocks = -(-(rows * TOP_K) // MOE_BLOCK) + n_exp
        n_rows = n_blocks * MOE_BLOCK
        expert_ids = jnp.arange(n_exp, dtype=jnp.int32)
        pad_start_of_slot = jnp.sum(
            jnp.where(idx_kt[None] == expert_ids[:, None, None], pad_start[:, None, None], 0), axis=0)
        dest_kt = pad_start_of_slot + rank_kt
        fill_rows = (pad_start + counts)[:, None] + jnp.arange(MOE_BLOCK, dtype=jnp.int32)[None, :]
        spare_rows = n_rows + jnp.arange(n_exp * MOE_BLOCK, dtype=jnp.int32).reshape(n_exp, MOE_BLOCK)
        fill_idx = jnp.where(fill_rows < pad_end[:, None], fill_rows, spare_rows).reshape(-1)
        block_row0 = jnp.arange(n_blocks, dtype=jnp.int32) * MOE_BLOCK
        block_expert = jnp.minimum(
            jnp.sum((pad_end[None, :] <= block_row0[:, None]).astype(jnp.int32), axis=1), n_exp - 1)
        n_used = (pad_end[-1:] // MOE_BLOCK).astype(jnp.int32)
        real_end_of_block = jnp.sum(jnp.where(block_expert[:, None] == expert_ids[None, :],
                                              (pad_start + counts)[None, :], 0), axis=1)
        block_valid = jnp.clip(real_end_of_block - block_row0, 0, MOE_BLOCK)
        gw_rows = jnp.pad(gw_kt.T, ((0, 0), (0, LANES - TOP_K)))

        xs = _sc_scatter_rows(h1_packed, dest_kt, fill_idx, n_rows + n_exp * MOE_BLOCK)
        ys = _experts(xs, block_expert, block_valid, n_used, w_gate_up, b_gate_up, w_down, b_down)
        y_slots = _sc_gather_rows(ys, dest_kt.reshape(-1))
        out = _combine(alpha, y_slots, gw_rows, h1, ln2_g[None, :], ln2_b[None, :], 512, out, row0, t)
    return out


def kernel(x, w_in, conv_w, conv_b, dt_bias, a_log, d_skip, ssm_norm_w, w_ssm_out, attn_sinks,
           w_attn_out, b_gates, w_mix_out, ln1_g, ln1_b, w_router, b_router, w_gate_up, b_gate_up,
           w_down, b_down, ln2_g, ln2_b):
    batch, seq, d = x.shape
    depth = w_in.shape[0]
    alpha = (2 * depth) ** 0.25
    h = x.reshape(batch * seq, d)
    for i in range(depth):
        h = _layer(h, batch, seq, alpha, w_in[i], conv_w[i], conv_b[i], dt_bias[i], a_log[i],
                   d_skip[i], ssm_norm_w[i], w_ssm_out[i], attn_sinks[i], w_attn_out[i], b_gates[i],
                   w_mix_out[i], ln1_g[i], ln1_b[i], w_router[i], b_router[i], w_gate_up[i],
                   b_gate_up[i], w_down[i], b_down[i], ln2_g[i], ln2_b[i])
    return h.reshape(batch, seq, d)
```

```python
import functools

import jax
import jax.numpy as jnp
from jax import lax
from jax.experimental import pallas as pl
from jax.experimental.pallas import tpu as pltpu
from jax.experimental.pallas import tpu_sc as plsc

SSM_HEAD_DIM = 64
SSM_GROUPS = 4
D_STATE = 128
CONV_WIDTH = 4
SSD_CHUNK = 128
SEQS_PER_STEP = 2
KV_HEADS = 4
HEAD_DIM = 64
WINDOW = 128
TOP_K = 4
SWIGLU_LIMIT = 7.0
SWIGLU_ALPHA = 1.702
MOE_BLOCK = 512
LN_EPS = 1e-5
RMS_EPS = 1e-5

LANES = 128
SUBLANES = 8
NEG_BIG = -1e30
LOG2_E = 1.4426950408889634
F32 = jnp.float32
BF16 = jnp.bfloat16
VMEM_LIMIT = 56 * 1024 * 1024


def _cparams(sem):
    return pltpu.CompilerParams(dimension_semantics=sem, vmem_limit_bytes=VMEM_LIMIT)


def _sigmoid(x):
    return 0.5 + 0.5 * jnp.tanh(0.5 * x)


def _pack_bf16_pair(v):
    w = v.shape[1] // 2
    bits = lax.bitcast_convert_type(v.astype(BF16).astype(F32), jnp.uint32)
    return (bits[:, :w] >> 16) | (bits[:, w:] & jnp.uint32(0xFFFF0000))


def _unpack_bf16_pair(p):
    lo = lax.bitcast_convert_type(p << 16, F32)
    hi = lax.bitcast_convert_type(p & jnp.uint32(0xFFFF0000), F32)
    return jnp.concatenate([lo, hi], axis=1)


def _silu(x):
    h = 0.5 * x
    return h + h * jnp.tanh(h)


def _mm_kernel(x_ref, w_ref, o_ref):
    o_ref[...] = jnp.dot(x_ref[...].astype(BF16), w_ref[...],
                         preferred_element_type=F32).astype(o_ref.dtype)


def _matmul(x, w, out_dtype, tm, tn):
    m, k = x.shape
    n = w.shape[1]
    tm = min(tm, m)
    assert m % tm == 0 and n % tn == 0
    return pl.pallas_call(
        _mm_kernel,
        grid=(n // tn, m // tm),
        in_specs=[pl.BlockSpec((tm, k), lambda j, i: (i, 0)),
                  pl.BlockSpec((k, tn), lambda j, i: (0, j))],
        out_specs=pl.BlockSpec((tm, tn), lambda j, i: (i, j)),
        out_shape=jax.ShapeDtypeStruct((m, n), out_dtype),
        compiler_params=_cparams(("parallel", "parallel")),
        name="in_proj",
    )(x, w)


def _conv_silu(u_ref, prev_ref, w_ref, b_ref, first):
    cur = u_ref[...]
    prev = prev_ref[...]
    prev = jnp.where(first, jnp.zeros_like(prev), prev)
    row8 = lax.broadcasted_iota(jnp.int32, prev.shape, 0)
    acc = b_ref[...] + w_ref[CONV_WIDTH - 1:CONV_WIDTH, :] * cur
    for k in range(1, CONV_WIDTH):
        rolled = pltpu.roll(cur, k, axis=0)
        head = jnp.where(row8 < k, pltpu.roll(prev, k, axis=0), rolled[:SUBLANES])
        shifted = jnp.concatenate([head, rolled[SUBLANES:]], axis=0)
        j = CONV_WIDTH - 1 - k
        acc = acc + w_ref[j:j + 1, :] * shifted
    return _silu(acc)


def _expand_columns(v, e3_ref):
    x1 = v.astype(BF16)
    r1 = v - x1.astype(F32)
    x2 = r1.astype(BF16)
    x3 = (r1 - x2.astype(F32)).astype(BF16)
    return jnp.dot(jnp.concatenate([x1, x2, x3], axis=1), e3_ref[...], preferred_element_type=F32)


def _ssd_kernel(z_ref, x_ref, b_ref, c_ref, dt_ref, xp_ref, bp_ref, cp_ref,
                cwx_ref, cwb_ref, cwc_ref, cbx_ref, cbb_ref, cbc_ref,
                dtb_ref, alog_ref, dsk_ref, nw_ref, ehead_ref, ecol_ref,
                o_ref, st_ref):
    first = pl.program_id(1) == 0

    @pl.when(first)
    def _():
        st_ref[...] = jnp.zeros_like(st_ref)

    for s in range(z_ref.shape[0]):
        _ssd_chunk(z_ref.at[s], x_ref.at[s], b_ref.at[s], c_ref.at[s], dt_ref.at[s],
                   xp_ref.at[s], bp_ref.at[s], cp_ref.at[s],
                   cwx_ref, cwb_ref, cwc_ref, cbx_ref, cbb_ref, cbc_ref,
                   dtb_ref, alog_ref, dsk_ref, nw_ref, ehead_ref, ecol_ref,
                   o_ref.at[s], st_ref.at[s], first)


def _ssd_chunk(z_ref, x_ref, b_ref, c_ref, dt_ref, xp_ref, bp_ref, cp_ref,
               cwx_ref, cwb_ref, cwc_ref, cbx_ref, cbb_ref, cbc_ref,
               dtb_ref, alog_ref, dsk_ref, nw_ref, ehead_ref, ecol_ref, o_ref, st_ref, first):
    n = SSD_CHUNK
    heads_per_group = st_ref.shape[2] // SSM_HEAD_DIM
    gw = heads_per_group * SSM_HEAD_DIM

    xc = _conv_silu(x_ref, xp_ref, cwx_ref, cbx_ref, first)
    bc = _conv_silu(b_ref, bp_ref, cwb_ref, cbb_ref, first)
    cc = _conv_silu(c_ref, cp_ref, cwc_ref, cbc_ref, first)

    dt_in = dt_ref[...] + dtb_ref[...]
    dt = jnp.maximum(dt_in, 0.0) + jnp.log(1.0 + jnp.exp(-jnp.abs(dt_in)))
    a = dt * (-jnp.exp(alog_ref[...]))
    row = lax.broadcasted_iota(jnp.int32, (n, n), 0)
    col = lax.broadcasted_iota(jnp.int32, (n, n), 1)
    causal = row >= col
    tri = jnp.where(causal, 1.0, 0.0).astype(F32)
    acum = jnp.dot(tri, a, preferred_element_type=F32, precision=lax.Precision.HIGHEST)
    acum2 = acum * LOG2_E
    acum2_t = acum2.T
    a_last = acum[n - 1:n, :]
    lo_mask = lax.broadcasted_iota(jnp.int32, (n, LANES), 1) < SSM_HEAD_DIM
    dt_x = _expand_columns(dt, ehead_ref)
    d2e_x = _expand_columns(jnp.exp(a_last - acum), ehead_ref)
    eac_x = _expand_columns(jnp.exp(acum), ehead_ref)
    acum2_x = _expand_columns(acum2, ecol_ref)

    for g in range(SSM_GROUPS):
        xg = xc[:, g * gw:(g + 1) * gw]
        bg = bc[:, g * D_STATE:(g + 1) * D_STATE]
        cg = cc[:, g * D_STATE:(g + 1) * D_STATE]
        cg_b = cg.astype(BF16)
        bg_t = bg.T.astype(BF16)
        cb = jnp.dot(cg_b, bg_t, preferred_element_type=F32) * tri
        st_prev = st_ref[g]
        y_off = jnp.dot(cg_b, st_prev.astype(BF16), preferred_element_type=F32)
        y_parts, xw_parts, ea_parts = [], [], []
        for j in range(heads_per_group // 2):
            h0 = g * heads_per_group + 2 * j
            lanes = slice(j * LANES, (j + 1) * LANES)
            glanes = slice(g * gw + j * LANES, g * gw + (j + 1) * LANES)
            xs_pair = xg[:, lanes] * dt_x[:, glanes]
            xs_b = xs_pair.astype(BF16)
            xw_parts.append((xs_pair * d2e_x[:, glanes]).astype(BF16))
            ea_parts.append(eac_x[:, glanes])
            ys = []
            for h in (h0, h0 + 1):
                seg2 = acum2_x[:, h * n:(h + 1) * n] - acum2_t[h:h + 1, :]
                decay = jnp.exp2(jnp.minimum(seg2, 0.0))
                m = (cb * decay).astype(BF16)
                ys.append(jnp.dot(m, xs_b, preferred_element_type=F32))
            y_parts.append(jnp.where(lo_mask, ys[0], ys[1]))
        y_diag = jnp.concatenate(y_parts, axis=1)
        ea = jnp.concatenate(ea_parts, axis=1)
        xw = jnp.concatenate(xw_parts, axis=1)
        cols = slice(g * gw, (g + 1) * gw)
        y = y_diag + y_off * ea + xg * dsk_ref[:, cols]
        zg = z_ref[:, cols]
        y = y * _silu(zg)
        ms = jnp.mean(y * y, axis=-1, keepdims=True)
        o_ref[:, cols] = (y * lax.rsqrt(ms + RMS_EPS) * nw_ref[:, cols]).astype(o_ref.dtype)
        st_ref[g] = st_prev * ea[n - 1:n, :] + jnp.dot(bg_t, xw, preferred_element_type=F32)


def _ssd(proj, batch, seq, col, conv_wx, conv_wb, conv_wc, conv_bx, conv_bb, conv_bc,
         dt_bias, a_log, d_skip, norm_w):
    t, width = proj.shape
    d_inner = norm_w.shape[1]
    gn = SSM_GROUPS * D_STATE
    n = SSD_CHUNK
    nc = seq // n
    gw = d_inner // SSM_GROUPS
    ns = SEQS_PER_STEP if batch % SEQS_PER_STEP == 0 else 1
    proj3 = proj.reshape(batch, seq, width)
    n_heads = d_inner // SSM_HEAD_DIM
    head_id = jnp.arange(LANES, dtype=jnp.int32)[:, None]
    e_head = (head_id == (jnp.arange(d_inner, dtype=jnp.int32) // SSM_HEAD_DIM)[None, :]).astype(BF16)
    e_col = (head_id == (jnp.arange(n_heads * n, dtype=jnp.int32) // n)[None, :]).astype(BF16)
    e_head3 = jnp.tile(e_head, (3, 1))
    e_col3 = jnp.tile(e_col, (3, 1))

    def prev_rows(c):
        return jnp.maximum(c * (n // SUBLANES) - 1, 0)

    def const(b, c):
        return (0, 0)

    in_specs = [
        pl.BlockSpec((ns, n, d_inner), lambda b, c: (b, c, col["z"] // d_inner)),
        pl.BlockSpec((ns, n, d_inner), lambda b, c: (b, c, col["x"] // d_inner)),
        pl.BlockSpec((ns, n, gn), lambda b, c: (b, c, col["B"] // gn)),
        pl.BlockSpec((ns, n, gn), lambda b, c: (b, c, col["C"] // gn)),
        pl.BlockSpec((ns, n, LANES), lambda b, c: (b, c, col["dt"] // LANES)),
        pl.BlockSpec((ns, SUBLANES, d_inner), lambda b, c: (b, prev_rows(c), col["x"] // d_inner)),
        pl.BlockSpec((ns, SUBLANES, gn), lambda b, c: (b, prev_rows(c), col["B"] // gn)),
        pl.BlockSpec((ns, SUBLANES, gn), lambda b, c: (b, prev_rows(c), col["C"] // gn)),
        pl.BlockSpec((CONV_WIDTH, d_inner), const),
        pl.BlockSpec((CONV_WIDTH, gn), const),
        pl.BlockSpec((CONV_WIDTH, gn), const),
        pl.BlockSpec((1, d_inner), const),
        pl.BlockSpec((1, gn), const),
        pl.BlockSpec((1, gn), const),
        pl.BlockSpec((1, LANES), const),
        pl.BlockSpec((1, LANES), const),
        pl.BlockSpec((1, d_inner), const),
        pl.BlockSpec((1, d_inner), const),
        pl.BlockSpec(e_head3.shape, const),
        pl.BlockSpec(e_col3.shape, const),
    ]
    out = pl.pallas_call(
        _ssd_kernel,
        grid=(batch // ns, nc),
        in_specs=in_specs,
        out_specs=pl.BlockSpec((ns, n, d_inner), lambda b, c: (b, c, 0)),
        out_shape=jax.ShapeDtypeStruct((batch, seq, d_inner), BF16),
        scratch_shapes=[pltpu.VMEM((ns, SSM_GROUPS, D_STATE, gw), F32)],
        compiler_params=_cparams(("arbitrary", "arbitrary")),
        name="ssd_mixer",
    )(proj3, proj3, proj3, proj3, proj3, proj3, proj3, proj3,
      conv_wx, conv_wb, conv_wc, conv_bx, conv_bb, conv_bc, dt_bias, a_log, d_skip, norm_w,
      e_head3, e_col3)
    return out.reshape(t, d_inner)


def _swa_kernel(sink_ref, q_ref, kc_ref, vc_ref, kp_ref, vp_ref, o_ref):
    for s in range(q_ref.shape[0]):
        _swa_block(sink_ref, q_ref.at[s], kc_ref.at[s], vc_ref.at[s], kp_ref.at[s], vp_ref.at[s],
                   o_ref.at[s])


def _swa_block(sink_ref, q_ref, kc_ref, vc_ref, kp_ref, vp_ref, o_ref):
    i = pl.program_id(1)
    n = WINDOW
    n_heads = q_ref.shape[1] // HEAD_DIM
    grp = n_heads // KV_HEADS
    qpos = lax.broadcasted_iota(jnp.int32, (n, 2 * n), 0) + n
    kpos = lax.broadcasted_iota(jnp.int32, (n, 2 * n), 1)
    diff = qpos - kpos
    mask = (diff >= 0) & (diff < n) & ((kpos >= n) | (i > 0))
    lo_q = lax.broadcasted_iota(jnp.int32, (n, LANES), 1) < HEAD_DIM
    lo_k = lax.broadcasted_iota(jnp.int32, (2 * n, LANES), 1) < HEAD_DIM
    scale = HEAD_DIM ** -0.5
    exp2_scale = scale * LOG2_E
    for p in range(KV_HEADS // 2):
        lanes = slice(p * LANES, (p + 1) * LANES)
        kk = jnp.concatenate([kp_ref[:, lanes], kc_ref[:, lanes]], axis=0).astype(F32)
        vv = jnp.concatenate([vp_ref[:, lanes], vc_ref[:, lanes]], axis=0).astype(F32)
        kk_sw = pltpu.roll(kk, HEAD_DIM, axis=1)
        vv_sw = pltpu.roll(vv, HEAD_DIM, axis=1)
        for par in range(2):
            g = 2 * p + par
            k2 = (jnp.where(lo_k, kk, kk_sw) if par == 0 else jnp.where(lo_k, kk_sw, kk)).astype(BF16)
            v2 = (jnp.where(lo_k, vv, vv_sw) if par == 0 else jnp.where(lo_k, vv_sw, vv)).astype(BF16)
            for qp in range(grp // 2):
                h0 = g * grp + 2 * qp
                qlanes = slice((h0 // 2) * LANES, (h0 // 2 + 1) * LANES)
                q_pair = q_ref[:, qlanes]
                outs = []
                for hh in range(2):
                    keep = lo_q if hh == 0 else jnp.logical_not(lo_q)
                    qm = jnp.where(keep, q_pair, jnp.zeros_like(q_pair))
                    s = lax.dot_general(qm, k2, (((1,), (1,)), ((), ())), preferred_element_type=F32)
                    s = jnp.where(mask, s, NEG_BIG)
                    sink = sink_ref[h0 + hh] / scale
                    mx = jnp.maximum(jnp.max(s, axis=-1, keepdims=True), sink)
                    pr = jnp.exp2((s - mx) * exp2_scale)
                    den = jnp.sum(pr, axis=-1, keepdims=True) + jnp.exp2((sink - mx) * exp2_scale)
                    o = jnp.dot(pr.astype(BF16), v2, preferred_element_type=F32)
                    outs.append(o / den)
                o_ref[:, qlanes] = jnp.where(lo_q, outs[0], outs[1]).astype(o_ref.dtype)


def _swa(qkv, sinks, batch, seq, n_heads):
    t, width = qkv.shape
    n = WINDOW
    nb = seq // n
    qw = n_heads * HEAD_DIM
    kw = KV_HEADS * HEAD_DIM
    k_blk = qw // kw
    v_blk = k_blk + 1
    ns = SEQS_PER_STEP if batch % SEQS_PER_STEP == 0 else 1
    qkv3 = qkv.reshape(batch, seq, width)

    def prev(i):
        return jnp.maximum(i - 1, 0)

    out = pl.pallas_call(
        _swa_kernel,
        grid=(batch // ns, nb),
        in_specs=[pl.BlockSpec(memory_space=pltpu.SMEM),
                  pl.BlockSpec((ns, n, qw), lambda b, i: (b, i, 0)),
                  pl.BlockSpec((ns, n, kw), lambda b, i: (b, i, k_blk)),
                  pl.BlockSpec((ns, n, kw), lambda b, i: (b, i, v_blk)),
                  pl.BlockSpec((ns, n, kw), lambda b, i: (b, prev(i), k_blk)),
                  pl.BlockSpec((ns, n, kw), lambda b, i: (b, prev(i), v_blk))],
        out_specs=pl.BlockSpec((ns, n, qw), lambda b, i: (b, i, 0)),
        out_shape=jax.ShapeDtypeStruct((batch, seq, qw), BF16),
        compiler_params=_cparams(("parallel", "parallel")),
        name="swa",
    )(sinks, qkv3, qkv3, qkv3, qkv3, qkv3)
    return out.reshape(t, qw)


def _layer_norm(v, g, b):
    mu = jnp.mean(v, axis=-1, keepdims=True)
    d = v - mu
    var = jnp.mean(d * d, axis=-1, keepdims=True)
    return d * lax.rsqrt(var + LN_EPS) * g + b


def _merge_kernel(alpha, ys_ref, ya_ref, gs_ref, ga_ref, x_ref, wso_ref, wao_ref, wmix_ref,
                  bgs_ref, bga_ref, lg_ref, lb_ref, h_ref, hp_ref):
    y_ssm = jnp.dot(ys_ref[...], wso_ref[...], preferred_element_type=F32)
    y_att = jnp.dot(ya_ref[...], wao_ref[...], preferred_element_type=F32)
    merged = (_sigmoid(gs_ref[...] + bgs_ref[...]) * y_ssm
              + _sigmoid(ga_ref[...] + bga_ref[...]) * y_att)
    mix = jnp.dot(merged.astype(BF16), wmix_ref[...], preferred_element_type=F32)
    h = _layer_norm(alpha * x_ref[...] + mix, lg_ref[...], lb_ref[...])
    h_ref[...] = h
    hp_ref[...] = _pack_bf16_pair(h)


def _merge(alpha, y_ssm, y_att, proj, col, x, wso, wao, wmix, bgs, bga, lg, lb, tm, row0, rows):
    d = x.shape[1]
    tm = min(tm, rows)
    assert rows % tm == 0 and row0 % tm == 0
    blk0 = row0 // tm
    di = y_ssm.shape[1]
    da = y_att.shape[1]

    def const(i):
        return (0, 0)

    return pl.pallas_call(
        functools.partial(_merge_kernel, alpha),
        grid=(rows // tm,),
        in_specs=[pl.BlockSpec((tm, di), lambda i: (i + blk0, 0)),
                  pl.BlockSpec((tm, da), lambda i: (i + blk0, 0)),
                  pl.BlockSpec((tm, d), lambda i: (i + blk0, col["gs"] // d)),
                  pl.BlockSpec((tm, d), lambda i: (i + blk0, col["ga"] // d)),
                  pl.BlockSpec((tm, d), lambda i: (i + blk0, 0)),
                  pl.BlockSpec((di, d), const),
                  pl.BlockSpec((da, d), const),
                  pl.BlockSpec((d, d), const),
                  pl.BlockSpec((1, d), const),
                  pl.BlockSpec((1, d), const),
                  pl.BlockSpec((1, d), const),
                  pl.BlockSpec((1, d), const)],
        out_specs=[pl.BlockSpec((tm, d), lambda i: (i, 0)),
                   pl.BlockSpec((tm, d // 2), lambda i: (i, 0))],
        out_shape=[jax.ShapeDtypeStruct((rows, d), F32),
                   jax.ShapeDtypeStruct((rows, d // 2), jnp.uint32)],
        compiler_params=_cparams(("parallel",)),
        name="merge_ln1",
    )(y_ssm, y_att, proj, proj, x, wso, wao, wmix, bgs, bga, lg, lb)


def _router_kernel(h_ref, wr_ref, br_ref, idx_ref, gw_ref, rank_ref, cnt_ref, tri_ref, base_ref):
    i = pl.program_id(0)
    tm = h_ref.shape[0]
    n_exp = wr_ref.shape[0]

    @pl.when(i == 0)
    def _():
        r = lax.broadcasted_iota(jnp.int32, (tm, tm), 0)
        c = lax.broadcasted_iota(jnp.int32, (tm, tm), 1)
        tri_ref[...] = jnp.where(r < c, 1.0, 0.0).astype(BF16)
        base_ref[...] = jnp.zeros_like(base_ref)

    logits = lax.dot_general(wr_ref[...], h_ref[...], (((1,), (1,)), ((), ())),
                             preferred_element_type=F32,
                             precision=lax.Precision.HIGHEST) + br_ref[...]
    eid = lax.broadcasted_iota(jnp.int32, (n_exp, tm), 0)
    work = logits
    vals, idxs = [], []
    for _ in range(TOP_K):
        mx = jnp.max(work, axis=0, keepdims=True)
        sel = jnp.min(jnp.where(work == mx, eid, n_exp), axis=0, keepdims=True)
        vals.append(mx)
        idxs.append(sel)
        work = jnp.where(eid == sel, -jnp.inf, work)
    exps = [jnp.exp(v - vals[0]) for v in vals]
    den = exps[0]
    for e in exps[1:]:
        den = den + e
    base = base_ref[...]
    for k in range(TOP_K):
        onehot = eid == idxs[k]
        oh = jnp.where(onehot, 1.0, 0.0)
        before = jnp.dot(oh.astype(BF16), tri_ref[...], preferred_element_type=F32)
        rank = jnp.sum(jnp.where(onehot, before + base, 0.0), axis=0, keepdims=True)
        idx_ref[k:k + 1, :] = idxs[k]
        gw_ref[k:k + 1, :] = exps[k] / den
        rank_ref[k:k + 1, :] = rank.astype(jnp.int32)
        base = base + jnp.sum(oh, axis=1, keepdims=True)
    base_ref[...] = base
    cnt_ref[...] = jnp.broadcast_to(base, cnt_ref.shape).astype(jnp.int32)


def _router(h, w_router_t, b_router, tm):
    t, d = h.shape
    n_exp = w_router_t.shape[0]
    tm = min(tm, t)
    outs = pl.pallas_call(
        _router_kernel,
        grid=(t // tm,),
        in_specs=[pl.BlockSpec((tm, d), lambda i: (i, 0)),
                  pl.BlockSpec((n_exp, d), lambda i: (0, 0)),
                  pl.BlockSpec((n_exp, 1), lambda i: (0, 0))],
        out_specs=[pl.BlockSpec((TOP_K, tm), lambda i: (0, i)),
                   pl.BlockSpec((TOP_K, tm), lambda i: (0, i)),
                   pl.BlockSpec((TOP_K, tm), lambda i: (0, i)),
                   pl.BlockSpec((n_exp, LANES), lambda i: (0, 0))],
        out_shape=[jax.ShapeDtypeStruct((TOP_K, t), jnp.int32),
                   jax.ShapeDtypeStruct((TOP_K, t), F32),
                   jax.ShapeDtypeStruct((TOP_K, t), jnp.int32),
                   jax.ShapeDtypeStruct((n_exp, LANES), jnp.int32)],
        scratch_shapes=[pltpu.VMEM((tm, tm), BF16), pltpu.VMEM((n_exp, 1), F32)],
        compiler_params=_cparams(("arbitrary",)),
        name="router",
    )(h, w_router_t, b_router)
    return outs


SC_CORES = 2
SC_SUBCORES = 16
SC_GATHER_CHUNK = 128
MOE_SPLITS = 2


def _sc_gather_rows(table, idx):
    b = idx.shape[0]
    d = table.shape[1]
    n_workers = SC_CORES * SC_SUBCORES
    chunk = SC_GATHER_CHUNK
    assert b % (n_workers * chunk) == 0
    per_worker = b // n_workers
    n_chunks = per_worker // chunk
    mesh = plsc.VectorSubcoreMesh(core_axis_name="c", subcore_axis_name="s",
                                  num_cores=SC_CORES, num_subcores=SC_SUBCORES)

    def body(table_hbm, idx_hbm, out_hbm, idx_v, rows_v, sem):
        wid = lax.axis_index("s") * SC_CORES + lax.axis_index("c")
        base = wid * per_worker

        @pl.loop(0, n_chunks)
        def _(c):
            off = pl.multiple_of(base + c * chunk, SUBLANES)
            pltpu.sync_copy(idx_hbm.at[pl.ds(off, chunk)], idx_v)
            pltpu.async_copy(table_hbm.at[idx_v], rows_v, sem).wait()
            pltpu.sync_copy(rows_v, out_hbm.at[pl.ds(off, chunk)])

    return pl.kernel(
        body,
        out_type=jax.ShapeDtypeStruct((b, d), table.dtype),
        mesh=mesh,
        scratch_types=[pltpu.VMEM((chunk,), jnp.int32),
                       pltpu.VMEM((chunk, d), table.dtype),
                       pltpu.SemaphoreType.DMA],
        name="sc_gather_rows",
    )(table, idx)


def _sc_scatter_rows(src, dest_kt, fill_idx, n_out_rows):
    t, d = src.shape
    n_k = dest_kt.shape[0]
    n_fill = fill_idx.shape[0]
    n_workers = SC_CORES * SC_SUBCORES
    chunk = SC_GATHER_CHUNK
    assert t % (n_workers * chunk) == 0 and n_fill % (n_workers * chunk) == 0
    tok_per_worker = t // n_workers
    fill_per_worker = n_fill // n_workers
    mesh = plsc.VectorSubcoreMesh(core_axis_name="c", subcore_axis_name="s",
                                  num_cores=SC_CORES, num_subcores=SC_SUBCORES)

    def body(src_hbm, dest_hbm, fill_hbm, zeros_hbm, out_hbm, idx_v, rows_v):
        wid = lax.axis_index("s") * SC_CORES + lax.axis_index("c")

        pltpu.sync_copy(zeros_hbm, rows_v)

        @pl.loop(0, fill_per_worker // chunk)
        def _(c):
            off = pl.multiple_of(wid * fill_per_worker + c * chunk, SUBLANES)
            pltpu.sync_copy(fill_hbm.at[pl.ds(off, chunk)], idx_v)
            pltpu.sync_copy(rows_v, out_hbm.at[idx_v])

        @pl.loop(0, tok_per_worker // chunk)
        def _(c):
            off = pl.multiple_of(wid * tok_per_worker + c * chunk, SUBLANES)
            pltpu.sync_copy(src_hbm.at[pl.ds(off, chunk)], rows_v)
            for k in range(n_k):
                pltpu.sync_copy(dest_hbm.at[pl.ds(k * t + off, chunk)], idx_v)
                pltpu.sync_copy(rows_v, out_hbm.at[idx_v])

    return pl.kernel(
        body,
        out_type=jax.ShapeDtypeStruct((n_out_rows, d), src.dtype),
        mesh=mesh,
        scratch_types=[pltpu.VMEM((chunk,), jnp.int32),
                       pltpu.VMEM((chunk, d), src.dtype)],
        name="sc_scatter_rows",
    )(src, dest_kt.reshape(-1), fill_idx, jnp.zeros((chunk, d), src.dtype))


EXPERT_SUBROWS = 128


def _expert_kernel(be_ref, nxt_ref, eslot_ref, nsub_ref, nu_ref, x_ref, wgu_hbm, bgu_ref, wd_hbm,
                   bd_ref, y_ref, wgu_f, wd_f, wsem, wgu_b, wd_b):
    i = pl.program_id(0)
    n_used = nu_ref[0]
    expert = be_ref[i]
    changed = jnp.logical_or(i == 0, expert != be_ref[jnp.maximum(i - 1, 0)])

    def weight_copies(e, slot):
        return (pltpu.make_async_copy(wgu_hbm.at[e], wgu_f.at[slot], wsem.at[0, slot]),
                pltpu.make_async_copy(wd_hbm.at[e], wd_f.at[slot], wsem.at[1, slot]))

    @pl.when(jnp.logical_and(changed, i < n_used))
    def _():
        slot = eslot_ref[i]

        @pl.when(i == 0)
        def _():
            for cp in weight_copies(expert, slot):
                cp.start()

        for cp in weight_copies(expert, slot):
            cp.wait()
        nxt = nxt_ref[i]

        @pl.when(nxt >= 0)
        def _():
            for cp in weight_copies(nxt, 1 - slot):
                cp.start()

        wgu_b[...] = wgu_f[slot].astype(BF16)
        wd_b[...] = wd_f[slot].astype(BF16)

    def ffn(rows):
        d_ff = wd_b.shape[0]
        xb = _unpack_bf16_pair(x_ref[0:rows, :]).astype(BF16)
        hgu = jnp.dot(xb, wgu_b[...], preferred_element_type=F32) + bgu_ref[0]
        glu = jnp.minimum(hgu[:, :d_ff], SWIGLU_LIMIT)
        lin = jnp.clip(hgu[:, d_ff:], -SWIGLU_LIMIT, SWIGLU_LIMIT)
        act = glu * _sigmoid(SWIGLU_ALPHA * glu) * (lin + 1.0)
        y = jnp.dot(act.astype(BF16), wd_b[...], preferred_element_type=F32) + bd_ref[0]
        y_ref[0:rows, :] = _pack_bf16_pair(y)
        if rows < y_ref.shape[0]:
            y_ref[rows:, :] = jnp.zeros((y_ref.shape[0] - rows, y_ref.shape[1]), y_ref.dtype)

    n_sub = nsub_ref[i]
    for sub in range(1, y_ref.shape[0] // EXPERT_SUBROWS + 1):
        @pl.when(jnp.logical_and(i < n_used, n_sub == sub))
        def _(sub=sub):
            ffn(sub * EXPERT_SUBROWS)

    @pl.when(i >= n_used)
    def _():
        y_ref[...] = jnp.zeros_like(y_ref)


def _experts(xs, block_expert, block_valid, n_used, wgu, bgu, wd, bd):
    d = wgu.shape[1]
    dp = xs.shape[1]
    assert 2 * dp == d
    n_blocks = block_expert.shape[0]
    n_exp, _, two_f = wgu.shape
    d_ff = wd.shape[1]
    rows = MOE_BLOCK
    n_rows = n_blocks * rows
    blk = jnp.arange(n_blocks, dtype=jnp.int32)
    is_first = jnp.concatenate([jnp.ones((1,), bool), block_expert[1:] != block_expert[:-1]])
    first_used = jnp.logical_and(is_first, blk < n_used[0])
    first_pos = jnp.where(first_used, blk, n_blocks)
    next_first = jnp.concatenate([lax.cummin(first_pos[::-1])[::-1][1:],
                                  jnp.full((1,), n_blocks, jnp.int32)])
    next_expert = jnp.where(next_first < n_blocks,
                            block_expert[jnp.minimum(next_first, n_blocks - 1)], -1).astype(jnp.int32)
    expert_slot = ((jnp.cumsum(is_first.astype(jnp.int32)) - 1) % 2).astype(jnp.int32)
    n_sub = jnp.clip(-(-block_valid // EXPERT_SUBROWS), 1, rows // EXPERT_SUBROWS).astype(jnp.int32)

    def last_used(i, be, nxt, es, ns, nu):
        return (jnp.minimum(i, jnp.maximum(nu[0] - 1, 0)), 0)

    grid_spec = pltpu.PrefetchScalarGridSpec(
        num_scalar_prefetch=5,
        grid=(n_blocks,),
        in_specs=[
            pl.BlockSpec((rows, dp), last_used),
            pl.BlockSpec(memory_space=pl.ANY),
            pl.BlockSpec((1, 1, two_f), lambda i, be, nxt, es, ns, nu: (be[i], 0, 0)),
            pl.BlockSpec(memory_space=pl.ANY),
            pl.BlockSpec((1, 1, d), lambda i, be, nxt, es, ns, nu: (be[i], 0, 0)),
        ],
        out_specs=pl.BlockSpec((rows, dp), lambda i, be, nxt, es, ns, nu: (i, 0)),
        scratch_shapes=[pltpu.VMEM((2, d, two_f), F32),
                        pltpu.VMEM((2, d_ff, d), F32),
                        pltpu.SemaphoreType.DMA((2, 2)),
                        pltpu.VMEM((d, two_f), BF16),
                        pltpu.VMEM((d_ff, d), BF16)],
    )
    return pl.pallas_call(
        _expert_kernel,
        grid_spec=grid_spec,
        out_shape=jax.ShapeDtypeStruct((n_rows, dp), jnp.uint32),
        compiler_params=_cparams(("arbitrary",)),
        name="experts",
    )(block_expert, next_expert, expert_slot, n_sub, n_used, xs, wgu, bgu.reshape(n_exp, 1, two_f), wd,
      bd.reshape(n_exp, 1, d))


def _combine_kernel(alpha, *refs):
    ys_refs = refs[:TOP_K]
    gw_ref, h_ref, lg_ref, lb_ref = refs[TOP_K:TOP_K + 4]
    o_ref = refs[-1]
    gw = gw_ref[...]
    acc = alpha * h_ref[...]
    for k in range(TOP_K):
        acc = acc + gw[:, k:k + 1] * _unpack_bf16_pair(ys_refs[k][...])
    o_ref[...] = _layer_norm(acc, lg_ref[...], lb_ref[...])


def _combine(alpha, y_slots, gw_rows, h, lg, lb, tm, out_prev, row0, total_rows):
    rows, d = h.shape
    tm = min(tm, rows)
    assert rows % tm == 0 and row0 % tm == 0
    nsteps = rows // tm
    blk0 = row0 // tm
    ys_specs = [pl.BlockSpec((tm, d // 2), functools.partial(lambda i, k: (k * nsteps + i, 0), k=k))
                for k in range(TOP_K)]
    in_specs = ys_specs + [pl.BlockSpec((tm, LANES), lambda i: (i, 0)),
                           pl.BlockSpec((tm, d), lambda i: (i, 0)),
                           pl.BlockSpec((1, d), lambda i: (0, 0)),
                           pl.BlockSpec((1, d), lambda i: (0, 0))]
    args = [y_slots] * TOP_K + [gw_rows, h, lg, lb]
    aliases = {}
    if out_prev is not None:
        in_specs.append(pl.BlockSpec(memory_space=pl.ANY))
        aliases = {len(args): 0}
        args.append(out_prev)
    return pl.pallas_call(
        functools.partial(_combine_kernel, alpha),
        grid=(nsteps,),
        in_specs=in_specs,
        out_specs=pl.BlockSpec((tm, d), lambda i: (i + blk0, 0)),
        out_shape=jax.ShapeDtypeStruct((total_rows, d), F32),
        input_output_aliases=aliases,
        compiler_params=_cparams(("parallel",)),
        name="combine_ln2",
    )(*args)


def _pack_in_proj(w_in, d_inner, n_ssm_heads, attn_dim):
    gn = SSM_GROUPS * D_STATE
    kv_dim = KV_HEADS * HEAD_DIM
    d = w_in.shape[0]
    o = 0
    seg = {}
    for name, width in (("z", d_inner), ("x", d_inner), ("B", gn), ("C", gn), ("dt", n_ssm_heads),
                        ("q", attn_dim), ("k", kv_dim), ("v", kv_dim), ("gs", d), ("ga", d)):
        seg[name] = w_in[:, o:o + width]
        o += width
    assert o == w_in.shape[1]
    dt_pad = jnp.pad(seg["dt"], ((0, 0), (0, LANES - n_ssm_heads)))
    order = (("z", seg["z"]), ("x", seg["x"]), ("gs", seg["gs"]), ("ga", seg["ga"]),
             ("B", seg["B"]), ("C", seg["C"]), ("dt", dt_pad))
    col, off = {}, 0
    for name, w in order:
        col[name] = off
        assert off % w.shape[1] == 0
        off += w.shape[1]
    w_f32grp = jnp.concatenate([w for _, w in order], axis=1).astype(BF16)
    w_qkv = jnp.concatenate([seg["q"], seg["k"], seg["v"]], axis=1).astype(BF16)
    return w_f32grp, w_qkv, col


def _largest_tile(n, cap):
    best = LANES
    for k in range(1, n // LANES + 1):
        if n % (k * LANES) == 0 and k * LANES <= cap:
            best = k * LANES
    return best


def _layer(h_in, batch, seq, alpha, w_in, conv_w, conv_b, dt_bias, a_log, d_skip, ssm_norm_w,
           w_ssm_out, attn_sinks, w_attn_out, b_gates, w_mix_out, ln1_g, ln1_b, w_router, b_router,
           w_gate_up, b_gate_up, w_down, b_down, ln2_g, ln2_b):
    t, d = h_in.shape
    d_inner = ssm_norm_w.shape[0]
    n_ssm_heads = dt_bias.shape[0]
    attn_dim = w_attn_out.shape[0]
    n_heads = attn_sinks.shape[0]
    n_exp = w_router.shape[1]
    gn = SSM_GROUPS * D_STATE

    w_f32grp, w_qkv, col = _pack_in_proj(w_in, d_inner, n_ssm_heads, attn_dim)
    proj = _matmul(h_in, w_f32grp, F32, 1024, _largest_tile(w_f32grp.shape[1], 2560))
    qkv = _matmul(h_in, w_qkv, BF16, 1024, _largest_tile(w_qkv.shape[1], 1536))

    pad_h = (0, LANES - n_ssm_heads)
    y_ssm = _ssd(proj, batch, seq, col,
                 conv_w[:, :d_inner], conv_w[:, d_inner:d_inner + gn], conv_w[:, d_inner + gn:],
                 conv_b[None, :d_inner], conv_b[None, d_inner:d_inner + gn], conv_b[None, d_inner + gn:],
                 jnp.pad(dt_bias, pad_h)[None, :], jnp.pad(a_log, pad_h)[None, :],
                 jnp.repeat(d_skip, SSM_HEAD_DIM)[None, :], ssm_norm_w[None, :])
    y_att = _swa(qkv, attn_sinks, batch, seq, n_heads)

    wso, wao, wmix = w_ssm_out.astype(BF16), w_attn_out.astype(BF16), w_mix_out.astype(BF16)
    w_router_t = w_router.T

    n_parts = MOE_SPLITS if t % (MOE_SPLITS * SC_CORES * SC_SUBCORES * SC_GATHER_CHUNK) == 0 else 1
    rows = t // n_parts
    out = None
    for part in range(n_parts):
        row0 = part * rows
        h1, h1_packed = _merge(alpha, y_ssm, y_att, proj, col, h_in, wso, wao, wmix, b_gates[None, :d],
                               b_gates[None, d:], ln1_g[None, :], ln1_b[None, :], 512, row0, rows)
        idx_kt, gw_kt, rank_kt, counts = _router(h1, w_router_t, b_router[:, None], 1024)

        counts = counts[:, 0]
        padded = (counts + MOE_BLOCK - 1) // MOE_BLOCK * MOE_BLOCK
        pad_end = jnp.cumsum(padded)
        pad_start = pad_end - padded
        n_blocks = -(-(rows * TOP_K) // MOE_BLOCK) + n_exp
        n_rows = n_blocks * MOE_BLOCK
        expert_ids = jnp.arange(n_exp, dtype=jnp.int32)
        pad_start_of_slot = jnp.sum(
            jnp.where(idx_kt[None] == expert_ids[:, None, None], pad_start[:, None, None], 0), axis=0)
        dest_kt = pad_start_of_slot + rank_kt
        fill_rows = (pad_start + counts)[:, None] + jnp.arange(MOE_BLOCK, dtype=jnp.int32)[None, :]
        spare_rows = n_rows + jnp.arange(n_exp * MOE_BLOCK, dtype=jnp.int32).reshape(n_exp, MOE_BLOCK)
        fill_idx = jnp.where(fill_rows < pad_end[:, None], fill_rows, spare_rows).reshape(-1)
        block_row0 = jnp.arange(n_blocks, dtype=jnp.int32) * MOE_BLOCK
        block_expert = jnp.minimum(
            jnp.sum((pad_end[None, :] <= block_row0[:, None]).astype(jnp.int32), axis=1), n_exp - 1)
        n_used = (pad_end[-1:] // MOE_BLOCK).astype(jnp.int32)
        real_end_of_block = jnp.sum(jnp.where(block_expert[:, None] == expert_ids[None, :],
                                              (pad_start + counts)[None, :], 0), axis=1)
        block_valid = jnp.clip(real_end_of_block - block_row0, 0, MOE_BLOCK)
        gw_rows = jnp.pad(gw_kt.T, ((0, 0), (0, LANES - TOP_K)))

        xs = _sc_scatter_rows(h1_packed, dest_kt, fill_idx, n_rows + n_exp * MOE_BLOCK)
        ys = _experts(xs, block_expert, block_valid, n_used, w_gate_up, b_gate_up, w_down, b_down)
        y_slots = _sc_gather_rows(ys, dest_kt.reshape(-1))
        out = _combine(alpha, y_slots, gw_rows, h1, ln2_g[None, :], ln2_b[None, :], 512, out, row0, t)
    return out


def kernel(x, w_in, conv_w, conv_b, dt_bias, a_log, d_skip, ssm_norm_w, w_ssm_out, attn_sinks,
           w_attn_out, b_gates, w_mix_out, ln1_g, ln1_b, w_router, b_router, w_gate_up, b_gate_up,
           w_down, b_down, ln2_g, ln2_b):
    batch, seq, d = x.shape
    depth = w_in.shape[0]
    alpha = (2 * depth) ** 0.25
    h = x.reshape(batch * seq, d)
    for i in range(depth):
        h = _layer(h, batch, seq, alpha, w_in[i], conv_w[i], conv_b[i], dt_bias[i], a_log[i],
                   d_skip[i], ssm_norm_w[i], w_ssm_out[i], attn_sinks[i], w_attn_out[i], b_gates[i],
                   w_mix_out[i], ln1_g[i], ln1_b[i], w_router[i], b_router[i], w_gate_up[i],
                   b_gate_up[i], w_down[i], b_down[i], ln2_g[i], ln2_b[i])
    return h.reshape(batch, seq, d)
```

```python
import functools

import jax
import jax.numpy as jnp
from jax import lax
from jax.experimental import pallas as pl
from jax.experimental.pallas import tpu as pltpu
from jax.experimental.pallas import tpu_sc as plsc

SSM_HEAD_DIM = 64
SSM_GROUPS = 4
D_STATE = 128
CONV_WIDTH = 4
SSD_CHUNK = 128
SEQS_PER_STEP = 2
KV_HEADS = 4
HEAD_DIM = 64
WINDOW = 128
TOP_K = 4
SWIGLU_LIMIT = 7.0
SWIGLU_ALPHA = 1.702
MOE_BLOCK = 512
LN_EPS = 1e-5
RMS_EPS = 1e-5

LANES = 128
SUBLANES = 8
NEG_BIG = -1e30
LOG2_E = 1.4426950408889634
F32 = jnp.float32
BF16 = jnp.bfloat16
VMEM_LIMIT = 56 * 1024 * 1024


def _cparams(sem):
    return pltpu.CompilerParams(dimension_semantics=sem, vmem_limit_bytes=VMEM_LIMIT)


def _sigmoid(x):
    return 0.5 + 0.5 * jnp.tanh(0.5 * x)


def _pack_bf16_pair(v):
    w = v.shape[1] // 2
    bits = lax.bitcast_convert_type(v.astype(BF16).astype(F32), jnp.uint32)
    return (bits[:, :w] >> 16) | (bits[:, w:] & jnp.uint32(0xFFFF0000))


def _unpack_bf16_pair(p):
    lo = lax.bitcast_convert_type(p << 16, F32)
    hi = lax.bitcast_convert_type(p & jnp.uint32(0xFFFF0000), F32)
    return jnp.concatenate([lo, hi], axis=1)


def _silu(x):
    h = 0.5 * x
    return h + h * jnp.tanh(h)


def _mm_kernel(x_ref, w_ref, o_ref):
    o_ref[...] = jnp.dot(x_ref[...].astype(BF16), w_ref[...],
                         preferred_element_type=F32).astype(o_ref.dtype)


def _matmul(x, w, out_dtype, tm, tn):
    m, k = x.shape
    n = w.shape[1]
    tm = min(tm, m)
    assert m % tm == 0 and n % tn == 0
    return pl.pallas_call(
        _mm_kernel,
        grid=(n // tn, m // tm),
        in_specs=[pl.BlockSpec((tm, k), lambda j, i: (i, 0)),
                  pl.BlockSpec((k, tn), lambda j, i: (0, j))],
        out_specs=pl.BlockSpec((tm, tn), lambda j, i: (i, j)),
        out_shape=jax.ShapeDtypeStruct((m, n), out_dtype),
        compiler_params=_cparams(("parallel", "parallel")),
        name="in_proj",
    )(x, w)


def _conv_silu(u_ref, prev_ref, w_ref, b_ref, first):
    cur = u_ref[...]
    prev = prev_ref[...]
    prev = jnp.where(first, jnp.zeros_like(prev), prev)
    row8 = lax.broadcasted_iota(jnp.int32, prev.shape, 0)
    acc = b_ref[...] + w_ref[CONV_WIDTH - 1:CONV_WIDTH, :] * cur
    for k in range(1, CONV_WIDTH):
        rolled = pltpu.roll(cur, k, axis=0)
        head = jnp.where(row8 < k, pltpu.roll(prev, k, axis=0), rolled[:SUBLANES])
        shifted = jnp.concatenate([head, rolled[SUBLANES:]], axis=0)
        j = CONV_WIDTH - 1 - k
        acc = acc + w_ref[j:j + 1, :] * shifted
    return _silu(acc)


def _expand_columns(v, e3_ref):
    x1 = v.astype(BF16)
    r1 = v - x1.astype(F32)
    x2 = r1.astype(BF16)
    x3 = (r1 - x2.astype(F32)).astype(BF16)
    return jnp.dot(jnp.concatenate([x1, x2, x3], axis=1), e3_ref[...], preferred_element_type=F32)


def _ssd_kernel(z_ref, x_ref, b_ref, c_ref, dt_ref, xp_ref, bp_ref, cp_ref,
                cwx_ref, cwb_ref, cwc_ref, cbx_ref, cbb_ref, cbc_ref,
                dtb_ref, alog_ref, dsk_ref, nw_ref, ehead_ref, ecol_ref,
                o_ref, st_ref):
    first = pl.program_id(1) == 0

    @pl.when(first)
    def _():
        st_ref[...] = jnp.zeros_like(st_ref)

    for s in range(z_ref.shape[0]):
        _ssd_chunk(z_ref.at[s], x_ref.at[s], b_ref.at[s], c_ref.at[s], dt_ref.at[s],
                   xp_ref.at[s], bp_ref.at[s], cp_ref.at[s],
                   cwx_ref, cwb_ref, cwc_ref, cbx_ref, cbb_ref, cbc_ref,
                   dtb_ref, alog_ref, dsk_ref, nw_ref, ehead_ref, ecol_ref,
                   o_ref.at[s], st_ref.at[s], first)


def _ssd_chunk(z_ref, x_ref, b_ref, c_ref, dt_ref, xp_ref, bp_ref, cp_ref,
               cwx_ref, cwb_ref, cwc_ref, cbx_ref, cbb_ref, cbc_ref,
               dtb_ref, alog_ref, dsk_ref, nw_ref, ehead_ref, ecol_ref, o_ref, st_ref, first):
    n = SSD_CHUNK
    heads_per_group = st_ref.shape[2] // SSM_HEAD_DIM
    gw = heads_per_group * SSM_HEAD_DIM

    xc = _conv_silu(x_ref, xp_ref, cwx_ref, cbx_ref, first)
    bc = _conv_silu(b_ref, bp_ref, cwb_ref, cbb_ref, first)
    cc = _conv_silu(c_ref, cp_ref, cwc_ref, cbc_ref, first)

    dt_in = dt_ref[...] + dtb_ref[...]
    dt = jnp.maximum(dt_in, 0.0) + jnp.log(1.0 + jnp.exp(-jnp.abs(dt_in)))
    a = dt * (-jnp.exp(alog_ref[...]))
    row = lax.broadcasted_iota(jnp.int32, (n, n), 0)
    col = lax.broadcasted_iota(jnp.int32, (n, n), 1)
    causal = row >= col
    tri = jnp.where(causal, 1.0, 0.0).astype(F32)
    acum = jnp.dot(tri, a, preferred_element_type=F32, precision=lax.Precision.HIGHEST)
    acum2 = acum * LOG2_E
    acum2_t = acum2.T
    a_last = acum[n - 1:n, :]
    lo_mask = lax.broadcasted_iota(jnp.int32, (n, LANES), 1) < SSM_HEAD_DIM
    dt_x = _expand_columns(dt, ehead_ref)
    d2e_x = _expand_columns(jnp.exp(a_last - acum), ehead_ref)
    eac_x = _expand_columns(jnp.exp(acum), ehead_ref)
    acum2_x = _expand_columns(acum2, ecol_ref)

    for g in range(SSM_GROUPS):
        xg = xc[:, g * gw:(g + 1) * gw]
        bg = bc[:, g * D_STATE:(g + 1) * D_STATE]
        cg = cc[:, g * D_STATE:(g + 1) * D_STATE]
        cg_b = cg.astype(BF16)
        bg_t = bg.T.astype(BF16)
        cb = jnp.dot(cg_b, bg_t, preferred_element_type=F32) * tri
        st_prev = st_ref[g]
        y_off = jnp.dot(cg_b, st_prev.astype(BF16), preferred_element_type=F32)
        y_parts, xw_parts, ea_parts = [], [], []
        for j in range(heads_per_group // 2):
            h0 = g * heads_per_group + 2 * j
            lanes = slice(j * LANES, (j + 1) * LANES)
            glanes = slice(g * gw + j * LANES, g * gw + (j + 1) * LANES)
            xs_pair = xg[:, lanes] * dt_x[:, glanes]
            xs_b = xs_pair.astype(BF16)
            xw_parts.append((xs_pair * d2e_x[:, glanes]).astype(BF16))
            ea_parts.append(eac_x[:, glanes])
            ys = []
            for h in (h0, h0 + 1):
                seg2 = acum2_x[:, h * n:(h + 1) * n] - acum2_t[h:h + 1, :]
                decay = jnp.exp2(jnp.minimum(seg2, 0.0))
                m = (cb * decay).astype(BF16)
                ys.append(jnp.dot(m, xs_b, preferred_element_type=F32))
            y_parts.append(jnp.where(lo_mask, ys[0], ys[1]))
        y_diag = jnp.concatenate(y_parts, axis=1)
        ea = jnp.concatenate(ea_parts, axis=1)
        xw = jnp.concatenate(xw_parts, axis=1)
        cols = slice(g * gw, (g + 1) * gw)
        y = y_diag + y_off * ea + xg * dsk_ref[:, cols]
        zg = z_ref[:, cols]
        y = y * _silu(zg)
        ms = jnp.mean(y * y, axis=-1, keepdims=True)
        o_ref[:, cols] = (y * lax.rsqrt(ms + RMS_EPS) * nw_ref[:, cols]).astype(o_ref.dtype)
        st_ref[g] = st_prev * ea[n - 1:n, :] + jnp.dot(bg_t, xw, preferred_element_type=F32)


def _ssd(proj, batch, seq, col, conv_wx, conv_wb, conv_wc, conv_bx, conv_bb, conv_bc,
         dt_bias, a_log, d_skip, norm_w):
    t, width = proj.shape
    d_inner = norm_w.shape[1]
    gn = SSM_GROUPS * D_STATE
    n = SSD_CHUNK
    nc = seq // n
    gw = d_inner // SSM_GROUPS
    ns = SEQS_PER_STEP if batch % SEQS_PER_STEP == 0 else 1
    proj3 = proj.reshape(batch, seq, width)
    n_heads = d_inner // SSM_HEAD_DIM
    head_id = jnp.arange(LANES, dtype=jnp.int32)[:, None]
    e_head = (head_id == (jnp.arange(d_inner, dtype=jnp.int32) // SSM_HEAD_DIM)[None, :]).astype(BF16)
    e_col = (head_id == (jnp.arange(n_heads * n, dtype=jnp.int32) // n)[None, :]).astype(BF16)
    e_head3 = jnp.tile(e_head, (3, 1))
    e_col3 = jnp.tile(e_col, (3, 1))

    def prev_rows(c):
        return jnp.maximum(c * (n // SUBLANES) - 1, 0)

    def const(b, c):
        return (0, 0)

    in_specs = [
        pl.BlockSpec((ns, n, d_inner), lambda b, c: (b, c, col["z"] // d_inner)),
        pl.BlockSpec((ns, n, d_inner), lambda b, c: (b, c, col["x"] // d_inner)),
        pl.BlockSpec((ns, n, gn), lambda b, c: (b, c, col["B"] // gn)),
        pl.BlockSpec((ns, n, gn), lambda b, c: (b, c, col["C"] // gn)),
        pl.BlockSpec((ns, n, LANES), lambda b, c: (b, c, col["dt"] // LANES)),
        pl.BlockSpec((ns, SUBLANES, d_inner), lambda b, c: (b, prev_rows(c), col["x"] // d_inner)),
        pl.BlockSpec((ns, SUBLANES, gn), lambda b, c: (b, prev_rows(c), col["B"] // gn)),
        pl.BlockSpec((ns, SUBLANES, gn), lambda b, c: (b, prev_rows(c), col["C"] // gn)),
        pl.BlockSpec((CONV_WIDTH, d_inner), const),
        pl.BlockSpec((CONV_WIDTH, gn), const),
        pl.BlockSpec((CONV_WIDTH, gn), const),
        pl.BlockSpec((1, d_inner), const),
        pl.BlockSpec((1, gn), const),
        pl.BlockSpec((1, gn), const),
        pl.BlockSpec((1, LANES), const),
        pl.BlockSpec((1, LANES), const),
        pl.BlockSpec((1, d_inner), const),
        pl.BlockSpec((1, d_inner), const),
        pl.BlockSpec(e_head3.shape, const),
        pl.BlockSpec(e_col3.shape, const),
    ]
    out = pl.pallas_call(
        _ssd_kernel,
        grid=(batch // ns, nc),
        in_specs=in_specs,
        out_specs=pl.BlockSpec((ns, n, d_inner), lambda b, c: (b, c, 0)),
        out_shape=jax.ShapeDtypeStruct((batch, seq, d_inner), BF16),
        scratch_shapes=[pltpu.VMEM((ns, SSM_GROUPS, D_STATE, gw), F32)],
        compiler_params=_cparams(("arbitrary", "arbitrary")),
        name="ssd_mixer",
    )(proj3, proj3, proj3, proj3, proj3, proj3, proj3, proj3,
      conv_wx, conv_wb, conv_wc, conv_bx, conv_bb, conv_bc, dt_bias, a_log, d_skip, norm_w,
      e_head3, e_col3)
    return out.reshape(t, d_inner)


def _swa_kernel(sink_ref, q_ref, kc_ref, vc_ref, kp_ref, vp_ref, o_ref):
    for s in range(q_ref.shape[0]):
        _swa_block(sink_ref, q_ref.at[s], kc_ref.at[s], vc_ref.at[s], kp_ref.at[s], vp_ref.at[s],
                   o_ref.at[s])


def _swa_block(sink_ref, q_ref, kc_ref, vc_ref, kp_ref, vp_ref, o_ref):
    i = pl.program_id(1)
    n = WINDOW
    n_heads = q_ref.shape[1] // HEAD_DIM
    grp = n_heads // KV_HEADS
    qpos = lax.broadcasted_iota(jnp.int32, (n, 2 * n), 0) + n
    kpos = lax.broadcasted_iota(jnp.int32, (n, 2 * n), 1)
    diff = qpos - kpos
    mask = (diff >= 0) & (diff < n) & ((kpos >= n) | (i > 0))
    lo_q = lax.broadcasted_iota(jnp.int32, (n, LANES), 1) < HEAD_DIM
    lo_k = lax.broadcasted_iota(jnp.int32, (2 * n, LANES), 1) < HEAD_DIM
    scale = HEAD_DIM ** -0.5
    exp2_scale = scale * LOG2_E
    for p in range(KV_HEADS // 2):
        lanes = slice(p * LANES, (p + 1) * LANES)
        kk = jnp.concatenate([kp_ref[:, lanes], kc_ref[:, lanes]], axis=0).astype(F32)
        vv = jnp.concatenate([vp_ref[:, lanes], vc_ref[:, lanes]], axis=0).astype(F32)
        kk_sw = pltpu.roll(kk, HEAD_DIM, axis=1)
        vv_sw = pltpu.roll(vv, HEAD_DIM, axis=1)
        for par in range(2):
            g = 2 * p + par
            k2 = (jnp.where(lo_k, kk, kk_sw) if par == 0 else jnp.where(lo_k, kk_sw, kk)).astype(BF16)
            v2 = (jnp.where(lo_k, vv, vv_sw) if par == 0 else jnp.where(lo_k, vv_sw, vv)).astype(BF16)
            for qp in range(grp // 2):
                h0 = g * grp + 2 * qp
                qlanes = slice((h0 // 2) * LANES, (h0 // 2 + 1) * LANES)
                q_pair = q_ref[:, qlanes]
                outs = []
                for hh in range(2):
                    keep = lo_q if hh == 0 else jnp.logical_not(lo_q)
                    qm = jnp.where(keep, q_pair, jnp.zeros_like(q_pair))
                    s = lax.dot_general(qm, k2, (((1,), (1,)), ((), ())), preferred_element_type=F32)
                    s = jnp.where(mask, s, NEG_BIG)
                    sink = sink_ref[h0 + hh] / scale
                    mx = jnp.maximum(jnp.max(s, axis=-1, keepdims=True), sink)
                    pr = jnp.exp2((s - mx) * exp2_scale)
                    den = jnp.sum(pr, axis=-1, keepdims=True) + jnp.exp2((sink - mx) * exp2_scale)
                    o = jnp.dot(pr.astype(BF16), v2, preferred_element_type=F32)
                    outs.append(o / den)
                o_ref[:, qlanes] = jnp.where(lo_q, outs[0], outs[1]).astype(o_ref.dtype)


def _swa(qkv, sinks, batch, seq, n_heads):
    t, width = qkv.shape
    n = WINDOW
    nb = seq // n
    qw = n_heads * HEAD_DIM
    kw = KV_HEADS * HEAD_DIM
    k_blk = qw // kw
    v_blk = k_blk + 1
    ns = SEQS_PER_STEP if batch % SEQS_PER_STEP == 0 else 1
    qkv3 = qkv.reshape(batch, seq, width)

    def prev(i):
        return jnp.maximum(i - 1, 0)

    out = pl.pallas_call(
        _swa_kernel,
        grid=(batch // ns, nb),
        in_specs=[pl.BlockSpec(memory_space=pltpu.SMEM),
                  pl.BlockSpec((ns, n, qw), lambda b, i: (b, i, 0)),
                  pl.BlockSpec((ns, n, kw), lambda b, i: (b, i, k_blk)),
                  pl.BlockSpec((ns, n, kw), lambda b, i: (b, i, v_blk)),
                  pl.BlockSpec((ns, n, kw), lambda b, i: (b, prev(i), k_blk)),
                  pl.BlockSpec((ns, n, kw), lambda b, i: (b, prev(i), v_blk))],
        out_specs=pl.BlockSpec((ns, n, qw), lambda b, i: (b, i, 0)),
        out_shape=jax.ShapeDtypeStruct((batch, seq, qw), BF16),
        compiler_params=_cparams(("parallel", "parallel")),
        name="swa",
    )(sinks, qkv3, qkv3, qkv3, qkv3, qkv3)
    return out.reshape(t, qw)


def _layer_norm(v, g, b):
    mu = jnp.mean(v, axis=-1, keepdims=True)
    d = v - mu
    var = jnp.mean(d * d, axis=-1, keepdims=True)
    return d * lax.rsqrt(var + LN_EPS) * g + b


def _merge_kernel(alpha, ys_ref, ya_ref, gs_ref, ga_ref, x_ref, wso_ref, wao_ref, wmix_ref,
                  bgs_ref, bga_ref, lg_ref, lb_ref, h_ref, hp_ref):
    y_ssm = jnp.dot(ys_ref[...], wso_ref[...], preferred_element_type=F32)
    y_att = jnp.dot(ya_ref[...], wao_ref[...], preferred_element_type=F32)
    merged = (_sigmoid(gs_ref[...] + bgs_ref[...]) * y_ssm
              + _sigmoid(ga_ref[...] + bga_ref[...]) * y_att)
    mix = jnp.dot(merged.astype(BF16), wmix_ref[...], preferred_element_type=F32)
    h = _layer_norm(alpha * x_ref[...] + mix, lg_ref[...], lb_ref[...])
    h_ref[...] = h
    hp_ref[...] = _pack_bf16_pair(h)


def _merge(alpha, y_ssm, y_att, proj, col, x, wso, wao, wmix, bgs, bga, lg, lb, tm, row0, rows):
    d = x.shape[1]
    tm = min(tm, rows)
    assert rows % tm == 0 and row0 % tm == 0
    blk0 = row0 // tm
    di = y_ssm.shape[1]
    da = y_att.shape[1]

    def const(i):
        return (0, 0)

    return pl.pallas_call(
        functools.partial(_merge_kernel, alpha),
        grid=(rows // tm,),
        in_specs=[pl.BlockSpec((tm, di), lambda i: (i + blk0, 0)),
                  pl.BlockSpec((tm, da), lambda i: (i + blk0, 0)),
                  pl.BlockSpec((tm, d), lambda i: (i + blk0, col["gs"] // d)),
                  pl.BlockSpec((tm, d), lambda i: (i + blk0, col["ga"] // d)),
                  pl.BlockSpec((tm, d), lambda i: (i + blk0, 0)),
                  pl.BlockSpec((di, d), const),
                  pl.BlockSpec((da, d), const),
                  pl.BlockSpec((d, d), const),
                  pl.BlockSpec((1, d), const),
                  pl.BlockSpec((1, d), const),
                  pl.BlockSpec((1, d), const),
                  pl.BlockSpec((1, d), const)],
        out_specs=[pl.BlockSpec((tm, d), lambda i: (i, 0)),
                   pl.BlockSpec((tm, d // 2), lambda i: (i, 0))],
        out_shape=[jax.ShapeDtypeStruct((rows, d), F32),
                   jax.ShapeDtypeStruct((rows, d // 2), jnp.uint32)],
        compiler_params=_cparams(("parallel",)),
        name="merge_ln1",
    )(y_ssm, y_att, proj, proj, x, wso, wao, wmix, bgs, bga, lg, lb)


ROUTER_SUBTILE = 128


def _router_kernel(h_ref, wr_ref, br_ref, idx_ref, gw_ref, rank_ref, cnt_ref, base_ref):
    i = pl.program_id(0)
    tm = h_ref.shape[0]
    n_exp = br_ref.shape[0]
    sub = ROUTER_SUBTILE

    @pl.when(i == 0)
    def _():
        base_ref[...] = jnp.zeros_like(base_ref)

    h = h_ref[...]
    h1 = h.astype(BF16)
    h2 = (h - h1.astype(F32)).astype(BF16)
    first = jnp.dot(h1, wr_ref[...], preferred_element_type=F32)
    logits_t = (first[:, :LANES] + first[:, LANES:]
                + jnp.dot(h2, wr_ref[:, :LANES], preferred_element_type=F32))
    logits = logits_t.T[:n_exp, :] + br_ref[...]
    eid = lax.broadcasted_iota(jnp.int32, (n_exp, tm), 0)
    work = logits
    vals, idxs = [], []
    for _ in range(TOP_K):
        mx = jnp.max(work, axis=0, keepdims=True)
        sel = jnp.min(jnp.where(work == mx, eid, n_exp), axis=0, keepdims=True)
        vals.append(mx)
        idxs.append(sel)
        work = jnp.where(eid == sel, -jnp.inf, work)
    exps = [jnp.exp(v - vals[0]) for v in vals]
    den = exps[0]
    for e in exps[1:]:
        den = den + e
    r = lax.broadcasted_iota(jnp.int32, (sub, sub), 0)
    c = lax.broadcasted_iota(jnp.int32, (sub, sub), 1)
    tri = jnp.where(r < c, 1.0, 0.0).astype(BF16)
    base = base_ref[...]
    for k in range(TOP_K):
        onehot = eid == idxs[k]
        oh = jnp.where(onehot, 1.0, 0.0)
        ranks = []
        for s in range(tm // sub):
            lanes = slice(s * sub, (s + 1) * sub)
            before = jnp.dot(oh[:, lanes].astype(BF16), tri, preferred_element_type=F32)
            ranks.append(jnp.sum(jnp.where(onehot[:, lanes], before + base, 0.0), axis=0, keepdims=True))
            base = base + jnp.sum(oh[:, lanes], axis=1, keepdims=True)
        idx_ref[k:k + 1, :] = idxs[k]
        gw_ref[k:k + 1, :] = exps[k] / den
        rank_ref[k:k + 1, :] = jnp.concatenate(ranks, axis=1).astype(jnp.int32)
    base_ref[...] = base
    cnt_ref[...] = jnp.broadcast_to(base, cnt_ref.shape).astype(jnp.int32)


def _router(h, w_router, b_router, tm):
    t, d = h.shape
    n_exp = w_router.shape[1]
    tm = min(tm, t)
    assert tm % ROUTER_SUBTILE == 0 and n_exp <= LANES
    w_padded = jnp.pad(w_router, ((0, 0), (0, LANES - n_exp)))
    w_hi = w_padded.astype(BF16)
    w_lo = (w_padded - w_hi.astype(F32)).astype(BF16)
    w_padded = jnp.concatenate([w_hi, w_lo], axis=1)
    outs = pl.pallas_call(
        _router_kernel,
        grid=(t // tm,),
        in_specs=[pl.BlockSpec((tm, d), lambda i: (i, 0)),
                  pl.BlockSpec((d, 2 * LANES), lambda i: (0, 0)),
                  pl.BlockSpec((n_exp, 1), lambda i: (0, 0))],
        out_specs=[pl.BlockSpec((TOP_K, tm), lambda i: (0, i)),
                   pl.BlockSpec((TOP_K, tm), lambda i: (0, i)),
                   pl.BlockSpec((TOP_K, tm), lambda i: (0, i)),
                   pl.BlockSpec((n_exp, LANES), lambda i: (0, 0))],
        out_shape=[jax.ShapeDtypeStruct((TOP_K, t), jnp.int32),
                   jax.ShapeDtypeStruct((TOP_K, t), F32),
                   jax.ShapeDtypeStruct((TOP_K, t), jnp.int32),
                   jax.ShapeDtypeStruct((n_exp, LANES), jnp.int32)],
        scratch_shapes=[pltpu.VMEM((n_exp, 1), F32)],
        compiler_params=_cparams(("arbitrary",)),
        name="router",
    )(h, w_padded, b_router)
    return outs


SC_CORES = 2
SC_SUBCORES = 16
SC_GATHER_CHUNK = 128
MOE_SPLITS = 2


def _sc_gather_rows(table, idx):
    b = idx.shape[0]
    d = table.shape[1]
    n_workers = SC_CORES * SC_SUBCORES
    chunk = SC_GATHER_CHUNK
    assert b % (n_workers * chunk) == 0
    per_worker = b // n_workers
    n_chunks = per_worker // chunk
    mesh = plsc.VectorSubcoreMesh(core_axis_name="c", subcore_axis_name="s",
                                  num_cores=SC_CORES, num_subcores=SC_SUBCORES)

    def body(table_hbm, idx_hbm, out_hbm, idx_v, rows_v, sem):
        wid = lax.axis_index("s") * SC_CORES + lax.axis_index("c")
        base = wid * per_worker

        @pl.loop(0, n_chunks)
        def _(c):
            off = pl.multiple_of(base + c * chunk, SUBLANES)
            pltpu.sync_copy(idx_hbm.at[pl.ds(off, chunk)], idx_v)
            pltpu.async_copy(table_hbm.at[idx_v], rows_v, sem).wait()
            pltpu.sync_copy(rows_v, out_hbm.at[pl.ds(off, chunk)])

    return pl.kernel(
        body,
        out_type=jax.ShapeDtypeStruct((b, d), table.dtype),
        mesh=mesh,
        scratch_types=[pltpu.VMEM((chunk,), jnp.int32),
                       pltpu.VMEM((chunk, d), table.dtype),
                       pltpu.SemaphoreType.DMA],
        name="sc_gather_rows",
    )(table, idx)


def _sc_scatter_rows(src, dest_kt, fill_idx, n_out_rows):
    t, d = src.shape
    n_k = dest_kt.shape[0]
    n_fill = fill_idx.shape[0]
    n_workers = SC_CORES * SC_SUBCORES
    chunk = SC_GATHER_CHUNK
    assert t % (n_workers * chunk) == 0 and n_fill % (n_workers * chunk) == 0
    tok_per_worker = t // n_workers
    fill_per_worker = n_fill // n_workers
    mesh = plsc.VectorSubcoreMesh(core_axis_name="c", subcore_axis_name="s",
                                  num_cores=SC_CORES, num_subcores=SC_SUBCORES)

    def body(src_hbm, dest_hbm, fill_hbm, zeros_hbm, out_hbm, idx_v, rows_v):
        wid = lax.axis_index("s") * SC_CORES + lax.axis_index("c")

        pltpu.sync_copy(zeros_hbm, rows_v)

        @pl.loop(0, fill_per_worker // chunk)
        def _(c):
            off = pl.multiple_of(wid * fill_per_worker + c * chunk, SUBLANES)
            pltpu.sync_copy(fill_hbm.at[pl.ds(off, chunk)], idx_v)
            pltpu.sync_copy(rows_v, out_hbm.at[idx_v])

        @pl.loop(0, tok_per_worker // chunk)
        def _(c):
            off = pl.multiple_of(wid * tok_per_worker + c * chunk, SUBLANES)
            pltpu.sync_copy(src_hbm.at[pl.ds(off, chunk)], rows_v)
            for k in range(n_k):
                pltpu.sync_copy(dest_hbm.at[pl.ds(k * t + off, chunk)], idx_v)
                pltpu.sync_copy(rows_v, out_hbm.at[idx_v])

    return pl.kernel(
        body,
        out_type=jax.ShapeDtypeStruct((n_out_rows, d), src.dtype),
        mesh=mesh,
        scratch_types=[pltpu.VMEM((chunk,), jnp.int32),
                       pltpu.VMEM((chunk, d), src.dtype)],
        name="sc_scatter_rows",
    )(src, dest_kt.reshape(-1), fill_idx, jnp.zeros((chunk, d), src.dtype))


EXPERT_SUBROWS = 128


def _expert_kernel(be_ref, nxt_ref, eslot_ref, nsub_ref, nu_ref, x_ref, wgu_hbm, bgu_ref, wd_hbm,
                   bd_ref, y_ref, wgu_f, wd_f, wsem, wgu_b, wd_b):
    i = pl.program_id(0)
    n_used = nu_ref[0]
    expert = be_ref[i]
    changed = jnp.logical_or(i == 0, expert != be_ref[jnp.maximum(i - 1, 0)])

    def weight_copies(e, slot):
        return (pltpu.make_async_copy(wgu_hbm.at[e], wgu_f.at[slot], wsem.at[0, slot]),
                pltpu.make_async_copy(wd_hbm.at[e], wd_f.at[slot], wsem.at[1, slot]))

    @pl.when(jnp.logical_and(changed, i < n_used))
    def _():
        slot = eslot_ref[i]

        @pl.when(i == 0)
        def _():
            for cp in weight_copies(expert, slot):
                cp.start()

        for cp in weight_copies(expert, slot):
            cp.wait()
        nxt = nxt_ref[i]

        @pl.when(nxt >= 0)
        def _():
            for cp in weight_copies(nxt, 1 - slot):
                cp.start()

        wgu_b[...] = wgu_f[slot].astype(BF16)
        wd_b[...] = wd_f[slot].astype(BF16)

    def ffn(rows):
        d_ff = wd_b.shape[0]
        xb = _unpack_bf16_pair(x_ref[0:rows, :]).astype(BF16)
        hgu = jnp.dot(xb, wgu_b[...], preferred_element_type=F32) + bgu_ref[0]
        glu = jnp.minimum(hgu[:, :d_ff], SWIGLU_LIMIT)
        lin = jnp.clip(hgu[:, d_ff:], -SWIGLU_LIMIT, SWIGLU_LIMIT)
        act = glu * _sigmoid(SWIGLU_ALPHA * glu) * (lin + 1.0)
        y = jnp.dot(act.astype(BF16), wd_b[...], preferred_element_type=F32) + bd_ref[0]
        y_ref[0:rows, :] = _pack_bf16_pair(y)
        if rows < y_ref.shape[0]:
            y_ref[rows:, :] = jnp.zeros((y_ref.shape[0] - rows, y_ref.shape[1]), y_ref.dtype)

    n_sub = nsub_ref[i]
    for sub in range(1, y_ref.shape[0] // EXPERT_SUBROWS + 1):
        @pl.when(jnp.logical_and(i < n_used, n_sub == sub))
        def _(sub=sub):
            ffn(sub * EXPERT_SUBROWS)


def _experts(xs, block_expert, block_valid, n_used, wgu, bgu, wd, bd):
    d = wgu.shape[1]
    dp = xs.shape[1]
    assert 2 * dp == d
    n_blocks = block_expert.shape[0]
    n_exp, _, two_f = wgu.shape
    d_ff = wd.shape[1]
    rows = MOE_BLOCK
    n_rows = n_blocks * rows
    blk = jnp.arange(n_blocks, dtype=jnp.int32)
    is_first = jnp.concatenate([jnp.ones((1,), bool), block_expert[1:] != block_expert[:-1]])
    first_used = jnp.logical_and(is_first, blk < n_used[0])
    first_pos = jnp.where(first_used, blk, n_blocks)
    next_first = jnp.concatenate([lax.cummin(first_pos[::-1])[::-1][1:],
                                  jnp.full((1,), n_blocks, jnp.int32)])
    next_expert = jnp.where(next_first < n_blocks,
                            block_expert[jnp.minimum(next_first, n_blocks - 1)], -1).astype(jnp.int32)
    expert_slot = ((jnp.cumsum(is_first.astype(jnp.int32)) - 1) % 2).astype(jnp.int32)
    n_sub = jnp.clip(-(-block_valid // EXPERT_SUBROWS), 1, rows // EXPERT_SUBROWS).astype(jnp.int32)

    def last_used(i, be, nxt, es, ns, nu):
        return (jnp.minimum(i, jnp.maximum(nu[0] - 1, 0)), 0)

    grid_spec = pltpu.PrefetchScalarGridSpec(
        num_scalar_prefetch=5,
        grid=(n_blocks,),
        in_specs=[
            pl.BlockSpec((rows, dp), last_used),
            pl.BlockSpec(memory_space=pl.ANY),
            pl.BlockSpec((1, 1, two_f), lambda i, be, nxt, es, ns, nu: (be[i], 0, 0)),
            pl.BlockSpec(memory_space=pl.ANY),
            pl.BlockSpec((1, 1, d), lambda i, be, nxt, es, ns, nu: (be[i], 0, 0)),
        ],
        out_specs=pl.BlockSpec((rows, dp), last_used),
        scratch_shapes=[pltpu.VMEM((2, d, two_f), F32),
                        pltpu.VMEM((2, d_ff, d), F32),
                        pltpu.SemaphoreType.DMA((2, 2)),
                        pltpu.VMEM((d, two_f), BF16),
                        pltpu.VMEM((d_ff, d), BF16)],
    )
    return pl.pallas_call(
        _expert_kernel,
        grid_spec=grid_spec,
        out_shape=jax.ShapeDtypeStruct((n_rows, dp), jnp.uint32),
        compiler_params=_cparams(("arbitrary",)),
        name="experts",
    )(block_expert, next_expert, expert_slot, n_sub, n_used, xs, wgu, bgu.reshape(n_exp, 1, two_f), wd,
      bd.reshape(n_exp, 1, d))


def _combine_kernel(alpha, *refs):
    ys_refs = refs[:TOP_K]
    gw_ref, h_ref, lg_ref, lb_ref = refs[TOP_K:TOP_K + 4]
    o_ref = refs[-1]
    gw = gw_ref[...]
    acc = alpha * h_ref[...]
    for k in range(TOP_K):
        acc = acc + gw[:, k:k + 1] * _unpack_bf16_pair(ys_refs[k][...])
    o_ref[...] = _layer_norm(acc, lg_ref[...], lb_ref[...])


def _combine(alpha, y_slots, gw_rows, h, lg, lb, tm, out_prev, row0, total_rows):
    rows, d = h.shape
    tm = min(tm, rows)
    assert rows % tm == 0 and row0 % tm == 0
    nsteps = rows // tm
    blk0 = row0 // tm
    ys_specs = [pl.BlockSpec((tm, d // 2), functools.partial(lambda i, k: (k * nsteps + i, 0), k=k))
                for k in range(TOP_K)]
    in_specs = ys_specs + [pl.BlockSpec((tm, LANES), lambda i: (i, 0)),
                           pl.BlockSpec((tm, d), lambda i: (i, 0)),
                           pl.BlockSpec((1, d), lambda i: (0, 0)),
                           pl.BlockSpec((1, d), lambda i: (0, 0))]
    args = [y_slots] * TOP_K + [gw_rows, h, lg, lb]
    aliases = {}
    if out_prev is not None:
        in_specs.append(pl.BlockSpec(memory_space=pl.ANY))
        aliases = {len(args): 0}
        args.append(out_prev)
    return pl.pallas_call(
        functools.partial(_combine_kernel, alpha),
        grid=(nsteps,),
        in_specs=in_specs,
        out_specs=pl.BlockSpec((tm, d), lambda i: (i + blk0, 0)),
        out_shape=jax.ShapeDtypeStruct((total_rows, d), F32),
        input_output_aliases=aliases,
        compiler_params=_cparams(("parallel",)),
        name="combine_ln2",
    )(*args)


def _pack_in_proj(w_in, d_inner, n_ssm_heads, attn_dim):
    gn = SSM_GROUPS * D_STATE
    kv_dim = KV_HEADS * HEAD_DIM
    d = w_in.shape[0]
    o = 0
    seg = {}
    for name, width in (("z", d_inner), ("x", d_inner), ("B", gn), ("C", gn), ("dt", n_ssm_heads),
                        ("q", attn_dim), ("k", kv_dim), ("v", kv_dim), ("gs", d), ("ga", d)):
        seg[name] = w_in[:, o:o + width]
        o += width
    assert o == w_in.shape[1]
    dt_pad = jnp.pad(seg["dt"], ((0, 0), (0, LANES - n_ssm_heads)))
    order = (("z", seg["z"]), ("x", seg["x"]), ("gs", seg["gs"]), ("ga", seg["ga"]),
             ("B", seg["B"]), ("C", seg["C"]), ("dt", dt_pad))
    col, off = {}, 0
    for name, w in order:
        col[name] = off
        assert off % w.shape[1] == 0
        off += w.shape[1]
    w_f32grp = jnp.concatenate([w for _, w in order], axis=1).astype(BF16)
    w_qkv = jnp.concatenate([seg["q"], seg["k"], seg["v"]], axis=1).astype(BF16)
    return w_f32grp, w_qkv, col


def _largest_tile(n, cap):
    best = LANES
    for k in range(1, n // LANES + 1):
        if n % (k * LANES) == 0 and k * LANES <= cap:
            best = k * LANES
    return best


def _layer(h_in, batch, seq, alpha, w_in, conv_w, conv_b, dt_bias, a_log, d_skip, ssm_norm_w,
           w_ssm_out, attn_sinks, w_attn_out, b_gates, w_mix_out, ln1_g, ln1_b, w_router, b_router,
           w_gate_up, b_gate_up, w_down, b_down, ln2_g, ln2_b):
    t, d = h_in.shape
    d_inner = ssm_norm_w.shape[0]
    n_ssm_heads = dt_bias.shape[0]
    attn_dim = w_attn_out.shape[0]
    n_heads = attn_sinks.shape[0]
    n_exp = w_router.shape[1]
    gn = SSM_GROUPS * D_STATE

    w_f32grp, w_qkv, col = _pack_in_proj(w_in, d_inner, n_ssm_heads, attn_dim)
    proj = _matmul(h_in, w_f32grp, F32, 1024, _largest_tile(w_f32grp.shape[1], 2560))
    qkv = _matmul(h_in, w_qkv, BF16, 1024, _largest_tile(w_qkv.shape[1], 1536))

    pad_h = (0, LANES - n_ssm_heads)
    y_ssm = _ssd(proj, batch, seq, col,
                 conv_w[:, :d_inner], conv_w[:, d_inner:d_inner + gn], conv_w[:, d_inner + gn:],
                 conv_b[None, :d_inner], conv_b[None, d_inner:d_inner + gn], conv_b[None, d_inner + gn:],
                 jnp.pad(dt_bias, pad_h)[None, :], jnp.pad(a_log, pad_h)[None, :],
                 jnp.repeat(d_skip, SSM_HEAD_DIM)[None, :], ssm_norm_w[None, :])
    y_att = _swa(qkv, attn_sinks, batch, seq, n_heads)

    wso, wao, wmix = w_ssm_out.astype(BF16), w_attn_out.astype(BF16), w_mix_out.astype(BF16)

    n_parts = MOE_SPLITS if t % (MOE_SPLITS * SC_CORES * SC_SUBCORES * SC_GATHER_CHUNK) == 0 else 1
    rows = t // n_parts
    out = None
    for part in range(n_parts):
        row0 = part * rows
        h1, h1_packed = _merge(alpha, y_ssm, y_att, proj, col, h_in, wso, wao, wmix, b_gates[None, :d],
                               b_gates[None, d:], ln1_g[None, :], ln1_b[None, :], 512, row0, rows)
        idx_kt, gw_kt, rank_kt, counts = _router(h1, w_router, b_router[:, None], 1024)

        counts = counts[:, 0]
        padded = (counts + MOE_BLOCK - 1) // MOE_BLOCK * MOE_BLOCK
        pad_end = jnp.cumsum(padded)
        pad_start = pad_end - padded
        n_blocks = -(-(rows * TOP_K) // MOE_BLOCK) + n_exp
        n_rows = n_blocks * MOE_BLOCK
        expert_ids = jnp.arange(n_exp, dtype=jnp.int32)
        pad_start_of_slot = jnp.sum(
            jnp.where(idx_kt[None] == expert_ids[:, None, None], pad_start[:, None, None], 0), axis=0)
        dest_kt = pad_start_of_slot + rank_kt
        fill_rows = (pad_start + counts)[:, None] + jnp.arange(MOE_BLOCK, dtype=jnp.int32)[None, :]
        spare_rows = n_rows + jnp.arange(n_exp * MOE_BLOCK, dtype=jnp.int32).reshape(n_exp, MOE_BLOCK)
        fill_idx = jnp.where(fill_rows < pad_end[:, None], fill_rows, spare_rows).reshape(-1)
        block_row0 = jnp.arange(n_blocks, dtype=jnp.int32) * MOE_BLOCK
        block_expert = jnp.minimum(
            jnp.sum((pad_end[None, :] <= block_row0[:, None]).astype(jnp.int32), axis=1), n_exp - 1)
        n_used = (pad_end[-1:] // MOE_BLOCK).astype(jnp.int32)
        real_end_of_block = jnp.sum(jnp.where(block_expert[:, None] == expert_ids[None, :],
                                              (pad_start + counts)[None, :], 0), axis=1)
        block_valid = jnp.clip(real_end_of_block - block_row0, 0, MOE_BLOCK)
        gw_rows = jnp.pad(gw_kt.T, ((0, 0), (0, LANES - TOP_K)))

        xs = _sc_scatter_rows(h1_packed, dest_kt, fill_idx, n_rows + n_exp * MOE_BLOCK)
        ys = _experts(xs, block_expert, block_valid, n_used, w_gate_up, b_gate_up, w_down, b_down)
        y_slots = _sc_gather_rows(ys, dest_kt.reshape(-1))
        out = _combine(alpha, y_slots, gw_rows, h1, ln2_g[None, :], ln2_b[None, :], 512, out, row0, t)
    return out


def kernel(x, w_in, conv_w, conv_b, dt_bias, a_log, d_skip, ssm_norm_w, w_ssm_out, attn_sinks,
           w_attn_out, b_gates, w_mix_out, ln1_g, ln1_b, w_router, b_router, w_gate_up, b_gate_up,
           w_down, b_down, ln2_g, ln2_b):
    batch, seq, d = x.shape
    depth = w_in.shape[0]
    alpha = (2 * depth) ** 0.25
    h = x.reshape(batch * seq, d)
    for i in range(depth):
        h = _layer(h, batch, seq, alpha, w_in[i], conv_w[i], conv_b[i], dt_bias[i], a_log[i],
                   d_skip[i], ssm_norm_w[i], w_ssm_out[i], attn_sinks[i], w_attn_out[i], b_gates[i],
                   w_mix_out[i], ln1_g[i], ln1_b[i], w_router[i], b_router[i], w_gate_up[i],
                   b_gate_up[i], w_down[i], b_down[i], ln2_g[i], ln2_b[i])
    return h.reshape(batch, seq, d)
```

```python
import functools

import jax
import jax.numpy as jnp
from jax import lax
from jax.experimental import pallas as pl
from jax.experimental.pallas import tpu as pltpu
from jax.experimental.pallas import tpu_sc as plsc

SSM_HEAD_DIM = 64
SSM_GROUPS = 4
D_STATE = 128
CONV_WIDTH = 4
SSD_CHUNK = 128
SEQS_PER_STEP = 2
KV_HEADS = 4
HEAD_DIM = 64
WINDOW = 128
TOP_K = 4
SWIGLU_LIMIT = 7.0
SWIGLU_ALPHA = 1.702
MOE_BLOCK = 512
LN_EPS = 1e-5
RMS_EPS = 1e-5

LANES = 128
SUBLANES = 8
NEG_BIG = -1e30
LOG2_E = 1.4426950408889634
F32 = jnp.float32
BF16 = jnp.bfloat16
VMEM_LIMIT = 56 * 1024 * 1024


def _cparams(sem):
    return pltpu.CompilerParams(dimension_semantics=sem, vmem_limit_bytes=VMEM_LIMIT)


def _sigmoid(x):
    return 0.5 + 0.5 * jnp.tanh(0.5 * x)


def _pack_bf16_pair(v):
    w = v.shape[1] // 2
    bits = lax.bitcast_convert_type(v.astype(BF16).astype(F32), jnp.uint32)
    return (bits[:, :w] >> 16) | (bits[:, w:] & jnp.uint32(0xFFFF0000))


def _unpack_bf16_pair(p):
    lo = lax.bitcast_convert_type(p << 16, F32)
    hi = lax.bitcast_convert_type(p & jnp.uint32(0xFFFF0000), F32)
    return jnp.concatenate([lo, hi], axis=1)


def _silu(x):
    h = 0.5 * x
    return h + h * jnp.tanh(h)


def _mm_kernel(x_ref, w_ref, o_ref):
    o_ref[...] = jnp.dot(x_ref[...].astype(BF16), w_ref[...],
                         preferred_element_type=F32).astype(o_ref.dtype)


def _matmul(x, w, out_dtype, tm, tn):
    m, k = x.shape
    n = w.shape[1]
    tm = min(tm, m)
    assert m % tm == 0 and n % tn == 0
    return pl.pallas_call(
        _mm_kernel,
        grid=(n // tn, m // tm),
        in_specs=[pl.BlockSpec((tm, k), lambda j, i: (i, 0)),
                  pl.BlockSpec((k, tn), lambda j, i: (0, j))],
        out_specs=pl.BlockSpec((tm, tn), lambda j, i: (i, j)),
        out_shape=jax.ShapeDtypeStruct((m, n), out_dtype),
        compiler_params=_cparams(("parallel", "parallel")),
        name="in_proj",
    )(x, w)


def _conv_silu(u_ref, prev_ref, w_ref, b_ref, first):
    cur = u_ref[...]
    prev = prev_ref[...]
    prev = jnp.where(first, jnp.zeros_like(prev), prev)
    row8 = lax.broadcasted_iota(jnp.int32, prev.shape, 0)
    acc = b_ref[...] + w_ref[CONV_WIDTH - 1:CONV_WIDTH, :] * cur
    for k in range(1, CONV_WIDTH):
        rolled = pltpu.roll(cur, k, axis=0)
        head = jnp.where(row8 < k, pltpu.roll(prev, k, axis=0), rolled[:SUBLANES])
        shifted = jnp.concatenate([head, rolled[SUBLANES:]], axis=0)
        j = CONV_WIDTH - 1 - k
        acc = acc + w_ref[j:j + 1, :] * shifted
    return _silu(acc)


def _expand_columns(v, e3_ref):
    x1 = v.astype(BF16)
    r1 = v - x1.astype(F32)
    x2 = r1.astype(BF16)
    x3 = (r1 - x2.astype(F32)).astype(BF16)
    return jnp.dot(jnp.concatenate([x1, x2, x3], axis=1), e3_ref[...], preferred_element_type=F32)


def _ssd_kernel(z_ref, x_ref, b_ref, c_ref, dt_ref, xp_ref, bp_ref, cp_ref,
                cwx_ref, cwb_ref, cwc_ref, cbx_ref, cbb_ref, cbc_ref,
                dtb_ref, alog_ref, dsk_ref, nw_ref, ehead_ref, ecol_ref,
                o_ref, st_ref):
    first = pl.program_id(1) == 0

    @pl.when(first)
    def _():
        st_ref[...] = jnp.zeros_like(st_ref)

    for s in range(z_ref.shape[0]):
        _ssd_chunk(z_ref.at[s], x_ref.at[s], b_ref.at[s], c_ref.at[s], dt_ref.at[s],
                   xp_ref.at[s], bp_ref.at[s], cp_ref.at[s],
                   cwx_ref, cwb_ref, cwc_ref, cbx_ref, cbb_ref, cbc_ref,
                   dtb_ref, alog_ref, dsk_ref, nw_ref, ehead_ref, ecol_ref,
                   o_ref.at[s], st_ref.at[s], first)


def _ssd_chunk(z_ref, x_ref, b_ref, c_ref, dt_ref, xp_ref, bp_ref, cp_ref,
               cwx_ref, cwb_ref, cwc_ref, cbx_ref, cbb_ref, cbc_ref,
               dtb_ref, alog_ref, dsk_ref, nw_ref, ehead_ref, ecol_ref, o_ref, st_ref, first):
    n = SSD_CHUNK
    heads_per_group = st_ref.shape[2] // SSM_HEAD_DIM
    gw = heads_per_group * SSM_HEAD_DIM

    xc = _conv_silu(x_ref, xp_ref, cwx_ref, cbx_ref, first)
    bc = _conv_silu(b_ref, bp_ref, cwb_ref, cbb_ref, first)
    cc = _conv_silu(c_ref, cp_ref, cwc_ref, cbc_ref, first)

    dt_in = dt_ref[...] + dtb_ref[...]
    dt = jnp.maximum(dt_in, 0.0) + jnp.log(1.0 + jnp.exp(-jnp.abs(dt_in)))
    a = dt * (-jnp.exp(alog_ref[...]))
    row = lax.broadcasted_iota(jnp.int32, (n, n), 0)
    col = lax.broadcasted_iota(jnp.int32, (n, n), 1)
    causal = row >= col
    tri = jnp.where(causal, 1.0, 0.0).astype(F32)
    acum = jnp.dot(tri, a, preferred_element_type=F32, precision=lax.Precision.HIGHEST)
    acum2 = acum * LOG2_E
    acum2_t = acum2.T
    a_last = acum[n - 1:n, :]
    lo_mask = lax.broadcasted_iota(jnp.int32, (n, LANES), 1) < SSM_HEAD_DIM
    dt_x = _expand_columns(dt, ehead_ref)
    d2e_x = _expand_columns(jnp.exp(a_last - acum), ehead_ref)
    eac_x = _expand_columns(jnp.exp(acum), ehead_ref)
    acum2_x = _expand_columns(acum2, ecol_ref)

    for g in range(SSM_GROUPS):
        xg = xc[:, g * gw:(g + 1) * gw]
        bg = bc[:, g * D_STATE:(g + 1) * D_STATE]
        cg = cc[:, g * D_STATE:(g + 1) * D_STATE]
        cg_b = cg.astype(BF16)
        bg_t = bg.T.astype(BF16)
        cb = jnp.dot(cg_b, bg_t, preferred_element_type=F32) * tri
        st_prev = st_ref[g]
        y_off = jnp.dot(cg_b, st_prev.astype(BF16), preferred_element_type=F32)
        y_parts, xw_parts, ea_parts = [], [], []
        for j in range(heads_per_group // 2):
            h0 = g * heads_per_group + 2 * j
            lanes = slice(j * LANES, (j + 1) * LANES)
            glanes = slice(g * gw + j * LANES, g * gw + (j + 1) * LANES)
            xs_pair = xg[:, lanes] * dt_x[:, glanes]
            xs_b = xs_pair.astype(BF16)
            xw_parts.append((xs_pair * d2e_x[:, glanes]).astype(BF16))
            ea_parts.append(eac_x[:, glanes])
            ys = []
            for h in (h0, h0 + 1):
                seg2 = acum2_x[:, h * n:(h + 1) * n] - acum2_t[h:h + 1, :]
                decay = jnp.exp2(jnp.minimum(seg2, 0.0))
                m = (cb * decay).astype(BF16)
                ys.append(jnp.dot(m, xs_b, preferred_element_type=F32))
            y_parts.append(jnp.where(lo_mask, ys[0], ys[1]))
        y_diag = jnp.concatenate(y_parts, axis=1)
        ea = jnp.concatenate(ea_parts, axis=1)
        xw = jnp.concatenate(xw_parts, axis=1)
        cols = slice(g * gw, (g + 1) * gw)
        y = y_diag + y_off * ea + xg * dsk_ref[:, cols]
        zg = z_ref[:, cols]
        y = y * _silu(zg)
        ms = jnp.mean(y * y, axis=-1, keepdims=True)
        o_ref[:, cols] = (y * lax.rsqrt(ms + RMS_EPS) * nw_ref[:, cols]).astype(o_ref.dtype)
        st_ref[g] = st_prev * ea[n - 1:n, :] + jnp.dot(bg_t, xw, preferred_element_type=F32)


def _ssd(proj, batch, seq, col, conv_wx, conv_wb, conv_wc, conv_bx, conv_bb, conv_bc,
         dt_bias, a_log, d_skip, norm_w):
    t, width = proj.shape
    d_inner = norm_w.shape[1]
    gn = SSM_GROUPS * D_STATE
    n = SSD_CHUNK
    nc = seq // n
    gw = d_inner // SSM_GROUPS
    ns = SEQS_PER_STEP if batch % SEQS_PER_STEP == 0 else 1
    proj3 = proj.reshape(batch, seq, width)
    n_heads = d_inner // SSM_HEAD_DIM
    head_id = jnp.arange(LANES, dtype=jnp.int32)[:, None]
    e_head = (head_id == (jnp.arange(d_inner, dtype=jnp.int32) // SSM_HEAD_DIM)[None, :]).astype(BF16)
    e_col = (head_id == (jnp.arange(n_heads * n, dtype=jnp.int32) // n)[None, :]).astype(BF16)
    e_head3 = jnp.tile(e_head, (3, 1))
    e_col3 = jnp.tile(e_col, (3, 1))

    def prev_rows(c):
        return jnp.maximum(c * (n // SUBLANES) - 1, 0)

    def const(b, c):
        return (0, 0)

    in_specs = [
        pl.BlockSpec((ns, n, d_inner), lambda b, c: (b, c, col["z"] // d_inner)),
        pl.BlockSpec((ns, n, d_inner), lambda b, c: (b, c, col["x"] // d_inner)),
        pl.BlockSpec((ns, n, gn), lambda b, c: (b, c, col["B"] // gn)),
        pl.BlockSpec((ns, n, gn), lambda b, c: (b, c, col["C"] // gn)),
        pl.BlockSpec((ns, n, LANES), lambda b, c: (b, c, col["dt"] // LANES)),
        pl.BlockSpec((ns, SUBLANES, d_inner), lambda b, c: (b, prev_rows(c), col["x"] // d_inner)),
        pl.BlockSpec((ns, SUBLANES, gn), lambda b, c: (b, prev_rows(c), col["B"] // gn)),
        pl.BlockSpec((ns, SUBLANES, gn), lambda b, c: (b, prev_rows(c), col["C"] // gn)),
        pl.BlockSpec((CONV_WIDTH, d_inner), const),
        pl.BlockSpec((CONV_WIDTH, gn), const),
        pl.BlockSpec((CONV_WIDTH, gn), const),
        pl.BlockSpec((1, d_inner), const),
        pl.BlockSpec((1, gn), const),
        pl.BlockSpec((1, gn), const),
        pl.BlockSpec((1, LANES), const),
        pl.BlockSpec((1, LANES), const),
        pl.BlockSpec((1, d_inner), const),
        pl.BlockSpec((1, d_inner), const),
        pl.BlockSpec(e_head3.shape, const),
        pl.BlockSpec(e_col3.shape, const),
    ]
    out = pl.pallas_call(
        _ssd_kernel,
        grid=(batch // ns, nc),
        in_specs=in_specs,
        out_specs=pl.BlockSpec((ns, n, d_inner), lambda b, c: (b, c, 0)),
        out_shape=jax.ShapeDtypeStruct((batch, seq, d_inner), BF16),
        scratch_shapes=[pltpu.VMEM((ns, SSM_GROUPS, D_STATE, gw), F32)],
        compiler_params=_cparams(("arbitrary", "arbitrary")),
        name="ssd_mixer",
    )(proj3, proj3, proj3, proj3, proj3, proj3, proj3, proj3,
      conv_wx, conv_wb, conv_wc, conv_bx, conv_bb, conv_bc, dt_bias, a_log, d_skip, norm_w,
      e_head3, e_col3)
    return out.reshape(t, d_inner)


def _swa_kernel(sink_ref, q_ref, kc_ref, vc_ref, kp_ref, vp_ref, o_ref):
    for s in range(q_ref.shape[0]):
        _swa_block(sink_ref, q_ref.at[s], kc_ref.at[s], vc_ref.at[s], kp_ref.at[s], vp_ref.at[s],
                   o_ref.at[s])


def _swa_block(sink_ref, q_ref, kc_ref, vc_ref, kp_ref, vp_ref, o_ref):
    i = pl.program_id(1)
    n = WINDOW
    n_heads = q_ref.shape[1] // HEAD_DIM
    grp = n_heads // KV_HEADS
    qpos = lax.broadcasted_iota(jnp.int32, (n, 2 * n), 0) + n
    kpos = lax.broadcasted_iota(jnp.int32, (n, 2 * n), 1)
    diff = qpos - kpos
    mask = (diff >= 0) & (diff < n) & ((kpos >= n) | (i > 0))
    lo_q = lax.broadcasted_iota(jnp.int32, (n, LANES), 1) < HEAD_DIM
    lo_k = lax.broadcasted_iota(jnp.int32, (2 * n, LANES), 1) < HEAD_DIM
    scale = HEAD_DIM ** -0.5
    exp2_scale = scale * LOG2_E
    for p in range(KV_HEADS // 2):
        lanes = slice(p * LANES, (p + 1) * LANES)
        kk = jnp.concatenate([kp_ref[:, lanes], kc_ref[:, lanes]], axis=0).astype(F32)
        vv = jnp.concatenate([vp_ref[:, lanes], vc_ref[:, lanes]], axis=0).astype(F32)
        kk_sw = pltpu.roll(kk, HEAD_DIM, axis=1)
        vv_sw = pltpu.roll(vv, HEAD_DIM, axis=1)
        for par in range(2):
            g = 2 * p + par
            k2 = (jnp.where(lo_k, kk, kk_sw) if par == 0 else jnp.where(lo_k, kk_sw, kk)).astype(BF16)
            v2 = (jnp.where(lo_k, vv, vv_sw) if par == 0 else jnp.where(lo_k, vv_sw, vv)).astype(BF16)
            q_rows = []
            for hq in range(grp):
                h = g * grp + hq
                q_pair = q_ref[:, (h // 2) * LANES:(h // 2 + 1) * LANES]
                keep = lo_q if h % 2 == 0 else jnp.logical_not(lo_q)
                q_rows.append(jnp.where(keep, q_pair, jnp.zeros_like(q_pair)))
            s_all = lax.dot_general(jnp.concatenate(q_rows, axis=0), k2, (((1,), (1,)), ((), ())),
                                    preferred_element_type=F32)
            probs, dens = [], []
            for hq in range(grp):
                s = jnp.where(mask, s_all[hq * n:(hq + 1) * n, :], NEG_BIG)
                sink = sink_ref[g * grp + hq] / scale
                mx = jnp.maximum(jnp.max(s, axis=-1, keepdims=True), sink)
                pr = jnp.exp2((s - mx) * exp2_scale)
                dens.append(jnp.sum(pr, axis=-1, keepdims=True) + jnp.exp2((sink - mx) * exp2_scale))
                probs.append(pr.astype(BF16))
            o_all = jnp.dot(jnp.concatenate(probs, axis=0), v2, preferred_element_type=F32)
            for qp in range(grp // 2):
                outs = [o_all[(2 * qp + hh) * n:(2 * qp + hh + 1) * n, :] / dens[2 * qp + hh] for hh in range(2)]
                pair = (g * grp) // 2 + qp
                o_ref[:, pair * LANES:(pair + 1) * LANES] = jnp.where(lo_q, outs[0], outs[1]).astype(o_ref.dtype)


def _swa(qkv, sinks, batch, seq, n_heads):
    t, width = qkv.shape
    n = WINDOW
    nb = seq // n
    qw = n_heads * HEAD_DIM
    kw = KV_HEADS * HEAD_DIM
    k_blk = qw // kw
    v_blk = k_blk + 1
    ns = SEQS_PER_STEP if batch % SEQS_PER_STEP == 0 else 1
    qkv3 = qkv.reshape(batch, seq, width)

    def prev(i):
        return jnp.maximum(i - 1, 0)

    out = pl.pallas_call(
        _swa_kernel,
        grid=(batch // ns, nb),
        in_specs=[pl.BlockSpec(memory_space=pltpu.SMEM),
                  pl.BlockSpec((ns, n, qw), lambda b, i: (b, i, 0)),
                  pl.BlockSpec((ns, n, kw), lambda b, i: (b, i, k_blk)),
                  pl.BlockSpec((ns, n, kw), lambda b, i: (b, i, v_blk)),
                  pl.BlockSpec((ns, n, kw), lambda b, i: (b, prev(i), k_blk)),
                  pl.BlockSpec((ns, n, kw), lambda b, i: (b, prev(i), v_blk))],
        out_specs=pl.BlockSpec((ns, n, qw), lambda b, i: (b, i, 0)),
        out_shape=jax.ShapeDtypeStruct((batch, seq, qw), BF16),
        compiler_params=_cparams(("parallel", "parallel")),
        name="swa",
    )(sinks, qkv3, qkv3, qkv3, qkv3, qkv3)
    return out.reshape(t, qw)


def _layer_norm(v, g, b):
    mu = jnp.mean(v, axis=-1, keepdims=True)
    d = v - mu
    var = jnp.mean(d * d, axis=-1, keepdims=True)
    return d * lax.rsqrt(var + LN_EPS) * g + b


def _merge_kernel(alpha, ys_ref, ya_ref, gs_ref, ga_ref, x_ref, wso_ref, wao_ref, wmix_ref,
                  bgs_ref, bga_ref, lg_ref, lb_ref, *rest):
    h_ref, hp_ref = rest[-2:]
    y_ssm = jnp.dot(ys_ref[...], wso_ref[...], preferred_element_type=F32)
    y_att = jnp.dot(ya_ref[...], wao_ref[...], preferred_element_type=F32)
    merged = (_sigmoid(gs_ref[...] + bgs_ref[...]) * y_ssm
              + _sigmoid(ga_ref[...] + bga_ref[...]) * y_att)
    mix = jnp.dot(merged.astype(BF16), wmix_ref[...], preferred_element_type=F32)
    h = _layer_norm(alpha * x_ref[...] + mix, lg_ref[...], lb_ref[...])
    h_ref[...] = h
    hp_ref[...] = _pack_bf16_pair(h)


def _merge(alpha, y_ssm, y_att, proj, col, x, wso, wao, wmix, bgs, bga, lg, lb, tm, row0, rows, after):
    d = x.shape[1]
    tm = min(tm, rows)
    assert rows % tm == 0 and row0 % tm == 0
    blk0 = row0 // tm
    di = y_ssm.shape[1]
    da = y_att.shape[1]

    def const(i):
        return (0, 0)

    ordering = [] if after is None else [after]
    return pl.pallas_call(
        functools.partial(_merge_kernel, alpha),
        grid=(rows // tm,),
        in_specs=[pl.BlockSpec((tm, di), lambda i: (i + blk0, 0)),
                  pl.BlockSpec((tm, da), lambda i: (i + blk0, 0)),
                  pl.BlockSpec((tm, d), lambda i: (i + blk0, col["gs"] // d)),
                  pl.BlockSpec((tm, d), lambda i: (i + blk0, col["ga"] // d)),
                  pl.BlockSpec((tm, d), lambda i: (i + blk0, 0)),
                  pl.BlockSpec((di, d), const),
                  pl.BlockSpec((da, d), const),
                  pl.BlockSpec((d, d), const),
                  pl.BlockSpec((1, d), const),
                  pl.BlockSpec((1, d), const),
                  pl.BlockSpec((1, d), const),
                  pl.BlockSpec((1, d), const)] + [pl.BlockSpec(memory_space=pl.ANY)] * len(ordering),
        out_specs=[pl.BlockSpec((tm, d), lambda i: (i, 0)),
                   pl.BlockSpec((tm, d // 2), lambda i: (i, 0))],
        out_shape=[jax.ShapeDtypeStruct((rows, d), F32),
                   jax.ShapeDtypeStruct((rows, d // 2), jnp.uint32)],
        compiler_params=_cparams(("parallel",)),
        name="merge_ln1",
    )(y_ssm, y_att, proj, proj, x, wso, wao, wmix, bgs, bga, lg, lb, *ordering)


ROUTER_SUBTILE = 128


def _router_kernel(h_ref, wr_ref, br_ref, idx_ref, gw_ref, rank_ref, cnt_ref, base_ref):
    i = pl.program_id(0)
    tm = h_ref.shape[0]
    n_exp = br_ref.shape[0]
    sub = ROUTER_SUBTILE

    @pl.when(i == 0)
    def _():
        base_ref[...] = jnp.zeros_like(base_ref)

    h = h_ref[...]
    h1 = h.astype(BF16)
    h2 = (h - h1.astype(F32)).astype(BF16)
    first = jnp.dot(h1, wr_ref[...], preferred_element_type=F32)
    logits_t = (first[:, :LANES] + first[:, LANES:]
                + jnp.dot(h2, wr_ref[:, :LANES], preferred_element_type=F32))
    logits = logits_t.T[:n_exp, :] + br_ref[...]
    eid = lax.broadcasted_iota(jnp.int32, (n_exp, tm), 0)
    work = logits
    vals, idxs = [], []
    for _ in range(TOP_K):
        mx = jnp.max(work, axis=0, keepdims=True)
        sel = jnp.min(jnp.where(work == mx, eid, n_exp), axis=0, keepdims=True)
        vals.append(mx)
        idxs.append(sel)
        work = jnp.where(eid == sel, -jnp.inf, work)
    exps = [jnp.exp(v - vals[0]) for v in vals]
    den = exps[0]
    for e in exps[1:]:
        den = den + e
    r = lax.broadcasted_iota(jnp.int32, (sub, sub), 0)
    c = lax.broadcasted_iota(jnp.int32, (sub, sub), 1)
    tri = jnp.where(r < c, 1.0, 0.0).astype(BF16)
    base = base_ref[...]
    for k in range(TOP_K):
        onehot = eid == idxs[k]
        oh = jnp.where(onehot, 1.0, 0.0)
        ranks = []
        for s in range(tm // sub):
            lanes = slice(s * sub, (s + 1) * sub)
            before = jnp.dot(oh[:, lanes].astype(BF16), tri, preferred_element_type=F32)
            ranks.append(jnp.sum(jnp.where(onehot[:, lanes], before + base, 0.0), axis=0, keepdims=True))
            base = base + jnp.sum(oh[:, lanes], axis=1, keepdims=True)
        idx_ref[k:k + 1, :] = idxs[k]
        gw_ref[k:k + 1, :] = exps[k] / den
        rank_ref[k:k + 1, :] = jnp.concatenate(ranks, axis=1).astype(jnp.int32)
    base_ref[...] = base
    cnt_ref[...] = jnp.broadcast_to(base, cnt_ref.shape).astype(jnp.int32)


def _router(h, w_router, b_router, tm):
    t, d = h.shape
    n_exp = w_router.shape[1]
    tm = min(tm, t)
    assert tm % ROUTER_SUBTILE == 0 and n_exp <= LANES
    w_padded = jnp.pad(w_router, ((0, 0), (0, LANES - n_exp)))
    w_hi = w_padded.astype(BF16)
    w_lo = (w_padded - w_hi.astype(F32)).astype(BF16)
    w_padded = jnp.concatenate([w_hi, w_lo], axis=1)
    outs = pl.pallas_call(
        _router_kernel,
        grid=(t // tm,),
        in_specs=[pl.BlockSpec((tm, d), lambda i: (i, 0)),
                  pl.BlockSpec((d, 2 * LANES), lambda i: (0, 0)),
                  pl.BlockSpec((n_exp, 1), lambda i: (0, 0))],
        out_specs=[pl.BlockSpec((TOP_K, tm), lambda i: (0, i)),
                   pl.BlockSpec((TOP_K, tm), lambda i: (0, i)),
                   pl.BlockSpec((TOP_K, tm), lambda i: (0, i)),
                   pl.BlockSpec((n_exp, LANES), lambda i: (0, 0))],
        out_shape=[jax.ShapeDtypeStruct((TOP_K, t), jnp.int32),
                   jax.ShapeDtypeStruct((TOP_K, t), F32),
                   jax.ShapeDtypeStruct((TOP_K, t), jnp.int32),
                   jax.ShapeDtypeStruct((n_exp, LANES), jnp.int32)],
        scratch_shapes=[pltpu.VMEM((n_exp, 1), F32)],
        compiler_params=_cparams(("arbitrary",)),
        name="router",
    )(h, w_padded, b_router)
    return outs


SC_CORES = 2
SC_SUBCORES = 16
SC_GATHER_CHUNK = 128
MOE_SPLITS = 2


def _sc_gather_rows(table, idx):
    b = idx.shape[0]
    d = table.shape[1]
    n_workers = SC_CORES * SC_SUBCORES
    chunk = SC_GATHER_CHUNK
    assert b % (n_workers * chunk) == 0
    per_worker = b // n_workers
    n_chunks = per_worker // chunk
    mesh = plsc.VectorSubcoreMesh(core_axis_name="c", subcore_axis_name="s",
                                  num_cores=SC_CORES, num_subcores=SC_SUBCORES)

    def body(table_hbm, idx_hbm, out_hbm, idx_v, rows_v, sem):
        wid = lax.axis_index("s") * SC_CORES + lax.axis_index("c")
        base = wid * per_worker

        @pl.loop(0, n_chunks)
        def _(c):
            off = pl.multiple_of(base + c * chunk, SUBLANES)
            pltpu.sync_copy(idx_hbm.at[pl.ds(off, chunk)], idx_v)
            pltpu.async_copy(table_hbm.at[idx_v], rows_v, sem).wait()
            pltpu.sync_copy(rows_v, out_hbm.at[pl.ds(off, chunk)])

    return pl.kernel(
        body,
        out_type=jax.ShapeDtypeStruct((b, d), table.dtype),
        mesh=mesh,
        scratch_types=[pltpu.VMEM((chunk,), jnp.int32),
                       pltpu.VMEM((chunk, d), table.dtype),
                       pltpu.SemaphoreType.DMA],
        name="sc_gather_rows",
    )(table, idx)


def _sc_scatter_rows(src, dest_kt, fill_idx, n_out_rows):
    t, d = src.shape
    n_k = dest_kt.shape[0]
    n_fill = fill_idx.shape[0]
    n_workers = SC_CORES * SC_SUBCORES
    chunk = SC_GATHER_CHUNK
    assert t % (n_workers * chunk) == 0 and n_fill % (n_workers * chunk) == 0
    tok_per_worker = t // n_workers
    fill_per_worker = n_fill // n_workers
    mesh = plsc.VectorSubcoreMesh(core_axis_name="c", subcore_axis_name="s",
                                  num_cores=SC_CORES, num_subcores=SC_SUBCORES)

    def body(src_hbm, dest_hbm, fill_hbm, zeros_hbm, out_hbm, idx_v, rows_v):
        wid = lax.axis_index("s") * SC_CORES + lax.axis_index("c")

        pltpu.sync_copy(zeros_hbm, rows_v)

        @pl.loop(0, fill_per_worker // chunk)
        def _(c):
            off = pl.multiple_of(wid * fill_per_worker + c * chunk, SUBLANES)
            pltpu.sync_copy(fill_hbm.at[pl.ds(off, chunk)], idx_v)
            pltpu.sync_copy(rows_v, out_hbm.at[idx_v])

        @pl.loop(0, tok_per_worker // chunk)
        def _(c):
            off = pl.multiple_of(wid * tok_per_worker + c * chunk, SUBLANES)
            pltpu.sync_copy(src_hbm.at[pl.ds(off, chunk)], rows_v)
            for k in range(n_k):
                pltpu.sync_copy(dest_hbm.at[pl.ds(k * t + off, chunk)], idx_v)
                pltpu.sync_copy(rows_v, out_hbm.at[idx_v])

    return pl.kernel(
        body,
        out_type=jax.ShapeDtypeStruct((n_out_rows, d), src.dtype),
        mesh=mesh,
        scratch_types=[pltpu.VMEM((chunk,), jnp.int32),
                       pltpu.VMEM((chunk, d), src.dtype)],
        name="sc_scatter_rows",
    )(src, dest_kt.reshape(-1), fill_idx, jnp.zeros((chunk, d), src.dtype))


EXPERT_SUBROWS = 128


def _expert_kernel(be_ref, nxt_ref, eslot_ref, nsub_ref, nu_ref, x_ref, wgu_hbm, bgu_ref, wd_hbm,
                   bd_ref, y_ref, wgu_f, wd_f, wsem, wgu_b, wd_b):
    i = pl.program_id(0)
    n_used = nu_ref[0]
    expert = be_ref[i]
    changed = jnp.logical_or(i == 0, expert != be_ref[jnp.maximum(i - 1, 0)])

    def weight_copies(e, slot):
        return (pltpu.make_async_copy(wgu_hbm.at[e], wgu_f.at[slot], wsem.at[0, slot]),
                pltpu.make_async_copy(wd_hbm.at[e], wd_f.at[slot], wsem.at[1, slot]))

    @pl.when(jnp.logical_and(changed, i < n_used))
    def _():
        slot = eslot_ref[i]

        @pl.when(i == 0)
        def _():
            for cp in weight_copies(expert, slot):
                cp.start()

        for cp in weight_copies(expert, slot):
            cp.wait()
        nxt = nxt_ref[i]

        @pl.when(nxt >= 0)
        def _():
            for cp in weight_copies(nxt, 1 - slot):
                cp.start()

        wgu_b[...] = wgu_f[slot].astype(BF16)
        wd_b[...] = wd_f[slot].astype(BF16)

    def ffn(rows):
        d_ff = wd_b.shape[0]
        xb = _unpack_bf16_pair(x_ref[0:rows, :]).astype(BF16)
        hgu = jnp.dot(xb, wgu_b[...], preferred_element_type=F32) + bgu_ref[0]
        glu = jnp.minimum(hgu[:, :d_ff], SWIGLU_LIMIT)
        lin = jnp.clip(hgu[:, d_ff:], -SWIGLU_LIMIT, SWIGLU_LIMIT)
        act = glu * _sigmoid(SWIGLU_ALPHA * glu) * (lin + 1.0)
        y = jnp.dot(act.astype(BF16), wd_b[...], preferred_element_type=F32) + bd_ref[0]
        y_ref[0:rows, :] = _pack_bf16_pair(y)
        if rows < y_ref.shape[0]:
            y_ref[rows:, :] = jnp.zeros((y_ref.shape[0] - rows, y_ref.shape[1]), y_ref.dtype)

    n_sub = nsub_ref[i]
    for sub in range(1, y_ref.shape[0] // EXPERT_SUBROWS + 1):
        @pl.when(jnp.logical_and(i < n_used, n_sub == sub))
        def _(sub=sub):
            ffn(sub * EXPERT_SUBROWS)


def _experts(xs, block_expert, block_valid, n_used, wgu, bgu, wd, bd):
    d = wgu.shape[1]
    dp = xs.shape[1]
    assert 2 * dp == d
    n_blocks = block_expert.shape[0]
    n_exp, _, two_f = wgu.shape
    d_ff = wd.shape[1]
    rows = MOE_BLOCK
    n_rows = n_blocks * rows
    blk = jnp.arange(n_blocks, dtype=jnp.int32)
    is_first = jnp.concatenate([jnp.ones((1,), bool), block_expert[1:] != block_expert[:-1]])
    first_used = jnp.logical_and(is_first, blk < n_used[0])
    first_pos = jnp.where(first_used, blk, n_blocks)
    next_first = jnp.concatenate([lax.cummin(first_pos[::-1])[::-1][1:],
                                  jnp.full((1,), n_blocks, jnp.int32)])
    next_expert = jnp.where(next_first < n_blocks,
                            block_expert[jnp.minimum(next_first, n_blocks - 1)], -1).astype(jnp.int32)
    expert_slot = ((jnp.cumsum(is_first.astype(jnp.int32)) - 1) % 2).astype(jnp.int32)
    n_sub = jnp.clip(-(-block_valid // EXPERT_SUBROWS), 1, rows // EXPERT_SUBROWS).astype(jnp.int32)

    def last_used(i, be, nxt, es, ns, nu):
        return (jnp.minimum(i, jnp.maximum(nu[0] - 1, 0)), 0)

    grid_spec = pltpu.PrefetchScalarGridSpec(
        num_scalar_prefetch=5,
        grid=(n_blocks,),
        in_specs=[
            pl.BlockSpec((rows, dp), last_used),
            pl.BlockSpec(memory_space=pl.ANY),
            pl.BlockSpec((1, 1, two_f), lambda i, be, nxt, es, ns, nu: (be[i], 0, 0)),
            pl.BlockSpec(memory_space=pl.ANY),
            pl.BlockSpec((1, 1, d), lambda i, be, nxt, es, ns, nu: (be[i], 0, 0)),
        ],
        out_specs=pl.BlockSpec((rows, dp), last_used),
        scratch_shapes=[pltpu.VMEM((2, d, two_f), F32),
                        pltpu.VMEM((2, d_ff, d), F32),
                        pltpu.SemaphoreType.DMA((2, 2)),
                        pltpu.VMEM((d, two_f), BF16),
                        pltpu.VMEM((d_ff, d), BF16)],
    )
    return pl.pallas_call(
        _expert_kernel,
        grid_spec=grid_spec,
        out_shape=jax.ShapeDtypeStruct((n_rows, dp), jnp.uint32),
        compiler_params=_cparams(("arbitrary",)),
        name="experts",
    )(block_expert, next_expert, expert_slot, n_sub, n_used, xs, wgu, bgu.reshape(n_exp, 1, two_f), wd,
      bd.reshape(n_exp, 1, d))


def _combine_kernel(alpha, *refs):
    ys_refs = refs[:TOP_K]
    gw_ref, h_ref, lg_ref, lb_ref = refs[TOP_K:TOP_K + 4]
    o_ref = refs[-1]
    gw = gw_ref[...]
    acc = alpha * h_ref[...]
    for k in range(TOP_K):
        acc = acc + gw[:, k:k + 1] * _unpack_bf16_pair(ys_refs[k][...])
    o_ref[...] = _layer_norm(acc, lg_ref[...], lb_ref[...])


def _combine(alpha, y_slots, gw_rows, h, lg, lb, tm, out_prev, row0, total_rows):
    rows, d = h.shape
    tm = min(tm, rows)
    assert rows % tm == 0 and row0 % tm == 0
    nsteps = rows // tm
    blk0 = row0 // tm
    ys_specs = [pl.BlockSpec((tm, d // 2), functools.partial(lambda i, k: (k * nsteps + i, 0), k=k))
                for k in range(TOP_K)]
    in_specs = ys_specs + [pl.BlockSpec((tm, LANES), lambda i: (i, 0)),
                           pl.BlockSpec((tm, d), lambda i: (i, 0)),
                           pl.BlockSpec((1, d), lambda i: (0, 0)),
                           pl.BlockSpec((1, d), lambda i: (0, 0))]
    args = [y_slots] * TOP_K + [gw_rows, h, lg, lb]
    aliases = {}
    if out_prev is not None:
        in_specs.append(pl.BlockSpec(memory_space=pl.ANY))
        aliases = {len(args): 0}
        args.append(out_prev)
    return pl.pallas_call(
        functools.partial(_combine_kernel, alpha),
        grid=(nsteps,),
        in_specs=in_specs,
        out_specs=pl.BlockSpec((tm, d), lambda i: (i + blk0, 0)),
        out_shape=jax.ShapeDtypeStruct((total_rows, d), F32),
        input_output_aliases=aliases,
        compiler_params=_cparams(("parallel",)),
        name="combine_ln2",
    )(*args)


def _pack_in_proj(w_in, d_inner, n_ssm_heads, attn_dim):
    gn = SSM_GROUPS * D_STATE
    kv_dim = KV_HEADS * HEAD_DIM
    d = w_in.shape[0]
    o = 0
    seg = {}
    for name, width in (("z", d_inner), ("x", d_inner), ("B", gn), ("C", gn), ("dt", n_ssm_heads),
                        ("q", attn_dim), ("k", kv_dim), ("v", kv_dim), ("gs", d), ("ga", d)):
        seg[name] = w_in[:, o:o + width]
        o += width
    assert o == w_in.shape[1]
    dt_pad = jnp.pad(seg["dt"], ((0, 0), (0, LANES - n_ssm_heads)))
    order = (("z", seg["z"]), ("x", seg["x"]), ("gs", seg["gs"]), ("ga", seg["ga"]),
             ("B", seg["B"]), ("C", seg["C"]), ("dt", dt_pad))
    col, off = {}, 0
    for name, w in order:
        col[name] = off
        assert off % w.shape[1] == 0
        off += w.shape[1]
    w_f32grp = jnp.concatenate([w for _, w in order], axis=1).astype(BF16)
    w_qkv = jnp.concatenate([seg["q"], seg["k"], seg["v"]], axis=1).astype(BF16)
    return w_f32grp, w_qkv, col


def _largest_tile(n, cap):
    best = LANES
    for k in range(1, n // LANES + 1):
        if n % (k * LANES) == 0 and k * LANES <= cap:
            best = k * LANES
    return best


def _layer(h_in, batch, seq, alpha, w_in, conv_w, conv_b, dt_bias, a_log, d_skip, ssm_norm_w,
           w_ssm_out, attn_sinks, w_attn_out, b_gates, w_mix_out, ln1_g, ln1_b, w_router, b_router,
           w_gate_up, b_gate_up, w_down, b_down, ln2_g, ln2_b):
    t, d = h_in.shape
    d_inner = ssm_norm_w.shape[0]
    n_ssm_heads = dt_bias.shape[0]
    attn_dim = w_attn_out.shape[0]
    n_heads = attn_sinks.shape[0]
    n_exp = w_router.shape[1]
    gn = SSM_GROUPS * D_STATE

    w_f32grp, w_qkv, col = _pack_in_proj(w_in, d_inner, n_ssm_heads, attn_dim)
    proj = _matmul(h_in, w_f32grp, F32, 1024, _largest_tile(w_f32grp.shape[1], 2560))
    qkv = _matmul(h_in, w_qkv, BF16, 1024, _largest_tile(w_qkv.shape[1], 1536))

    pad_h = (0, LANES - n_ssm_heads)
    y_ssm = _ssd(proj, batch, seq, col,
                 conv_w[:, :d_inner], conv_w[:, d_inner:d_inner + gn], conv_w[:, d_inner + gn:],
                 conv_b[None, :d_inner], conv_b[None, d_inner:d_inner + gn], conv_b[None, d_inner + gn:],
                 jnp.pad(dt_bias, pad_h)[None, :], jnp.pad(a_log, pad_h)[None, :],
                 jnp.repeat(d_skip, SSM_HEAD_DIM)[None, :], ssm_norm_w[None, :])
    y_att = _swa(qkv, attn_sinks, batch, seq, n_heads)

    wso, wao, wmix = w_ssm_out.astype(BF16), w_attn_out.astype(BF16), w_mix_out.astype(BF16)

    n_parts = MOE_SPLITS if t % (MOE_SPLITS * SC_CORES * SC_SUBCORES * SC_GATHER_CHUNK) == 0 else 1
    rows = t // n_parts
    out = None
    dest_kt = None
    for part in range(n_parts):
        row0 = part * rows
        h1, h1_packed = _merge(alpha, y_ssm, y_att, proj, col, h_in, wso, wao, wmix, b_gates[None, :d],
                               b_gates[None, d:], ln1_g[None, :], ln1_b[None, :], 512, row0, rows, dest_kt)
        idx_kt, gw_kt, rank_kt, counts = _router(h1, w_router, b_router[:, None], 1024)

        counts = counts[:, 0]
        padded = (counts + MOE_BLOCK - 1) // MOE_BLOCK * MOE_BLOCK
        pad_end = jnp.cumsum(padded)
        pad_start = pad_end - padded
        n_blocks = -(-(rows * TOP_K) // MOE_BLOCK) + n_exp
        n_rows = n_blocks * MOE_BLOCK
        expert_ids = jnp.arange(n_exp, dtype=jnp.int32)
        pad_start_of_slot = jnp.sum(
            jnp.where(idx_kt[None] == expert_ids[:, None, None], pad_start[:, None, None], 0), axis=0)
        dest_kt = pad_start_of_slot + rank_kt
        fill_rows = (pad_start + counts)[:, None] + jnp.arange(MOE_BLOCK, dtype=jnp.int32)[None, :]
        spare_rows = n_rows + jnp.arange(n_exp * MOE_BLOCK, dtype=jnp.int32).reshape(n_exp, MOE_BLOCK)
        fill_idx = jnp.where(fill_rows < pad_end[:, None], fill_rows, spare_rows).reshape(-1)
        block_row0 = jnp.arange(n_blocks, dtype=jnp.int32) * MOE_BLOCK
        block_expert = jnp.minimum(
            jnp.sum((pad_end[None, :] <= block_row0[:, None]).astype(jnp.int32), axis=1), n_exp - 1)
        n_used = (pad_end[-1:] // MOE_BLOCK).astype(jnp.int32)
        real_end_of_block = jnp.sum(jnp.where(block_expert[:, None] == expert_ids[None, :],
                                              (pad_start + counts)[None, :], 0), axis=1)
        block_valid = jnp.clip(real_end_of_block - block_row0, 0, MOE_BLOCK)
        gw_rows = jnp.pad(gw_kt.T, ((0, 0), (0, LANES - TOP_K)))

        xs = _sc_scatter_rows(h1_packed, dest_kt, fill_idx, n_rows + n_exp * MOE_BLOCK)
        ys = _experts(xs, block_expert, block_valid, n_used, w_gate_up, b_gate_up, w_down, b_down)
        y_slots = _sc_gather_rows(ys, dest_kt.reshape(-1))
        out = _combine(alpha, y_slots, gw_rows, h1, ln2_g[None, :], ln2_b[None, :], 512, out, row0, t)
    return out


def kernel(x, w_in, conv_w, conv_b, dt_bias, a_log, d_skip, ssm_norm_w, w_ssm_out, attn_sinks,
           w_attn_out, b_gates, w_mix_out, ln1_g, ln1_b, w_router, b_router, w_gate_up, b_gate_up,
           w_down, b_down, ln2_g, ln2_b):
    batch, seq, d = x.shape
    depth = w_in.shape[0]
    alpha = (2 * depth) ** 0.25
    h = x.reshape(batch * seq, d)
    for i in range(depth):
        h = _layer(h, batch, seq, alpha, w_in[i], conv_w[i], conv_b[i], dt_bias[i], a_log[i],
                   d_skip[i], ssm_norm_w[i], w_ssm_out[i], attn_sinks[i], w_attn_out[i], b_gates[i],
                   w_mix_out[i], ln1_g[i], ln1_b[i], w_router[i], b_router[i], w_gate_up[i],
                   b_gate_up[i], w_down[i], b_down[i], ln2_g[i], ln2_b[i])
    return h.reshape(batch, seq, d)
```

```python
import functools

import jax
import jax.numpy as jnp
from jax import lax
from jax.experimental import pallas as pl
from jax.experimental.pallas import tpu as pltpu
from jax.experimental.pallas import tpu_sc as plsc

SSM_HEAD_DIM = 64
SSM_GROUPS = 4
D_STATE = 128
CONV_WIDTH = 4
SSD_CHUNK = 128
SEQS_PER_STEP = 2
KV_HEADS = 4
HEAD_DIM = 64
WINDOW = 128
TOP_K = 4
SWIGLU_LIMIT = 7.0
SWIGLU_ALPHA = 1.702
MOE_BLOCK = 512
LN_EPS = 1e-5
RMS_EPS = 1e-5

LANES = 128
SUBLANES = 8
NEG_BIG = -1e30
LOG2_E = 1.4426950408889634
F32 = jnp.float32
BF16 = jnp.bfloat16
VMEM_LIMIT = 56 * 1024 * 1024


def _cparams(sem):
    return pltpu.CompilerParams(dimension_semantics=sem, vmem_limit_bytes=VMEM_LIMIT)


def _sigmoid(x):
    return 0.5 + 0.5 * jnp.tanh(0.5 * x)


def _pack_bf16_pair(v):
    w = v.shape[1] // 2
    bits = lax.bitcast_convert_type(v.astype(BF16).astype(F32), jnp.uint32)
    return (bits[:, :w] >> 16) | (bits[:, w:] & jnp.uint32(0xFFFF0000))


def _unpack_bf16_pair(p):
    lo = lax.bitcast_convert_type(p << 16, F32)
    hi = lax.bitcast_convert_type(p & jnp.uint32(0xFFFF0000), F32)
    return jnp.concatenate([lo, hi], axis=1)


def _silu(x):
    h = 0.5 * x
    return h + h * jnp.tanh(h)


def _mm_kernel(x_ref, w_ref, o_ref):
    o_ref[...] = jnp.dot(x_ref[...].astype(BF16), w_ref[...],
                         preferred_element_type=F32).astype(o_ref.dtype)


def _matmul(x, w, out_dtype, tm, tn):
    m, k = x.shape
    n = w.shape[1]
    tm = min(tm, m)
    assert m % tm == 0 and n % tn == 0
    return pl.pallas_call(
        _mm_kernel,
        grid=(n // tn, m // tm),
        in_specs=[pl.BlockSpec((tm, k), lambda j, i: (i, 0)),
                  pl.BlockSpec((k, tn), lambda j, i: (0, j))],
        out_specs=pl.BlockSpec((tm, tn), lambda j, i: (i, j)),
        out_shape=jax.ShapeDtypeStruct((m, n), out_dtype),
        compiler_params=_cparams(("parallel", "parallel")),
        name="in_proj",
    )(x, w)


def _conv_silu(u_ref, prev_ref, w_ref, b_ref, first):
    cur = u_ref[...]
    prev = prev_ref[...]
    prev = jnp.where(first, jnp.zeros_like(prev), prev)
    row8 = lax.broadcasted_iota(jnp.int32, prev.shape, 0)
    acc = b_ref[...] + w_ref[CONV_WIDTH - 1:CONV_WIDTH, :] * cur
    for k in range(1, CONV_WIDTH):
        rolled = pltpu.roll(cur, k, axis=0)
        head = jnp.where(row8 < k, pltpu.roll(prev, k, axis=0), rolled[:SUBLANES])
        shifted = jnp.concatenate([head, rolled[SUBLANES:]], axis=0)
        j = CONV_WIDTH - 1 - k
        acc = acc + w_ref[j:j + 1, :] * shifted
    return _silu(acc)


def _expand_columns(v, e3_ref):
    x1 = v.astype(BF16)
    r1 = v - x1.astype(F32)
    x2 = r1.astype(BF16)
    x3 = (r1 - x2.astype(F32)).astype(BF16)
    return jnp.dot(jnp.concatenate([x1, x2, x3], axis=1), e3_ref[...], preferred_element_type=F32)


def _ssd_kernel(z_ref, x_ref, b_ref, c_ref, dt_ref, xp_ref, bp_ref, cp_ref,
                cwx_ref, cwb_ref, cwc_ref, cbx_ref, cbb_ref, cbc_ref,
                dtb_ref, alog_ref, dsk_ref, nw_ref, ehead_ref, ecol_ref,
                o_ref, st_ref):
    first = pl.program_id(1) == 0

    @pl.when(first)
    def _():
        st_ref[...] = jnp.zeros_like(st_ref)

    for s in range(z_ref.shape[0]):
        _ssd_chunk(z_ref.at[s], x_ref.at[s], b_ref.at[s], c_ref.at[s], dt_ref.at[s],
                   xp_ref.at[s], bp_ref.at[s], cp_ref.at[s],
                   cwx_ref, cwb_ref, cwc_ref, cbx_ref, cbb_ref, cbc_ref,
                   dtb_ref, alog_ref, dsk_ref, nw_ref, ehead_ref, ecol_ref,
                   o_ref.at[s], st_ref.at[s], first)


def _ssd_chunk(z_ref, x_ref, b_ref, c_ref, dt_ref, xp_ref, bp_ref, cp_ref,
               cwx_ref, cwb_ref, cwc_ref, cbx_ref, cbb_ref, cbc_ref,
               dtb_ref, alog_ref, dsk_ref, nw_ref, ehead_ref, ecol_ref, o_ref, st_ref, first):
    n = SSD_CHUNK
    heads_per_group = st_ref.shape[2] // SSM_HEAD_DIM
    gw = heads_per_group * SSM_HEAD_DIM

    xc = _conv_silu(x_ref, xp_ref, cwx_ref, cbx_ref, first)
    bc = _conv_silu(b_ref, bp_ref, cwb_ref, cbb_ref, first)
    cc = _conv_silu(c_ref, cp_ref, cwc_ref, cbc_ref, first)

    dt_in = dt_ref[...] + dtb_ref[...]
    dt = jnp.maximum(dt_in, 0.0) + jnp.log(1.0 + jnp.exp(-jnp.abs(dt_in)))
    a = dt * (-jnp.exp(alog_ref[...]))
    row = lax.broadcasted_iota(jnp.int32, (n, n), 0)
    col = lax.broadcasted_iota(jnp.int32, (n, n), 1)
    causal = row >= col
    tri = jnp.where(causal, 1.0, 0.0).astype(F32)
    acum = jnp.dot(tri, a, preferred_element_type=F32, precision=lax.Precision.HIGHEST)
    acum2 = acum * LOG2_E
    acum2_t = acum2.T
    a_last = acum[n - 1:n, :]
    lo_mask = lax.broadcasted_iota(jnp.int32, (n, LANES), 1) < SSM_HEAD_DIM
    dt_x = _expand_columns(dt, ehead_ref)
    d2e_x = _expand_columns(jnp.exp(a_last - acum), ehead_ref)
    eac_x = _expand_columns(jnp.exp(acum), ehead_ref)
    acum2_x = _expand_columns(acum2, ecol_ref)

    for g in range(SSM_GROUPS):
        xg = xc[:, g * gw:(g + 1) * gw]
        bg = bc[:, g * D_STATE:(g + 1) * D_STATE]
        cg = cc[:, g * D_STATE:(g + 1) * D_STATE]
        cg_b = cg.astype(BF16)
        bg_t = bg.T.astype(BF16)
        cb = jnp.dot(cg_b, bg_t, preferred_element_type=F32) * tri
        st_prev = st_ref[g]
        y_off = jnp.dot(cg_b, st_prev.astype(BF16), preferred_element_type=F32)
        y_parts, xw_parts, ea_parts = [], [], []
        for j in range(heads_per_group // 2):
            h0 = g * heads_per_group + 2 * j
            lanes = slice(j * LANES, (j + 1) * LANES)
            glanes = slice(g * gw + j * LANES, g * gw + (j + 1) * LANES)
            xs_pair = xg[:, lanes] * dt_x[:, glanes]
            xs_b = xs_pair.astype(BF16)
            xw_parts.append((xs_pair * d2e_x[:, glanes]).astype(BF16))
            ea_parts.append(eac_x[:, glanes])
            ys = []
            for h in (h0, h0 + 1):
                seg2 = acum2_x[:, h * n:(h + 1) * n] - acum2_t[h:h + 1, :]
                decay = jnp.exp2(jnp.minimum(seg2, 0.0))
                m = (cb * decay).astype(BF16)
                ys.append(jnp.dot(m, xs_b, preferred_element_type=F32))
            y_parts.append(jnp.where(lo_mask, ys[0], ys[1]))
        y_diag = jnp.concatenate(y_parts, axis=1)
        ea = jnp.concatenate(ea_parts, axis=1)
        xw = jnp.concatenate(xw_parts, axis=1)
        cols = slice(g * gw, (g + 1) * gw)
        y = y_diag + y_off * ea + xg * dsk_ref[:, cols]
        zg = z_ref[:, cols]
        y = y * _silu(zg)
        ms = jnp.mean(y * y, axis=-1, keepdims=True)
        o_ref[:, cols] = (y * lax.rsqrt(ms + RMS_EPS) * nw_ref[:, cols]).astype(o_ref.dtype)
        st_ref[g] = st_prev * ea[n - 1:n, :] + jnp.dot(bg_t, xw, preferred_element_type=F32)


def _ssd(proj, batch, seq, col, conv_wx, conv_wb, conv_wc, conv_bx, conv_bb, conv_bc,
         dt_bias, a_log, d_skip, norm_w):
    t, width = proj.shape
    d_inner = norm_w.shape[1]
    gn = SSM_GROUPS * D_STATE
    n = SSD_CHUNK
    nc = seq // n
    gw = d_inner // SSM_GROUPS
    ns = SEQS_PER_STEP if batch % SEQS_PER_STEP == 0 else 1
    proj3 = proj.reshape(batch, seq, width)
    n_heads = d_inner // SSM_HEAD_DIM
    head_id = jnp.arange(LANES, dtype=jnp.int32)[:, None]
    e_head = (head_id == (jnp.arange(d_inner, dtype=jnp.int32) // SSM_HEAD_DIM)[None, :]).astype(BF16)
    e_col = (head_id == (jnp.arange(n_heads * n, dtype=jnp.int32) // n)[None, :]).astype(BF16)
    e_head3 = jnp.tile(e_head, (3, 1))
    e_col3 = jnp.tile(e_col, (3, 1))

    def prev_rows(c):
        return jnp.maximum(c * (n // SUBLANES) - 1, 0)

    def const(b, c):
        return (0, 0)

    in_specs = [
        pl.BlockSpec((ns, n, d_inner), lambda b, c: (b, c, col["z"] // d_inner)),
        pl.BlockSpec((ns, n, d_inner), lambda b, c: (b, c, col["x"] // d_inner)),
        pl.BlockSpec((ns, n, gn), lambda b, c: (b, c, col["B"] // gn)),
        pl.BlockSpec((ns, n, gn), lambda b, c: (b, c, col["C"] // gn)),
        pl.BlockSpec((ns, n, LANES), lambda b, c: (b, c, col["dt"] // LANES)),
        pl.BlockSpec((ns, SUBLANES, d_inner), lambda b, c: (b, prev_rows(c), col["x"] // d_inner)),
        pl.BlockSpec((ns, SUBLANES, gn), lambda b, c: (b, prev_rows(c), col["B"] // gn)),
        pl.BlockSpec((ns, SUBLANES, gn), lambda b, c: (b, prev_rows(c), col["C"] // gn)),
        pl.BlockSpec((CONV_WIDTH, d_inner), const),
        pl.BlockSpec((CONV_WIDTH, gn), const),
        pl.BlockSpec((CONV_WIDTH, gn), const),
        pl.BlockSpec((1, d_inner), const),
        pl.BlockSpec((1, gn), const),
        pl.BlockSpec((1, gn), const),
        pl.BlockSpec((1, LANES), const),
        pl.BlockSpec((1, LANES), const),
        pl.BlockSpec((1, d_inner), const),
        pl.BlockSpec((1, d_inner), const),
        pl.BlockSpec(e_head3.shape, const),
        pl.BlockSpec(e_col3.shape, const),
    ]
    out = pl.pallas_call(
        _ssd_kernel,
        grid=(batch // ns, nc),
        in_specs=in_specs,
        out_specs=pl.BlockSpec((ns, n, d_inner), lambda b, c: (b, c, 0)),
        out_shape=jax.ShapeDtypeStruct((batch, seq, d_inner), BF16),
        scratch_shapes=[pltpu.VMEM((ns, SSM_GROUPS, D_STATE, gw), F32)],
        compiler_params=_cparams(("arbitrary", "arbitrary")),
        name="ssd_mixer",
    )(proj3, proj3, proj3, proj3, proj3, proj3, proj3, proj3,
      conv_wx, conv_wb, conv_wc, conv_bx, conv_bb, conv_bc, dt_bias, a_log, d_skip, norm_w,
      e_head3, e_col3)
    return out.reshape(t, d_inner)


def _swa_kernel(sink_ref, q_ref, kc_ref, vc_ref, kp_ref, vp_ref, wao_ref, o_ref, att_ref):
    i = pl.program_id(1)
    n_blocks = pl.num_programs(1) - 1
    ns, n, qw = att_ref.shape

    def project_previous():
        proj = jnp.dot(att_ref[...].reshape(ns * n, qw), wao_ref[...], preferred_element_type=F32)
        for s in range(ns):
            o_ref[s] = proj[s * n:(s + 1) * n, :]

    @pl.when(i == 0)
    def _():
        att_ref[...] = jnp.zeros_like(att_ref)

    @pl.when(i < n_blocks)
    def _():
        project_previous()
        for s in range(ns):
            _swa_block(sink_ref, q_ref.at[s], kc_ref.at[s], vc_ref.at[s], kp_ref.at[s], vp_ref.at[s],
                       att_ref.at[s])

    @pl.when(i == n_blocks)
    def _():
        project_previous()


def _swa_block(sink_ref, q_ref, kc_ref, vc_ref, kp_ref, vp_ref, o_ref):
    i = pl.program_id(1)
    n = WINDOW
    n_heads = q_ref.shape[1] // HEAD_DIM
    grp = n_heads // KV_HEADS
    qpos = lax.broadcasted_iota(jnp.int32, (n, 2 * n), 0) + n
    kpos = lax.broadcasted_iota(jnp.int32, (n, 2 * n), 1)
    diff = qpos - kpos
    mask = (diff >= 0) & (diff < n) & ((kpos >= n) | (i > 0))
    lo_q = lax.broadcasted_iota(jnp.int32, (n, LANES), 1) < HEAD_DIM
    lo_k = lax.broadcasted_iota(jnp.int32, (2 * n, LANES), 1) < HEAD_DIM
    scale = HEAD_DIM ** -0.5
    exp2_scale = scale * LOG2_E
    for p in range(KV_HEADS // 2):
        lanes = slice(p * LANES, (p + 1) * LANES)
        kk = jnp.concatenate([kp_ref[:, lanes], kc_ref[:, lanes]], axis=0).astype(F32)
        vv = jnp.concatenate([vp_ref[:, lanes], vc_ref[:, lanes]], axis=0).astype(F32)
        kk_sw = pltpu.roll(kk, HEAD_DIM, axis=1)
        vv_sw = pltpu.roll(vv, HEAD_DIM, axis=1)
        for par in range(2):
            g = 2 * p + par
            k2 = (jnp.where(lo_k, kk, kk_sw) if par == 0 else jnp.where(lo_k, kk_sw, kk)).astype(BF16)
            v2 = (jnp.where(lo_k, vv, vv_sw) if par == 0 else jnp.where(lo_k, vv_sw, vv)).astype(BF16)
            q_rows = []
            for hq in range(grp):
                h = g * grp + hq
                q_pair = q_ref[:, (h // 2) * LANES:(h // 2 + 1) * LANES]
                keep = lo_q if h % 2 == 0 else jnp.logical_not(lo_q)
                q_rows.append(jnp.where(keep, q_pair, jnp.zeros_like(q_pair)))
            s_all = lax.dot_general(jnp.concatenate(q_rows, axis=0), k2, (((1,), (1,)), ((), ())),
                                    preferred_element_type=F32)
            probs, dens = [], []
            for hq in range(grp):
                s = jnp.where(mask, s_all[hq * n:(hq + 1) * n, :], NEG_BIG)
                sink = sink_ref[g * grp + hq] / scale
                mx = jnp.maximum(jnp.max(s, axis=-1, keepdims=True), sink)
                pr = jnp.exp2((s - mx) * exp2_scale)
                dens.append(jnp.sum(pr, axis=-1, keepdims=True) + jnp.exp2((sink - mx) * exp2_scale))
                probs.append(pr.astype(BF16))
            o_all = jnp.dot(jnp.concatenate(probs, axis=0), v2, preferred_element_type=F32)
            for qp in range(grp // 2):
                outs = [o_all[(2 * qp + hh) * n:(2 * qp + hh + 1) * n, :] / dens[2 * qp + hh] for hh in range(2)]
                pair = (g * grp) // 2 + qp
                o_ref[:, pair * LANES:(pair + 1) * LANES] = jnp.where(lo_q, outs[0], outs[1]).astype(o_ref.dtype)


def _swa(qkv, sinks, batch, seq, n_heads, w_out):
    t, width = qkv.shape
    n = WINDOW
    nb = seq // n
    qw = n_heads * HEAD_DIM
    kw = KV_HEADS * HEAD_DIM
    k_blk = qw // kw
    v_blk = k_blk + 1
    ns = SEQS_PER_STEP if batch % SEQS_PER_STEP == 0 else 1
    qkv3 = qkv.reshape(batch, seq, width)

    d_out = w_out.shape[1]

    def cur(i):
        return jnp.minimum(i, nb - 1)

    def prev(i):
        return jnp.maximum(cur(i) - 1, 0)

    out = pl.pallas_call(
        _swa_kernel,
        grid=(batch // ns, nb + 1),
        in_specs=[pl.BlockSpec(memory_space=pltpu.SMEM),
                  pl.BlockSpec((ns, n, qw), lambda b, i: (b, cur(i), 0)),
                  pl.BlockSpec((ns, n, kw), lambda b, i: (b, cur(i), k_blk)),
                  pl.BlockSpec((ns, n, kw), lambda b, i: (b, cur(i), v_blk)),
                  pl.BlockSpec((ns, n, kw), lambda b, i: (b, prev(i), k_blk)),
                  pl.BlockSpec((ns, n, kw), lambda b, i: (b, prev(i), v_blk)),
                  pl.BlockSpec((qw, d_out), lambda b, i: (0, 0))],
        out_specs=pl.BlockSpec((ns, n, d_out), lambda b, i: (b, jnp.maximum(i - 1, 0), 0)),
        out_shape=jax.ShapeDtypeStruct((batch, seq, d_out), F32),
        scratch_shapes=[pltpu.VMEM((ns, n, qw), BF16)],
        compiler_params=_cparams(("arbitrary", "arbitrary")),
        name="swa",
    )(sinks, qkv3, qkv3, qkv3, qkv3, qkv3, w_out)
    return out.reshape(t, d_out)


def _layer_norm(v, g, b):
    mu = jnp.mean(v, axis=-1, keepdims=True)
    d = v - mu
    var = jnp.mean(d * d, axis=-1, keepdims=True)
    return d * lax.rsqrt(var + LN_EPS) * g + b


def _merge_kernel(alpha, ys_ref, ya_ref, gs_ref, ga_ref, x_ref, wso_ref, wmix_ref,
                  bgs_ref, bga_ref, lg_ref, lb_ref, *rest):
    h_ref, hp_ref = rest[-2:]
    y_ssm = jnp.dot(ys_ref[...], wso_ref[...], preferred_element_type=F32)
    merged = (_sigmoid(gs_ref[...] + bgs_ref[...]) * y_ssm
              + _sigmoid(ga_ref[...] + bga_ref[...]) * ya_ref[...])
    mix = jnp.dot(merged.astype(BF16), wmix_ref[...], preferred_element_type=F32)
    h = _layer_norm(alpha * x_ref[...] + mix, lg_ref[...], lb_ref[...])
    h_ref[...] = h
    hp_ref[...] = _pack_bf16_pair(h)


def _merge(alpha, y_ssm, y_att, proj, col, x, wso, wmix, bgs, bga, lg, lb, tm, row0, rows, after):
    d = x.shape[1]
    tm = min(tm, rows)
    assert rows % tm == 0 and row0 % tm == 0
    blk0 = row0 // tm
    di = y_ssm.shape[1]

    def const(i):
        return (0, 0)

    ordering = [] if after is None else [after]
    return pl.pallas_call(
        functools.partial(_merge_kernel, alpha),
        grid=(rows // tm,),
        in_specs=[pl.BlockSpec((tm, di), lambda i: (i + blk0, 0)),
                  pl.BlockSpec((tm, d), lambda i: (i + blk0, 0)),
                  pl.BlockSpec((tm, d), lambda i: (i + blk0, col["gs"] // d)),
                  pl.BlockSpec((tm, d), lambda i: (i + blk0, col["ga"] // d)),
                  pl.BlockSpec((tm, d), lambda i: (i + blk0, 0)),
                  pl.BlockSpec((di, d), const),
                  pl.BlockSpec((d, d), const),
                  pl.BlockSpec((1, d), const),
                  pl.BlockSpec((1, d), const),
                  pl.BlockSpec((1, d), const),
                  pl.BlockSpec((1, d), const)] + [pl.BlockSpec(memory_space=pl.ANY)] * len(ordering),
        out_specs=[pl.BlockSpec((tm, d), lambda i: (i, 0)),
                   pl.BlockSpec((tm, d // 2), lambda i: (i, 0))],
        out_shape=[jax.ShapeDtypeStruct((rows, d), F32),
                   jax.ShapeDtypeStruct((rows, d // 2), jnp.uint32)],
        compiler_params=_cparams(("parallel",)),
        name="merge_ln1",
    )(y_ssm, y_att, proj, proj, x, wso, wmix, bgs, bga, lg, lb, *ordering)


ROUTER_SUBTILE = 128


def _router_kernel(h_ref, wr_ref, br_ref, idx_ref, gw_ref, rank_ref, cnt_ref, base_ref):
    i = pl.program_id(0)
    tm = h_ref.shape[0]
    n_exp = br_ref.shape[0]
    sub = ROUTER_SUBTILE

    @pl.when(i == 0)
    def _():
        base_ref[...] = jnp.zeros_like(base_ref)

    h = h_ref[...]
    h1 = h.astype(BF16)
    h2 = (h - h1.astype(F32)).astype(BF16)
    first = jnp.dot(h1, wr_ref[...], preferred_element_type=F32)
    logits_t = (first[:, :LANES] + first[:, LANES:]
                + jnp.dot(h2, wr_ref[:, :LANES], preferred_element_type=F32))
    logits = logits_t.T[:n_exp, :] + br_ref[...]
    eid = lax.broadcasted_iota(jnp.int32, (n_exp, tm), 0)
    work = logits
    vals, idxs = [], []
    for _ in range(TOP_K):
        mx = jnp.max(work, axis=0, keepdims=True)
        sel = jnp.min(jnp.where(work == mx, eid, n_exp), axis=0, keepdims=True)
        vals.append(mx)
        idxs.append(sel)
        work = jnp.where(eid == sel, -jnp.inf, work)
    exps = [jnp.exp(v - vals[0]) for v in vals]
    den = exps[0]
    for e in exps[1:]:
        den = den + e
    r = lax.broadcasted_iota(jnp.int32, (sub, sub), 0)
    c = lax.broadcasted_iota(jnp.int32, (sub, sub), 1)
    tri = jnp.where(r < c, 1.0, 0.0).astype(BF16)
    base = base_ref[...]
    for k in range(TOP_K):
        onehot = eid == idxs[k]
        oh = jnp.where(onehot, 1.0, 0.0)
        ranks = []
        for s in range(tm // sub):
            lanes = slice(s * sub, (s + 1) * sub)
            before = jnp.dot(oh[:, lanes].astype(BF16), tri, preferred_element_type=F32)
            ranks.append(jnp.sum(jnp.where(onehot[:, lanes], before + base, 0.0), axis=0, keepdims=True))
            base = base + jnp.sum(oh[:, lanes], axis=1, keepdims=True)
        idx_ref[k:k + 1, :] = idxs[k]
        gw_ref[k:k + 1, :] = exps[k] / den
        rank_ref[k:k + 1, :] = jnp.concatenate(ranks, axis=1).astype(jnp.int32)
    base_ref[...] = base
    cnt_ref[...] = jnp.broadcast_to(base, cnt_ref.shape).astype(jnp.int32)


def _router(h, w_router, b_router, tm):
    t, d = h.shape
    n_exp = w_router.shape[1]
    tm = min(tm, t)
    assert tm % ROUTER_SUBTILE == 0 and n_exp <= LANES
    w_padded = jnp.pad(w_router, ((0, 0), (0, LANES - n_exp)))
    w_hi = w_padded.astype(BF16)
    w_lo = (w_padded - w_hi.astype(F32)).astype(BF16)
    w_padded = jnp.concatenate([w_hi, w_lo], axis=1)
    outs = pl.pallas_call(
        _router_kernel,
        grid=(t // tm,),
        in_specs=[pl.BlockSpec((tm, d), lambda i: (i, 0)),
                  pl.BlockSpec((d, 2 * LANES), lambda i: (0, 0)),
                  pl.BlockSpec((n_exp, 1), lambda i: (0, 0))],
        out_specs=[pl.BlockSpec((TOP_K, tm), lambda i: (0, i)),
                   pl.BlockSpec((TOP_K, tm), lambda i: (0, i)),
                   pl.BlockSpec((TOP_K, tm), lambda i: (0, i)),
                   pl.BlockSpec((n_exp, LANES), lambda i: (0, 0))],
        out_shape=[jax.ShapeDtypeStruct((TOP_K, t), jnp.int32),
                   jax.ShapeDtypeStruct((TOP_K, t), F32),
                   jax.ShapeDtypeStruct((TOP_K, t), jnp.int32),
                   jax.ShapeDtypeStruct((n_exp, LANES), jnp.int32)],
        scratch_shapes=[pltpu.VMEM((n_exp, 1), F32)],
        compiler_params=_cparams(("arbitrary",)),
        name="router",
    )(h, w_padded, b_router)
    return outs


SC_CORES = 2
SC_SUBCORES = 16
SC_GATHER_CHUNK = 128
MOE_SPLITS = 2


def _sc_gather_rows(table, idx):
    b = idx.shape[0]
    d = table.shape[1]
    n_workers = SC_CORES * SC_SUBCORES
    chunk = SC_GATHER_CHUNK
    assert b % (n_workers * chunk) == 0
    per_worker = b // n_workers
    n_chunks = per_worker // chunk
    mesh = plsc.VectorSubcoreMesh(core_axis_name="c", subcore_axis_name="s",
                                  num_cores=SC_CORES, num_subcores=SC_SUBCORES)

    def body(table_hbm, idx_hbm, out_hbm, idx_v, rows_v, sem):
        wid = lax.axis_index("s") * SC_CORES + lax.axis_index("c")
        base = wid * per_worker

        @pl.loop(0, n_chunks)
        def _(c):
            off = pl.multiple_of(base + c * chunk, SUBLANES)
            pltpu.sync_copy(idx_hbm.at[pl.ds(off, chunk)], idx_v)
            pltpu.async_copy(table_hbm.at[idx_v], rows_v, sem).wait()
            pltpu.sync_copy(rows_v, out_hbm.at[pl.ds(off, chunk)])

    return pl.kernel(
        body,
        out_type=jax.ShapeDtypeStruct((b, d), table.dtype),
        mesh=mesh,
        scratch_types=[pltpu.VMEM((chunk,), jnp.int32),
                       pltpu.VMEM((chunk, d), table.dtype),
                       pltpu.SemaphoreType.DMA],
        name="sc_gather_rows",
    )(table, idx)


def _sc_scatter_rows(src, dest_kt, fill_idx, n_out_rows):
    t, d = src.shape
    n_k = dest_kt.shape[0]
    n_fill = fill_idx.shape[0]
    n_workers = SC_CORES * SC_SUBCORES
    chunk = SC_GATHER_CHUNK
    assert t % (n_workers * chunk) == 0 and n_fill % (n_workers * chunk) == 0
    tok_per_worker = t // n_workers
    fill_per_worker = n_fill // n_workers
    mesh = plsc.VectorSubcoreMesh(core_axis_name="c", subcore_axis_name="s",
                                  num_cores=SC_CORES, num_subcores=SC_SUBCORES)

    def body(src_hbm, dest_hbm, fill_hbm, zeros_hbm, out_hbm, idx_v, rows_v):
        wid = lax.axis_index("s") * SC_CORES + lax.axis_index("c")

        pltpu.sync_copy(zeros_hbm, rows_v)

        @pl.loop(0, fill_per_worker // chunk)
        def _(c):
            off = pl.multiple_of(wid * fill_per_worker + c * chunk, SUBLANES)
            pltpu.sync_copy(fill_hbm.at[pl.ds(off, chunk)], idx_v)
            pltpu.sync_copy(rows_v, out_hbm.at[idx_v])

        @pl.loop(0, tok_per_worker // chunk)
        def _(c):
            off = pl.multiple_of(wid * tok_per_worker + c * chunk, SUBLANES)
            pltpu.sync_copy(src_hbm.at[pl.ds(off, chunk)], rows_v)
            for k in range(n_k):
                pltpu.sync_copy(dest_hbm.at[pl.ds(k * t + off, chunk)], idx_v)
                pltpu.sync_copy(rows_v, out_hbm.at[idx_v])

    return pl.kernel(
        body,
        out_type=jax.ShapeDtypeStruct((n_out_rows, d), src.dtype),
        mesh=mesh,
        scratch_types=[pltpu.VMEM((chunk,), jnp.int32),
                       pltpu.VMEM((chunk, d), src.dtype)],
        name="sc_scatter_rows",
    )(src, dest_kt.reshape(-1), fill_idx, jnp.zeros((chunk, d), src.dtype))


EXPERT_SUBROWS = 128


def _expert_kernel(be_ref, nxt_ref, eslot_ref, nsub_ref, nu_ref, x_ref, wgu_hbm, bgu_ref, wd_hbm,
                   bd_ref, y_ref, wgu_f, wd_f, wsem, wgu_b, wd_b):
    i = pl.program_id(0)
    n_used = nu_ref[0]
    expert = be_ref[i]
    changed = jnp.logical_or(i == 0, expert != be_ref[jnp.maximum(i - 1, 0)])

    def weight_copies(e, slot):
        return (pltpu.make_async_copy(wgu_hbm.at[e], wgu_f.at[slot], wsem.at[0, slot]),
                pltpu.make_async_copy(wd_hbm.at[e], wd_f.at[slot], wsem.at[1, slot]))

    @pl.when(jnp.logical_and(changed, i < n_used))
    def _():
        slot = eslot_ref[i]

        @pl.when(i == 0)
        def _():
            for cp in weight_copies(expert, slot):
                cp.start()

        for cp in weight_copies(expert, slot):
            cp.wait()
        nxt = nxt_ref[i]

        @pl.when(nxt >= 0)
        def _():
            for cp in weight_copies(nxt, 1 - slot):
                cp.start()

        wgu_b[...] = wgu_f[slot].astype(BF16)
        wd_b[...] = wd_f[slot].astype(BF16)

    def ffn(rows):
        d_ff = wd_b.shape[0]
        xb = _unpack_bf16_pair(x_ref[0:rows, :]).astype(BF16)
        hgu = jnp.dot(xb, wgu_b[...], preferred_element_type=F32) + bgu_ref[0]
        glu = jnp.minimum(hgu[:, :d_ff], SWIGLU_LIMIT)
        lin = jnp.clip(hgu[:, d_ff:], -SWIGLU_LIMIT, SWIGLU_LIMIT)
        act = glu * _sigmoid(SWIGLU_ALPHA * glu) * (lin + 1.0)
        y = jnp.dot(act.astype(BF16), wd_b[...], preferred_element_type=F32) + bd_ref[0]
        y_ref[0:rows, :] = _pack_bf16_pair(y)
        if rows < y_ref.shape[0]:
            y_ref[rows:, :] = jnp.zeros((y_ref.shape[0] - rows, y_ref.shape[1]), y_ref.dtype)

    n_sub = nsub_ref[i]
    for sub in range(1, y_ref.shape[0] // EXPERT_SUBROWS + 1):
        @pl.when(jnp.logical_and(i < n_used, n_sub == sub))
        def _(sub=sub):
            ffn(sub * EXPERT_SUBROWS)


def _experts(xs, block_expert, block_valid, n_used, wgu, bgu, wd, bd):
    d = wgu.shape[1]
    dp = xs.shape[1]
    assert 2 * dp == d
    n_blocks = block_expert.shape[0]
    n_exp, _, two_f = wgu.shape
    d_ff = wd.shape[1]
    rows = MOE_BLOCK
    n_rows = n_blocks * rows
    blk = jnp.arange(n_blocks, dtype=jnp.int32)
    is_first = jnp.concatenate([jnp.ones((1,), bool), block_expert[1:] != block_expert[:-1]])
    first_used = jnp.logical_and(is_first, blk < n_used[0])
    first_pos = jnp.where(first_used, blk, n_blocks)
    next_first = jnp.concatenate([lax.cummin(first_pos[::-1])[::-1][1:],
                                  jnp.full((1,), n_blocks, jnp.int32)])
    next_expert = jnp.where(next_first < n_blocks,
                            block_expert[jnp.minimum(next_first, n_blocks - 1)], -1).astype(jnp.int32)
    expert_slot = ((jnp.cumsum(is_first.astype(jnp.int32)) - 1) % 2).astype(jnp.int32)
    n_sub = jnp.clip(-(-block_valid // EXPERT_SUBROWS), 1, rows // EXPERT_SUBROWS).astype(jnp.int32)

    def last_used(i, be, nxt, es, ns, nu):
        return (jnp.minimum(i, jnp.maximum(nu[0] - 1, 0)), 0)

    grid_spec = pltpu.PrefetchScalarGridSpec(
        num_scalar_prefetch=5,
        grid=(n_blocks,),
        in_specs=[
            pl.BlockSpec((rows, dp), last_used),
            pl.BlockSpec(memory_space=pl.ANY),
            pl.BlockSpec((1, 1, two_f), lambda i, be, nxt, es, ns, nu: (be[i], 0, 0)),
            pl.BlockSpec(memory_space=pl.ANY),
            pl.BlockSpec((1, 1, d), lambda i, be, nxt, es, ns, nu: (be[i], 0, 0)),
        ],
        out_specs=pl.BlockSpec((rows, dp), last_used),
        scratch_shapes=[pltpu.VMEM((2, d, two_f), F32),
                        pltpu.VMEM((2, d_ff, d), F32),
                        pltpu.SemaphoreType.DMA((2, 2)),
                        pltpu.VMEM((d, two_f), BF16),
                        pltpu.VMEM((d_ff, d), BF16)],
    )
    return pl.pallas_call(
        _expert_kernel,
        grid_spec=grid_spec,
        out_shape=jax.ShapeDtypeStruct((n_rows, dp), jnp.uint32),
        compiler_params=_cparams(("arbitrary",)),
        name="experts",
    )(block_expert, next_expert, expert_slot, n_sub, n_used, xs, wgu, bgu.reshape(n_exp, 1, two_f), wd,
      bd.reshape(n_exp, 1, d))


def _combine_kernel(alpha, *refs):
    ys_refs = refs[:TOP_K]
    gw_ref, h_ref, lg_ref, lb_ref = refs[TOP_K:TOP_K + 4]
    o_ref = refs[-1]
    gw = gw_ref[...]
    acc = alpha * h_ref[...]
    for k in range(TOP_K):
        acc = acc + gw[:, k:k + 1] * _unpack_bf16_pair(ys_refs[k][...])
    o_ref[...] = _layer_norm(acc, lg_ref[...], lb_ref[...])


def _combine(alpha, y_slots, gw_rows, h, lg, lb, tm, out_prev, row0, total_rows):
    rows, d = h.shape
    tm = min(tm, rows)
    assert rows % tm == 0 and row0 % tm == 0
    nsteps = rows // tm
    blk0 = row0 // tm
    ys_specs = [pl.BlockSpec((tm, d // 2), functools.partial(lambda i, k: (k * nsteps + i, 0), k=k))
                for k in range(TOP_K)]
    in_specs = ys_specs + [pl.BlockSpec((tm, LANES), lambda i: (i, 0)),
                           pl.BlockSpec((tm, d), lambda i: (i, 0)),
                           pl.BlockSpec((1, d), lambda i: (0, 0)),
                           pl.BlockSpec((1, d), lambda i: (0, 0))]
    args = [y_slots] * TOP_K + [gw_rows, h, lg, lb]
    aliases = {}
    if out_prev is not None:
        in_specs.append(pl.BlockSpec(memory_space=pl.ANY))
        aliases = {len(args): 0}
        args.append(out_prev)
    return pl.pallas_call(
        functools.partial(_combine_kernel, alpha),
        grid=(nsteps,),
        in_specs=in_specs,
        out_specs=pl.BlockSpec((tm, d), lambda i: (i + blk0, 0)),
        out_shape=jax.ShapeDtypeStruct((total_rows, d), F32),
        input_output_aliases=aliases,
        compiler_params=_cparams(("parallel",)),
        name="combine_ln2",
    )(*args)


def _pack_in_proj(w_in, d_inner, n_ssm_heads, attn_dim):
    gn = SSM_GROUPS * D_STATE
    kv_dim = KV_HEADS * HEAD_DIM
    d = w_in.shape[0]
    o = 0
    seg = {}
    for name, width in (("z", d_inner), ("x", d_inner), ("B", gn), ("C", gn), ("dt", n_ssm_heads),
                        ("q", attn_dim), ("k", kv_dim), ("v", kv_dim), ("gs", d), ("ga", d)):
        seg[name] = w_in[:, o:o + width]
        o += width
    assert o == w_in.shape[1]
    dt_pad = jnp.pad(seg["dt"], ((0, 0), (0, LANES - n_ssm_heads)))
    order = (("z", seg["z"]), ("x", seg["x"]), ("gs", seg["gs"]), ("ga", seg["ga"]),
             ("B", seg["B"]), ("C", seg["C"]), ("dt", dt_pad))
    col, off = {}, 0
    for name, w in order:
        col[name] = off
        assert off % w.shape[1] == 0
        off += w.shape[1]
    w_f32grp = jnp.concatenate([w for _, w in order], axis=1).astype(BF16)
    w_qkv = jnp.concatenate([seg["q"], seg["k"], seg["v"]], axis=1).astype(BF16)
    return w_f32grp, w_qkv, col


def _largest_tile(n, cap):
    best = LANES
    for k in range(1, n // LANES + 1):
        if n % (k * LANES) == 0 and k * LANES <= cap:
            best = k * LANES
    return best


def _layer(h_in, batch, seq, alpha, w_in, conv_w, conv_b, dt_bias, a_log, d_skip, ssm_norm_w,
           w_ssm_out, attn_sinks, w_attn_out, b_gates, w_mix_out, ln1_g, ln1_b, w_router, b_router,
           w_gate_up, b_gate_up, w_down, b_down, ln2_g, ln2_b):
    t, d = h_in.shape
    d_inner = ssm_norm_w.shape[0]
    n_ssm_heads = dt_bias.shape[0]
    attn_dim = w_attn_out.shape[0]
    n_heads = attn_sinks.shape[0]
    n_exp = w_router.shape[1]
    gn = SSM_GROUPS * D_STATE

    w_f32grp, w_qkv, col = _pack_in_proj(w_in, d_inner, n_ssm_heads, attn_dim)
    proj = _matmul(h_in, w_f32grp, F32, 1024, _largest_tile(w_f32grp.shape[1], 2560))
    qkv = _matmul(h_in, w_qkv, BF16, 1024, _largest_tile(w_qkv.shape[1], 1536))

    pad_h = (0, LANES - n_ssm_heads)
    y_ssm = _ssd(proj, batch, seq, col,
                 conv_w[:, :d_inner], conv_w[:, d_inner:d_inner + gn], conv_w[:, d_inner + gn:],
                 conv_b[None, :d_inner], conv_b[None, d_inner:d_inner + gn], conv_b[None, d_inner + gn:],
                 jnp.pad(dt_bias, pad_h)[None, :], jnp.pad(a_log, pad_h)[None, :],
                 jnp.repeat(d_skip, SSM_HEAD_DIM)[None, :], ssm_norm_w[None, :])
    y_att = _swa(qkv, attn_sinks, batch, seq, n_heads, w_attn_out.astype(BF16))

    wso, wmix = w_ssm_out.astype(BF16), w_mix_out.astype(BF16)

    n_parts = MOE_SPLITS if t % (MOE_SPLITS * SC_CORES * SC_SUBCORES * SC_GATHER_CHUNK) == 0 else 1
    rows = t // n_parts
    out = None
    dest_kt = None
    for part in range(n_parts):
        row0 = part * rows
        h1, h1_packed = _merge(alpha, y_ssm, y_att, proj, col, h_in, wso, wmix, b_gates[None, :d],
                               b_gates[None, d:], ln1_g[None, :], ln1_b[None, :], 512, row0, rows, dest_kt)
        idx_kt, gw_kt, rank_kt, counts = _router(h1, w_router, b_router[:, None], 1024)

        counts = counts[:, 0]
        padded = (counts + MOE_BLOCK - 1) // MOE_BLOCK * MOE_BLOCK
        pad_end = jnp.cumsum(padded)
        pad_start = pad_end - padded
        n_blocks = -(-(rows * TOP_K) // MOE_BLOCK) + n_exp
        n_rows = n_blocks * MOE_BLOCK
        expert_ids = jnp.arange(n_exp, dtype=jnp.int32)
        pad_start_of_slot = jnp.sum(
            jnp.where(idx_kt[None] == expert_ids[:, None, None], pad_start[:, None, None], 0), axis=0)
        dest_kt = pad_start_of_slot + rank_kt
        fill_rows = (pad_start + counts)[:, None] + jnp.arange(MOE_BLOCK, dtype=jnp.int32)[None, :]
        spare_rows = n_rows + jnp.arange(n_exp * MOE_BLOCK, dtype=jnp.int32).reshape(n_exp, MOE_BLOCK)
        fill_idx = jnp.where(fill_rows < pad_end[:, None], fill_rows, spare_rows).reshape(-1)
        block_row0 = jnp.arange(n_blocks, dtype=jnp.int32) * MOE_BLOCK
        block_expert = jnp.minimum(
            jnp.sum((pad_end[None, :] <= block_row0[:, None]).astype(jnp.int32), axis=1), n_exp - 1)
        n_used = (pad_end[-1:] // MOE_BLOCK).astype(jnp.int32)
        real_end_of_block = jnp.sum(jnp.where(block_expert[:, None] == expert_ids[None, :],
                                              (pad_start + counts)[None, :], 0), axis=1)
        block_valid = jnp.clip(real_end_of_block - block_row0, 0, MOE_BLOCK)
        gw_rows = jnp.pad(gw_kt.T, ((0, 0), (0, LANES - TOP_K)))

        xs = _sc_scatter_rows(h1_packed, dest_kt, fill_idx, n_rows + n_exp * MOE_BLOCK)
        ys = _experts(xs, block_expert, block_valid, n_used, w_gate_up, b_gate_up, w_down, b_down)
        y_slots = _sc_gather_rows(ys, dest_kt.reshape(-1))
        out = _combine(alpha, y_slots, gw_rows, h1, ln2_g[None, :], ln2_b[None, :], 512, out, row0, t)
    return out


def kernel(x, w_in, conv_w, conv_b, dt_bias, a_log, d_skip, ssm_norm_w, w_ssm_out, attn_sinks,
           w_attn_out, b_gates, w_mix_out, ln1_g, ln1_b, w_router, b_router, w_gate_up, b_gate_up,
           w_down, b_down, ln2_g, ln2_b):
    batch, seq, d = x.shape
    depth = w_in.shape[0]
    alpha = (2 * depth) ** 0.25
    h = x.reshape(batch * seq, d)
    for i in range(depth):
        h = _layer(h, batch, seq, alpha, w_in[i], conv_w[i], conv_b[i], dt_bias[i], a_log[i],
                   d_skip[i], ssm_norm_w[i], w_ssm_out[i], attn_sinks[i], w_attn_out[i], b_gates[i],
                   w_mix_out[i], ln1_g[i], ln1_b[i], w_router[i], b_router[i], w_gate_up[i],
                   b_gate_up[i], w_down[i], b_down[i], ln2_g[i], ln2_b[i])
    return h.reshape(batch, seq, d)
```

```python
import functools

import jax
import jax.numpy as jnp
from jax import lax
from jax.experimental import pallas as pl
from jax.experimental.pallas import tpu as pltpu
from jax.experimental.pallas import tpu_sc as plsc

SSM_HEAD_DIM = 64
SSM_GROUPS = 4
D_STATE = 128
CONV_WIDTH = 4
SSD_CHUNK = 128
SEQS_PER_STEP = 2
KV_HEADS = 4
HEAD_DIM = 64
WINDOW = 128
TOP_K = 4
SWIGLU_LIMIT = 7.0
SWIGLU_ALPHA = 1.702
MOE_BLOCK = 512
LN_EPS = 1e-5
RMS_EPS = 1e-5

LANES = 128
SUBLANES = 8
NEG_BIG = -1e30
LOG2_E = 1.4426950408889634
F32 = jnp.float32
BF16 = jnp.bfloat16
VMEM_LIMIT = 56 * 1024 * 1024


def _cparams(sem):
    return pltpu.CompilerParams(dimension_semantics=sem, vmem_limit_bytes=VMEM_LIMIT)


def _sigmoid(x):
    return 0.5 + 0.5 * jnp.tanh(0.5 * x)


def _pack_bf16_pair(v):
    w = v.shape[1] // 2
    bits = lax.bitcast_convert_type(v.astype(BF16).astype(F32), jnp.uint32)
    return (bits[:, :w] >> 16) | (bits[:, w:] & jnp.uint32(0xFFFF0000))


def _unpack_bf16_pair(p):
    lo = lax.bitcast_convert_type(p << 16, F32)
    hi = lax.bitcast_convert_type(p & jnp.uint32(0xFFFF0000), F32)
    return jnp.concatenate([lo, hi], axis=1)


def _silu(x):
    h = 0.5 * x
    return h + h * jnp.tanh(h)


def _mm_kernel(x_ref, w_ref, o_ref):
    o_ref[...] = jnp.dot(x_ref[...].astype(BF16), w_ref[...],
                         preferred_element_type=F32).astype(o_ref.dtype)


def _matmul(x, w, out_dtype, tm, tn):
    m, k = x.shape
    n = w.shape[1]
    tm = min(tm, m)
    assert m % tm == 0 and n % tn == 0
    return pl.pallas_call(
        _mm_kernel,
        grid=(n // tn, m // tm),
        in_specs=[pl.BlockSpec((tm, k), lambda j, i: (i, 0)),
                  pl.BlockSpec((k, tn), lambda j, i: (0, j))],
        out_specs=pl.BlockSpec((tm, tn), lambda j, i: (i, j)),
        out_shape=jax.ShapeDtypeStruct((m, n), out_dtype),
        compiler_params=_cparams(("parallel", "parallel")),
        name="in_proj",
    )(x, w)


def _conv_silu(u_ref, prev_ref, w_ref, b_ref, first):
    cur = u_ref[...]
    prev = prev_ref[...]
    prev = jnp.where(first, jnp.zeros_like(prev), prev)
    row8 = lax.broadcasted_iota(jnp.int32, prev.shape, 0)
    acc = b_ref[...] + w_ref[CONV_WIDTH - 1:CONV_WIDTH, :] * cur
    for k in range(1, CONV_WIDTH):
        rolled = pltpu.roll(cur, k, axis=0)
        head = jnp.where(row8 < k, pltpu.roll(prev, k, axis=0), rolled[:SUBLANES])
        shifted = jnp.concatenate([head, rolled[SUBLANES:]], axis=0)
        j = CONV_WIDTH - 1 - k
        acc = acc + w_ref[j:j + 1, :] * shifted
    return _silu(acc)


def _expand_columns(v, e3_ref):
    x1 = v.astype(BF16)
    r1 = v - x1.astype(F32)
    x2 = r1.astype(BF16)
    x3 = (r1 - x2.astype(F32)).astype(BF16)
    return jnp.dot(jnp.concatenate([x1, x2, x3], axis=1), e3_ref[...], preferred_element_type=F32)


def _ssd_kernel(z_ref, x_ref, b_ref, c_ref, dt_ref, xp_ref, bp_ref, cp_ref,
                cwx_ref, cwb_ref, cwc_ref, cbx_ref, cbb_ref, cbc_ref,
                dtb_ref, alog_ref, dsk_ref, nw_ref, ehead_ref, ecol_ref,
                o_ref, st_ref):
    first = pl.program_id(1) == 0

    @pl.when(first)
    def _():
        st_ref[...] = jnp.zeros_like(st_ref)

    for s in range(z_ref.shape[0]):
        _ssd_chunk(z_ref.at[s], x_ref.at[s], b_ref.at[s], c_ref.at[s], dt_ref.at[s],
                   xp_ref.at[s], bp_ref.at[s], cp_ref.at[s],
                   cwx_ref, cwb_ref, cwc_ref, cbx_ref, cbb_ref, cbc_ref,
                   dtb_ref, alog_ref, dsk_ref, nw_ref, ehead_ref, ecol_ref,
                   o_ref.at[s], st_ref.at[s], first)


def _ssd_chunk(z_ref, x_ref, b_ref, c_ref, dt_ref, xp_ref, bp_ref, cp_ref,
               cwx_ref, cwb_ref, cwc_ref, cbx_ref, cbb_ref, cbc_ref,
               dtb_ref, alog_ref, dsk_ref, nw_ref, ehead_ref, ecol_ref, o_ref, st_ref, first):
    n = SSD_CHUNK
    heads_per_group = st_ref.shape[2] // SSM_HEAD_DIM
    gw = heads_per_group * SSM_HEAD_DIM

    xc = _conv_silu(x_ref, xp_ref, cwx_ref, cbx_ref, first)
    bc = _conv_silu(b_ref, bp_ref, cwb_ref, cbb_ref, first)
    cc = _conv_silu(c_ref, cp_ref, cwc_ref, cbc_ref, first)

    dt_in = dt_ref[...] + dtb_ref[...]
    dt = jnp.maximum(dt_in, 0.0) + jnp.log(1.0 + jnp.exp(-jnp.abs(dt_in)))
    a = dt * (-jnp.exp(alog_ref[...]))
    row = lax.broadcasted_iota(jnp.int32, (n, n), 0)
    col = lax.broadcasted_iota(jnp.int32, (n, n), 1)
    causal = row >= col
    tri = jnp.where(causal, 1.0, 0.0).astype(F32)
    acum = jnp.dot(tri, a, preferred_element_type=F32, precision=lax.Precision.HIGHEST)
    acum2 = acum * LOG2_E
    acum2_t = acum2.T
    a_last = acum[n - 1:n, :]
    lo_mask = lax.broadcasted_iota(jnp.int32, (n, LANES), 1) < SSM_HEAD_DIM
    dt_x = _expand_columns(dt, ehead_ref)
    d2e_x = _expand_columns(jnp.exp(a_last - acum), ehead_ref)
    eac_x = _expand_columns(jnp.exp(acum), ehead_ref)
    acum2_x = _expand_columns(acum2, ecol_ref)

    for g in range(SSM_GROUPS):
        xg = xc[:, g * gw:(g + 1) * gw]
        bg = bc[:, g * D_STATE:(g + 1) * D_STATE]
        cg = cc[:, g * D_STATE:(g + 1) * D_STATE]
        cg_b = cg.astype(BF16)
        bg_t = bg.T.astype(BF16)
        cb = jnp.dot(cg_b, bg_t, preferred_element_type=F32) * tri
        st_prev = st_ref[g]
        y_off = jnp.dot(cg_b, st_prev.astype(BF16), preferred_element_type=F32)
        y_parts, xw_parts, ea_parts = [], [], []
        for j in range(heads_per_group // 2):
            h0 = g * heads_per_group + 2 * j
            lanes = slice(j * LANES, (j + 1) * LANES)
            glanes = slice(g * gw + j * LANES, g * gw + (j + 1) * LANES)
            xs_pair = xg[:, lanes] * dt_x[:, glanes]
            xs_b = xs_pair.astype(BF16)
            xw_parts.append((xs_pair * d2e_x[:, glanes]).astype(BF16))
            ea_parts.append(eac_x[:, glanes])
            ys = []
            for h in (h0, h0 + 1):
                seg2 = acum2_x[:, h * n:(h + 1) * n] - acum2_t[h:h + 1, :]
                decay = jnp.exp2(jnp.minimum(seg2, 0.0))
                m = (cb * decay).astype(BF16)
                ys.append(jnp.dot(m, xs_b, preferred_element_type=F32))
            y_parts.append(jnp.where(lo_mask, ys[0], ys[1]))
        y_diag = jnp.concatenate(y_parts, axis=1)
        ea = jnp.concatenate(ea_parts, axis=1)
        xw = jnp.concatenate(xw_parts, axis=1)
        cols = slice(g * gw, (g + 1) * gw)
        y = y_diag + y_off * ea + xg * dsk_ref[:, cols]
        zg = z_ref[:, cols]
        y = y * _silu(zg)
        ms = jnp.mean(y * y, axis=-1, keepdims=True)
        o_ref[:, cols] = (y * lax.rsqrt(ms + RMS_EPS) * nw_ref[:, cols]).astype(o_ref.dtype)
        st_ref[g] = st_prev * ea[n - 1:n, :] + jnp.dot(bg_t, xw, preferred_element_type=F32)


def _ssd(proj, batch, seq, col, conv_wx, conv_wb, conv_wc, conv_bx, conv_bb, conv_bc,
         dt_bias, a_log, d_skip, norm_w):
    t, width = proj.shape
    d_inner = norm_w.shape[1]
    gn = SSM_GROUPS * D_STATE
    n = SSD_CHUNK
    nc = seq // n
    gw = d_inner // SSM_GROUPS
    ns = SEQS_PER_STEP if batch % SEQS_PER_STEP == 0 else 1
    proj3 = proj.reshape(batch, seq, width)
    n_heads = d_inner // SSM_HEAD_DIM
    head_id = jnp.arange(LANES, dtype=jnp.int32)[:, None]
    e_head = (head_id == (jnp.arange(d_inner, dtype=jnp.int32) // SSM_HEAD_DIM)[None, :]).astype(BF16)
    e_col = (head_id == (jnp.arange(n_heads * n, dtype=jnp.int32) // n)[None, :]).astype(BF16)
    e_head3 = jnp.tile(e_head, (3, 1))
    e_col3 = jnp.tile(e_col, (3, 1))

    def prev_rows(c):
        return jnp.maximum(c * (n // SUBLANES) - 1, 0)

    def const(b, c):
        return (0, 0)

    in_specs = [
        pl.BlockSpec((ns, n, d_inner), lambda b, c: (b, c, col["z"] // d_inner)),
        pl.BlockSpec((ns, n, d_inner), lambda b, c: (b, c, col["x"] // d_inner)),
        pl.BlockSpec((ns, n, gn), lambda b, c: (b, c, col["B"] // gn)),
        pl.BlockSpec((ns, n, gn), lambda b, c: (b, c, col["C"] // gn)),
        pl.BlockSpec((ns, n, LANES), lambda b, c: (b, c, col["dt"] // LANES)),
        pl.BlockSpec((ns, SUBLANES, d_inner), lambda b, c: (b, prev_rows(c), col["x"] // d_inner)),
        pl.BlockSpec((ns, SUBLANES, gn), lambda b, c: (b, prev_rows(c), col["B"] // gn)),
        pl.BlockSpec((ns, SUBLANES, gn), lambda b, c: (b, prev_rows(c), col["C"] // gn)),
        pl.BlockSpec((CONV_WIDTH, d_inner), const),
        pl.BlockSpec((CONV_WIDTH, gn), const),
        pl.BlockSpec((CONV_WIDTH, gn), const),
        pl.BlockSpec((1, d_inner), const),
        pl.BlockSpec((1, gn), const),
        pl.BlockSpec((1, gn), const),
        pl.BlockSpec((1, LANES), const),
        pl.BlockSpec((1, LANES), const),
        pl.BlockSpec((1, d_inner), const),
        pl.BlockSpec((1, d_inner), const),
        pl.BlockSpec(e_head3.shape, const),
        pl.BlockSpec(e_col3.shape, const),
    ]
    out = pl.pallas_call(
        _ssd_kernel,
        grid=(batch // ns, nc),
        in_specs=in_specs,
        out_specs=pl.BlockSpec((ns, n, d_inner), lambda b, c: (b, c, 0)),
        out_shape=jax.ShapeDtypeStruct((batch, seq, d_inner), BF16),
        scratch_shapes=[pltpu.VMEM((ns, SSM_GROUPS, D_STATE, gw), F32)],
        compiler_params=_cparams(("arbitrary", "arbitrary")),
        name="ssd_mixer",
    )(proj3, proj3, proj3, proj3, proj3, proj3, proj3, proj3,
      conv_wx, conv_wb, conv_wc, conv_bx, conv_bb, conv_bc, dt_bias, a_log, d_skip, norm_w,
      e_head3, e_col3)
    return out.reshape(t, d_inner)


def _swa_kernel(sink_ref, q_ref, kc_ref, vc_ref, kp_ref, vp_ref, o_ref):
    for s in range(q_ref.shape[0]):
        _swa_block(sink_ref, q_ref.at[s], kc_ref.at[s], vc_ref.at[s], kp_ref.at[s], vp_ref.at[s],
                   o_ref.at[s])


def _swa_block(sink_ref, q_ref, kc_ref, vc_ref, kp_ref, vp_ref, o_ref):
    i = pl.program_id(1)
    n = WINDOW
    n_heads = q_ref.shape[1] // HEAD_DIM
    grp = n_heads // KV_HEADS
    qpos = lax.broadcasted_iota(jnp.int32, (n, 2 * n), 0) + n
    kpos = lax.broadcasted_iota(jnp.int32, (n, 2 * n), 1)
    diff = qpos - kpos
    mask = (diff >= 0) & (diff < n) & ((kpos >= n) | (i > 0))
    lo_q = lax.broadcasted_iota(jnp.int32, (n, LANES), 1) < HEAD_DIM
    lo_k = lax.broadcasted_iota(jnp.int32, (2 * n, LANES), 1) < HEAD_DIM
    scale = HEAD_DIM ** -0.5
    exp2_scale = scale * LOG2_E
    for p in range(KV_HEADS // 2):
        lanes = slice(p * LANES, (p + 1) * LANES)
        kk = jnp.concatenate([kp_ref[:, lanes], kc_ref[:, lanes]], axis=0).astype(F32)
        vv = jnp.concatenate([vp_ref[:, lanes], vc_ref[:, lanes]], axis=0).astype(F32)
        kk_sw = pltpu.roll(kk, HEAD_DIM, axis=1)
        vv_sw = pltpu.roll(vv, HEAD_DIM, axis=1)
        for par in range(2):
            g = 2 * p + par
            k2 = (jnp.where(lo_k, kk, kk_sw) if par == 0 else jnp.where(lo_k, kk_sw, kk)).astype(BF16)
            v2 = (jnp.where(lo_k, vv, vv_sw) if par == 0 else jnp.where(lo_k, vv_sw, vv)).astype(BF16)
            q_rows = []
            for hq in range(grp):
                h = g * grp + hq
                q_pair = q_ref[:, (h // 2) * LANES:(h // 2 + 1) * LANES]
                keep = lo_q if h % 2 == 0 else jnp.logical_not(lo_q)
                q_rows.append(jnp.where(keep, q_pair, jnp.zeros_like(q_pair)))
            s_all = lax.dot_general(jnp.concatenate(q_rows, axis=0), k2, (((1,), (1,)), ((), ())),
                                    preferred_element_type=F32)
            probs, dens = [], []
            for hq in range(grp):
                s = jnp.where(mask, s_all[hq * n:(hq + 1) * n, :], NEG_BIG)
                sink = sink_ref[g * grp + hq] / scale
                mx = jnp.maximum(jnp.max(s, axis=-1, keepdims=True), sink)
                pr = jnp.exp2((s - mx) * exp2_scale)
                dens.append(jnp.sum(pr, axis=-1, keepdims=True) + jnp.exp2((sink - mx) * exp2_scale))
                probs.append(pr.astype(BF16))
            o_all = jnp.dot(jnp.concatenate(probs, axis=0), v2, preferred_element_type=F32)
            for qp in range(grp // 2):
                outs = [o_all[(2 * qp + hh) * n:(2 * qp + hh + 1) * n, :] / dens[2 * qp + hh] for hh in range(2)]
                pair = (g * grp) // 2 + qp
                o_ref[:, pair * LANES:(pair + 1) * LANES] = jnp.where(lo_q, outs[0], outs[1]).astype(o_ref.dtype)


def _swa(qkv, sinks, batch, seq, n_heads):
    t, width = qkv.shape
    n = WINDOW
    nb = seq // n
    qw = n_heads * HEAD_DIM
    kw = KV_HEADS * HEAD_DIM
    k_blk = qw // kw
    v_blk = k_blk + 1
    ns = SEQS_PER_STEP if batch % SEQS_PER_STEP == 0 else 1
    qkv3 = qkv.reshape(batch, seq, width)

    def prev(i):
        return jnp.maximum(i - 1, 0)

    out = pl.pallas_call(
        _swa_kernel,
        grid=(batch // ns, nb),
        in_specs=[pl.BlockSpec(memory_space=pltpu.SMEM),
                  pl.BlockSpec((ns, n, qw), lambda b, i: (b, i, 0)),
                  pl.BlockSpec((ns, n, kw), lambda b, i: (b, i, k_blk)),
                  pl.BlockSpec((ns, n, kw), lambda b, i: (b, i, v_blk)),
                  pl.BlockSpec((ns, n, kw), lambda b, i: (b, prev(i), k_blk)),
                  pl.BlockSpec((ns, n, kw), lambda b, i: (b, prev(i), v_blk))],
        out_specs=pl.BlockSpec((ns, n, qw), lambda b, i: (b, i, 0)),
        out_shape=jax.ShapeDtypeStruct((batch, seq, qw), BF16),
        compiler_params=_cparams(("parallel", "parallel")),
        name="swa",
    )(sinks, qkv3, qkv3, qkv3, qkv3, qkv3)
    return out.reshape(t, qw)


def _layer_norm(v, g, b):
    mu = jnp.mean(v, axis=-1, keepdims=True)
    d = v - mu
    var = jnp.mean(d * d, axis=-1, keepdims=True)
    return d * lax.rsqrt(var + LN_EPS) * g + b


def _merge_kernel(alpha, ys_ref, ya_ref, gs_ref, ga_ref, x_ref, wso_ref, wao_ref, wmix_ref,
                  bgs_ref, bga_ref, lg_ref, lb_ref, *rest):
    h_ref, hp_ref = rest[-2:]
    y_ssm = jnp.dot(ys_ref[...], wso_ref[...], preferred_element_type=F32)
    y_att = jnp.dot(ya_ref[...], wao_ref[...], preferred_element_type=F32)
    merged = (_sigmoid(gs_ref[...] + bgs_ref[...]) * y_ssm
              + _sigmoid(ga_ref[...] + bga_ref[...]) * y_att)
    mix = jnp.dot(merged.astype(BF16), wmix_ref[...], preferred_element_type=F32)
    h = _layer_norm(alpha * x_ref[...] + mix, lg_ref[...], lb_ref[...])
    h_ref[...] = h
    hp_ref[...] = _pack_bf16_pair(h)


def _merge(alpha, y_ssm, y_att, proj, col, x, wso, wao, wmix, bgs, bga, lg, lb, tm, row0, rows, after):
    d = x.shape[1]
    tm = min(tm, rows)
    assert rows % tm == 0 and row0 % tm == 0
    blk0 = row0 // tm
    di = y_ssm.shape[1]
    da = y_att.shape[1]

    def const(i):
        return (0, 0)

    ordering = list(after)
    return pl.pallas_call(
        functools.partial(_merge_kernel, alpha),
        grid=(rows // tm,),
        in_specs=[pl.BlockSpec((tm, di), lambda i: (i + blk0, 0)),
                  pl.BlockSpec((tm, da), lambda i: (i + blk0, 0)),
                  pl.BlockSpec((tm, d), lambda i: (i + blk0, col["gs"] // d)),
                  pl.BlockSpec((tm, d), lambda i: (i + blk0, col["ga"] // d)),
                  pl.BlockSpec((tm, d), lambda i: (i + blk0, 0)),
                  pl.BlockSpec((di, d), const),
                  pl.BlockSpec((da, d), const),
                  pl.BlockSpec((d, d), const),
                  pl.BlockSpec((1, d), const),
                  pl.BlockSpec((1, d), const),
                  pl.BlockSpec((1, d), const),
                  pl.BlockSpec((1, d), const)] + [pl.BlockSpec(memory_space=pl.ANY)] * len(ordering),
        out_specs=[pl.BlockSpec((tm, d), lambda i: (i, 0)),
                   pl.BlockSpec((tm, d // 2), lambda i: (i, 0))],
        out_shape=[jax.ShapeDtypeStruct((rows, d), F32),
                   jax.ShapeDtypeStruct((rows, d // 2), jnp.uint32)],
        compiler_params=_cparams(("parallel",)),
        name="merge_ln1",
    )(y_ssm, y_att, proj, proj, x, wso, wao, wmix, bgs, bga, lg, lb, *ordering)


ROUTER_SUBTILE = 128


def _router_kernel(h_ref, wr_ref, br_ref, idx_ref, gw_ref, rank_ref, cnt_ref, base_ref):
    i = pl.program_id(0)
    tm = h_ref.shape[0]
    n_exp = br_ref.shape[0]
    sub = ROUTER_SUBTILE

    @pl.when(i == 0)
    def _():
        base_ref[...] = jnp.zeros_like(base_ref)

    h = h_ref[...]
    h1 = h.astype(BF16)
    h2 = (h - h1.astype(F32)).astype(BF16)
    first = jnp.dot(h1, wr_ref[...], preferred_element_type=F32)
    logits_t = (first[:, :LANES] + first[:, LANES:]
                + jnp.dot(h2, wr_ref[:, :LANES], preferred_element_type=F32))
    logits = logits_t.T[:n_exp, :] + br_ref[...]
    eid = lax.broadcasted_iota(jnp.int32, (n_exp, tm), 0)
    work = logits
    vals, idxs = [], []
    for _ in range(TOP_K):
        mx = jnp.max(work, axis=0, keepdims=True)
        sel = jnp.min(jnp.where(work == mx, eid, n_exp), axis=0, keepdims=True)
        vals.append(mx)
        idxs.append(sel)
        work = jnp.where(eid == sel, -jnp.inf, work)
    exps = [jnp.exp(v - vals[0]) for v in vals]
    den = exps[0]
    for e in exps[1:]:
        den = den + e
    r = lax.broadcasted_iota(jnp.int32, (sub, sub), 0)
    c = lax.broadcasted_iota(jnp.int32, (sub, sub), 1)
    tri = jnp.where(r < c, 1.0, 0.0).astype(BF16)
    base = base_ref[...]
    for k in range(TOP_K):
        onehot = eid == idxs[k]
        oh = jnp.where(onehot, 1.0, 0.0)
        ranks = []
        for s in range(tm // sub):
            lanes = slice(s * sub, (s + 1) * sub)
            before = jnp.dot(oh[:, lanes].astype(BF16), tri, preferred_element_type=F32)
            ranks.append(jnp.sum(jnp.where(onehot[:, lanes], before + base, 0.0), axis=0, keepdims=True))
            base = base + jnp.sum(oh[:, lanes], axis=1, keepdims=True)
        idx_ref[k:k + 1, :] = idxs[k]
        gw_ref[k:k + 1, :] = exps[k] / den
        rank_ref[k:k + 1, :] = jnp.concatenate(ranks, axis=1).astype(jnp.int32)
    base_ref[...] = base
    cnt_ref[...] = jnp.broadcast_to(base, cnt_ref.shape).astype(jnp.int32)


def _router(h, w_router, b_router, tm):
    t, d = h.shape
    n_exp = w_router.shape[1]
    tm = min(tm, t)
    assert tm % ROUTER_SUBTILE == 0 and n_exp <= LANES
    w_padded = jnp.pad(w_router, ((0, 0), (0, LANES - n_exp)))
    w_hi = w_padded.astype(BF16)
    w_lo = (w_padded - w_hi.astype(F32)).astype(BF16)
    w_padded = jnp.concatenate([w_hi, w_lo], axis=1)
    outs = pl.pallas_call(
        _router_kernel,
        grid=(t // tm,),
        in_specs=[pl.BlockSpec((tm, d), lambda i: (i, 0)),
                  pl.BlockSpec((d, 2 * LANES), lambda i: (0, 0)),
                  pl.BlockSpec((n_exp, 1), lambda i: (0, 0))],
        out_specs=[pl.BlockSpec((TOP_K, tm), lambda i: (0, i)),
                   pl.BlockSpec((TOP_K, tm), lambda i: (0, i)),
                   pl.BlockSpec((TOP_K, tm), lambda i: (0, i)),
                   pl.BlockSpec((n_exp, LANES), lambda i: (0, 0))],
        out_shape=[jax.ShapeDtypeStruct((TOP_K, t), jnp.int32),
                   jax.ShapeDtypeStruct((TOP_K, t), F32),
                   jax.ShapeDtypeStruct((TOP_K, t), jnp.int32),
                   jax.ShapeDtypeStruct((n_exp, LANES), jnp.int32)],
        scratch_shapes=[pltpu.VMEM((n_exp, 1), F32)],
        compiler_params=_cparams(("arbitrary",)),
        name="router",
    )(h, w_padded, b_router)
    return outs


SC_CORES = 2
SC_SUBCORES = 16
SC_GATHER_CHUNK = 128
MOE_SPLITS = 2


def _sc_gather_rows(table, idx):
    b = idx.shape[0]
    d = table.shape[1]
    n_workers = SC_CORES * SC_SUBCORES
    chunk = SC_GATHER_CHUNK
    assert b % (n_workers * chunk) == 0
    per_worker = b // n_workers
    n_chunks = per_worker // chunk
    mesh = plsc.VectorSubcoreMesh(core_axis_name="c", subcore_axis_name="s",
                                  num_cores=SC_CORES, num_subcores=SC_SUBCORES)

    def body(table_hbm, idx_hbm, out_hbm, idx_v, rows_v, sem):
        wid = lax.axis_index("s") * SC_CORES + lax.axis_index("c")
        base = wid * per_worker

        @pl.loop(0, n_chunks)
        def _(c):
            off = pl.multiple_of(base + c * chunk, SUBLANES)
            pltpu.sync_copy(idx_hbm.at[pl.ds(off, chunk)], idx_v)
            pltpu.async_copy(table_hbm.at[idx_v], rows_v, sem).wait()
            pltpu.sync_copy(rows_v, out_hbm.at[pl.ds(off, chunk)])

    return pl.kernel(
        body,
        out_type=jax.ShapeDtypeStruct((b, d), table.dtype),
        mesh=mesh,
        scratch_types=[pltpu.VMEM((chunk,), jnp.int32),
                       pltpu.VMEM((chunk, d), table.dtype),
                       pltpu.SemaphoreType.DMA],
        name="sc_gather_rows",
    )(table, idx)


def _sc_scatter_rows(src, dest_kt, fill_idx, n_out_rows):
    t, d = src.shape
    n_k = dest_kt.shape[0]
    n_fill = fill_idx.shape[0]
    n_workers = SC_CORES * SC_SUBCORES
    chunk = SC_GATHER_CHUNK
    assert t % (n_workers * chunk) == 0 and n_fill % (n_workers * chunk) == 0
    tok_per_worker = t // n_workers
    fill_per_worker = n_fill // n_workers
    mesh = plsc.VectorSubcoreMesh(core_axis_name="c", subcore_axis_name="s",
                                  num_cores=SC_CORES, num_subcores=SC_SUBCORES)

    def body(src_hbm, dest_hbm, fill_hbm, zeros_hbm, out_hbm, idx_v, rows_v):
        wid = lax.axis_index("s") * SC_CORES + lax.axis_index("c")

        pltpu.sync_copy(zeros_hbm, rows_v)

        @pl.loop(0, fill_per_worker // chunk)
        def _(c):
            off = pl.multiple_of(wid * fill_per_worker + c * chunk, SUBLANES)
            pltpu.sync_copy(fill_hbm.at[pl.ds(off, chunk)], idx_v)
            pltpu.sync_copy(rows_v, out_hbm.at[idx_v])

        @pl.loop(0, tok_per_worker // chunk)
        def _(c):
            off = pl.multiple_of(wid * tok_per_worker + c * chunk, SUBLANES)
            pltpu.sync_copy(src_hbm.at[pl.ds(off, chunk)], rows_v)
            for k in range(n_k):
                pltpu.sync_copy(dest_hbm.at[pl.ds(k * t + off, chunk)], idx_v)
                pltpu.sync_copy(rows_v, out_hbm.at[idx_v])

    return pl.kernel(
        body,
        out_type=jax.ShapeDtypeStruct((n_out_rows, d), src.dtype),
        mesh=mesh,
        scratch_types=[pltpu.VMEM((chunk,), jnp.int32),
                       pltpu.VMEM((chunk, d), src.dtype)],
        name="sc_scatter_rows",
    )(src, dest_kt.reshape(-1), fill_idx, jnp.zeros((chunk, d), src.dtype))


EXPERT_SUBROWS = 128


def _expert_kernel(be_ref, nxt_ref, eslot_ref, nsub_ref, nu_ref, x_ref, wgu_hbm, bgu_ref, wd_hbm,
                   bd_ref, y_ref, wgu_f, wd_f, wsem, wgu_b, wd_b):
    i = pl.program_id(0)
    n_used = nu_ref[0]
    expert = be_ref[i]
    changed = jnp.logical_or(i == 0, expert != be_ref[jnp.maximum(i - 1, 0)])

    def weight_copies(e, slot):
        return (pltpu.make_async_copy(wgu_hbm.at[e], wgu_f.at[slot], wsem.at[0, slot]),
                pltpu.make_async_copy(wd_hbm.at[e], wd_f.at[slot], wsem.at[1, slot]))

    @pl.when(jnp.logical_and(changed, i < n_used))
    def _():
        slot = eslot_ref[i]

        @pl.when(i == 0)
        def _():
            for cp in weight_copies(expert, slot):
                cp.start()

        for cp in weight_copies(expert, slot):
            cp.wait()
        nxt = nxt_ref[i]

        @pl.when(nxt >= 0)
        def _():
            for cp in weight_copies(nxt, 1 - slot):
                cp.start()

        wgu_b[...] = wgu_f[slot].astype(BF16)
        wd_b[...] = wd_f[slot].astype(BF16)

    def ffn(rows):
        d_ff = wd_b.shape[0]
        xb = _unpack_bf16_pair(x_ref[0:rows, :]).astype(BF16)
        hgu = jnp.dot(xb, wgu_b[...], preferred_element_type=F32) + bgu_ref[0]
        glu = jnp.minimum(hgu[:, :d_ff], SWIGLU_LIMIT)
        lin = jnp.clip(hgu[:, d_ff:], -SWIGLU_LIMIT, SWIGLU_LIMIT)
        act = glu * _sigmoid(SWIGLU_ALPHA * glu) * (lin + 1.0)
        y = jnp.dot(act.astype(BF16), wd_b[...], preferred_element_type=F32) + bd_ref[0]
        y_ref[0:rows, :] = _pack_bf16_pair(y)
        if rows < y_ref.shape[0]:
            y_ref[rows:, :] = jnp.zeros((y_ref.shape[0] - rows, y_ref.shape[1]), y_ref.dtype)

    n_sub = nsub_ref[i]
    for sub in range(1, y_ref.shape[0] // EXPERT_SUBROWS + 1):
        @pl.when(jnp.logical_and(i < n_used, n_sub == sub))
        def _(sub=sub):
            ffn(sub * EXPERT_SUBROWS)


def _experts(xs, block_expert, block_valid, n_used, wgu, bgu, wd, bd):
    d = wgu.shape[1]
    dp = xs.shape[1]
    assert 2 * dp == d
    n_blocks = block_expert.shape[0]
    n_exp, _, two_f = wgu.shape
    d_ff = wd.shape[1]
    rows = MOE_BLOCK
    n_rows = n_blocks * rows
    blk = jnp.arange(n_blocks, dtype=jnp.int32)
    is_first = jnp.concatenate([jnp.ones((1,), bool), block_expert[1:] != block_expert[:-1]])
    first_used = jnp.logical_and(is_first, blk < n_used[0])
    first_pos = jnp.where(first_used, blk, n_blocks)
    next_first = jnp.concatenate([lax.cummin(first_pos[::-1])[::-1][1:],
                                  jnp.full((1,), n_blocks, jnp.int32)])
    next_expert = jnp.where(next_first < n_blocks,
                            block_expert[jnp.minimum(next_first, n_blocks - 1)], -1).astype(jnp.int32)
    expert_slot = ((jnp.cumsum(is_first.astype(jnp.int32)) - 1) % 2).astype(jnp.int32)
    n_sub = jnp.clip(-(-block_valid // EXPERT_SUBROWS), 1, rows // EXPERT_SUBROWS).astype(jnp.int32)

    def last_used(i, be, nxt, es, ns, nu):
        return (jnp.minimum(i, jnp.maximum(nu[0] - 1, 0)), 0)

    grid_spec = pltpu.PrefetchScalarGridSpec(
        num_scalar_prefetch=5,
        grid=(n_blocks,),
        in_specs=[
            pl.BlockSpec((rows, dp), last_used),
            pl.BlockSpec(memory_space=pl.ANY),
            pl.BlockSpec((1, 1, two_f), lambda i, be, nxt, es, ns, nu: (be[i], 0, 0)),
            pl.BlockSpec(memory_space=pl.ANY),
            pl.BlockSpec((1, 1, d), lambda i, be, nxt, es, ns, nu: (be[i], 0, 0)),
        ],
        out_specs=pl.BlockSpec((rows, dp), last_used),
        scratch_shapes=[pltpu.VMEM((2, d, two_f), F32),
                        pltpu.VMEM((2, d_ff, d), F32),
                        pltpu.SemaphoreType.DMA((2, 2)),
                        pltpu.VMEM((d, two_f), BF16),
                        pltpu.VMEM((d_ff, d), BF16)],
    )
    return pl.pallas_call(
        _expert_kernel,
        grid_spec=grid_spec,
        out_shape=jax.ShapeDtypeStruct((n_rows, dp), jnp.uint32),
        compiler_params=_cparams(("arbitrary",)),
        name="experts",
    )(block_expert, next_expert, expert_slot, n_sub, n_used, xs, wgu, bgu.reshape(n_exp, 1, two_f), wd,
      bd.reshape(n_exp, 1, d))


def _combine_kernel(alpha, *refs):
    ys_refs = refs[:TOP_K]
    gw_ref, h_ref, lg_ref, lb_ref = refs[TOP_K:TOP_K + 4]
    o_ref = refs[-1]
    gw = gw_ref[...]
    acc = alpha * h_ref[...]
    for k in range(TOP_K):
        acc = acc + gw[:, k:k + 1] * _unpack_bf16_pair(ys_refs[k][...])
    o_ref[...] = _layer_norm(acc, lg_ref[...], lb_ref[...])


def _combine(alpha, y_slots, gw_rows, h, lg, lb, tm, out_prev, row0, total_rows):
    rows, d = h.shape
    tm = min(tm, rows)
    assert rows % tm == 0 and row0 % tm == 0
    nsteps = rows // tm
    blk0 = row0 // tm
    ys_specs = [pl.BlockSpec((tm, d // 2), functools.partial(lambda i, k: (k * nsteps + i, 0), k=k))
                for k in range(TOP_K)]
    in_specs = ys_specs + [pl.BlockSpec((tm, LANES), lambda i: (i, 0)),
                           pl.BlockSpec((tm, d), lambda i: (i, 0)),
                           pl.BlockSpec((1, d), lambda i: (0, 0)),
                           pl.BlockSpec((1, d), lambda i: (0, 0))]
    args = [y_slots] * TOP_K + [gw_rows, h, lg, lb]
    aliases = {}
    if out_prev is not None:
        in_specs.append(pl.BlockSpec(memory_space=pl.ANY))
        aliases = {len(args): 0}
        args.append(out_prev)
    return pl.pallas_call(
        functools.partial(_combine_kernel, alpha),
        grid=(nsteps,),
        in_specs=in_specs,
        out_specs=pl.BlockSpec((tm, d), lambda i: (i + blk0, 0)),
        out_shape=jax.ShapeDtypeStruct((total_rows, d), F32),
        input_output_aliases=aliases,
        compiler_params=_cparams(("parallel",)),
        name="combine_ln2",
    )(*args)


def _pack_in_proj(w_in, d_inner, n_ssm_heads, attn_dim):
    gn = SSM_GROUPS * D_STATE
    kv_dim = KV_HEADS * HEAD_DIM
    d = w_in.shape[0]
    o = 0
    seg = {}
    for name, width in (("z", d_inner), ("x", d_inner), ("B", gn), ("C", gn), ("dt", n_ssm_heads),
                        ("q", attn_dim), ("k", kv_dim), ("v", kv_dim), ("gs", d), ("ga", d)):
        seg[name] = w_in[:, o:o + width]
        o += width
    assert o == w_in.shape[1]
    dt_pad = jnp.pad(seg["dt"], ((0, 0), (0, LANES - n_ssm_heads)))
    order = (("z", seg["z"]), ("x", seg["x"]), ("gs", seg["gs"]), ("ga", seg["ga"]),
             ("B", seg["B"]), ("C", seg["C"]), ("dt", dt_pad))
    col, off = {}, 0
    for name, w in order:
        col[name] = off
        assert off % w.shape[1] == 0
        off += w.shape[1]
    w_f32grp = jnp.concatenate([w for _, w in order], axis=1).astype(BF16)
    w_qkv = jnp.concatenate([seg["q"], seg["k"], seg["v"]], axis=1).astype(BF16)
    return w_f32grp, w_qkv, col


def _largest_tile(n, cap):
    best = LANES
    for k in range(1, n // LANES + 1):
        if n % (k * LANES) == 0 and k * LANES <= cap:
            best = k * LANES
    return best


def _layer(h_in, batch, seq, alpha, w_in, conv_w, conv_b, dt_bias, a_log, d_skip, ssm_norm_w,
           w_ssm_out, attn_sinks, w_attn_out, b_gates, w_mix_out, ln1_g, ln1_b, w_router, b_router,
           w_gate_up, b_gate_up, w_down, b_down, ln2_g, ln2_b):
    t, d = h_in.shape
    d_inner = ssm_norm_w.shape[0]
    n_ssm_heads = dt_bias.shape[0]
    attn_dim = w_attn_out.shape[0]
    n_heads = attn_sinks.shape[0]
    n_exp = w_router.shape[1]
    gn = SSM_GROUPS * D_STATE

    w_f32grp, w_qkv, col = _pack_in_proj(w_in, d_inner, n_ssm_heads, attn_dim)
    proj = _matmul(h_in, w_f32grp, F32, 1024, _largest_tile(w_f32grp.shape[1], 2560))
    qkv = _matmul(h_in, w_qkv, BF16, 1024, _largest_tile(w_qkv.shape[1], 1536))

    pad_h = (0, LANES - n_ssm_heads)
    y_ssm = _ssd(proj, batch, seq, col,
                 conv_w[:, :d_inner], conv_w[:, d_inner:d_inner + gn], conv_w[:, d_inner + gn:],
                 conv_b[None, :d_inner], conv_b[None, d_inner:d_inner + gn], conv_b[None, d_inner + gn:],
                 jnp.pad(dt_bias, pad_h)[None, :], jnp.pad(a_log, pad_h)[None, :],
                 jnp.repeat(d_skip, SSM_HEAD_DIM)[None, :], ssm_norm_w[None, :])
    y_att = _swa(qkv, attn_sinks, batch, seq, n_heads)

    wso, wao, wmix = w_ssm_out.astype(BF16), w_attn_out.astype(BF16), w_mix_out.astype(BF16)

    n_parts = MOE_SPLITS if t % (MOE_SPLITS * SC_CORES * SC_SUBCORES * SC_GATHER_CHUNK) == 0 else 1
    rows = t // n_parts
    out = None
    scatter_operands = ()
    for part in range(n_parts):
        row0 = part * rows
        h1, h1_packed = _merge(alpha, y_ssm, y_att, proj, col, h_in, wso, wao, wmix, b_gates[None, :d],
                               b_gates[None, d:], ln1_g[None, :], ln1_b[None, :], 512, row0, rows,
                               scatter_operands)
        idx_kt, gw_kt, rank_kt, counts = _router(h1, w_router, b_router[:, None], 1024)

        counts = counts[:, 0]
        padded = (counts + MOE_BLOCK - 1) // MOE_BLOCK * MOE_BLOCK
        pad_end = jnp.cumsum(padded)
        pad_start = pad_end - padded
        n_blocks = -(-(rows * TOP_K) // MOE_BLOCK) + n_exp
        n_rows = n_blocks * MOE_BLOCK
        expert_ids = jnp.arange(n_exp, dtype=jnp.int32)
        pad_start_of_slot = jnp.sum(
            jnp.where(idx_kt[None] == expert_ids[:, None, None], pad_start[:, None, None], 0), axis=0)
        dest_kt = pad_start_of_slot + rank_kt
        fill_rows = (pad_start + counts)[:, None] + jnp.arange(MOE_BLOCK, dtype=jnp.int32)[None, :]
        spare_rows = n_rows + jnp.arange(n_exp * MOE_BLOCK, dtype=jnp.int32).reshape(n_exp, MOE_BLOCK)
        fill_idx = jnp.where(fill_rows < pad_end[:, None], fill_rows, spare_rows).reshape(-1)
        block_row0 = jnp.arange(n_blocks, dtype=jnp.int32) * MOE_BLOCK
        block_expert = jnp.minimum(
            jnp.sum((pad_end[None, :] <= block_row0[:, None]).astype(jnp.int32), axis=1), n_exp - 1)
        n_used = (pad_end[-1:] // MOE_BLOCK).astype(jnp.int32)
        real_end_of_block = jnp.sum(jnp.where(block_expert[:, None] == expert_ids[None, :],
                                              (pad_start + counts)[None, :], 0), axis=1)
        block_valid = jnp.clip(real_end_of_block - block_row0, 0, MOE_BLOCK)
        gw_rows = jnp.pad(gw_kt.T, ((0, 0), (0, LANES - TOP_K)))

        scatter_operands = (dest_kt, fill_idx)
        xs = _sc_scatter_rows(h1_packed, dest_kt, fill_idx, n_rows + n_exp * MOE_BLOCK)
        ys = _experts(xs, block_expert, block_valid, n_used, w_gate_up, b_gate_up, w_down, b_down)
        y_slots = _sc_gather_rows(ys, dest_kt.reshape(-1))
        out = _combine(alpha, y_slots, gw_rows, h1, ln2_g[None, :], ln2_b[None, :], 512, out, row0, t)
    return out


def kernel(x, w_in, conv_w, conv_b, dt_bias, a_log, d_skip, ssm_norm_w, w_ssm_out, attn_sinks,
           w_attn_out, b_gates, w_mix_out, ln1_g, ln1_b, w_router, b_router, w_gate_up, b_gate_up,
           w_down, b_down, ln2_g, ln2_b):
    batch, seq, d = x.shape
    depth = w_in.shape[0]
    alpha = (2 * depth) ** 0.25
    h = x.reshape(batch * seq, d)
    for i in range(depth):
        h = _layer(h, batch, seq, alpha, w_in[i], conv_w[i], conv_b[i], dt_bias[i], a_log[i],
                   d_skip[i], ssm_norm_w[i], w_ssm_out[i], attn_sinks[i], w_attn_out[i], b_gates[i],
                   w_mix_out[i], ln1_g[i], ln1_b[i], w_router[i], b_router[i], w_gate_up[i],
                   b_gate_up[i], w_down[i], b_down[i], ln2_g[i], ln2_b[i])
    return h.reshape(batch, seq, d)
```

```python
import functools

import jax
import jax.numpy as jnp
from jax import lax
from jax.experimental import pallas as pl
from jax.experimental.pallas import tpu as pltpu
from jax.experimental.pallas import tpu_sc as plsc

SSM_HEAD_DIM = 64
SSM_GROUPS = 4
D_STATE = 128
CONV_WIDTH = 4
SSD_CHUNK = 128
SEQS_PER_STEP = 2
KV_HEADS = 4
HEAD_DIM = 64
WINDOW = 128
TOP_K = 4
SWIGLU_LIMIT = 7.0
SWIGLU_ALPHA = 1.702
MOE_BLOCK = 512
LN_EPS = 1e-5
RMS_EPS = 1e-5

LANES = 128
SUBLANES = 8
NEG_BIG = -1e30
LOG2_E = 1.4426950408889634
F32 = jnp.float32
BF16 = jnp.bfloat16
VMEM_LIMIT = 56 * 1024 * 1024

IN_PROJ_ROW_TILE = 1024
IN_PROJ_MAX_COL_TILE = 2560
QKV_MAX_COL_TILE = 1536
MERGE_ROW_TILE = 512
ROUTER_ROW_TILE = 1024
COMBINE_ROW_TILE = 1024


def _cparams(sem):
    return pltpu.CompilerParams(dimension_semantics=sem, vmem_limit_bytes=VMEM_LIMIT)


def _sigmoid(x):
    return 0.5 + 0.5 * jnp.tanh(0.5 * x)


def _pack_bf16_pair(v):
    w = v.shape[1] // 2
    bits = lax.bitcast_convert_type(v.astype(BF16).astype(F32), jnp.uint32)
    return (bits[:, :w] >> 16) | (bits[:, w:] & jnp.uint32(0xFFFF0000))


def _unpack_bf16_pair(p):
    lo = lax.bitcast_convert_type(p << 16, F32)
    hi = lax.bitcast_convert_type(p & jnp.uint32(0xFFFF0000), F32)
    return jnp.concatenate([lo, hi], axis=1)


def _silu(x):
    h = 0.5 * x
    return h + h * jnp.tanh(h)


def _mm_kernel(x_ref, w_ref, o_ref):
    o_ref[...] = jnp.dot(x_ref[...].astype(BF16), w_ref[...],
                         preferred_element_type=F32).astype(o_ref.dtype)


def _matmul(x, w, out_dtype, tm, tn):
    m, k = x.shape
    n = w.shape[1]
    tm = min(tm, m)
    assert m % tm == 0 and n % tn == 0
    return pl.pallas_call(
        _mm_kernel,
        grid=(n // tn, m // tm),
        in_specs=[pl.BlockSpec((tm, k), lambda j, i: (i, 0)),
                  pl.BlockSpec((k, tn), lambda j, i: (0, j))],
        out_specs=pl.BlockSpec((tm, tn), lambda j, i: (i, j)),
        out_shape=jax.ShapeDtypeStruct((m, n), out_dtype),
        compiler_params=_cparams(("parallel", "parallel")),
        name="in_proj",
    )(x, w)


def _conv_silu(u_ref, prev_ref, w_ref, b_ref, first):
    cur = u_ref[...]
    prev = prev_ref[...]
    prev = jnp.where(first, jnp.zeros_like(prev), prev)
    row8 = lax.broadcasted_iota(jnp.int32, prev.shape, 0)
    acc = b_ref[...] + w_ref[CONV_WIDTH - 1:CONV_WIDTH, :] * cur
    for k in range(1, CONV_WIDTH):
        rolled = pltpu.roll(cur, k, axis=0)
        head = jnp.where(row8 < k, pltpu.roll(prev, k, axis=0), rolled[:SUBLANES])
        shifted = jnp.concatenate([head, rolled[SUBLANES:]], axis=0)
        j = CONV_WIDTH - 1 - k
        acc = acc + w_ref[j:j + 1, :] * shifted
    return _silu(acc)


def _expand_columns(v, e3_ref):
    x1 = v.astype(BF16)
    r1 = v - x1.astype(F32)
    x2 = r1.astype(BF16)
    x3 = (r1 - x2.astype(F32)).astype(BF16)
    return jnp.dot(jnp.concatenate([x1, x2, x3], axis=1), e3_ref[...], preferred_element_type=F32)


def _ssd_kernel(z_ref, x_ref, b_ref, c_ref, dt_ref, xp_ref, bp_ref, cp_ref,
                cwx_ref, cwb_ref, cwc_ref, cbx_ref, cbb_ref, cbc_ref,
                dtb_ref, alog_ref, dsk_ref, nw_ref, ehead_ref, ecol_ref,
                o_ref, st_ref):
    first = pl.program_id(1) == 0

    @pl.when(first)
    def _():
        st_ref[...] = jnp.zeros_like(st_ref)

    for s in range(z_ref.shape[0]):
        _ssd_chunk(z_ref.at[s], x_ref.at[s], b_ref.at[s], c_ref.at[s], dt_ref.at[s],
                   xp_ref.at[s], bp_ref.at[s], cp_ref.at[s],
                   cwx_ref, cwb_ref, cwc_ref, cbx_ref, cbb_ref, cbc_ref,
                   dtb_ref, alog_ref, dsk_ref, nw_ref, ehead_ref, ecol_ref,
                   o_ref.at[s], st_ref.at[s], first)


def _ssd_chunk(z_ref, x_ref, b_ref, c_ref, dt_ref, xp_ref, bp_ref, cp_ref,
               cwx_ref, cwb_ref, cwc_ref, cbx_ref, cbb_ref, cbc_ref,
               dtb_ref, alog_ref, dsk_ref, nw_ref, ehead_ref, ecol_ref, o_ref, st_ref, first):
    n = SSD_CHUNK
    heads_per_group = st_ref.shape[2] // SSM_HEAD_DIM
    gw = heads_per_group * SSM_HEAD_DIM

    xc = _conv_silu(x_ref, xp_ref, cwx_ref, cbx_ref, first)
    bc = _conv_silu(b_ref, bp_ref, cwb_ref, cbb_ref, first)
    cc = _conv_silu(c_ref, cp_ref, cwc_ref, cbc_ref, first)

    dt_in = dt_ref[...] + dtb_ref[...]
    dt = jnp.maximum(dt_in, 0.0) + jnp.log(1.0 + jnp.exp(-jnp.abs(dt_in)))
    a = dt * (-jnp.exp(alog_ref[...]))
    row = lax.broadcasted_iota(jnp.int32, (n, n), 0)
    col = lax.broadcasted_iota(jnp.int32, (n, n), 1)
    causal = row >= col
    tri = jnp.where(causal, 1.0, 0.0).astype(F32)
    acum = jnp.dot(tri, a, preferred_element_type=F32, precision=lax.Precision.HIGHEST)
    acum2 = acum * LOG2_E
    acum2_t = acum2.T
    a_last = acum[n - 1:n, :]
    lo_mask = lax.broadcasted_iota(jnp.int32, (n, LANES), 1) < SSM_HEAD_DIM
    dt_x = _expand_columns(dt, ehead_ref)
    d2e_x = _expand_columns(jnp.exp(a_last - acum), ehead_ref)
    eac_x = _expand_columns(jnp.exp(acum), ehead_ref)
    acum2_x = _expand_columns(acum2, ecol_ref)

    for g in range(SSM_GROUPS):
        xg = xc[:, g * gw:(g + 1) * gw]
        bg = bc[:, g * D_STATE:(g + 1) * D_STATE]
        cg = cc[:, g * D_STATE:(g + 1) * D_STATE]
        cg_b = cg.astype(BF16)
        bg_t = bg.T.astype(BF16)
        cb = jnp.dot(cg_b, bg_t, preferred_element_type=F32) * tri
        st_prev = st_ref[g]
        y_off = jnp.dot(cg_b, st_prev.astype(BF16), preferred_element_type=F32)
        y_parts, xw_parts, ea_parts = [], [], []
        for j in range(heads_per_group // 2):
            h0 = g * heads_per_group + 2 * j
            lanes = slice(j * LANES, (j + 1) * LANES)
            glanes = slice(g * gw + j * LANES, g * gw + (j + 1) * LANES)
            xs_pair = xg[:, lanes] * dt_x[:, glanes]
            xs_b = xs_pair.astype(BF16)
            xw_parts.append((xs_pair * d2e_x[:, glanes]).astype(BF16))
            ea_parts.append(eac_x[:, glanes])
            ys = []
            for h in (h0, h0 + 1):
                seg2 = acum2_x[:, h * n:(h + 1) * n] - acum2_t[h:h + 1, :]
                decay = jnp.exp2(jnp.minimum(seg2, 0.0))
                m = (cb * decay).astype(BF16)
                ys.append(jnp.dot(m, xs_b, preferred_element_type=F32))
            y_parts.append(jnp.where(lo_mask, ys[0], ys[1]))
        y_diag = jnp.concatenate(y_parts, axis=1)
        ea = jnp.concatenate(ea_parts, axis=1)
        xw = jnp.concatenate(xw_parts, axis=1)
        cols = slice(g * gw, (g + 1) * gw)
        y = y_diag + y_off * ea + xg * dsk_ref[:, cols]
        zg = z_ref[:, cols]
        y = y * _silu(zg)
        ms = jnp.mean(y * y, axis=-1, keepdims=True)
        o_ref[:, cols] = (y * lax.rsqrt(ms + RMS_EPS) * nw_ref[:, cols]).astype(o_ref.dtype)
        st_ref[g] = st_prev * ea[n - 1:n, :] + jnp.dot(bg_t, xw, preferred_element_type=F32)


def _ssd(proj, batch, seq, col, conv_wx, conv_wb, conv_wc, conv_bx, conv_bb, conv_bc,
         dt_bias, a_log, d_skip, norm_w):
    t, width = proj.shape
    d_inner = norm_w.shape[1]
    gn = SSM_GROUPS * D_STATE
    n = SSD_CHUNK
    nc = seq // n
    gw = d_inner // SSM_GROUPS
    ns = SEQS_PER_STEP if batch % SEQS_PER_STEP == 0 else 1
    proj3 = proj.reshape(batch, seq, width)
    n_heads = d_inner // SSM_HEAD_DIM
    head_id = jnp.arange(LANES, dtype=jnp.int32)[:, None]
    e_head = (head_id == (jnp.arange(d_inner, dtype=jnp.int32) // SSM_HEAD_DIM)[None, :]).astype(BF16)
    e_col = (head_id == (jnp.arange(n_heads * n, dtype=jnp.int32) // n)[None, :]).astype(BF16)
    e_head3 = jnp.tile(e_head, (3, 1))
    e_col3 = jnp.tile(e_col, (3, 1))

    def prev_rows(c):
        return jnp.maximum(c * (n // SUBLANES) - 1, 0)

    def const(b, c):
        return (0, 0)

    in_specs = [
        pl.BlockSpec((ns, n, d_inner), lambda b, c: (b, c, col["z"] // d_inner)),
        pl.BlockSpec((ns, n, d_inner), lambda b, c: (b, c, col["x"] // d_inner)),
        pl.BlockSpec((ns, n, gn), lambda b, c: (b, c, col["B"] // gn)),
        pl.BlockSpec((ns, n, gn), lambda b, c: (b, c, col["C"] // gn)),
        pl.BlockSpec((ns, n, LANES), lambda b, c: (b, c, col["dt"] // LANES)),
        pl.BlockSpec((ns, SUBLANES, d_inner), lambda b, c: (b, prev_rows(c), col["x"] // d_inner)),
        pl.BlockSpec((ns, SUBLANES, gn), lambda b, c: (b, prev_rows(c), col["B"] // gn)),
        pl.BlockSpec((ns, SUBLANES, gn), lambda b, c: (b, prev_rows(c), col["C"] // gn)),
        pl.BlockSpec((CONV_WIDTH, d_inner), const),
        pl.BlockSpec((CONV_WIDTH, gn), const),
        pl.BlockSpec((CONV_WIDTH, gn), const),
        pl.BlockSpec((1, d_inner), const),
        pl.BlockSpec((1, gn), const),
        pl.BlockSpec((1, gn), const),
        pl.BlockSpec((1, LANES), const),
        pl.BlockSpec((1, LANES), const),
        pl.BlockSpec((1, d_inner), const),
        pl.BlockSpec((1, d_inner), const),
        pl.BlockSpec(e_head3.shape, const),
        pl.BlockSpec(e_col3.shape, const),
    ]
    out = pl.pallas_call(
        _ssd_kernel,
        grid=(batch // ns, nc),
        in_specs=in_specs,
        out_specs=pl.BlockSpec((ns, n, d_inner), lambda b, c: (b, c, 0)),
        out_shape=jax.ShapeDtypeStruct((batch, seq, d_inner), BF16),
        scratch_shapes=[pltpu.VMEM((ns, SSM_GROUPS, D_STATE, gw), F32)],
        compiler_params=_cparams(("arbitrary", "arbitrary")),
        name="ssd_mixer",
    )(proj3, proj3, proj3, proj3, proj3, proj3, proj3, proj3,
      conv_wx, conv_wb, conv_wc, conv_bx, conv_bb, conv_bc, dt_bias, a_log, d_skip, norm_w,
      e_head3, e_col3)
    return out.reshape(t, d_inner)


def _swa_kernel(sink_ref, q_ref, kc_ref, vc_ref, kp_ref, vp_ref, o_ref):
    for s in range(q_ref.shape[0]):
        _swa_block(sink_ref, q_ref.at[s], kc_ref.at[s], vc_ref.at[s], kp_ref.at[s], vp_ref.at[s],
                   o_ref.at[s])


def _swa_block(sink_ref, q_ref, kc_ref, vc_ref, kp_ref, vp_ref, o_ref):
    i = pl.program_id(1)
    n = WINDOW
    n_heads = q_ref.shape[1] // HEAD_DIM
    grp = n_heads // KV_HEADS
    qpos = lax.broadcasted_iota(jnp.int32, (n, 2 * n), 0) + n
    kpos = lax.broadcasted_iota(jnp.int32, (n, 2 * n), 1)
    diff = qpos - kpos
    mask = (diff >= 0) & (diff < n) & ((kpos >= n) | (i > 0))
    lo_q = lax.broadcasted_iota(jnp.int32, (n, LANES), 1) < HEAD_DIM
    lo_k = lax.broadcasted_iota(jnp.int32, (2 * n, LANES), 1) < HEAD_DIM
    scale = HEAD_DIM ** -0.5
    exp2_scale = scale * LOG2_E
    for p in range(KV_HEADS // 2):
        lanes = slice(p * LANES, (p + 1) * LANES)
        kk = jnp.concatenate([kp_ref[:, lanes], kc_ref[:, lanes]], axis=0).astype(F32)
        vv = jnp.concatenate([vp_ref[:, lanes], vc_ref[:, lanes]], axis=0).astype(F32)
        kk_sw = pltpu.roll(kk, HEAD_DIM, axis=1)
        vv_sw = pltpu.roll(vv, HEAD_DIM, axis=1)
        for par in range(2):
            g = 2 * p + par
            k2 = (jnp.where(lo_k, kk, kk_sw) if par == 0 else jnp.where(lo_k, kk_sw, kk)).astype(BF16)
            v2 = (jnp.where(lo_k, vv, vv_sw) if par == 0 else jnp.where(lo_k, vv_sw, vv)).astype(BF16)
            q_rows = []
            for hq in range(grp):
                h = g * grp + hq
                q_pair = q_ref[:, (h // 2) * LANES:(h // 2 + 1) * LANES]
                keep = lo_q if h % 2 == 0 else jnp.logical_not(lo_q)
                q_rows.append(jnp.where(keep, q_pair, jnp.zeros_like(q_pair)))
            s_all = lax.dot_general(jnp.concatenate(q_rows, axis=0), k2, (((1,), (1,)), ((), ())),
                                    preferred_element_type=F32)
            probs, dens = [], []
            for hq in range(grp):
                s = jnp.where(mask, s_all[hq * n:(hq + 1) * n, :], NEG_BIG)
                sink = sink_ref[g * grp + hq] / scale
                mx = jnp.maximum(jnp.max(s, axis=-1, keepdims=True), sink)
                pr = jnp.exp2((s - mx) * exp2_scale)
                dens.append(jnp.sum(pr, axis=-1, keepdims=True) + jnp.exp2((sink - mx) * exp2_scale))
                probs.append(pr.astype(BF16))
            o_all = jnp.dot(jnp.concatenate(probs, axis=0), v2, preferred_element_type=F32)
            for qp in range(grp // 2):
                outs = [o_all[(2 * qp + hh) * n:(2 * qp + hh + 1) * n, :] / dens[2 * qp + hh] for hh in range(2)]
                pair = (g * grp) // 2 + qp
                o_ref[:, pair * LANES:(pair + 1) * LANES] = jnp.where(lo_q, outs[0], outs[1]).astype(o_ref.dtype)


def _swa(qkv, sinks, batch, seq, n_heads):
    t, width = qkv.shape
    n = WINDOW
    nb = seq // n
    qw = n_heads * HEAD_DIM
    kw = KV_HEADS * HEAD_DIM
    k_blk = qw // kw
    v_blk = k_blk + 1
    ns = SEQS_PER_STEP if batch % SEQS_PER_STEP == 0 else 1
    qkv3 = qkv.reshape(batch, seq, width)

    def prev(i):
        return jnp.maximum(i - 1, 0)

    out = pl.pallas_call(
        _swa_kernel,
        grid=(batch // ns, nb),
        in_specs=[pl.BlockSpec(memory_space=pltpu.SMEM),
                  pl.BlockSpec((ns, n, qw), lambda b, i: (b, i, 0)),
                  pl.BlockSpec((ns, n, kw), lambda b, i: (b, i, k_blk)),
                  pl.BlockSpec((ns, n, kw), lambda b, i: (b, i, v_blk)),
                  pl.BlockSpec((ns, n, kw), lambda b, i: (b, prev(i), k_blk)),
                  pl.BlockSpec((ns, n, kw), lambda b, i: (b, prev(i), v_blk))],
        out_specs=pl.BlockSpec((ns, n, qw), lambda b, i: (b, i, 0)),
        out_shape=jax.ShapeDtypeStruct((batch, seq, qw), BF16),
        compiler_params=_cparams(("parallel", "parallel")),
        name="swa",
    )(sinks, qkv3, qkv3, qkv3, qkv3, qkv3)
    return out.reshape(t, qw)


def _layer_norm(v, g, b):
    mu = jnp.mean(v, axis=-1, keepdims=True)
    d = v - mu
    var = jnp.mean(d * d, axis=-1, keepdims=True)
    return d * lax.rsqrt(var + LN_EPS) * g + b


def _merge_kernel(alpha, ys_ref, ya_ref, gs_ref, ga_ref, x_ref, wso_ref, wao_ref, wmix_ref,
                  bgs_ref, bga_ref, lg_ref, lb_ref, *rest):
    h_ref, hp_ref = rest[-2:]
    y_ssm = jnp.dot(ys_ref[...], wso_ref[...], preferred_element_type=F32)
    y_att = jnp.dot(ya_ref[...], wao_ref[...], preferred_element_type=F32)
    merged = (_sigmoid(gs_ref[...] + bgs_ref[...]) * y_ssm
              + _sigmoid(ga_ref[...] + bga_ref[...]) * y_att)
    mix = jnp.dot(merged.astype(BF16), wmix_ref[...], preferred_element_type=F32)
    h = _layer_norm(alpha * x_ref[...] + mix, lg_ref[...], lb_ref[...])
    h_ref[...] = h
    hp_ref[...] = _pack_bf16_pair(h)


def _merge(alpha, y_ssm, y_att, proj, col, x, wso, wao, wmix, bgs, bga, lg, lb, tm, row0, rows, after):
    d = x.shape[1]
    tm = min(tm, rows)
    assert rows % tm == 0 and row0 % tm == 0
    blk0 = row0 // tm
    di = y_ssm.shape[1]
    da = y_att.shape[1]

    def const(i):
        return (0, 0)

    ordering = list(after)
    return pl.pallas_call(
        functools.partial(_merge_kernel, alpha),
        grid=(rows // tm,),
        in_specs=[pl.BlockSpec((tm, di), lambda i: (i + blk0, 0)),
                  pl.BlockSpec((tm, da), lambda i: (i + blk0, 0)),
                  pl.BlockSpec((tm, d), lambda i: (i + blk0, col["gs"] // d)),
                  pl.BlockSpec((tm, d), lambda i: (i + blk0, col["ga"] // d)),
                  pl.BlockSpec((tm, d), lambda i: (i + blk0, 0)),
                  pl.BlockSpec((di, d), const),
                  pl.BlockSpec((da, d), const),
                  pl.BlockSpec((d, d), const),
                  pl.BlockSpec((1, d), const),
                  pl.BlockSpec((1, d), const),
                  pl.BlockSpec((1, d), const),
                  pl.BlockSpec((1, d), const)] + [pl.BlockSpec(memory_space=pl.ANY)] * len(ordering),
        out_specs=[pl.BlockSpec((tm, d), lambda i: (i, 0)),
                   pl.BlockSpec((tm, d // 2), lambda i: (i, 0))],
        out_shape=[jax.ShapeDtypeStruct((rows, d), F32),
                   jax.ShapeDtypeStruct((rows, d // 2), jnp.uint32)],
        compiler_params=_cparams(("parallel",)),
        name="merge_ln1",
    )(y_ssm, y_att, proj, proj, x, wso, wao, wmix, bgs, bga, lg, lb, *ordering)


ROUTER_SUBTILE = 128


def _router_kernel(h_ref, wr_ref, br_ref, idx_ref, gw_ref, rank_ref, cnt_ref, base_ref):
    i = pl.program_id(0)
    tm = h_ref.shape[0]
    n_exp = br_ref.shape[0]
    sub = ROUTER_SUBTILE

    @pl.when(i == 0)
    def _():
        base_ref[...] = jnp.zeros_like(base_ref)

    h = h_ref[...]
    h1 = h.astype(BF16)
    h2 = (h - h1.astype(F32)).astype(BF16)
    first = jnp.dot(h1, wr_ref[...], preferred_element_type=F32)
    logits_t = (first[:, :LANES] + first[:, LANES:]
                + jnp.dot(h2, wr_ref[:, :LANES], preferred_element_type=F32))
    logits = logits_t.T[:n_exp, :] + br_ref[...]
    eid = lax.broadcasted_iota(jnp.int32, (n_exp, tm), 0)
    work = logits
    vals, idxs = [], []
    for _ in range(TOP_K):
        mx = jnp.max(work, axis=0, keepdims=True)
        sel = jnp.min(jnp.where(work == mx, eid, n_exp), axis=0, keepdims=True)
        vals.append(mx)
        idxs.append(sel)
        work = jnp.where(eid == sel, -jnp.inf, work)
    exps = [jnp.exp(v - vals[0]) for v in vals]
    den = exps[0]
    for e in exps[1:]:
        den = den + e
    r = lax.broadcasted_iota(jnp.int32, (sub, sub), 0)
    c = lax.broadcasted_iota(jnp.int32, (sub, sub), 1)
    tri = jnp.where(r < c, 1.0, 0.0).astype(BF16)
    base = base_ref[...]
    for k in range(TOP_K):
        onehot = eid == idxs[k]
        oh = jnp.where(onehot, 1.0, 0.0)
        ranks = []
        for s in range(tm // sub):
            lanes = slice(s * sub, (s + 1) * sub)
            before = jnp.dot(oh[:, lanes].astype(BF16), tri, preferred_element_type=F32)
            ranks.append(jnp.sum(jnp.where(onehot[:, lanes], before + base, 0.0), axis=0, keepdims=True))
            base = base + jnp.sum(oh[:, lanes], axis=1, keepdims=True)
        idx_ref[k:k + 1, :] = idxs[k]
        gw_ref[k:k + 1, :] = exps[k] / den
        rank_ref[k:k + 1, :] = jnp.concatenate(ranks, axis=1).astype(jnp.int32)
    base_ref[...] = base
    cnt_ref[...] = jnp.broadcast_to(base, cnt_ref.shape).astype(jnp.int32)


def _router(h, w_router, b_router, tm):
    t, d = h.shape
    n_exp = w_router.shape[1]
    tm = min(tm, t)
    assert tm % ROUTER_SUBTILE == 0 and n_exp <= LANES
    w_padded = jnp.pad(w_router, ((0, 0), (0, LANES - n_exp)))
    w_hi = w_padded.astype(BF16)
    w_lo = (w_padded - w_hi.astype(F32)).astype(BF16)
    w_padded = jnp.concatenate([w_hi, w_lo], axis=1)
    outs = pl.pallas_call(
        _router_kernel,
        grid=(t // tm,),
        in_specs=[pl.BlockSpec((tm, d), lambda i: (i, 0)),
                  pl.BlockSpec((d, 2 * LANES), lambda i: (0, 0)),
                  pl.BlockSpec((n_exp, 1), lambda i: (0, 0))],
        out_specs=[pl.BlockSpec((TOP_K, tm), lambda i: (0, i)),
                   pl.BlockSpec((TOP_K, tm), lambda i: (0, i)),
                   pl.BlockSpec((TOP_K, tm), lambda i: (0, i)),
                   pl.BlockSpec((n_exp, LANES), lambda i: (0, 0))],
        out_shape=[jax.ShapeDtypeStruct((TOP_K, t), jnp.int32),
                   jax.ShapeDtypeStruct((TOP_K, t), F32),
                   jax.ShapeDtypeStruct((TOP_K, t), jnp.int32),
                   jax.ShapeDtypeStruct((n_exp, LANES), jnp.int32)],
        scratch_shapes=[pltpu.VMEM((n_exp, 1), F32)],
        compiler_params=_cparams(("arbitrary",)),
        name="router",
    )(h, w_padded, b_router)
    return outs


SC_CORES = 2
SC_SUBCORES = 16
SC_GATHER_CHUNK = 128
MOE_SPLITS = 2


def _sc_gather_rows(table, idx):
    b = idx.shape[0]
    d = table.shape[1]
    n_workers = SC_CORES * SC_SUBCORES
    chunk = SC_GATHER_CHUNK
    assert b % (n_workers * chunk) == 0
    per_worker = b // n_workers
    n_chunks = per_worker // chunk
    mesh = plsc.VectorSubcoreMesh(core_axis_name="c", subcore_axis_name="s",
                                  num_cores=SC_CORES, num_subcores=SC_SUBCORES)

    def body(table_hbm, idx_hbm, out_hbm, idx_v, rows_v, sem):
        wid = lax.axis_index("s") * SC_CORES + lax.axis_index("c")
        base = wid * per_worker

        @pl.loop(0, n_chunks)
        def _(c):
            off = pl.multiple_of(base + c * chunk, SUBLANES)
            pltpu.sync_copy(idx_hbm.at[pl.ds(off, chunk)], idx_v)
            pltpu.async_copy(table_hbm.at[idx_v], rows_v, sem).wait()
            pltpu.sync_copy(rows_v, out_hbm.at[pl.ds(off, chunk)])

    return pl.kernel(
        body,
        out_type=jax.ShapeDtypeStruct((b, d), table.dtype),
        mesh=mesh,
        scratch_types=[pltpu.VMEM((chunk,), jnp.int32),
                       pltpu.VMEM((chunk, d), table.dtype),
                       pltpu.SemaphoreType.DMA],
        name="sc_gather_rows",
    )(table, idx)


def _sc_scatter_rows(src, dest_kt, fill_idx, n_out_rows):
    t, d = src.shape
    n_k = dest_kt.shape[0]
    n_fill = fill_idx.shape[0]
    n_workers = SC_CORES * SC_SUBCORES
    chunk = SC_GATHER_CHUNK
    assert t % (n_workers * chunk) == 0 and n_fill % (n_workers * chunk) == 0
    tok_per_worker = t // n_workers
    fill_per_worker = n_fill // n_workers
    mesh = plsc.VectorSubcoreMesh(core_axis_name="c", subcore_axis_name="s",
                                  num_cores=SC_CORES, num_subcores=SC_SUBCORES)

    def body(src_hbm, dest_hbm, fill_hbm, zeros_hbm, out_hbm, idx_v, rows_v):
        wid = lax.axis_index("s") * SC_CORES + lax.axis_index("c")

        pltpu.sync_copy(zeros_hbm, rows_v)

        @pl.loop(0, fill_per_worker // chunk)
        def _(c):
            off = pl.multiple_of(wid * fill_per_worker + c * chunk, SUBLANES)
            pltpu.sync_copy(fill_hbm.at[pl.ds(off, chunk)], idx_v)
            pltpu.sync_copy(rows_v, out_hbm.at[idx_v])

        @pl.loop(0, tok_per_worker // chunk)
        def _(c):
            off = pl.multiple_of(wid * tok_per_worker + c * chunk, SUBLANES)
            pltpu.sync_copy(src_hbm.at[pl.ds(off, chunk)], rows_v)
            for k in range(n_k):
                pltpu.sync_copy(dest_hbm.at[pl.ds(k * t + off, chunk)], idx_v)
                pltpu.sync_copy(rows_v, out_hbm.at[idx_v])

    return pl.kernel(
        body,
        out_type=jax.ShapeDtypeStruct((n_out_rows, d), src.dtype),
        mesh=mesh,
        scratch_types=[pltpu.VMEM((chunk,), jnp.int32),
                       pltpu.VMEM((chunk, d), src.dtype)],
        name="sc_scatter_rows",
    )(src, dest_kt.reshape(-1), fill_idx, jnp.zeros((chunk, d), src.dtype))


EXPERT_SUBROWS = 128


def _expert_kernel(be_ref, nxt_ref, eslot_ref, nsub_ref, nu_ref, x_ref, wgu_hbm, bgu_ref, wd_hbm,
                   bd_ref, y_ref, wgu_f, wd_f, wsem, wgu_b, wd_b):
    i = pl.program_id(0)
    n_used = nu_ref[0]
    expert = be_ref[i]
    changed = jnp.logical_or(i == 0, expert != be_ref[jnp.maximum(i - 1, 0)])

    def weight_copies(e, slot):
        return (pltpu.make_async_copy(wgu_hbm.at[e], wgu_f.at[slot], wsem.at[0, slot]),
                pltpu.make_async_copy(wd_hbm.at[e], wd_f.at[slot], wsem.at[1, slot]))

    @pl.when(jnp.logical_and(changed, i < n_used))
    def _():
        slot = eslot_ref[i]

        @pl.when(i == 0)
        def _():
            for cp in weight_copies(expert, slot):
                cp.start()

        for cp in weight_copies(expert, slot):
            cp.wait()
        nxt = nxt_ref[i]

        @pl.when(nxt >= 0)
        def _():
            for cp in weight_copies(nxt, 1 - slot):
                cp.start()

        wgu_b[...] = wgu_f[slot].astype(BF16)
        wd_b[...] = wd_f[slot].astype(BF16)

    def ffn(rows):
        d_ff = wd_b.shape[0]
        xb = _unpack_bf16_pair(x_ref[0:rows, :]).astype(BF16)
        hgu = jnp.dot(xb, wgu_b[...], preferred_element_type=F32) + bgu_ref[0]
        glu = jnp.minimum(hgu[:, :d_ff], SWIGLU_LIMIT)
        lin = jnp.clip(hgu[:, d_ff:], -SWIGLU_LIMIT, SWIGLU_LIMIT)
        act = glu * _sigmoid(SWIGLU_ALPHA * glu) * (lin + 1.0)
        y = jnp.dot(act.astype(BF16), wd_b[...], preferred_element_type=F32) + bd_ref[0]
        y_ref[0:rows, :] = _pack_bf16_pair(y)
        if rows < y_ref.shape[0]:
            y_ref[rows:, :] = jnp.zeros((y_ref.shape[0] - rows, y_ref.shape[1]), y_ref.dtype)

    n_sub = nsub_ref[i]
    for sub in range(1, y_ref.shape[0] // EXPERT_SUBROWS + 1):
        @pl.when(jnp.logical_and(i < n_used, n_sub == sub))
        def _(sub=sub):
            ffn(sub * EXPERT_SUBROWS)


def _experts(xs, block_expert, block_valid, n_used, wgu, bgu, wd, bd):
    d = wgu.shape[1]
    dp = xs.shape[1]
    assert 2 * dp == d
    n_blocks = block_expert.shape[0]
    n_exp, _, two_f = wgu.shape
    d_ff = wd.shape[1]
    rows = MOE_BLOCK
    n_rows = n_blocks * rows
    blk = jnp.arange(n_blocks, dtype=jnp.int32)
    is_first = jnp.concatenate([jnp.ones((1,), bool), block_expert[1:] != block_expert[:-1]])
    first_used = jnp.logical_and(is_first, blk < n_used[0])
    first_pos = jnp.where(first_used, blk, n_blocks)
    next_first = jnp.concatenate([lax.cummin(first_pos[::-1])[::-1][1:],
                                  jnp.full((1,), n_blocks, jnp.int32)])
    next_expert = jnp.where(next_first < n_blocks,
                            block_expert[jnp.minimum(next_first, n_blocks - 1)], -1).astype(jnp.int32)
    expert_slot = ((jnp.cumsum(is_first.astype(jnp.int32)) - 1) % 2).astype(jnp.int32)
    n_sub = jnp.clip(-(-block_valid // EXPERT_SUBROWS), 1, rows // EXPERT_SUBROWS).astype(jnp.int32)

    def last_used(i, be, nxt, es, ns, nu):
        return (jnp.minimum(i, jnp.maximum(nu[0] - 1, 0)), 0)

    grid_spec = pltpu.PrefetchScalarGridSpec(
        num_scalar_prefetch=5,
        grid=(n_blocks,),
        in_specs=[
            pl.BlockSpec((rows, dp), last_used),
            pl.BlockSpec(memory_space=pl.ANY),
            pl.BlockSpec((1, 1, two_f), lambda i, be, nxt, es, ns, nu: (be[i], 0, 0)),
            pl.BlockSpec(memory_space=pl.ANY),
            pl.BlockSpec((1, 1, d), lambda i, be, nxt, es, ns, nu: (be[i], 0, 0)),
        ],
        out_specs=pl.BlockSpec((rows, dp), last_used),
        scratch_shapes=[pltpu.VMEM((2, d, two_f), F32),
                        pltpu.VMEM((2, d_ff, d), F32),
                        pltpu.SemaphoreType.DMA((2, 2)),
                        pltpu.VMEM((d, two_f), BF16),
                        pltpu.VMEM((d_ff, d), BF16)],
    )
    return pl.pallas_call(
        _expert_kernel,
        grid_spec=grid_spec,
        out_shape=jax.ShapeDtypeStruct((n_rows, dp), jnp.uint32),
        compiler_params=_cparams(("arbitrary",)),
        name="experts",
    )(block_expert, next_expert, expert_slot, n_sub, n_used, xs, wgu, bgu.reshape(n_exp, 1, two_f), wd,
      bd.reshape(n_exp, 1, d))


def _combine_kernel(alpha, *refs):
    ys_refs = refs[:TOP_K]
    gw_ref, h_ref, lg_ref, lb_ref = refs[TOP_K:TOP_K + 4]
    o_ref = refs[-1]
    gw = gw_ref[...]
    acc = alpha * h_ref[...]
    for k in range(TOP_K):
        acc = acc + gw[:, k:k + 1] * _unpack_bf16_pair(ys_refs[k][...])
    o_ref[...] = _layer_norm(acc, lg_ref[...], lb_ref[...])


def _combine(alpha, y_slots, gw_rows, h, lg, lb, tm, out_prev, row0, total_rows):
    rows, d = h.shape
    tm = min(tm, rows)
    assert rows % tm == 0 and row0 % tm == 0
    nsteps = rows // tm
    blk0 = row0 // tm
    ys_specs = [pl.BlockSpec((tm, d // 2), functools.partial(lambda i, k: (k * nsteps + i, 0), k=k))
                for k in range(TOP_K)]
    in_specs = ys_specs + [pl.BlockSpec((tm, LANES), lambda i: (i, 0)),
                           pl.BlockSpec((tm, d), lambda i: (i, 0)),
                           pl.BlockSpec((1, d), lambda i: (0, 0)),
                           pl.BlockSpec((1, d), lambda i: (0, 0))]
    args = [y_slots] * TOP_K + [gw_rows, h, lg, lb]
    aliases = {}
    if out_prev is not None:
        in_specs.append(pl.BlockSpec(memory_space=pl.ANY))
        aliases = {len(args): 0}
        args.append(out_prev)
    return pl.pallas_call(
        functools.partial(_combine_kernel, alpha),
        grid=(nsteps,),
        in_specs=in_specs,
        out_specs=pl.BlockSpec((tm, d), lambda i: (i + blk0, 0)),
        out_shape=jax.ShapeDtypeStruct((total_rows, d), F32),
        input_output_aliases=aliases,
        compiler_params=_cparams(("parallel",)),
        name="combine_ln2",
    )(*args)


def _pack_in_proj(w_in, d_inner, n_ssm_heads, attn_dim):
    gn = SSM_GROUPS * D_STATE
    kv_dim = KV_HEADS * HEAD_DIM
    d = w_in.shape[0]
    o = 0
    seg = {}
    for name, width in (("z", d_inner), ("x", d_inner), ("B", gn), ("C", gn), ("dt", n_ssm_heads),
                        ("q", attn_dim), ("k", kv_dim), ("v", kv_dim), ("gs", d), ("ga", d)):
        seg[name] = w_in[:, o:o + width]
        o += width
    assert o == w_in.shape[1]
    dt_pad = jnp.pad(seg["dt"], ((0, 0), (0, LANES - n_ssm_heads)))
    order = (("z", seg["z"]), ("x", seg["x"]), ("gs", seg["gs"]), ("ga", seg["ga"]),
             ("B", seg["B"]), ("C", seg["C"]), ("dt", dt_pad))
    col, off = {}, 0
    for name, w in order:
        col[name] = off
        assert off % w.shape[1] == 0
        off += w.shape[1]
    w_f32grp = jnp.concatenate([w for _, w in order], axis=1).astype(BF16)
    w_qkv = jnp.concatenate([seg["q"], seg["k"], seg["v"]], axis=1).astype(BF16)
    return w_f32grp, w_qkv, col


def _largest_tile(n, cap):
    best = LANES
    for k in range(1, n // LANES + 1):
        if n % (k * LANES) == 0 and k * LANES <= cap:
            best = k * LANES
    return best


def _layer(h_in, batch, seq, alpha, w_in, conv_w, conv_b, dt_bias, a_log, d_skip, ssm_norm_w,
           w_ssm_out, attn_sinks, w_attn_out, b_gates, w_mix_out, ln1_g, ln1_b, w_router, b_router,
           w_gate_up, b_gate_up, w_down, b_down, ln2_g, ln2_b):
    t, d = h_in.shape
    d_inner = ssm_norm_w.shape[0]
    n_ssm_heads = dt_bias.shape[0]
    attn_dim = w_attn_out.shape[0]
    n_heads = attn_sinks.shape[0]
    n_exp = w_router.shape[1]
    gn = SSM_GROUPS * D_STATE

    w_f32grp, w_qkv, col = _pack_in_proj(w_in, d_inner, n_ssm_heads, attn_dim)
    proj = _matmul(h_in, w_f32grp, F32, IN_PROJ_ROW_TILE,
                   _largest_tile(w_f32grp.shape[1], IN_PROJ_MAX_COL_TILE))
    qkv = _matmul(h_in, w_qkv, BF16, IN_PROJ_ROW_TILE, _largest_tile(w_qkv.shape[1], QKV_MAX_COL_TILE))

    pad_h = (0, LANES - n_ssm_heads)
    y_ssm = _ssd(proj, batch, seq, col,
                 conv_w[:, :d_inner], conv_w[:, d_inner:d_inner + gn], conv_w[:, d_inner + gn:],
                 conv_b[None, :d_inner], conv_b[None, d_inner:d_inner + gn], conv_b[None, d_inner + gn:],
                 jnp.pad(dt_bias, pad_h)[None, :], jnp.pad(a_log, pad_h)[None, :],
                 jnp.repeat(d_skip, SSM_HEAD_DIM)[None, :], ssm_norm_w[None, :])
    y_att = _swa(qkv, attn_sinks, batch, seq, n_heads)

    wso, wao, wmix = w_ssm_out.astype(BF16), w_attn_out.astype(BF16), w_mix_out.astype(BF16)

    n_parts = MOE_SPLITS if t % (MOE_SPLITS * SC_CORES * SC_SUBCORES * SC_GATHER_CHUNK) == 0 else 1
    rows = t // n_parts
    out = None
    scatter_operands = ()
    for part in range(n_parts):
        row0 = part * rows
        h1, h1_packed = _merge(alpha, y_ssm, y_att, proj, col, h_in, wso, wao, wmix, b_gates[None, :d],
                               b_gates[None, d:], ln1_g[None, :], ln1_b[None, :], MERGE_ROW_TILE, row0, rows,
                               scatter_operands)
        idx_kt, gw_kt, rank_kt, counts = _router(h1, w_router, b_router[:, None], ROUTER_ROW_TILE)

        counts = counts[:, 0]
        padded = (counts + MOE_BLOCK - 1) // MOE_BLOCK * MOE_BLOCK
        pad_end = jnp.cumsum(padded)
        pad_start = pad_end - padded
        n_blocks = -(-(rows * TOP_K) // MOE_BLOCK) + n_exp
        n_rows = n_blocks * MOE_BLOCK
        expert_ids = jnp.arange(n_exp, dtype=jnp.int32)
        pad_start_of_slot = jnp.sum(
            jnp.where(idx_kt[None] == expert_ids[:, None, None], pad_start[:, None, None], 0), axis=0)
        dest_kt = pad_start_of_slot + rank_kt
        fill_rows = (pad_start + counts)[:, None] + jnp.arange(MOE_BLOCK, dtype=jnp.int32)[None, :]
        spare_rows = n_rows + jnp.arange(n_exp * MOE_BLOCK, dtype=jnp.int32).reshape(n_exp, MOE_BLOCK)
        fill_idx = jnp.where(fill_rows < pad_end[:, None], fill_rows, spare_rows).reshape(-1)
        block_row0 = jnp.arange(n_blocks, dtype=jnp.int32) * MOE_BLOCK
        block_expert = jnp.minimum(
            jnp.sum((pad_end[None, :] <= block_row0[:, None]).astype(jnp.int32), axis=1), n_exp - 1)
        n_used = (pad_end[-1:] // MOE_BLOCK).astype(jnp.int32)
        real_end_of_block = jnp.sum(jnp.where(block_expert[:, None] == expert_ids[None, :],
                                              (pad_start + counts)[None, :], 0), axis=1)
        block_valid = jnp.clip(real_end_of_block - block_row0, 0, MOE_BLOCK)
        gw_rows = jnp.pad(gw_kt.T, ((0, 0), (0, LANES - TOP_K)))

        scatter_operands = (dest_kt, fill_idx)
        xs = _sc_scatter_rows(h1_packed, dest_kt, fill_idx, n_rows + n_exp * MOE_BLOCK)
        ys = _experts(xs, block_expert, block_valid, n_used, w_gate_up, b_gate_up, w_down, b_down)
        y_slots = _sc_gather_rows(ys, dest_kt.reshape(-1))
        out = _combine(alpha, y_slots, gw_rows, h1, ln2_g[None, :], ln2_b[None, :], COMBINE_ROW_TILE,
                       out, row0, t)
    return out


def kernel(x, w_in, conv_w, conv_b, dt_bias, a_log, d_skip, ssm_norm_w, w_ssm_out, attn_sinks,
           w_attn_out, b_gates, w_mix_out, ln1_g, ln1_b, w_router, b_router, w_gate_up, b_gate_up,
           w_down, b_down, ln2_g, ln2_b):
    batch, seq, d = x.shape
    depth = w_in.shape[0]
    alpha = (2 * depth) ** 0.25
    h = x.reshape(batch * seq, d)
    for i in range(depth):
        h = _layer(h, batch, seq, alpha, w_in[i], conv_w[i], conv_b[i], dt_bias[i], a_log[i],
                   d_skip[i], ssm_norm_w[i], w_ssm_out[i], attn_sinks[i], w_attn_out[i], b_gates[i],
                   w_mix_out[i], ln1_g[i], ln1_b[i], w_router[i], b_router[i], w_gate_up[i],
                   b_gate_up[i], w_down[i], b_down[i], ln2_g[i], ln2_b[i])
    return h.reshape(batch, seq, d)
```

```python
import functools

import jax
import jax.numpy as jnp
from jax import lax
from jax.experimental import pallas as pl
from jax.experimental.pallas import tpu as pltpu
from jax.experimental.pallas import tpu_sc as plsc

SSM_HEAD_DIM = 64
SSM_GROUPS = 4
D_STATE = 128
CONV_WIDTH = 4
SSD_CHUNK = 128
SEQS_PER_STEP = 2
KV_HEADS = 4
HEAD_DIM = 64
WINDOW = 128
TOP_K = 4
SWIGLU_LIMIT = 7.0
SWIGLU_ALPHA = 1.702
MOE_BLOCK = 1024
LN_EPS = 1e-5
RMS_EPS = 1e-5

LANES = 128
SUBLANES = 8
NEG_BIG = -1e30
LOG2_E = 1.4426950408889634
F32 = jnp.float32
BF16 = jnp.bfloat16
VMEM_LIMIT = 56 * 1024 * 1024

IN_PROJ_ROW_TILE = 1024
IN_PROJ_MAX_COL_TILE = 2560
QKV_MAX_COL_TILE = 1536
MERGE_ROW_TILE = 512
ROUTER_ROW_TILE = 1024
COMBINE_ROW_TILE = 1024


def _cparams(sem):
    return pltpu.CompilerParams(dimension_semantics=sem, vmem_limit_bytes=VMEM_LIMIT)


def _sigmoid(x):
    return 0.5 + 0.5 * jnp.tanh(0.5 * x)


def _pack_bf16_pair(v):
    w = v.shape[1] // 2
    bits = lax.bitcast_convert_type(v.astype(BF16).astype(F32), jnp.uint32)
    return (bits[:, :w] >> 16) | (bits[:, w:] & jnp.uint32(0xFFFF0000))


def _unpack_bf16_pair(p):
    lo = lax.bitcast_convert_type(p << 16, F32)
    hi = lax.bitcast_convert_type(p & jnp.uint32(0xFFFF0000), F32)
    return jnp.concatenate([lo, hi], axis=1)


def _silu(x):
    h = 0.5 * x
    return h + h * jnp.tanh(h)


def _mm_kernel(x_ref, w_ref, o_ref):
    o_ref[...] = jnp.dot(x_ref[...].astype(BF16), w_ref[...],
                         preferred_element_type=F32).astype(o_ref.dtype)


def _matmul(x, w, out_dtype, tm, tn):
    m, k = x.shape
    n = w.shape[1]
    tm = min(tm, m)
    assert m % tm == 0 and n % tn == 0
    return pl.pallas_call(
        _mm_kernel,
        grid=(n // tn, m // tm),
        in_specs=[pl.BlockSpec((tm, k), lambda j, i: (i, 0)),
                  pl.BlockSpec((k, tn), lambda j, i: (0, j))],
        out_specs=pl.BlockSpec((tm, tn), lambda j, i: (i, j)),
        out_shape=jax.ShapeDtypeStruct((m, n), out_dtype),
        compiler_params=_cparams(("parallel", "parallel")),
        name="in_proj",
    )(x, w)


def _conv_silu(u_ref, prev_ref, w_ref, b_ref, first):
    cur = u_ref[...]
    prev = prev_ref[...]
    prev = jnp.where(first, jnp.zeros_like(prev), prev)
    row8 = lax.broadcasted_iota(jnp.int32, prev.shape, 0)
    acc = b_ref[...] + w_ref[CONV_WIDTH - 1:CONV_WIDTH, :] * cur
    for k in range(1, CONV_WIDTH):
        rolled = pltpu.roll(cur, k, axis=0)
        head = jnp.where(row8 < k, pltpu.roll(prev, k, axis=0), rolled[:SUBLANES])
        shifted = jnp.concatenate([head, rolled[SUBLANES:]], axis=0)
        j = CONV_WIDTH - 1 - k
        acc = acc + w_ref[j:j + 1, :] * shifted
    return _silu(acc)


def _expand_columns(v, e3_ref):
    x1 = v.astype(BF16)
    r1 = v - x1.astype(F32)
    x2 = r1.astype(BF16)
    x3 = (r1 - x2.astype(F32)).astype(BF16)
    return jnp.dot(jnp.concatenate([x1, x2, x3], axis=1), e3_ref[...], preferred_element_type=F32)


def _ssd_kernel(z_ref, x_ref, b_ref, c_ref, dt_ref, xp_ref, bp_ref, cp_ref,
                cwx_ref, cwb_ref, cwc_ref, cbx_ref, cbb_ref, cbc_ref,
                dtb_ref, alog_ref, dsk_ref, nw_ref, ehead_ref, ecol_ref,
                o_ref, st_ref):
    first = pl.program_id(1) == 0

    @pl.when(first)
    def _():
        st_ref[...] = jnp.zeros_like(st_ref)

    for s in range(z_ref.shape[0]):
        _ssd_chunk(z_ref.at[s], x_ref.at[s], b_ref.at[s], c_ref.at[s], dt_ref.at[s],
                   xp_ref.at[s], bp_ref.at[s], cp_ref.at[s],
                   cwx_ref, cwb_ref, cwc_ref, cbx_ref, cbb_ref, cbc_ref,
                   dtb_ref, alog_ref, dsk_ref, nw_ref, ehead_ref, ecol_ref,
                   o_ref.at[s], st_ref.at[s], first)


def _ssd_chunk(z_ref, x_ref, b_ref, c_ref, dt_ref, xp_ref, bp_ref, cp_ref,
               cwx_ref, cwb_ref, cwc_ref, cbx_ref, cbb_ref, cbc_ref,
               dtb_ref, alog_ref, dsk_ref, nw_ref, ehead_ref, ecol_ref, o_ref, st_ref, first):
    n = SSD_CHUNK
    heads_per_group = st_ref.shape[2] // SSM_HEAD_DIM
    gw = heads_per_group * SSM_HEAD_DIM

    xc = _conv_silu(x_ref, xp_ref, cwx_ref, cbx_ref, first)
    bc = _conv_silu(b_ref, bp_ref, cwb_ref, cbb_ref, first)
    cc = _conv_silu(c_ref, cp_ref, cwc_ref, cbc_ref, first)

    dt_in = dt_ref[...] + dtb_ref[...]
    dt = jnp.maximum(dt_in, 0.0) + jnp.log(1.0 + jnp.exp(-jnp.abs(dt_in)))
    a = dt * (-jnp.exp(alog_ref[...]))
    row = lax.broadcasted_iota(jnp.int32, (n, n), 0)
    col = lax.broadcasted_iota(jnp.int32, (n, n), 1)
    causal = row >= col
    tri = jnp.where(causal, 1.0, 0.0).astype(F32)
    acum = jnp.dot(tri, a, preferred_element_type=F32, precision=lax.Precision.HIGHEST)
    acum2 = acum * LOG2_E
    acum2_t = acum2.T
    a_last = acum[n - 1:n, :]
    lo_mask = lax.broadcasted_iota(jnp.int32, (n, LANES), 1) < SSM_HEAD_DIM
    dt_x = _expand_columns(dt, ehead_ref)
    d2e_x = _expand_columns(jnp.exp(a_last - acum), ehead_ref)
    eac_x = _expand_columns(jnp.exp(acum), ehead_ref)
    acum2_x = _expand_columns(acum2, ecol_ref)

    for g in range(SSM_GROUPS):
        xg = xc[:, g * gw:(g + 1) * gw]
        bg = bc[:, g * D_STATE:(g + 1) * D_STATE]
        cg = cc[:, g * D_STATE:(g + 1) * D_STATE]
        cg_b = cg.astype(BF16)
        bg_t = bg.T.astype(BF16)
        cb = jnp.dot(cg_b, bg_t, preferred_element_type=F32) * tri
        st_prev = st_ref[g]
        y_off = jnp.dot(cg_b, st_prev.astype(BF16), preferred_element_type=F32)
        y_parts, xw_parts, ea_parts = [], [], []
        for j in range(heads_per_group // 2):
            h0 = g * heads_per_group + 2 * j
            lanes = slice(j * LANES, (j + 1) * LANES)
            glanes = slice(g * gw + j * LANES, g * gw + (j + 1) * LANES)
            xs_pair = xg[:, lanes] * dt_x[:, glanes]
            xs_b = xs_pair.astype(BF16)
            xw_parts.append((xs_pair * d2e_x[:, glanes]).astype(BF16))
            ea_parts.append(eac_x[:, glanes])
            ys = []
            for h in (h0, h0 + 1):
                seg2 = acum2_x[:, h * n:(h + 1) * n] - acum2_t[h:h + 1, :]
                decay = jnp.exp2(jnp.minimum(seg2, 0.0))
                m = (cb * decay).astype(BF16)
                ys.append(jnp.dot(m, xs_b, preferred_element_type=F32))
            y_parts.append(jnp.where(lo_mask, ys[0], ys[1]))
        y_diag = jnp.concatenate(y_parts, axis=1)
        ea = jnp.concatenate(ea_parts, axis=1)
        xw = jnp.concatenate(xw_parts, axis=1)
        cols = slice(g * gw, (g + 1) * gw)
        y = y_diag + y_off * ea + xg * dsk_ref[:, cols]
        zg = z_ref[:, cols]
        y = y * _silu(zg)
        ms = jnp.mean(y * y, axis=-1, keepdims=True)
        o_ref[:, cols] = (y * lax.rsqrt(ms + RMS_EPS) * nw_ref[:, cols]).astype(o_ref.dtype)
        st_ref[g] = st_prev * ea[n - 1:n, :] + jnp.dot(bg_t, xw, preferred_element_type=F32)


def _ssd(proj, batch, seq, col, conv_wx, conv_wb, conv_wc, conv_bx, conv_bb, conv_bc,
         dt_bias, a_log, d_skip, norm_w):
    t, width = proj.shape
    d_inner = norm_w.shape[1]
    gn = SSM_GROUPS * D_STATE
    n = SSD_CHUNK
    nc = seq // n
    gw = d_inner // SSM_GROUPS
    ns = SEQS_PER_STEP if batch % SEQS_PER_STEP == 0 else 1
    proj3 = proj.reshape(batch, seq, width)
    n_heads = d_inner // SSM_HEAD_DIM
    head_id = jnp.arange(LANES, dtype=jnp.int32)[:, None]
    e_head = (head_id == (jnp.arange(d_inner, dtype=jnp.int32) // SSM_HEAD_DIM)[None, :]).astype(BF16)
    e_col = (head_id == (jnp.arange(n_heads * n, dtype=jnp.int32) // n)[None, :]).astype(BF16)
    e_head3 = jnp.tile(e_head, (3, 1))
    e_col3 = jnp.tile(e_col, (3, 1))

    def prev_rows(c):
        return jnp.maximum(c * (n // SUBLANES) - 1, 0)

    def const(b, c):
        return (0, 0)

    in_specs = [
        pl.BlockSpec((ns, n, d_inner), lambda b, c: (b, c, col["z"] // d_inner)),
        pl.BlockSpec((ns, n, d_inner), lambda b, c: (b, c, col["x"] // d_inner)),
        pl.BlockSpec((ns, n, gn), lambda b, c: (b, c, col["B"] // gn)),
        pl.BlockSpec((ns, n, gn), lambda b, c: (b, c, col["C"] // gn)),
        pl.BlockSpec((ns, n, LANES), lambda b, c: (b, c, col["dt"] // LANES)),
        pl.BlockSpec((ns, SUBLANES, d_inner), lambda b, c: (b, prev_rows(c), col["x"] // d_inner)),
        pl.BlockSpec((ns, SUBLANES, gn), lambda b, c: (b, prev_rows(c), col["B"] // gn)),
        pl.BlockSpec((ns, SUBLANES, gn), lambda b, c: (b, prev_rows(c), col["C"] // gn)),
        pl.BlockSpec((CONV_WIDTH, d_inner), const),
        pl.BlockSpec((CONV_WIDTH, gn), const),
        pl.BlockSpec((CONV_WIDTH, gn), const),
        pl.BlockSpec((1, d_inner), const),
        pl.BlockSpec((1, gn), const),
        pl.BlockSpec((1, gn), const),
        pl.BlockSpec((1, LANES), const),
        pl.BlockSpec((1, LANES), const),
        pl.BlockSpec((1, d_inner), const),
        pl.BlockSpec((1, d_inner), const),
        pl.BlockSpec(e_head3.shape, const),
        pl.BlockSpec(e_col3.shape, const),
    ]
    out = pl.pallas_call(
        _ssd_kernel,
        grid=(batch // ns, nc),
        in_specs=in_specs,
        out_specs=pl.BlockSpec((ns, n, d_inner), lambda b, c: (b, c, 0)),
        out_shape=jax.ShapeDtypeStruct((batch, seq, d_inner), BF16),
        scratch_shapes=[pltpu.VMEM((ns, SSM_GROUPS, D_STATE, gw), F32)],
        compiler_params=_cparams(("arbitrary", "arbitrary")),
        name="ssd_mixer",
    )(proj3, proj3, proj3, proj3, proj3, proj3, proj3, proj3,
      conv_wx, conv_wb, conv_wc, conv_bx, conv_bb, conv_bc, dt_bias, a_log, d_skip, norm_w,
      e_head3, e_col3)
    return out.reshape(t, d_inner)


def _swa_kernel(sink_ref, q_ref, kc_ref, vc_ref, kp_ref, vp_ref, o_ref):
    for s in range(q_ref.shape[0]):
        _swa_block(sink_ref, q_ref.at[s], kc_ref.at[s], vc_ref.at[s], kp_ref.at[s], vp_ref.at[s],
                   o_ref.at[s])


def _swa_block(sink_ref, q_ref, kc_ref, vc_ref, kp_ref, vp_ref, o_ref):
    i = pl.program_id(1)
    n = WINDOW
    n_heads = q_ref.shape[1] // HEAD_DIM
    grp = n_heads // KV_HEADS
    qpos = lax.broadcasted_iota(jnp.int32, (n, 2 * n), 0) + n
    kpos = lax.broadcasted_iota(jnp.int32, (n, 2 * n), 1)
    diff = qpos - kpos
    mask = (diff >= 0) & (diff < n) & ((kpos >= n) | (i > 0))
    lo_q = lax.broadcasted_iota(jnp.int32, (n, LANES), 1) < HEAD_DIM
    lo_k = lax.broadcasted_iota(jnp.int32, (2 * n, LANES), 1) < HEAD_DIM
    scale = HEAD_DIM ** -0.5
    exp2_scale = scale * LOG2_E
    for p in range(KV_HEADS // 2):
        lanes = slice(p * LANES, (p + 1) * LANES)
        kk = jnp.concatenate([kp_ref[:, lanes], kc_ref[:, lanes]], axis=0).astype(F32)
        vv = jnp.concatenate([vp_ref[:, lanes], vc_ref[:, lanes]], axis=0).astype(F32)
        kk_sw = pltpu.roll(kk, HEAD_DIM, axis=1)
        vv_sw = pltpu.roll(vv, HEAD_DIM, axis=1)
        for par in range(2):
            g = 2 * p + par
            k2 = (jnp.where(lo_k, kk, kk_sw) if par == 0 else jnp.where(lo_k, kk_sw, kk)).astype(BF16)
            v2 = (jnp.where(lo_k, vv, vv_sw) if par == 0 else jnp.where(lo_k, vv_sw, vv)).astype(BF16)
            q_rows = []
            for hq in range(grp):
                h = g * grp + hq
                q_pair = q_ref[:, (h // 2) * LANES:(h // 2 + 1) * LANES]
                keep = lo_q if h % 2 == 0 else jnp.logical_not(lo_q)
                q_rows.append(jnp.where(keep, q_pair, jnp.zeros_like(q_pair)))
            s_all = lax.dot_general(jnp.concatenate(q_rows, axis=0), k2, (((1,), (1,)), ((), ())),
                                    preferred_element_type=F32)
            probs, dens = [], []
            for hq in range(grp):
                s = jnp.where(mask, s_all[hq * n:(hq + 1) * n, :], NEG_BIG)
                sink = sink_ref[g * grp + hq] / scale
                mx = jnp.maximum(jnp.max(s, axis=-1, keepdims=True), sink)
                pr = jnp.exp2((s - mx) * exp2_scale)
                dens.append(jnp.sum(pr, axis=-1, keepdims=True) + jnp.exp2((sink - mx) * exp2_scale))
                probs.append(pr.astype(BF16))
            o_all = jnp.dot(jnp.concatenate(probs, axis=0), v2, preferred_element_type=F32)
            for qp in range(grp // 2):
                outs = [o_all[(2 * qp + hh) * n:(2 * qp + hh + 1) * n, :] / dens[2 * qp + hh] for hh in range(2)]
                pair = (g * grp) // 2 + qp
                o_ref[:, pair * LANES:(pair + 1) * LANES] = jnp.where(lo_q, outs[0], outs[1]).astype(o_ref.dtype)


def _swa(qkv, sinks, batch, seq, n_heads):
    t, width = qkv.shape
    n = WINDOW
    nb = seq // n
    qw = n_heads * HEAD_DIM
    kw = KV_HEADS * HEAD_DIM
    k_blk = qw // kw
    v_blk = k_blk + 1
    ns = SEQS_PER_STEP if batch % SEQS_PER_STEP == 0 else 1
    qkv3 = qkv.reshape(batch, seq, width)

    def prev(i):
        return jnp.maximum(i - 1, 0)

    out = pl.pallas_call(
        _swa_kernel,
        grid=(batch // ns, nb),
        in_specs=[pl.BlockSpec(memory_space=pltpu.SMEM),
                  pl.BlockSpec((ns, n, qw), lambda b, i: (b, i, 0)),
                  pl.BlockSpec((ns, n, kw), lambda b, i: (b, i, k_blk)),
                  pl.BlockSpec((ns, n, kw), lambda b, i: (b, i, v_blk)),
                  pl.BlockSpec((ns, n, kw), lambda b, i: (b, prev(i), k_blk)),
                  pl.BlockSpec((ns, n, kw), lambda b, i: (b, prev(i), v_blk))],
        out_specs=pl.BlockSpec((ns, n, qw), lambda b, i: (b, i, 0)),
        out_shape=jax.ShapeDtypeStruct((batch, seq, qw), BF16),
        compiler_params=_cparams(("parallel", "parallel")),
        name="swa",
    )(sinks, qkv3, qkv3, qkv3, qkv3, qkv3)
    return out.reshape(t, qw)


def _layer_norm(v, g, b):
    mu = jnp.mean(v, axis=-1, keepdims=True)
    d = v - mu
    var = jnp.mean(d * d, axis=-1, keepdims=True)
    return d * lax.rsqrt(var + LN_EPS) * g + b


def _merge_kernel(alpha, ys_ref, ya_ref, gs_ref, ga_ref, x_ref, wso_ref, wao_ref, wmix_ref,
                  bgs_ref, bga_ref, lg_ref, lb_ref, *rest):
    h_ref, hp_ref = rest[-2:]
    y_ssm = jnp.dot(ys_ref[...], wso_ref[...], preferred_element_type=F32)
    y_att = jnp.dot(ya_ref[...], wao_ref[...], preferred_element_type=F32)
    merged = (_sigmoid(gs_ref[...] + bgs_ref[...]) * y_ssm
              + _sigmoid(ga_ref[...] + bga_ref[...]) * y_att)
    mix = jnp.dot(merged.astype(BF16), wmix_ref[...], preferred_element_type=F32)
    h = _layer_norm(alpha * x_ref[...] + mix, lg_ref[...], lb_ref[...])
    h_ref[...] = h
    hp_ref[...] = _pack_bf16_pair(h)


def _merge(alpha, y_ssm, y_att, proj, col, x, wso, wao, wmix, bgs, bga, lg, lb, tm, row0, rows, after):
    d = x.shape[1]
    tm = min(tm, rows)
    assert rows % tm == 0 and row0 % tm == 0
    blk0 = row0 // tm
    di = y_ssm.shape[1]
    da = y_att.shape[1]

    def const(i):
        return (0, 0)

    ordering = list(after)
    return pl.pallas_call(
        functools.partial(_merge_kernel, alpha),
        grid=(rows // tm,),
        in_specs=[pl.BlockSpec((tm, di), lambda i: (i + blk0, 0)),
                  pl.BlockSpec((tm, da), lambda i: (i + blk0, 0)),
                  pl.BlockSpec((tm, d), lambda i: (i + blk0, col["gs"] // d)),
                  pl.BlockSpec((tm, d), lambda i: (i + blk0, col["ga"] // d)),
                  pl.BlockSpec((tm, d), lambda i: (i + blk0, 0)),
                  pl.BlockSpec((di, d), const),
                  pl.BlockSpec((da, d), const),
                  pl.BlockSpec((d, d), const),
                  pl.BlockSpec((1, d), const),
                  pl.BlockSpec((1, d), const),
                  pl.BlockSpec((1, d), const),
                  pl.BlockSpec((1, d), const)] + [pl.BlockSpec(memory_space=pl.ANY)] * len(ordering),
        out_specs=[pl.BlockSpec((tm, d), lambda i: (i, 0)),
                   pl.BlockSpec((tm, d // 2), lambda i: (i, 0))],
        out_shape=[jax.ShapeDtypeStruct((rows, d), F32),
                   jax.ShapeDtypeStruct((rows, d // 2), jnp.uint32)],
        compiler_params=_cparams(("parallel",)),
        name="merge_ln1",
    )(y_ssm, y_att, proj, proj, x, wso, wao, wmix, bgs, bga, lg, lb, *ordering)


ROUTER_SUBTILE = 128


def _router_kernel(h_ref, wr_ref, br_ref, idx_ref, gw_ref, rank_ref, cnt_ref, base_ref):
    i = pl.program_id(0)
    tm = h_ref.shape[0]
    n_exp = br_ref.shape[0]
    sub = ROUTER_SUBTILE

    @pl.when(i == 0)
    def _():
        base_ref[...] = jnp.zeros_like(base_ref)

    h = h_ref[...]
    h1 = h.astype(BF16)
    h2 = (h - h1.astype(F32)).astype(BF16)
    first = jnp.dot(h1, wr_ref[...], preferred_element_type=F32)
    logits_t = (first[:, :LANES] + first[:, LANES:]
                + jnp.dot(h2, wr_ref[:, :LANES], preferred_element_type=F32))
    logits = logits_t.T[:n_exp, :] + br_ref[...]
    eid = lax.broadcasted_iota(jnp.int32, (n_exp, tm), 0)
    work = logits
    vals, idxs = [], []
    for _ in range(TOP_K):
        mx = jnp.max(work, axis=0, keepdims=True)
        sel = jnp.min(jnp.where(work == mx, eid, n_exp), axis=0, keepdims=True)
        vals.append(mx)
        idxs.append(sel)
        work = jnp.where(eid == sel, -jnp.inf, work)
    exps = [jnp.exp(v - vals[0]) for v in vals]
    den = exps[0]
    for e in exps[1:]:
        den = den + e
    r = lax.broadcasted_iota(jnp.int32, (sub, sub), 0)
    c = lax.broadcasted_iota(jnp.int32, (sub, sub), 1)
    tri = jnp.where(r < c, 1.0, 0.0).astype(BF16)
    base = base_ref[...]
    for k in range(TOP_K):
        onehot = eid == idxs[k]
        oh = jnp.where(onehot, 1.0, 0.0)
        ranks = []
        for s in range(tm // sub):
            lanes = slice(s * sub, (s + 1) * sub)
            before = jnp.dot(oh[:, lanes].astype(BF16), tri, preferred_element_type=F32)
            ranks.append(jnp.sum(jnp.where(onehot[:, lanes], before + base, 0.0), axis=0, keepdims=True))
            base = base + jnp.sum(oh[:, lanes], axis=1, keepdims=True)
        idx_ref[k:k + 1, :] = idxs[k]
        gw_ref[k:k + 1, :] = exps[k] / den
        rank_ref[k:k + 1, :] = jnp.concatenate(ranks, axis=1).astype(jnp.int32)
    base_ref[...] = base
    cnt_ref[...] = jnp.broadcast_to(base, cnt_ref.shape).astype(jnp.int32)


def _router(h, w_router, b_router, tm):
    t, d = h.shape
    n_exp = w_router.shape[1]
    tm = min(tm, t)
    assert tm % ROUTER_SUBTILE == 0 and n_exp <= LANES
    w_padded = jnp.pad(w_router, ((0, 0), (0, LANES - n_exp)))
    w_hi = w_padded.astype(BF16)
    w_lo = (w_padded - w_hi.astype(F32)).astype(BF16)
    w_padded = jnp.concatenate([w_hi, w_lo], axis=1)
    outs = pl.pallas_call(
        _router_kernel,
        grid=(t // tm,),
        in_specs=[pl.BlockSpec((tm, d), lambda i: (i, 0)),
                  pl.BlockSpec((d, 2 * LANES), lambda i: (0, 0)),
                  pl.BlockSpec((n_exp, 1), lambda i: (0, 0))],
        out_specs=[pl.BlockSpec((TOP_K, tm), lambda i: (0, i)),
                   pl.BlockSpec((TOP_K, tm), lambda i: (0, i)),
                   pl.BlockSpec((TOP_K, tm), lambda i: (0, i)),
                   pl.BlockSpec((n_exp, LANES), lambda i: (0, 0))],
        out_shape=[jax.ShapeDtypeStruct((TOP_K, t), jnp.int32),
                   jax.ShapeDtypeStruct((TOP_K, t), F32),
                   jax.ShapeDtypeStruct((TOP_K, t), jnp.int32),
                   jax.ShapeDtypeStruct((n_exp, LANES), jnp.int32)],
        scratch_shapes=[pltpu.VMEM((n_exp, 1), F32)],
        compiler_params=_cparams(("arbitrary",)),
        name="router",
    )(h, w_padded, b_router)
    return outs


SC_CORES = 2
SC_SUBCORES = 16
SC_GATHER_CHUNK = 128
MOE_SPLITS = 2


def _sc_gather_rows(table, idx):
    b = idx.shape[0]
    d = table.shape[1]
    n_workers = SC_CORES * SC_SUBCORES
    chunk = SC_GATHER_CHUNK
    assert b % (n_workers * chunk) == 0
    per_worker = b // n_workers
    n_chunks = per_worker // chunk
    mesh = plsc.VectorSubcoreMesh(core_axis_name="c", subcore_axis_name="s",
                                  num_cores=SC_CORES, num_subcores=SC_SUBCORES)

    def body(table_hbm, idx_hbm, out_hbm, idx_v, rows_v, sem):
        wid = lax.axis_index("s") * SC_CORES + lax.axis_index("c")
        base = wid * per_worker

        @pl.loop(0, n_chunks)
        def _(c):
            off = pl.multiple_of(base + c * chunk, SUBLANES)
            pltpu.sync_copy(idx_hbm.at[pl.ds(off, chunk)], idx_v)
            pltpu.async_copy(table_hbm.at[idx_v], rows_v, sem).wait()
            pltpu.sync_copy(rows_v, out_hbm.at[pl.ds(off, chunk)])

    return pl.kernel(
        body,
        out_type=jax.ShapeDtypeStruct((b, d), table.dtype),
        mesh=mesh,
        scratch_types=[pltpu.VMEM((chunk,), jnp.int32),
                       pltpu.VMEM((chunk, d), table.dtype),
                       pltpu.SemaphoreType.DMA],
        name="sc_gather_rows",
    )(table, idx)


def _sc_scatter_rows(src, dest_kt, fill_idx, n_out_rows):
    t, d = src.shape
    n_k = dest_kt.shape[0]
    n_fill = fill_idx.shape[0]
    n_workers = SC_CORES * SC_SUBCORES
    chunk = SC_GATHER_CHUNK
    assert t % (n_workers * chunk) == 0 and n_fill % (n_workers * chunk) == 0
    tok_per_worker = t // n_workers
    fill_per_worker = n_fill // n_workers
    mesh = plsc.VectorSubcoreMesh(core_axis_name="c", subcore_axis_name="s",
                                  num_cores=SC_CORES, num_subcores=SC_SUBCORES)

    def body(src_hbm, dest_hbm, fill_hbm, zeros_hbm, out_hbm, idx_v, rows_v):
        wid = lax.axis_index("s") * SC_CORES + lax.axis_index("c")

        pltpu.sync_copy(zeros_hbm, rows_v)

        @pl.loop(0, fill_per_worker // chunk)
        def _(c):
            off = pl.multiple_of(wid * fill_per_worker + c * chunk, SUBLANES)
            pltpu.sync_copy(fill_hbm.at[pl.ds(off, chunk)], idx_v)
            pltpu.sync_copy(rows_v, out_hbm.at[idx_v])

        @pl.loop(0, tok_per_worker // chunk)
        def _(c):
            off = pl.multiple_of(wid * tok_per_worker + c * chunk, SUBLANES)
            pltpu.sync_copy(src_hbm.at[pl.ds(off, chunk)], rows_v)
            for k in range(n_k):
                pltpu.sync_copy(dest_hbm.at[pl.ds(k * t + off, chunk)], idx_v)
                pltpu.sync_copy(rows_v, out_hbm.at[idx_v])

    return pl.kernel(
        body,
        out_type=jax.ShapeDtypeStruct((n_out_rows, d), src.dtype),
        mesh=mesh,
        scratch_types=[pltpu.VMEM((chunk,), jnp.int32),
                       pltpu.VMEM((chunk, d), src.dtype)],
        name="sc_scatter_rows",
    )(src, dest_kt.reshape(-1), fill_idx, jnp.zeros((chunk, d), src.dtype))


EXPERT_SUBROWS = 128


def _expert_kernel(be_ref, nxt_ref, eslot_ref, nsub_ref, nu_ref, x_ref, wgu_hbm, bgu_ref, wd_hbm,
                   bd_ref, y_ref, wgu_f, wd_f, wsem, wgu_b, wd_b):
    i = pl.program_id(0)
    n_used = nu_ref[0]
    expert = be_ref[i]
    changed = jnp.logical_or(i == 0, expert != be_ref[jnp.maximum(i - 1, 0)])

    def weight_copies(e, slot):
        return (pltpu.make_async_copy(wgu_hbm.at[e], wgu_f.at[slot], wsem.at[0, slot]),
                pltpu.make_async_copy(wd_hbm.at[e], wd_f.at[slot], wsem.at[1, slot]))

    @pl.when(jnp.logical_and(changed, i < n_used))
    def _():
        slot = eslot_ref[i]

        @pl.when(i == 0)
        def _():
            for cp in weight_copies(expert, slot):
                cp.start()

        for cp in weight_copies(expert, slot):
            cp.wait()
        nxt = nxt_ref[i]

        @pl.when(nxt >= 0)
        def _():
            for cp in weight_copies(nxt, 1 - slot):
                cp.start()

        wgu_b[...] = wgu_f[slot].astype(BF16)
        wd_b[...] = wd_f[slot].astype(BF16)

    def ffn(rows):
        d_ff = wd_b.shape[0]
        xb = _unpack_bf16_pair(x_ref[0:rows, :]).astype(BF16)
        hgu = jnp.dot(xb, wgu_b[...], preferred_element_type=F32) + bgu_ref[0]
        glu = jnp.minimum(hgu[:, :d_ff], SWIGLU_LIMIT)
        lin = jnp.clip(hgu[:, d_ff:], -SWIGLU_LIMIT, SWIGLU_LIMIT)
        act = glu * _sigmoid(SWIGLU_ALPHA * glu) * (lin + 1.0)
        y = jnp.dot(act.astype(BF16), wd_b[...], preferred_element_type=F32) + bd_ref[0]
        y_ref[0:rows, :] = _pack_bf16_pair(y)
        if rows < y_ref.shape[0]:
            y_ref[rows:, :] = jnp.zeros((y_ref.shape[0] - rows, y_ref.shape[1]), y_ref.dtype)

    n_sub = nsub_ref[i]
    for sub in range(1, y_ref.shape[0] // EXPERT_SUBROWS + 1):
        @pl.when(jnp.logical_and(i < n_used, n_sub == sub))
        def _(sub=sub):
            ffn(sub * EXPERT_SUBROWS)


def _experts(xs, block_expert, block_valid, n_used, wgu, bgu, wd, bd):
    d = wgu.shape[1]
    dp = xs.shape[1]
    assert 2 * dp == d
    n_blocks = block_expert.shape[0]
    n_exp, _, two_f = wgu.shape
    d_ff = wd.shape[1]
    rows = MOE_BLOCK
    n_rows = n_blocks * rows
    blk = jnp.arange(n_blocks, dtype=jnp.int32)
    is_first = jnp.concatenate([jnp.ones((1,), bool), block_expert[1:] != block_expert[:-1]])
    first_used = jnp.logical_and(is_first, blk < n_used[0])
    first_pos = jnp.where(first_used, blk, n_blocks)
    next_first = jnp.concatenate([lax.cummin(first_pos[::-1])[::-1][1:],
                                  jnp.full((1,), n_blocks, jnp.int32)])
    next_expert = jnp.where(next_first < n_blocks,
                            block_expert[jnp.minimum(next_first, n_blocks - 1)], -1).astype(jnp.int32)
    expert_slot = ((jnp.cumsum(is_first.astype(jnp.int32)) - 1) % 2).astype(jnp.int32)
    n_sub = jnp.clip(-(-block_valid // EXPERT_SUBROWS), 1, rows // EXPERT_SUBROWS).astype(jnp.int32)

    def last_used(i, be, nxt, es, ns, nu):
        return (jnp.minimum(i, jnp.maximum(nu[0] - 1, 0)), 0)

    grid_spec = pltpu.PrefetchScalarGridSpec(
        num_scalar_prefetch=5,
        grid=(n_blocks,),
        in_specs=[
            pl.BlockSpec((rows, dp), last_used),
            pl.BlockSpec(memory_space=pl.ANY),
            pl.BlockSpec((1, 1, two_f), lambda i, be, nxt, es, ns, nu: (be[i], 0, 0)),
            pl.BlockSpec(memory_space=pl.ANY),
            pl.BlockSpec((1, 1, d), lambda i, be, nxt, es, ns, nu: (be[i], 0, 0)),
        ],
        out_specs=pl.BlockSpec((rows, dp), last_used),
        scratch_shapes=[pltpu.VMEM((2, d, two_f), F32),
                        pltpu.VMEM((2, d_ff, d), F32),
                        pltpu.SemaphoreType.DMA((2, 2)),
                        pltpu.VMEM((d, two_f), BF16),
                        pltpu.VMEM((d_ff, d), BF16)],
    )
    return pl.pallas_call(
        _expert_kernel,
        grid_spec=grid_spec,
        out_shape=jax.ShapeDtypeStruct((n_rows, dp), jnp.uint32),
        compiler_params=_cparams(("arbitrary",)),
        name="experts",
    )(block_expert, next_expert, expert_slot, n_sub, n_used, xs, wgu, bgu.reshape(n_exp, 1, two_f), wd,
      bd.reshape(n_exp, 1, d))


def _combine_kernel(alpha, *refs):
    ys_refs = refs[:TOP_K]
    gw_ref, h_ref, lg_ref, lb_ref = refs[TOP_K:TOP_K + 4]
    o_ref = refs[-1]
    gw = gw_ref[...]
    acc = alpha * h_ref[...]
    for k in range(TOP_K):
        acc = acc + gw[:, k:k + 1] * _unpack_bf16_pair(ys_refs[k][...])
    o_ref[...] = _layer_norm(acc, lg_ref[...], lb_ref[...])


def _combine(alpha, y_slots, gw_rows, h, lg, lb, tm, out_prev, row0, total_rows):
    rows, d = h.shape
    tm = min(tm, rows)
    assert rows % tm == 0 and row0 % tm == 0
    nsteps = rows // tm
    blk0 = row0 // tm
    ys_specs = [pl.BlockSpec((tm, d // 2), functools.partial(lambda i, k: (k * nsteps + i, 0), k=k))
                for k in range(TOP_K)]
    in_specs = ys_specs + [pl.BlockSpec((tm, LANES), lambda i: (i, 0)),
                           pl.BlockSpec((tm, d), lambda i: (i, 0)),
                           pl.BlockSpec((1, d), lambda i: (0, 0)),
                           pl.BlockSpec((1, d), lambda i: (0, 0))]
    args = [y_slots] * TOP_K + [gw_rows, h, lg, lb]
    aliases = {}
    if out_prev is not None:
        in_specs.append(pl.BlockSpec(memory_space=pl.ANY))
        aliases = {len(args): 0}
        args.append(out_prev)
    return pl.pallas_call(
        functools.partial(_combine_kernel, alpha),
        grid=(nsteps,),
        in_specs=in_specs,
        out_specs=pl.BlockSpec((tm, d), lambda i: (i + blk0, 0)),
        out_shape=jax.ShapeDtypeStruct((total_rows, d), F32),
        input_output_aliases=aliases,
        compiler_params=_cparams(("parallel",)),
        name="combine_ln2",
    )(*args)


def _pack_in_proj(w_in, d_inner, n_ssm_heads, attn_dim):
    gn = SSM_GROUPS * D_STATE
    kv_dim = KV_HEADS * HEAD_DIM
    d = w_in.shape[0]
    o = 0
    seg = {}
    for name, width in (("z", d_inner), ("x", d_inner), ("B", gn), ("C", gn), ("dt", n_ssm_heads),
                        ("q", attn_dim), ("k", kv_dim), ("v", kv_dim), ("gs", d), ("ga", d)):
        seg[name] = w_in[:, o:o + width]
        o += width
    assert o == w_in.shape[1]
    dt_pad = jnp.pad(seg["dt"], ((0, 0), (0, LANES - n_ssm_heads)))
    order = (("z", seg["z"]), ("x", seg["x"]), ("gs", seg["gs"]), ("ga", seg["ga"]),
             ("B", seg["B"]), ("C", seg["C"]), ("dt", dt_pad))
    col, off = {}, 0
    for name, w in order:
        col[name] = off
        assert off % w.shape[1] == 0
        off += w.shape[1]
    w_f32grp = jnp.concatenate([w for _, w in order], axis=1).astype(BF16)
    w_qkv = jnp.concatenate([seg["q"], seg["k"], seg["v"]], axis=1).astype(BF16)
    return w_f32grp, w_qkv, col


def _largest_tile(n, cap):
    best = LANES
    for k in range(1, n // LANES + 1):
        if n % (k * LANES) == 0 and k * LANES <= cap:
            best = k * LANES
    return best


def _layer(h_in, batch, seq, alpha, w_in, conv_w, conv_b, dt_bias, a_log, d_skip, ssm_norm_w,
           w_ssm_out, attn_sinks, w_attn_out, b_gates, w_mix_out, ln1_g, ln1_b, w_router, b_router,
           w_gate_up, b_gate_up, w_down, b_down, ln2_g, ln2_b):
    t, d = h_in.shape
    d_inner = ssm_norm_w.shape[0]
    n_ssm_heads = dt_bias.shape[0]
    attn_dim = w_attn_out.shape[0]
    n_heads = attn_sinks.shape[0]
    n_exp = w_router.shape[1]
    gn = SSM_GROUPS * D_STATE

    w_f32grp, w_qkv, col = _pack_in_proj(w_in, d_inner, n_ssm_heads, attn_dim)
    proj = _matmul(h_in, w_f32grp, F32, IN_PROJ_ROW_TILE,
                   _largest_tile(w_f32grp.shape[1], IN_PROJ_MAX_COL_TILE))
    qkv = _matmul(h_in, w_qkv, BF16, IN_PROJ_ROW_TILE, _largest_tile(w_qkv.shape[1], QKV_MAX_COL_TILE))

    pad_h = (0, LANES - n_ssm_heads)
    y_ssm = _ssd(proj, batch, seq, col,
                 conv_w[:, :d_inner], conv_w[:, d_inner:d_inner + gn], conv_w[:, d_inner + gn:],
                 conv_b[None, :d_inner], conv_b[None, d_inner:d_inner + gn], conv_b[None, d_inner + gn:],
                 jnp.pad(dt_bias, pad_h)[None, :], jnp.pad(a_log, pad_h)[None, :],
                 jnp.repeat(d_skip, SSM_HEAD_DIM)[None, :], ssm_norm_w[None, :])
    y_att = _swa(qkv, attn_sinks, batch, seq, n_heads)

    wso, wao, wmix = w_ssm_out.astype(BF16), w_attn_out.astype(BF16), w_mix_out.astype(BF16)

    n_parts = MOE_SPLITS if t % (MOE_SPLITS * SC_CORES * SC_SUBCORES * SC_GATHER_CHUNK) == 0 else 1
    rows = t // n_parts
    out = None
    scatter_operands = ()
    for part in range(n_parts):
        row0 = part * rows
        h1, h1_packed = _merge(alpha, y_ssm, y_att, proj, col, h_in, wso, wao, wmix, b_gates[None, :d],
                               b_gates[None, d:], ln1_g[None, :], ln1_b[None, :], MERGE_ROW_TILE, row0, rows,
                               scatter_operands)
        idx_kt, gw_kt, rank_kt, counts = _router(h1, w_router, b_router[:, None], ROUTER_ROW_TILE)

        counts = counts[:, 0]
        padded = (counts + MOE_BLOCK - 1) // MOE_BLOCK * MOE_BLOCK
        pad_end = jnp.cumsum(padded)
        pad_start = pad_end - padded
        n_blocks = -(-(rows * TOP_K) // MOE_BLOCK) + n_exp
        n_rows = n_blocks * MOE_BLOCK
        expert_ids = jnp.arange(n_exp, dtype=jnp.int32)
        pad_start_of_slot = jnp.sum(
            jnp.where(idx_kt[None] == expert_ids[:, None, None], pad_start[:, None, None], 0), axis=0)
        dest_kt = pad_start_of_slot + rank_kt
        fill_rows = (pad_start + counts)[:, None] + jnp.arange(MOE_BLOCK, dtype=jnp.int32)[None, :]
        spare_rows = n_rows + jnp.arange(n_exp * MOE_BLOCK, dtype=jnp.int32).reshape(n_exp, MOE_BLOCK)
        fill_idx = jnp.where(fill_rows < pad_end[:, None], fill_rows, spare_rows).reshape(-1)
        block_row0 = jnp.arange(n_blocks, dtype=jnp.int32) * MOE_BLOCK
        block_expert = jnp.minimum(
            jnp.sum((pad_end[None, :] <= block_row0[:, None]).astype(jnp.int32), axis=1), n_exp - 1)
        n_used = (pad_end[-1:] // MOE_BLOCK).astype(jnp.int32)
        real_end_of_block = jnp.sum(jnp.where(block_expert[:, None] == expert_ids[None, :],
                                              (pad_start + counts)[None, :], 0), axis=1)
        block_valid = jnp.clip(real_end_of_block - block_row0, 0, MOE_BLOCK)
        gw_rows = jnp.pad(gw_kt.T, ((0, 0), (0, LANES - TOP_K)))

        scatter_operands = (dest_kt, fill_idx)
        xs = _sc_scatter_rows(h1_packed, dest_kt, fill_idx, n_rows + n_exp * MOE_BLOCK)
        ys = _experts(xs, block_expert, block_valid, n_used, w_gate_up, b_gate_up, w_down, b_down)
        y_slots = _sc_gather_rows(ys, dest_kt.reshape(-1))
        out = _combine(alpha, y_slots, gw_rows, h1, ln2_g[None, :], ln2_b[None, :], COMBINE_ROW_TILE,
                       out, row0, t)
    return out


def kernel(x, w_in, conv_w, conv_b, dt_bias, a_log, d_skip, ssm_norm_w, w_ssm_out, attn_sinks,
           w_attn_out, b_gates, w_mix_out, ln1_g, ln1_b, w_router, b_router, w_gate_up, b_gate_up,
           w_down, b_down, ln2_g, ln2_b):
    batch, seq, d = x.shape
    depth = w_in.shape[0]
    alpha = (2 * depth) ** 0.25
    h = x.reshape(batch * seq, d)
    for i in range(depth):
        h = _layer(h, batch, seq, alpha, w_in[i], conv_w[i], conv_b[i], dt_bias[i], a_log[i],
                   d_skip[i], ssm_norm_w[i], w_ssm_out[i], attn_sinks[i], w_attn_out[i], b_gates[i],
                   w_mix_out[i], ln1_g[i], ln1_b[i], w_router[i], b_router[i], w_gate_up[i],
                   b_gate_up[i], w_down[i], b_down[i], ln2_g[i], ln2_b[i])
    return h.reshape(batch, seq, d)
```

```python
import functools

import jax
import jax.numpy as jnp
from jax import lax
from jax.experimental import pallas as pl
from jax.experimental.pallas import tpu as pltpu
from jax.experimental.pallas import tpu_sc as plsc

SSM_HEAD_DIM = 64
SSM_GROUPS = 4
D_STATE = 128
CONV_WIDTH = 4
SSD_CHUNK = 128
SEQS_PER_STEP = 2
KV_HEADS = 4
HEAD_DIM = 64
WINDOW = 128
TOP_K = 4
SWIGLU_LIMIT = 7.0
SWIGLU_ALPHA = 1.702
MOE_BLOCK = 1024
LN_EPS = 1e-5
RMS_EPS = 1e-5

LANES = 128
SUBLANES = 8
NEG_BIG = -1e30
LOG2_E = 1.4426950408889634
F32 = jnp.float32
BF16 = jnp.bfloat16
VMEM_LIMIT = 56 * 1024 * 1024

IN_PROJ_ROW_TILE = 1024
IN_PROJ_MAX_COL_TILE = 2560
QKV_MAX_COL_TILE = 1536
MERGE_ROW_TILE = 512
ROUTER_ROW_TILE = 1024
COMBINE_ROW_TILE = 1024


def _cparams(sem):
    return pltpu.CompilerParams(dimension_semantics=sem, vmem_limit_bytes=VMEM_LIMIT)


def _sigmoid(x):
    return 0.5 + 0.5 * jnp.tanh(0.5 * x)


def _pack_bf16_pair(v):
    w = v.shape[1] // 2
    bits = lax.bitcast_convert_type(v.astype(BF16).astype(F32), jnp.uint32)
    return (bits[:, :w] >> 16) | (bits[:, w:] & jnp.uint32(0xFFFF0000))


def _unpack_bf16_pair(p):
    lo = lax.bitcast_convert_type(p << 16, F32)
    hi = lax.bitcast_convert_type(p & jnp.uint32(0xFFFF0000), F32)
    return jnp.concatenate([lo, hi], axis=1)


def _silu(x):
    h = 0.5 * x
    return h + h * jnp.tanh(h)


def _mm_kernel(x_ref, w_ref, o_ref):
    o_ref[...] = jnp.dot(x_ref[...].astype(BF16), w_ref[...],
                         preferred_element_type=F32).astype(o_ref.dtype)


def _matmul(x, w, out_dtype, tm, tn):
    m, k = x.shape
    n = w.shape[1]
    tm = min(tm, m)
    assert m % tm == 0 and n % tn == 0
    return pl.pallas_call(
        _mm_kernel,
        grid=(n // tn, m // tm),
        in_specs=[pl.BlockSpec((tm, k), lambda j, i: (i, 0)),
                  pl.BlockSpec((k, tn), lambda j, i: (0, j))],
        out_specs=pl.BlockSpec((tm, tn), lambda j, i: (i, j)),
        out_shape=jax.ShapeDtypeStruct((m, n), out_dtype),
        compiler_params=_cparams(("parallel", "parallel")),
        name="in_proj",
    )(x, w)


def _conv_silu(u_ref, prev_ref, w_ref, b_ref, first):
    cur = u_ref[...]
    prev = prev_ref[...]
    prev = jnp.where(first, jnp.zeros_like(prev), prev)
    row8 = lax.broadcasted_iota(jnp.int32, prev.shape, 0)
    acc = b_ref[...] + w_ref[CONV_WIDTH - 1:CONV_WIDTH, :] * cur
    for k in range(1, CONV_WIDTH):
        rolled = pltpu.roll(cur, k, axis=0)
        head = jnp.where(row8 < k, pltpu.roll(prev, k, axis=0), rolled[:SUBLANES])
        shifted = jnp.concatenate([head, rolled[SUBLANES:]], axis=0)
        j = CONV_WIDTH - 1 - k
        acc = acc + w_ref[j:j + 1, :] * shifted
    return _silu(acc)


def _expand_columns(v, e3_ref):
    x1 = v.astype(BF16)
    r1 = v - x1.astype(F32)
    x2 = r1.astype(BF16)
    x3 = (r1 - x2.astype(F32)).astype(BF16)
    return jnp.dot(jnp.concatenate([x1, x2, x3], axis=1), e3_ref[...], preferred_element_type=F32)


def _ssd_kernel(z_ref, x_ref, b_ref, c_ref, dt_ref, xp_ref, bp_ref, cp_ref,
                cwx_ref, cwb_ref, cwc_ref, cbx_ref, cbb_ref, cbc_ref,
                dtb_ref, alog_ref, dsk_ref, nw_ref, ehead_ref, ecol_ref,
                o_ref, st_ref):
    first = pl.program_id(1) == 0

    @pl.when(first)
    def _():
        st_ref[...] = jnp.zeros_like(st_ref)

    for s in range(z_ref.shape[0]):
        _ssd_chunk(z_ref.at[s], x_ref.at[s], b_ref.at[s], c_ref.at[s], dt_ref.at[s],
                   xp_ref.at[s], bp_ref.at[s], cp_ref.at[s],
                   cwx_ref, cwb_ref, cwc_ref, cbx_ref, cbb_ref, cbc_ref,
                   dtb_ref, alog_ref, dsk_ref, nw_ref, ehead_ref, ecol_ref,
                   o_ref.at[s], st_ref.at[s], first)


def _ssd_chunk(z_ref, x_ref, b_ref, c_ref, dt_ref, xp_ref, bp_ref, cp_ref,
               cwx_ref, cwb_ref, cwc_ref, cbx_ref, cbb_ref, cbc_ref,
               dtb_ref, alog_ref, dsk_ref, nw_ref, ehead_ref, ecol_ref, o_ref, st_ref, first):
    n = SSD_CHUNK
    heads_per_group = st_ref.shape[2] // SSM_HEAD_DIM
    gw = heads_per_group * SSM_HEAD_DIM

    xc = _conv_silu(x_ref, xp_ref, cwx_ref, cbx_ref, first)
    bc = _conv_silu(b_ref, bp_ref, cwb_ref, cbb_ref, first)
    cc = _conv_silu(c_ref, cp_ref, cwc_ref, cbc_ref, first)

    dt_in = dt_ref[...] + dtb_ref[...]
    dt = jnp.maximum(dt_in, 0.0) + jnp.log(1.0 + jnp.exp(-jnp.abs(dt_in)))
    a = dt * (-jnp.exp(alog_ref[...]))
    row = lax.broadcasted_iota(jnp.int32, (n, n), 0)
    col = lax.broadcasted_iota(jnp.int32, (n, n), 1)
    causal = row >= col
    tri = jnp.where(causal, 1.0, 0.0).astype(F32)
    acum = jnp.dot(tri, a, preferred_element_type=F32, precision=lax.Precision.HIGHEST)
    acum2 = acum * LOG2_E
    acum2_t = acum2.T
    a_last = acum[n - 1:n, :]
    lo_mask = lax.broadcasted_iota(jnp.int32, (n, LANES), 1) < SSM_HEAD_DIM
    dt_x = _expand_columns(dt, ehead_ref)
    d2e_x = _expand_columns(jnp.exp(a_last - acum), ehead_ref)
    eac_x = _expand_columns(jnp.exp(acum), ehead_ref)
    acum2_x = _expand_columns(acum2, ecol_ref)

    for g in range(SSM_GROUPS):
        xg = xc[:, g * gw:(g + 1) * gw]
        bg = bc[:, g * D_STATE:(g + 1) * D_STATE]
        cg = cc[:, g * D_STATE:(g + 1) * D_STATE]
        cg_b = cg.astype(BF16)
        bg_t = bg.T.astype(BF16)
        cb = jnp.dot(cg_b, bg_t, preferred_element_type=F32) * tri
        st_prev = st_ref[g]
        y_off = jnp.dot(cg_b, st_prev.astype(BF16), preferred_element_type=F32)
        y_parts, xw_parts, ea_parts = [], [], []
        for j in range(heads_per_group // 2):
            h0 = g * heads_per_group + 2 * j
            lanes = slice(j * LANES, (j + 1) * LANES)
            glanes = slice(g * gw + j * LANES, g * gw + (j + 1) * LANES)
            xs_pair = xg[:, lanes] * dt_x[:, glanes]
            xs_b = xs_pair.astype(BF16)
            xw_parts.append((xs_pair * d2e_x[:, glanes]).astype(BF16))
            ea_parts.append(eac_x[:, glanes])
            ys = []
            for h in (h0, h0 + 1):
                seg2 = acum2_x[:, h * n:(h + 1) * n] - acum2_t[h:h + 1, :]
                decay = jnp.exp2(jnp.minimum(seg2, 0.0))
                m = (cb * decay).astype(BF16)
                ys.append(jnp.dot(m, xs_b, preferred_element_type=F32))
            y_parts.append(jnp.where(lo_mask, ys[0], ys[1]))
        y_diag = jnp.concatenate(y_parts, axis=1)
        ea = jnp.concatenate(ea_parts, axis=1)
        xw = jnp.concatenate(xw_parts, axis=1)
        cols = slice(g * gw, (g + 1) * gw)
        y = y_diag + y_off * ea + xg * dsk_ref[:, cols]
        zg = z_ref[:, cols]
        y = y * _silu(zg)
        ms = jnp.mean(y * y, axis=-1, keepdims=True)
        o_ref[:, cols] = (y * lax.rsqrt(ms + RMS_EPS) * nw_ref[:, cols]).astype(o_ref.dtype)
        st_ref[g] = st_prev * ea[n - 1:n, :] + jnp.dot(bg_t, xw, preferred_element_type=F32)


def _ssd(proj, batch, seq, col, conv_wx, conv_wb, conv_wc, conv_bx, conv_bb, conv_bc,
         dt_bias, a_log, d_skip, norm_w):
    t, width = proj.shape
    d_inner = norm_w.shape[1]
    gn = SSM_GROUPS * D_STATE
    n = SSD_CHUNK
    nc = seq // n
    gw = d_inner // SSM_GROUPS
    ns = SEQS_PER_STEP if batch % SEQS_PER_STEP == 0 else 1
    proj3 = proj.reshape(batch, seq, width)
    n_heads = d_inner // SSM_HEAD_DIM
    head_id = jnp.arange(LANES, dtype=jnp.int32)[:, None]
    e_head = (head_id == (jnp.arange(d_inner, dtype=jnp.int32) // SSM_HEAD_DIM)[None, :]).astype(BF16)
    e_col = (head_id == (jnp.arange(n_heads * n, dtype=jnp.int32) // n)[None, :]).astype(BF16)
    e_head3 = jnp.tile(e_head, (3, 1))
    e_col3 = jnp.tile(e_col, (3, 1))

    def prev_rows(c):
        return jnp.maximum(c * (n // SUBLANES) - 1, 0)

    def const(b, c):
        return (0, 0)

    in_specs = [
        pl.BlockSpec((ns, n, d_inner), lambda b, c: (b, c, col["z"] // d_inner)),
        pl.BlockSpec((ns, n, d_inner), lambda b, c: (b, c, col["x"] // d_inner)),
        pl.BlockSpec((ns, n, gn), lambda b, c: (b, c, col["B"] // gn)),
        pl.BlockSpec((ns, n, gn), lambda b, c: (b, c, col["C"] // gn)),
        pl.BlockSpec((ns, n, LANES), lambda b, c: (b, c, col["dt"] // LANES)),
        pl.BlockSpec((ns, SUBLANES, d_inner), lambda b, c: (b, prev_rows(c), col["x"] // d_inner)),
        pl.BlockSpec((ns, SUBLANES, gn), lambda b, c: (b, prev_rows(c), col["B"] // gn)),
        pl.BlockSpec((ns, SUBLANES, gn), lambda b, c: (b, prev_rows(c), col["C"] // gn)),
        pl.BlockSpec((CONV_WIDTH, d_inner), const),
        pl.BlockSpec((CONV_WIDTH, gn), const),
        pl.BlockSpec((CONV_WIDTH, gn), const),
        pl.BlockSpec((1, d_inner), const),
        pl.BlockSpec((1, gn), const),
        pl.BlockSpec((1, gn), const),
        pl.BlockSpec((1, LANES), const),
        pl.BlockSpec((1, LANES), const),
        pl.BlockSpec((1, d_inner), const),
        pl.BlockSpec((1, d_inner), const),
        pl.BlockSpec(e_head3.shape, const),
        pl.BlockSpec(e_col3.shape, const),
    ]
    out = pl.pallas_call(
        _ssd_kernel,
        grid=(batch // ns, nc),
        in_specs=in_specs,
        out_specs=pl.BlockSpec((ns, n, d_inner), lambda b, c: (b, c, 0)),
        out_shape=jax.ShapeDtypeStruct((batch, seq, d_inner), BF16),
        scratch_shapes=[pltpu.VMEM((ns, SSM_GROUPS, D_STATE, gw), F32)],
        compiler_params=_cparams(("arbitrary", "arbitrary")),
        name="ssd_mixer",
    )(proj3, proj3, proj3, proj3, proj3, proj3, proj3, proj3,
      conv_wx, conv_wb, conv_wc, conv_bx, conv_bb, conv_bc, dt_bias, a_log, d_skip, norm_w,
      e_head3, e_col3)
    return out.reshape(t, d_inner)


def _swa_kernel(sink_ref, q_ref, kc_ref, vc_ref, kp_ref, vp_ref, o_ref):
    for s in range(q_ref.shape[0]):
        _swa_block(sink_ref, q_ref.at[s], kc_ref.at[s], vc_ref.at[s], kp_ref.at[s], vp_ref.at[s],
                   o_ref.at[s])


def _swa_block(sink_ref, q_ref, kc_ref, vc_ref, kp_ref, vp_ref, o_ref):
    i = pl.program_id(1)
    n = WINDOW
    n_heads = q_ref.shape[1] // HEAD_DIM
    grp = n_heads // KV_HEADS
    qpos = lax.broadcasted_iota(jnp.int32, (n, 2 * n), 0) + n
    kpos = lax.broadcasted_iota(jnp.int32, (n, 2 * n), 1)
    diff = qpos - kpos
    mask = (diff >= 0) & (diff < n) & ((kpos >= n) | (i > 0))
    lo_q = lax.broadcasted_iota(jnp.int32, (n, LANES), 1) < HEAD_DIM
    lo_k = lax.broadcasted_iota(jnp.int32, (2 * n, LANES), 1) < HEAD_DIM
    scale = HEAD_DIM ** -0.5
    exp2_scale = scale * LOG2_E
    for p in range(KV_HEADS // 2):
        lanes = slice(p * LANES, (p + 1) * LANES)
        kk = jnp.concatenate([kp_ref[:, lanes], kc_ref[:, lanes]], axis=0).astype(F32)
        vv = jnp.concatenate([vp_ref[:, lanes], vc_ref[:, lanes]], axis=0).astype(F32)
        kk_sw = pltpu.roll(kk, HEAD_DIM, axis=1)
        vv_sw = pltpu.roll(vv, HEAD_DIM, axis=1)
        for par in range(2):
            g = 2 * p + par
            k2 = (jnp.where(lo_k, kk, kk_sw) if par == 0 else jnp.where(lo_k, kk_sw, kk)).astype(BF16)
            v2 = (jnp.where(lo_k, vv, vv_sw) if par == 0 else jnp.where(lo_k, vv_sw, vv)).astype(BF16)
            q_rows = []
            for hq in range(grp):
                h = g * grp + hq
                q_pair = q_ref[:, (h // 2) * LANES:(h // 2 + 1) * LANES]
                keep = lo_q if h % 2 == 0 else jnp.logical_not(lo_q)
                q_rows.append(jnp.where(keep, q_pair, jnp.zeros_like(q_pair)))
            s_all = lax.dot_general(jnp.concatenate(q_rows, axis=0), k2, (((1,), (1,)), ((), ())),
                                    preferred_element_type=F32)
            probs, dens = [], []
            for hq in range(grp):
                s = jnp.where(mask, s_all[hq * n:(hq + 1) * n, :], NEG_BIG)
                sink = sink_ref[g * grp + hq] / scale
                mx = jnp.maximum(jnp.max(s, axis=-1, keepdims=True), sink)
                pr = jnp.exp2((s - mx) * exp2_scale)
                dens.append(jnp.sum(pr, axis=-1, keepdims=True) + jnp.exp2((sink - mx) * exp2_scale))
                probs.append(pr.astype(BF16))
            o_all = jnp.dot(jnp.concatenate(probs, axis=0), v2, preferred_element_type=F32)
            for qp in range(grp // 2):
                outs = [o_all[(2 * qp + hh) * n:(2 * qp + hh + 1) * n, :] / dens[2 * qp + hh] for hh in range(2)]
                pair = (g * grp) // 2 + qp
                o_ref[:, pair * LANES:(pair + 1) * LANES] = jnp.where(lo_q, outs[0], outs[1]).astype(o_ref.dtype)


def _swa(qkv, sinks, batch, seq, n_heads):
    t, width = qkv.shape
    n = WINDOW
    nb = seq // n
    qw = n_heads * HEAD_DIM
    kw = KV_HEADS * HEAD_DIM
    k_blk = qw // kw
    v_blk = k_blk + 1
    ns = SEQS_PER_STEP if batch % SEQS_PER_STEP == 0 else 1
    qkv3 = qkv.reshape(batch, seq, width)

    def prev(i):
        return jnp.maximum(i - 1, 0)

    out = pl.pallas_call(
        _swa_kernel,
        grid=(batch // ns, nb),
        in_specs=[pl.BlockSpec(memory_space=pltpu.SMEM),
                  pl.BlockSpec((ns, n, qw), lambda b, i: (b, i, 0)),
                  pl.BlockSpec((ns, n, kw), lambda b, i: (b, i, k_blk)),
                  pl.BlockSpec((ns, n, kw), lambda b, i: (b, i, v_blk)),
                  pl.BlockSpec((ns, n, kw), lambda b, i: (b, prev(i), k_blk)),
                  pl.BlockSpec((ns, n, kw), lambda b, i: (b, prev(i), v_blk))],
        out_specs=pl.BlockSpec((ns, n, qw), lambda b, i: (b, i, 0)),
        out_shape=jax.ShapeDtypeStruct((batch, seq, qw), BF16),
        compiler_params=_cparams(("parallel", "parallel")),
        name="swa",
    )(sinks, qkv3, qkv3, qkv3, qkv3, qkv3)
    return out.reshape(t, qw)


def _layer_norm(v, g, b):
    mu = jnp.mean(v, axis=-1, keepdims=True)
    d = v - mu
    var = jnp.mean(d * d, axis=-1, keepdims=True)
    return d * lax.rsqrt(var + LN_EPS) * g + b


def _merge_kernel(alpha, ys_ref, ya_ref, gs_ref, ga_ref, x_ref, wso_ref, wao_ref, wmix_ref,
                  bgs_ref, bga_ref, lg_ref, lb_ref, *rest):
    h_ref, hp_ref = rest[-2:]
    y_ssm = jnp.dot(ys_ref[...], wso_ref[...], preferred_element_type=F32)
    y_att = jnp.dot(ya_ref[...], wao_ref[...], preferred_element_type=F32)
    merged = (_sigmoid(gs_ref[...] + bgs_ref[...]) * y_ssm
              + _sigmoid(ga_ref[...] + bga_ref[...]) * y_att)
    mix = jnp.dot(merged.astype(BF16), wmix_ref[...], preferred_element_type=F32)
    h = _layer_norm(alpha * x_ref[...] + mix, lg_ref[...], lb_ref[...])
    h_ref[...] = h
    hp_ref[...] = _pack_bf16_pair(h)


def _merge(alpha, y_ssm, y_att, proj, col, x, wso, wao, wmix, bgs, bga, lg, lb, tm, row0, rows, after):
    d = x.shape[1]
    tm = min(tm, rows)
    assert rows % tm == 0 and row0 % tm == 0
    blk0 = row0 // tm
    di = y_ssm.shape[1]
    da = y_att.shape[1]

    def const(i):
        return (0, 0)

    ordering = list(after)
    return pl.pallas_call(
        functools.partial(_merge_kernel, alpha),
        grid=(rows // tm,),
        in_specs=[pl.BlockSpec((tm, di), lambda i: (i + blk0, 0)),
                  pl.BlockSpec((tm, da), lambda i: (i + blk0, 0)),
                  pl.BlockSpec((tm, d), lambda i: (i + blk0, col["gs"] // d)),
                  pl.BlockSpec((tm, d), lambda i: (i + blk0, col["ga"] // d)),
                  pl.BlockSpec((tm, d), lambda i: (i + blk0, 0)),
                  pl.BlockSpec((di, d), const),
                  pl.BlockSpec((da, d), const),
                  pl.BlockSpec((d, d), const),
                  pl.BlockSpec((1, d), const),
                  pl.BlockSpec((1, d), const),
                  pl.BlockSpec((1, d), const),
                  pl.BlockSpec((1, d), const)] + [pl.BlockSpec(memory_space=pl.ANY)] * len(ordering),
        out_specs=[pl.BlockSpec((tm, d), lambda i: (i, 0)),
                   pl.BlockSpec((tm, d // 2), lambda i: (i, 0))],
        out_shape=[jax.ShapeDtypeStruct((rows, d), F32),
                   jax.ShapeDtypeStruct((rows, d // 2), jnp.uint32)],
        compiler_params=_cparams(("parallel",)),
        name="merge_ln1",
    )(y_ssm, y_att, proj, proj, x, wso, wao, wmix, bgs, bga, lg, lb, *ordering)


ROUTER_SUBTILE = 128


def _router_kernel(h_ref, wr_ref, br_ref, idx_ref, gw_ref, rank_ref, cnt_ref, base_ref):
    i = pl.program_id(0)
    tm = h_ref.shape[0]
    n_exp = br_ref.shape[0]
    sub = ROUTER_SUBTILE

    @pl.when(i == 0)
    def _():
        base_ref[...] = jnp.zeros_like(base_ref)

    h = h_ref[...]
    h1 = h.astype(BF16)
    h2 = (h - h1.astype(F32)).astype(BF16)
    first = jnp.dot(h1, wr_ref[...], preferred_element_type=F32)
    logits_t = (first[:, :LANES] + first[:, LANES:]
                + jnp.dot(h2, wr_ref[:, :LANES], preferred_element_type=F32))
    logits = logits_t.T[:n_exp, :] + br_ref[...]
    eid = lax.broadcasted_iota(jnp.int32, (n_exp, tm), 0)
    work = logits
    vals, idxs = [], []
    for _ in range(TOP_K):
        mx = jnp.max(work, axis=0, keepdims=True)
        sel = jnp.min(jnp.where(work == mx, eid, n_exp), axis=0, keepdims=True)
        vals.append(mx)
        idxs.append(sel)
        work = jnp.where(eid == sel, -jnp.inf, work)
    exps = [jnp.exp(v - vals[0]) for v in vals]
    den = exps[0]
    for e in exps[1:]:
        den = den + e
    r = lax.broadcasted_iota(jnp.int32, (sub, sub), 0)
    c = lax.broadcasted_iota(jnp.int32, (sub, sub), 1)
    tri = jnp.where(r < c, 1.0, 0.0).astype(BF16)
    base = base_ref[...]
    for k in range(TOP_K):
        onehot = eid == idxs[k]
        oh = jnp.where(onehot, 1.0, 0.0)
        ranks = []
        for s in range(tm // sub):
            lanes = slice(s * sub, (s + 1) * sub)
            before = jnp.dot(oh[:, lanes].astype(BF16), tri, preferred_element_type=F32)
            ranks.append(jnp.sum(jnp.where(onehot[:, lanes], before + base, 0.0), axis=0, keepdims=True))
            base = base + jnp.sum(oh[:, lanes], axis=1, keepdims=True)
        idx_ref[k:k + 1, :] = idxs[k]
        gw_ref[k:k + 1, :] = exps[k] / den
        rank_ref[k:k + 1, :] = jnp.concatenate(ranks, axis=1).astype(jnp.int32)
    base_ref[...] = base
    cnt_ref[...] = jnp.broadcast_to(base, cnt_ref.shape).astype(jnp.int32)


def _router(h, w_router, b_router, tm):
    t, d = h.shape
    n_exp = w_router.shape[1]
    tm = min(tm, t)
    assert tm % ROUTER_SUBTILE == 0 and n_exp <= LANES
    w_padded = jnp.pad(w_router, ((0, 0), (0, LANES - n_exp)))
    w_hi = w_padded.astype(BF16)
    w_lo = (w_padded - w_hi.astype(F32)).astype(BF16)
    w_padded = jnp.concatenate([w_hi, w_lo], axis=1)
    outs = pl.pallas_call(
        _router_kernel,
        grid=(t // tm,),
        in_specs=[pl.BlockSpec((tm, d), lambda i: (i, 0)),
                  pl.BlockSpec((d, 2 * LANES), lambda i: (0, 0)),
                  pl.BlockSpec((n_exp, 1), lambda i: (0, 0))],
        out_specs=[pl.BlockSpec((TOP_K, tm), lambda i: (0, i)),
                   pl.BlockSpec((TOP_K, tm), lambda i: (0, i)),
                   pl.BlockSpec((TOP_K, tm), lambda i: (0, i)),
                   pl.BlockSpec((n_exp, LANES), lambda i: (0, 0))],
        out_shape=[jax.ShapeDtypeStruct((TOP_K, t), jnp.int32),
                   jax.ShapeDtypeStruct((TOP_K, t), F32),
                   jax.ShapeDtypeStruct((TOP_K, t), jnp.int32),
                   jax.ShapeDtypeStruct((n_exp, LANES), jnp.int32)],
        scratch_shapes=[pltpu.VMEM((n_exp, 1), F32)],
        compiler_params=_cparams(("arbitrary",)),
        name="router",
    )(h, w_padded, b_router)
    return outs


SC_CORES = 2
SC_SUBCORES = 16
SC_GATHER_CHUNK = 128
MOE_SPLITS = 2


def _sc_gather_rows(table, idx):
    b = idx.shape[0]
    d = table.shape[1]
    n_workers = SC_CORES * SC_SUBCORES
    chunk = SC_GATHER_CHUNK
    assert b % (n_workers * chunk) == 0
    per_worker = b // n_workers
    n_chunks = per_worker // chunk
    mesh = plsc.VectorSubcoreMesh(core_axis_name="c", subcore_axis_name="s",
                                  num_cores=SC_CORES, num_subcores=SC_SUBCORES)

    def body(table_hbm, idx_hbm, out_hbm, idx_v, rows_v, sem):
        wid = lax.axis_index("s") * SC_CORES + lax.axis_index("c")
        base = wid * per_worker

        @pl.loop(0, n_chunks)
        def _(c):
            off = pl.multiple_of(base + c * chunk, SUBLANES)
            pltpu.sync_copy(idx_hbm.at[pl.ds(off, chunk)], idx_v)
            pltpu.async_copy(table_hbm.at[idx_v], rows_v, sem).wait()
            pltpu.sync_copy(rows_v, out_hbm.at[pl.ds(off, chunk)])

    return pl.kernel(
        body,
        out_type=jax.ShapeDtypeStruct((b, d), table.dtype),
        mesh=mesh,
        scratch_types=[pltpu.VMEM((chunk,), jnp.int32),
                       pltpu.VMEM((chunk, d), table.dtype),
                       pltpu.SemaphoreType.DMA],
        name="sc_gather_rows",
    )(table, idx)


def _sc_scatter_rows(src, dest_kt, fill_idx, n_out_rows):
    t, d = src.shape
    n_k = dest_kt.shape[0]
    n_fill = fill_idx.shape[0]
    n_workers = SC_CORES * SC_SUBCORES
    chunk = SC_GATHER_CHUNK
    assert t % (n_workers * chunk) == 0 and n_fill % (n_workers * chunk) == 0
    tok_per_worker = t // n_workers
    fill_per_worker = n_fill // n_workers
    mesh = plsc.VectorSubcoreMesh(core_axis_name="c", subcore_axis_name="s",
                                  num_cores=SC_CORES, num_subcores=SC_SUBCORES)

    def body(src_hbm, dest_hbm, fill_hbm, zeros_hbm, out_hbm, idx_v, rows_v):
        wid = lax.axis_index("s") * SC_CORES + lax.axis_index("c")

        pltpu.sync_copy(zeros_hbm, rows_v)

        @pl.loop(0, fill_per_worker // chunk)
        def _(c):
            off = pl.multiple_of(wid * fill_per_worker + c * chunk, SUBLANES)
            pltpu.sync_copy(fill_hbm.at[pl.ds(off, chunk)], idx_v)
            pltpu.sync_copy(rows_v, out_hbm.at[idx_v])

        @pl.loop(0, tok_per_worker // chunk)
        def _(c):
            off = pl.multiple_of(wid * tok_per_worker + c * chunk, SUBLANES)
            pltpu.sync_copy(src_hbm.at[pl.ds(off, chunk)], rows_v)
            for k in range(n_k):
                pltpu.sync_copy(dest_hbm.at[pl.ds(k * t + off, chunk)], idx_v)
                pltpu.sync_copy(rows_v, out_hbm.at[idx_v])

    return pl.kernel(
        body,
        out_type=jax.ShapeDtypeStruct((n_out_rows, d), src.dtype),
        mesh=mesh,
        scratch_types=[pltpu.VMEM((chunk,), jnp.int32),
                       pltpu.VMEM((chunk, d), src.dtype)],
        name="sc_scatter_rows",
    )(src, dest_kt.reshape(-1), fill_idx, jnp.zeros((chunk, d), src.dtype))


EXPERT_SUBROWS = 128


def _expert_kernel(be_ref, nxt_ref, eslot_ref, nsub_ref, nu_ref, x_ref, wgu_hbm, bgu_ref, wd_hbm,
                   bd_ref, y_ref, wgu_f, wd_f, wsem, wgu_b, wd_b):
    i = pl.program_id(0)
    n_used = nu_ref[0]
    expert = be_ref[i]
    changed = jnp.logical_or(i == 0, expert != be_ref[jnp.maximum(i - 1, 0)])

    def weight_copies(e, slot):
        return (pltpu.make_async_copy(wgu_hbm.at[e], wgu_f.at[slot], wsem.at[0, slot]),
                pltpu.make_async_copy(wd_hbm.at[e], wd_f.at[slot], wsem.at[1, slot]))

    @pl.when(jnp.logical_and(changed, i < n_used))
    def _():
        slot = eslot_ref[i]

        @pl.when(i == 0)
        def _():
            for cp in weight_copies(expert, slot):
                cp.start()

        for cp in weight_copies(expert, slot):
            cp.wait()
        nxt = nxt_ref[i]

        @pl.when(nxt >= 0)
        def _():
            for cp in weight_copies(nxt, 1 - slot):
                cp.start()

        wgu_b[...] = wgu_f[slot].astype(BF16)
        wd_b[...] = wd_f[slot].astype(BF16)

    def ffn(rows):
        d_ff = wd_b.shape[0]
        xb = _unpack_bf16_pair(x_ref[0:rows, :]).astype(BF16)
        hgu = jnp.dot(xb, wgu_b[...], preferred_element_type=F32) + bgu_ref[0]
        glu = jnp.minimum(hgu[:, :d_ff], SWIGLU_LIMIT)
        lin = jnp.clip(hgu[:, d_ff:], -SWIGLU_LIMIT, SWIGLU_LIMIT)
        act = glu * _sigmoid(SWIGLU_ALPHA * glu) * (lin + 1.0)
        y = jnp.dot(act.astype(BF16), wd_b[...], preferred_element_type=F32) + bd_ref[0]
        y_ref[0:rows, :] = _pack_bf16_pair(y)
        if rows < y_ref.shape[0]:
            y_ref[rows:, :] = jnp.zeros((y_ref.shape[0] - rows, y_ref.shape[1]), y_ref.dtype)

    n_sub = nsub_ref[i]
    for sub in range(1, y_ref.shape[0] // EXPERT_SUBROWS + 1):
        @pl.when(jnp.logical_and(i < n_used, n_sub == sub))
        def _(sub=sub):
            ffn(sub * EXPERT_SUBROWS)


def _experts(xs, block_expert, block_valid, n_used, wgu, bgu, wd, bd):
    d = wgu.shape[1]
    dp = xs.shape[1]
    assert 2 * dp == d
    n_blocks = block_expert.shape[0]
    n_exp, _, two_f = wgu.shape
    d_ff = wd.shape[1]
    rows = MOE_BLOCK
    n_rows = n_blocks * rows
    blk = jnp.arange(n_blocks, dtype=jnp.int32)
    is_first = jnp.concatenate([jnp.ones((1,), bool), block_expert[1:] != block_expert[:-1]])
    first_used = jnp.logical_and(is_first, blk < n_used[0])
    first_pos = jnp.where(first_used, blk, n_blocks)
    next_first = jnp.concatenate([lax.cummin(first_pos[::-1])[::-1][1:],
                                  jnp.full((1,), n_blocks, jnp.int32)])
    next_expert = jnp.where(next_first < n_blocks,
                            block_expert[jnp.minimum(next_first, n_blocks - 1)], -1).astype(jnp.int32)
    expert_slot = ((jnp.cumsum(is_first.astype(jnp.int32)) - 1) % 2).astype(jnp.int32)
    n_sub = jnp.clip(-(-block_valid // EXPERT_SUBROWS), 1, rows // EXPERT_SUBROWS).astype(jnp.int32)

    def last_used(i, be, nxt, es, ns, nu):
        return (jnp.minimum(i, jnp.maximum(nu[0] - 1, 0)), 0)

    grid_spec = pltpu.PrefetchScalarGridSpec(
        num_scalar_prefetch=5,
        grid=(n_blocks,),
        in_specs=[
            pl.BlockSpec((rows, dp), last_used),
            pl.BlockSpec(memory_space=pl.ANY),
            pl.BlockSpec((1, 1, two_f), lambda i, be, nxt, es, ns, nu: (be[i], 0, 0)),
            pl.BlockSpec(memory_space=pl.ANY),
            pl.BlockSpec((1, 1, d), lambda i, be, nxt, es, ns, nu: (be[i], 0, 0)),
        ],
        out_specs=pl.BlockSpec((rows, dp), last_used),
        scratch_shapes=[pltpu.VMEM((2, d, two_f), F32),
                        pltpu.VMEM((2, d_ff, d), F32),
                        pltpu.SemaphoreType.DMA((2, 2)),
                        pltpu.VMEM((d, two_f), BF16),
                        pltpu.VMEM((d_ff, d), BF16)],
    )
    return pl.pallas_call(
        _expert_kernel,
        grid_spec=grid_spec,
        out_shape=jax.ShapeDtypeStruct((n_rows, dp), jnp.uint32),
        compiler_params=_cparams(("arbitrary",)),
        name="experts",
    )(block_expert, next_expert, expert_slot, n_sub, n_used, xs, wgu, bgu.reshape(n_exp, 1, two_f), wd,
      bd.reshape(n_exp, 1, d))


def _combine_kernel(alpha, *refs):
    ys_refs = refs[:TOP_K]
    gw_ref, h_ref, lg_ref, lb_ref = refs[TOP_K:TOP_K + 4]
    o_ref = refs[-1]
    gw = gw_ref[...]
    acc = alpha * h_ref[...]
    for k in range(TOP_K):
        acc = acc + gw[:, k:k + 1] * _unpack_bf16_pair(ys_refs[k][...])
    o_ref[...] = _layer_norm(acc, lg_ref[...], lb_ref[...])


def _combine(alpha, y_slots, gw_rows, h, lg, lb, tm, out_prev, row0, total_rows):
    rows, d = h.shape
    tm = min(tm, rows)
    assert rows % tm == 0 and row0 % tm == 0
    nsteps = rows // tm
    blk0 = row0 // tm
    ys_specs = [pl.BlockSpec((tm, d // 2), functools.partial(lambda i, k: (k * nsteps + i, 0), k=k))
                for k in range(TOP_K)]
    in_specs = ys_specs + [pl.BlockSpec((tm, TOP_K), lambda i: (i, 0)),
                           pl.BlockSpec((tm, d), lambda i: (i, 0)),
                           pl.BlockSpec((1, d), lambda i: (0, 0)),
                           pl.BlockSpec((1, d), lambda i: (0, 0))]
    args = [y_slots] * TOP_K + [gw_rows, h, lg, lb]
    aliases = {}
    if out_prev is not None:
        in_specs.append(pl.BlockSpec(memory_space=pl.ANY))
        aliases = {len(args): 0}
        args.append(out_prev)
    return pl.pallas_call(
        functools.partial(_combine_kernel, alpha),
        grid=(nsteps,),
        in_specs=in_specs,
        out_specs=pl.BlockSpec((tm, d), lambda i: (i + blk0, 0)),
        out_shape=jax.ShapeDtypeStruct((total_rows, d), F32),
        input_output_aliases=aliases,
        compiler_params=_cparams(("parallel",)),
        name="combine_ln2",
    )(*args)


def _pack_in_proj(w_in, d_inner, n_ssm_heads, attn_dim):
    gn = SSM_GROUPS * D_STATE
    kv_dim = KV_HEADS * HEAD_DIM
    d = w_in.shape[0]
    o = 0
    seg = {}
    for name, width in (("z", d_inner), ("x", d_inner), ("B", gn), ("C", gn), ("dt", n_ssm_heads),
                        ("q", attn_dim), ("k", kv_dim), ("v", kv_dim), ("gs", d), ("ga", d)):
        seg[name] = w_in[:, o:o + width]
        o += width
    assert o == w_in.shape[1]
    dt_pad = jnp.pad(seg["dt"], ((0, 0), (0, LANES - n_ssm_heads)))
    order = (("z", seg["z"]), ("x", seg["x"]), ("gs", seg["gs"]), ("ga", seg["ga"]),
             ("B", seg["B"]), ("C", seg["C"]), ("dt", dt_pad))
    col, off = {}, 0
    for name, w in order:
        col[name] = off
        assert off % w.shape[1] == 0
        off += w.shape[1]
    w_f32grp = jnp.concatenate([w for _, w in order], axis=1).astype(BF16)
    w_qkv = jnp.concatenate([seg["q"], seg["k"], seg["v"]], axis=1).astype(BF16)
    return w_f32grp, w_qkv, col


def _largest_tile(n, cap):
    best = LANES
    for k in range(1, n // LANES + 1):
        if n % (k * LANES) == 0 and k * LANES <= cap:
            best = k * LANES
    return best


def _layer(h_in, batch, seq, alpha, w_in, conv_w, conv_b, dt_bias, a_log, d_skip, ssm_norm_w,
           w_ssm_out, attn_sinks, w_attn_out, b_gates, w_mix_out, ln1_g, ln1_b, w_router, b_router,
           w_gate_up, b_gate_up, w_down, b_down, ln2_g, ln2_b):
    t, d = h_in.shape
    d_inner = ssm_norm_w.shape[0]
    n_ssm_heads = dt_bias.shape[0]
    attn_dim = w_attn_out.shape[0]
    n_heads = attn_sinks.shape[0]
    n_exp = w_router.shape[1]
    gn = SSM_GROUPS * D_STATE

    w_f32grp, w_qkv, col = _pack_in_proj(w_in, d_inner, n_ssm_heads, attn_dim)
    proj = _matmul(h_in, w_f32grp, F32, IN_PROJ_ROW_TILE,
                   _largest_tile(w_f32grp.shape[1], IN_PROJ_MAX_COL_TILE))
    qkv = _matmul(h_in, w_qkv, BF16, IN_PROJ_ROW_TILE, _largest_tile(w_qkv.shape[1], QKV_MAX_COL_TILE))

    pad_h = (0, LANES - n_ssm_heads)
    y_ssm = _ssd(proj, batch, seq, col,
                 conv_w[:, :d_inner], conv_w[:, d_inner:d_inner + gn], conv_w[:, d_inner + gn:],
                 conv_b[None, :d_inner], conv_b[None, d_inner:d_inner + gn], conv_b[None, d_inner + gn:],
                 jnp.pad(dt_bias, pad_h)[None, :], jnp.pad(a_log, pad_h)[None, :],
                 jnp.repeat(d_skip, SSM_HEAD_DIM)[None, :], ssm_norm_w[None, :])
    y_att = _swa(qkv, attn_sinks, batch, seq, n_heads)

    wso, wao, wmix = w_ssm_out.astype(BF16), w_attn_out.astype(BF16), w_mix_out.astype(BF16)

    n_parts = MOE_SPLITS if t % (MOE_SPLITS * SC_CORES * SC_SUBCORES * SC_GATHER_CHUNK) == 0 else 1
    rows = t // n_parts
    out = None
    scatter_operands = ()
    for part in range(n_parts):
        row0 = part * rows
        h1, h1_packed = _merge(alpha, y_ssm, y_att, proj, col, h_in, wso, wao, wmix, b_gates[None, :d],
                               b_gates[None, d:], ln1_g[None, :], ln1_b[None, :], MERGE_ROW_TILE, row0, rows,
                               scatter_operands)
        idx_kt, gw_kt, rank_kt, counts = _router(h1, w_router, b_router[:, None], ROUTER_ROW_TILE)

        counts = counts[:, 0]
        padded = (counts + MOE_BLOCK - 1) // MOE_BLOCK * MOE_BLOCK
        pad_end = jnp.cumsum(padded)
        pad_start = pad_end - padded
        n_blocks = -(-(rows * TOP_K) // MOE_BLOCK) + n_exp
        n_rows = n_blocks * MOE_BLOCK
        expert_ids = jnp.arange(n_exp, dtype=jnp.int32)
        pad_start_of_slot = jnp.sum(
            jnp.where(idx_kt[None] == expert_ids[:, None, None], pad_start[:, None, None], 0), axis=0)
        dest_kt = pad_start_of_slot + rank_kt
        fill_rows = (pad_start + counts)[:, None] + jnp.arange(MOE_BLOCK, dtype=jnp.int32)[None, :]
        spare_rows = n_rows + jnp.arange(n_exp * MOE_BLOCK, dtype=jnp.int32).reshape(n_exp, MOE_BLOCK)
        fill_idx = jnp.where(fill_rows < pad_end[:, None], fill_rows, spare_rows).reshape(-1)
        block_row0 = jnp.arange(n_blocks, dtype=jnp.int32) * MOE_BLOCK
        block_expert = jnp.minimum(
            jnp.sum((pad_end[None, :] <= block_row0[:, None]).astype(jnp.int32), axis=1), n_exp - 1)
        n_used = (pad_end[-1:] // MOE_BLOCK).astype(jnp.int32)
        real_end_of_block = jnp.sum(jnp.where(block_expert[:, None] == expert_ids[None, :],
                                              (pad_start + counts)[None, :], 0), axis=1)
        block_valid = jnp.clip(real_end_of_block - block_row0, 0, MOE_BLOCK)
        gw_rows = gw_kt.T

        scatter_operands = (dest_kt, fill_idx)
        xs = _sc_scatter_rows(h1_packed, dest_kt, fill_idx, n_rows + n_exp * MOE_BLOCK)
        ys = _experts(xs, block_expert, block_valid, n_used, w_gate_up, b_gate_up, w_down, b_down)
        y_slots = _sc_gather_rows(ys, dest_kt.reshape(-1))
        out = _combine(alpha, y_slots, gw_rows, h1, ln2_g[None, :], ln2_b[None, :], COMBINE_ROW_TILE,
                       out, row0, t)
    return out


def kernel(x, w_in, conv_w, conv_b, dt_bias, a_log, d_skip, ssm_norm_w, w_ssm_out, attn_sinks,
           w_attn_out, b_gates, w_mix_out, ln1_g, ln1_b, w_router, b_router, w_gate_up, b_gate_up,
           w_down, b_down, ln2_g, ln2_b):
    batch, seq, d = x.shape
    depth = w_in.shape[0]
    alpha = (2 * depth) ** 0.25
    h = x.reshape(batch * seq, d)
    for i in range(depth):
        h = _layer(h, batch, seq, alpha, w_in[i], conv_w[i], conv_b[i], dt_bias[i], a_log[i],
                   d_skip[i], ssm_norm_w[i], w_ssm_out[i], attn_sinks[i], w_attn_out[i], b_gates[i],
                   w_mix_out[i], ln1_g[i], ln1_b[i], w_router[i], b_router[i], w_gate_up[i],
                   b_gate_up[i], w_down[i], b_down[i], ln2_g[i], ln2_b[i])
    return h.reshape(batch, seq, d)
```

```python
import functools

import jax
import jax.numpy as jnp
from jax import lax
from jax.experimental import pallas as pl
from jax.experimental.pallas import tpu as pltpu
from jax.experimental.pallas import tpu_sc as plsc

SSM_HEAD_DIM = 64
SSM_GROUPS = 4
D_STATE = 128
CONV_WIDTH = 4
SSD_CHUNK = 128
SEQS_PER_STEP = 2
KV_HEADS = 4
HEAD_DIM = 64
WINDOW = 128
TOP_K = 4
SWIGLU_LIMIT = 7.0
SWIGLU_ALPHA = 1.702
MOE_BLOCK = 1024
LN_EPS = 1e-5
RMS_EPS = 1e-5

LANES = 128
SUBLANES = 8
NEG_BIG = -1e30
LOG2_E = 1.4426950408889634
F32 = jnp.float32
BF16 = jnp.bfloat16
VMEM_LIMIT = 56 * 1024 * 1024

IN_PROJ_ROW_TILE = 1024
IN_PROJ_MAX_COL_TILE = 2560
QKV_MAX_COL_TILE = 1536
MERGE_ROW_TILE = 512
ROUTER_ROW_TILE = 1024
COMBINE_ROW_TILE = 1024


def _cparams(sem):
    return pltpu.CompilerParams(dimension_semantics=sem, vmem_limit_bytes=VMEM_LIMIT)


def _sigmoid(x):
    return 0.5 + 0.5 * jnp.tanh(0.5 * x)


def _pack_bf16_pair(v):
    w = v.shape[1] // 2
    bits = lax.bitcast_convert_type(v.astype(BF16).astype(F32), jnp.uint32)
    return (bits[:, :w] >> 16) | (bits[:, w:] & jnp.uint32(0xFFFF0000))


def _unpack_bf16_pair(p):
    lo = lax.bitcast_convert_type(p << 16, F32)
    hi = lax.bitcast_convert_type(p & jnp.uint32(0xFFFF0000), F32)
    return jnp.concatenate([lo, hi], axis=1)


def _silu(x):
    h = 0.5 * x
    return h + h * jnp.tanh(h)


def _mm_kernel(x_ref, w_ref, o_ref):
    o_ref[...] = jnp.dot(x_ref[...].astype(BF16), w_ref[...],
                         preferred_element_type=F32).astype(o_ref.dtype)


def _matmul(x, w, out_dtype, tm, tn):
    m, k = x.shape
    n = w.shape[1]
    tm = min(tm, m)
    assert m % tm == 0 and n % tn == 0
    return pl.pallas_call(
        _mm_kernel,
        grid=(n // tn, m // tm),
        in_specs=[pl.BlockSpec((tm, k), lambda j, i: (i, 0)),
                  pl.BlockSpec((k, tn), lambda j, i: (0, j))],
        out_specs=pl.BlockSpec((tm, tn), lambda j, i: (i, j)),
        out_shape=jax.ShapeDtypeStruct((m, n), out_dtype),
        compiler_params=_cparams(("parallel", "parallel")),
        name="in_proj",
    )(x, w)


def _conv_silu(u_ref, prev_ref, w_ref, b_ref, first):
    cur = u_ref[...]
    prev = prev_ref[...]
    prev = jnp.where(first, jnp.zeros_like(prev), prev)
    row8 = lax.broadcasted_iota(jnp.int32, prev.shape, 0)
    acc = b_ref[...] + w_ref[CONV_WIDTH - 1:CONV_WIDTH, :] * cur
    for k in range(1, CONV_WIDTH):
        rolled = pltpu.roll(cur, k, axis=0)
        head = jnp.where(row8 < k, pltpu.roll(prev, k, axis=0), rolled[:SUBLANES])
        shifted = jnp.concatenate([head, rolled[SUBLANES:]], axis=0)
        j = CONV_WIDTH - 1 - k
        acc = acc + w_ref[j:j + 1, :] * shifted
    return _silu(acc)


def _expand_columns(v, e3_ref):
    terms, rest = [], v
    for _ in range(e3_ref.shape[0] // LANES):
        t = rest.astype(BF16)
        terms.append(t)
        rest = rest - t.astype(F32)
    return jnp.dot(jnp.concatenate(terms, axis=1), e3_ref[...], preferred_element_type=F32)


def _ssd_kernel(z_ref, x_ref, b_ref, c_ref, dt_ref, xp_ref, bp_ref, cp_ref,
                cwx_ref, cwb_ref, cwc_ref, cbx_ref, cbb_ref, cbc_ref,
                dtb_ref, alog_ref, dsk_ref, nw_ref, ehead_ref, ecol_ref,
                o_ref, st_ref):
    first = pl.program_id(1) == 0

    @pl.when(first)
    def _():
        st_ref[...] = jnp.zeros_like(st_ref)

    for s in range(z_ref.shape[0]):
        _ssd_chunk(z_ref.at[s], x_ref.at[s], b_ref.at[s], c_ref.at[s], dt_ref.at[s],
                   xp_ref.at[s], bp_ref.at[s], cp_ref.at[s],
                   cwx_ref, cwb_ref, cwc_ref, cbx_ref, cbb_ref, cbc_ref,
                   dtb_ref, alog_ref, dsk_ref, nw_ref, ehead_ref, ecol_ref,
                   o_ref.at[s], st_ref.at[s], first)


def _ssd_chunk(z_ref, x_ref, b_ref, c_ref, dt_ref, xp_ref, bp_ref, cp_ref,
               cwx_ref, cwb_ref, cwc_ref, cbx_ref, cbb_ref, cbc_ref,
               dtb_ref, alog_ref, dsk_ref, nw_ref, ehead_ref, ecol_ref, o_ref, st_ref, first):
    n = SSD_CHUNK
    heads_per_group = st_ref.shape[2] // SSM_HEAD_DIM
    gw = heads_per_group * SSM_HEAD_DIM

    xc = _conv_silu(x_ref, xp_ref, cwx_ref, cbx_ref, first)
    bc = _conv_silu(b_ref, bp_ref, cwb_ref, cbb_ref, first)
    cc = _conv_silu(c_ref, cp_ref, cwc_ref, cbc_ref, first)

    dt_in = dt_ref[...] + dtb_ref[...]
    dt = jnp.maximum(dt_in, 0.0) + jnp.log(1.0 + jnp.exp(-jnp.abs(dt_in)))
    a = dt * (-jnp.exp(alog_ref[...]))
    row = lax.broadcasted_iota(jnp.int32, (n, n), 0)
    col = lax.broadcasted_iota(jnp.int32, (n, n), 1)
    causal = row >= col
    tri = jnp.where(causal, 1.0, 0.0).astype(F32)
    acum = jnp.dot(tri, a, preferred_element_type=F32, precision=lax.Precision.HIGHEST)
    acum2 = acum * LOG2_E
    acum2_t = acum2.T
    a_last = acum[n - 1:n, :]
    lo_mask = lax.broadcasted_iota(jnp.int32, (n, LANES), 1) < SSM_HEAD_DIM
    dt_x = _expand_columns(dt, ehead_ref)
    d2e_x = _expand_columns(jnp.exp(a_last - acum), ehead_ref)
    eac_x = _expand_columns(jnp.exp(acum), ehead_ref)
    acum2_x = _expand_columns(acum2, ecol_ref)

    for g in range(SSM_GROUPS):
        xg = xc[:, g * gw:(g + 1) * gw]
        bg = bc[:, g * D_STATE:(g + 1) * D_STATE]
        cg = cc[:, g * D_STATE:(g + 1) * D_STATE]
        cg_b = cg.astype(BF16)
        bg_t = bg.T.astype(BF16)
        cb = jnp.dot(cg_b, bg_t, preferred_element_type=F32) * tri
        st_prev = st_ref[g]
        y_off = jnp.dot(cg_b, st_prev.astype(BF16), preferred_element_type=F32)
        y_parts, xw_parts, ea_parts = [], [], []
        for j in range(heads_per_group // 2):
            h0 = g * heads_per_group + 2 * j
            lanes = slice(j * LANES, (j + 1) * LANES)
            glanes = slice(g * gw + j * LANES, g * gw + (j + 1) * LANES)
            xs_pair = xg[:, lanes] * dt_x[:, glanes]
            xs_b = xs_pair.astype(BF16)
            xw_parts.append((xs_pair * d2e_x[:, glanes]).astype(BF16))
            ea_parts.append(eac_x[:, glanes])
            ys = []
            for h in (h0, h0 + 1):
                seg2 = acum2_x[:, h * n:(h + 1) * n] - acum2_t[h:h + 1, :]
                decay = jnp.exp2(jnp.minimum(seg2, 0.0))
                m = (cb * decay).astype(BF16)
                ys.append(jnp.dot(m, xs_b, preferred_element_type=F32))
            y_parts.append(jnp.where(lo_mask, ys[0], ys[1]))
        y_diag = jnp.concatenate(y_parts, axis=1)
        ea = jnp.concatenate(ea_parts, axis=1)
        xw = jnp.concatenate(xw_parts, axis=1)
        cols = slice(g * gw, (g + 1) * gw)
        y = y_diag + y_off * ea + xg * dsk_ref[:, cols]
        zg = z_ref[:, cols]
        y = y * _silu(zg)
        ms = jnp.mean(y * y, axis=-1, keepdims=True)
        o_ref[:, cols] = (y * lax.rsqrt(ms + RMS_EPS) * nw_ref[:, cols]).astype(o_ref.dtype)
        st_ref[g] = st_prev * ea[n - 1:n, :] + jnp.dot(bg_t, xw, preferred_element_type=F32)


def _ssd(proj, batch, seq, col, conv_wx, conv_wb, conv_wc, conv_bx, conv_bb, conv_bc,
         dt_bias, a_log, d_skip, norm_w):
    t, width = proj.shape
    d_inner = norm_w.shape[1]
    gn = SSM_GROUPS * D_STATE
    n = SSD_CHUNK
    nc = seq // n
    gw = d_inner // SSM_GROUPS
    ns = SEQS_PER_STEP if batch % SEQS_PER_STEP == 0 else 1
    proj3 = proj.reshape(batch, seq, width)
    n_heads = d_inner // SSM_HEAD_DIM
    head_id = jnp.arange(LANES, dtype=jnp.int32)[:, None]
    e_head = (head_id == (jnp.arange(d_inner, dtype=jnp.int32) // SSM_HEAD_DIM)[None, :]).astype(BF16)
    e_col = (head_id == (jnp.arange(n_heads * n, dtype=jnp.int32) // n)[None, :]).astype(BF16)
    e_head3 = jnp.tile(e_head, (2, 1))
    e_col3 = jnp.tile(e_col, (3, 1))

    def prev_rows(c):
        return jnp.maximum(c * (n // SUBLANES) - 1, 0)

    def const(b, c):
        return (0, 0)

    in_specs = [
        pl.BlockSpec((ns, n, d_inner), lambda b, c: (b, c, col["z"] // d_inner)),
        pl.BlockSpec((ns, n, d_inner), lambda b, c: (b, c, col["x"] // d_inner)),
        pl.BlockSpec((ns, n, gn), lambda b, c: (b, c, col["B"] // gn)),
        pl.BlockSpec((ns, n, gn), lambda b, c: (b, c, col["C"] // gn)),
        pl.BlockSpec((ns, n, LANES), lambda b, c: (b, c, col["dt"] // LANES)),
        pl.BlockSpec((ns, SUBLANES, d_inner), lambda b, c: (b, prev_rows(c), col["x"] // d_inner)),
        pl.BlockSpec((ns, SUBLANES, gn), lambda b, c: (b, prev_rows(c), col["B"] // gn)),
        pl.BlockSpec((ns, SUBLANES, gn), lambda b, c: (b, prev_rows(c), col["C"] // gn)),
        pl.BlockSpec((CONV_WIDTH, d_inner), const),
        pl.BlockSpec((CONV_WIDTH, gn), const),
        pl.BlockSpec((CONV_WIDTH, gn), const),
        pl.BlockSpec((1, d_inner), const),
        pl.BlockSpec((1, gn), const),
        pl.BlockSpec((1, gn), const),
        pl.BlockSpec((1, LANES), const),
        pl.BlockSpec((1, LANES), const),
        pl.BlockSpec((1, d_inner), const),
        pl.BlockSpec((1, d_inner), const),
        pl.BlockSpec(e_head3.shape, const),
        pl.BlockSpec(e_col3.shape, const),
    ]
    out = pl.pallas_call(
        _ssd_kernel,
        grid=(batch // ns, nc),
        in_specs=in_specs,
        out_specs=pl.BlockSpec((ns, n, d_inner), lambda b, c: (b, c, 0)),
        out_shape=jax.ShapeDtypeStruct((batch, seq, d_inner), BF16),
        scratch_shapes=[pltpu.VMEM((ns, SSM_GROUPS, D_STATE, gw), F32)],
        compiler_params=_cparams(("arbitrary", "arbitrary")),
        name="ssd_mixer",
    )(proj3, proj3, proj3, proj3, proj3, proj3, proj3, proj3,
      conv_wx, conv_wb, conv_wc, conv_bx, conv_bb, conv_bc, dt_bias, a_log, d_skip, norm_w,
      e_head3, e_col3)
    return out.reshape(t, d_inner)


def _swa_kernel(sink_ref, q_ref, kc_ref, vc_ref, kp_ref, vp_ref, o_ref):
    for s in range(q_ref.shape[0]):
        _swa_block(sink_ref, q_ref.at[s], kc_ref.at[s], vc_ref.at[s], kp_ref.at[s], vp_ref.at[s],
                   o_ref.at[s])


def _swa_block(sink_ref, q_ref, kc_ref, vc_ref, kp_ref, vp_ref, o_ref):
    i = pl.program_id(1)
    n = WINDOW
    n_heads = q_ref.shape[1] // HEAD_DIM
    grp = n_heads // KV_HEADS
    qpos = lax.broadcasted_iota(jnp.int32, (n, 2 * n), 0) + n
    kpos = lax.broadcasted_iota(jnp.int32, (n, 2 * n), 1)
    diff = qpos - kpos
    mask = (diff >= 0) & (diff < n) & ((kpos >= n) | (i > 0))
    lo_q = lax.broadcasted_iota(jnp.int32, (n, LANES), 1) < HEAD_DIM
    lo_k = lax.broadcasted_iota(jnp.int32, (2 * n, LANES), 1) < HEAD_DIM
    scale = HEAD_DIM ** -0.5
    exp2_scale = scale * LOG2_E
    for p in range(KV_HEADS // 2):
        lanes = slice(p * LANES, (p + 1) * LANES)
        kk = jnp.concatenate([kp_ref[:, lanes], kc_ref[:, lanes]], axis=0).astype(F32)
        vv = jnp.concatenate([vp_ref[:, lanes], vc_ref[:, lanes]], axis=0).astype(F32)
        kk_sw = pltpu.roll(kk, HEAD_DIM, axis=1)
        vv_sw = pltpu.roll(vv, HEAD_DIM, axis=1)
        for par in range(2):
            g = 2 * p + par
            k2 = (jnp.where(lo_k, kk, kk_sw) if par == 0 else jnp.where(lo_k, kk_sw, kk)).astype(BF16)
            v2 = (jnp.where(lo_k, vv, vv_sw) if par == 0 else jnp.where(lo_k, vv_sw, vv)).astype(BF16)
            q_rows = []
            for hq in range(grp):
                h = g * grp + hq
                q_pair = q_ref[:, (h // 2) * LANES:(h // 2 + 1) * LANES]
                keep = lo_q if h % 2 == 0 else jnp.logical_not(lo_q)
                q_rows.append(jnp.where(keep, q_pair, jnp.zeros_like(q_pair)))
            s_all = lax.dot_general(jnp.concatenate(q_rows, axis=0), k2, (((1,), (1,)), ((), ())),
                                    preferred_element_type=F32)
            probs, dens = [], []
            for hq in range(grp):
                s = jnp.where(mask, s_all[hq * n:(hq + 1) * n, :], NEG_BIG)
                sink = sink_ref[g * grp + hq] / scale
                mx = jnp.maximum(jnp.max(s, axis=-1, keepdims=True), sink)
                pr = jnp.exp2((s - mx) * exp2_scale)
                dens.append(jnp.sum(pr, axis=-1, keepdims=True) + jnp.exp2((sink - mx) * exp2_scale))
                probs.append(pr.astype(BF16))
            o_all = jnp.dot(jnp.concatenate(probs, axis=0), v2, preferred_element_type=F32)
            for qp in range(grp // 2):
                outs = [o_all[(2 * qp + hh) * n:(2 * qp + hh + 1) * n, :] / dens[2 * qp + hh] for hh in range(2)]
                pair = (g * grp) // 2 + qp
                o_ref[:, pair * LANES:(pair + 1) * LANES] = jnp.where(lo_q, outs[0], outs[1]).astype(o_ref.dtype)


def _swa(qkv, sinks, batch, seq, n_heads):
    t, width = qkv.shape
    n = WINDOW
    nb = seq // n
    qw = n_heads * HEAD_DIM
    kw = KV_HEADS * HEAD_DIM
    k_blk = qw // kw
    v_blk = k_blk + 1
    ns = SEQS_PER_STEP if batch % SEQS_PER_STEP == 0 else 1
    qkv3 = qkv.reshape(batch, seq, width)

    def prev(i):
        return jnp.maximum(i - 1, 0)

    out = pl.pallas_call(
        _swa_kernel,
        grid=(batch // ns, nb),
        in_specs=[pl.BlockSpec(memory_space=pltpu.SMEM),
                  pl.BlockSpec((ns, n, qw), lambda b, i: (b, i, 0)),
                  pl.BlockSpec((ns, n, kw), lambda b, i: (b, i, k_blk)),
                  pl.BlockSpec((ns, n, kw), lambda b, i: (b, i, v_blk)),
                  pl.BlockSpec((ns, n, kw), lambda b, i: (b, prev(i), k_blk)),
                  pl.BlockSpec((ns, n, kw), lambda b, i: (b, prev(i), v_blk))],
        out_specs=pl.BlockSpec((ns, n, qw), lambda b, i: (b, i, 0)),
        out_shape=jax.ShapeDtypeStruct((batch, seq, qw), BF16),
        compiler_params=_cparams(("parallel", "parallel")),
        name="swa",
    )(sinks, qkv3, qkv3, qkv3, qkv3, qkv3)
    return out.reshape(t, qw)


def _layer_norm(v, g, b):
    mu = jnp.mean(v, axis=-1, keepdims=True)
    d = v - mu
    var = jnp.mean(d * d, axis=-1, keepdims=True)
    return d * lax.rsqrt(var + LN_EPS) * g + b


def _merge_kernel(alpha, ys_ref, ya_ref, gs_ref, ga_ref, x_ref, wso_ref, wao_ref, wmix_ref,
                  bgs_ref, bga_ref, lg_ref, lb_ref, *rest):
    h_ref, hp_ref = rest[-2:]
    y_ssm = jnp.dot(ys_ref[...], wso_ref[...], preferred_element_type=F32)
    y_att = jnp.dot(ya_ref[...], wao_ref[...], preferred_element_type=F32)
    merged = (_sigmoid(gs_ref[...] + bgs_ref[...]) * y_ssm
              + _sigmoid(ga_ref[...] + bga_ref[...]) * y_att)
    mix = jnp.dot(merged.astype(BF16), wmix_ref[...], preferred_element_type=F32)
    h = _layer_norm(alpha * x_ref[...] + mix, lg_ref[...], lb_ref[...])
    h_ref[...] = h
    hp_ref[...] = _pack_bf16_pair(h)


def _merge(alpha, y_ssm, y_att, proj, col, x, wso, wao, wmix, bgs, bga, lg, lb, tm, row0, rows, after):
    d = x.shape[1]
    tm = min(tm, rows)
    assert rows % tm == 0 and row0 % tm == 0
    blk0 = row0 // tm
    di = y_ssm.shape[1]
    da = y_att.shape[1]

    def const(i):
        return (0, 0)

    ordering = list(after)
    return pl.pallas_call(
        functools.partial(_merge_kernel, alpha),
        grid=(rows // tm,),
        in_specs=[pl.BlockSpec((tm, di), lambda i: (i + blk0, 0)),
                  pl.BlockSpec((tm, da), lambda i: (i + blk0, 0)),
                  pl.BlockSpec((tm, d), lambda i: (i + blk0, col["gs"] // d)),
                  pl.BlockSpec((tm, d), lambda i: (i + blk0, col["ga"] // d)),
                  pl.BlockSpec((tm, d), lambda i: (i + blk0, 0)),
                  pl.BlockSpec((di, d), const),
                  pl.BlockSpec((da, d), const),
                  pl.BlockSpec((d, d), const),
                  pl.BlockSpec((1, d), const),
                  pl.BlockSpec((1, d), const),
                  pl.BlockSpec((1, d), const),
                  pl.BlockSpec((1, d), const)] + [pl.BlockSpec(memory_space=pl.ANY)] * len(ordering),
        out_specs=[pl.BlockSpec((tm, d), lambda i: (i, 0)),
                   pl.BlockSpec((tm, d // 2), lambda i: (i, 0))],
        out_shape=[jax.ShapeDtypeStruct((rows, d), F32),
                   jax.ShapeDtypeStruct((rows, d // 2), jnp.uint32)],
        compiler_params=_cparams(("parallel",)),
        name="merge_ln1",
    )(y_ssm, y_att, proj, proj, x, wso, wao, wmix, bgs, bga, lg, lb, *ordering)


ROUTER_SUBTILE = 128


def _router_kernel(h_ref, wr_ref, br_ref, idx_ref, gw_ref, rank_ref, cnt_ref, base_ref):
    i = pl.program_id(0)
    tm = h_ref.shape[0]
    n_exp = br_ref.shape[0]
    sub = ROUTER_SUBTILE

    @pl.when(i == 0)
    def _():
        base_ref[...] = jnp.zeros_like(base_ref)

    h = h_ref[...]
    h1 = h.astype(BF16)
    h2 = (h - h1.astype(F32)).astype(BF16)
    first = jnp.dot(h1, wr_ref[...], preferred_element_type=F32)
    logits_t = (first[:, :LANES] + first[:, LANES:]
                + jnp.dot(h2, wr_ref[:, :LANES], preferred_element_type=F32))
    logits = logits_t.T[:n_exp, :] + br_ref[...]
    eid = lax.broadcasted_iota(jnp.int32, (n_exp, tm), 0)
    work = logits
    vals, idxs = [], []
    for _ in range(TOP_K):
        mx = jnp.max(work, axis=0, keepdims=True)
        sel = jnp.min(jnp.where(work == mx, eid, n_exp), axis=0, keepdims=True)
        vals.append(mx)
        idxs.append(sel)
        work = jnp.where(eid == sel, -jnp.inf, work)
    exps = [jnp.exp(v - vals[0]) for v in vals]
    den = exps[0]
    for e in exps[1:]:
        den = den + e
    r = lax.broadcasted_iota(jnp.int32, (sub, sub), 0)
    c = lax.broadcasted_iota(jnp.int32, (sub, sub), 1)
    tri = jnp.where(r < c, 1.0, 0.0).astype(BF16)
    base = base_ref[...]
    for k in range(TOP_K):
        onehot = eid == idxs[k]
        oh = jnp.where(onehot, 1.0, 0.0)
        ranks = []
        for s in range(tm // sub):
            lanes = slice(s * sub, (s + 1) * sub)
            before = jnp.dot(oh[:, lanes].astype(BF16), tri, preferred_element_type=F32)
            ranks.append(jnp.sum(jnp.where(onehot[:, lanes], before + base, 0.0), axis=0, keepdims=True))
            base = base + jnp.sum(oh[:, lanes], axis=1, keepdims=True)
        idx_ref[k:k + 1, :] = idxs[k]
        gw_ref[k:k + 1, :] = exps[k] / den
        rank_ref[k:k + 1, :] = jnp.concatenate(ranks, axis=1).astype(jnp.int32)
    base_ref[...] = base
    cnt_ref[...] = jnp.broadcast_to(base, cnt_ref.shape).astype(jnp.int32)


def _router(h, w_router, b_router, tm):
    t, d = h.shape
    n_exp = w_router.shape[1]
    tm = min(tm, t)
    assert tm % ROUTER_SUBTILE == 0 and n_exp <= LANES
    w_padded = jnp.pad(w_router, ((0, 0), (0, LANES - n_exp)))
    w_hi = w_padded.astype(BF16)
    w_lo = (w_padded - w_hi.astype(F32)).astype(BF16)
    w_padded = jnp.concatenate([w_hi, w_lo], axis=1)
    outs = pl.pallas_call(
        _router_kernel,
        grid=(t // tm,),
        in_specs=[pl.BlockSpec((tm, d), lambda i: (i, 0)),
                  pl.BlockSpec((d, 2 * LANES), lambda i: (0, 0)),
                  pl.BlockSpec((n_exp, 1), lambda i: (0, 0))],
        out_specs=[pl.BlockSpec((TOP_K, tm), lambda i: (0, i)),
                   pl.BlockSpec((TOP_K, tm), lambda i: (0, i)),
                   pl.BlockSpec((TOP_K, tm), lambda i: (0, i)),
                   pl.BlockSpec((n_exp, LANES), lambda i: (0, 0))],
        out_shape=[jax.ShapeDtypeStruct((TOP_K, t), jnp.int32),
                   jax.ShapeDtypeStruct((TOP_K, t), F32),
                   jax.ShapeDtypeStruct((TOP_K, t), jnp.int32),
                   jax.ShapeDtypeStruct((n_exp, LANES), jnp.int32)],
        scratch_shapes=[pltpu.VMEM((n_exp, 1), F32)],
        compiler_params=_cparams(("arbitrary",)),
        name="router",
    )(h, w_padded, b_router)
    return outs


SC_CORES = 2
SC_SUBCORES = 16
SC_GATHER_CHUNK = 128
MOE_SPLITS = 2


def _sc_gather_rows(table, idx):
    b = idx.shape[0]
    d = table.shape[1]
    n_workers = SC_CORES * SC_SUBCORES
    chunk = SC_GATHER_CHUNK
    assert b % (n_workers * chunk) == 0
    per_worker = b // n_workers
    n_chunks = per_worker // chunk
    mesh = plsc.VectorSubcoreMesh(core_axis_name="c", subcore_axis_name="s",
                                  num_cores=SC_CORES, num_subcores=SC_SUBCORES)

    def body(table_hbm, idx_hbm, out_hbm, idx_v, rows_v, sem):
        wid = lax.axis_index("s") * SC_CORES + lax.axis_index("c")
        base = wid * per_worker

        @pl.loop(0, n_chunks)
        def _(c):
            off = pl.multiple_of(base + c * chunk, SUBLANES)
            pltpu.sync_copy(idx_hbm.at[pl.ds(off, chunk)], idx_v)
            pltpu.async_copy(table_hbm.at[idx_v], rows_v, sem).wait()
            pltpu.sync_copy(rows_v, out_hbm.at[pl.ds(off, chunk)])

    return pl.kernel(
        body,
        out_type=jax.ShapeDtypeStruct((b, d), table.dtype),
        mesh=mesh,
        scratch_types=[pltpu.VMEM((chunk,), jnp.int32),
                       pltpu.VMEM((chunk, d), table.dtype),
                       pltpu.SemaphoreType.DMA],
        name="sc_gather_rows",
    )(table, idx)


def _sc_scatter_rows(src, dest_kt, fill_idx, n_out_rows):
    t, d = src.shape
    n_k = dest_kt.shape[0]
    n_fill = fill_idx.shape[0]
    n_workers = SC_CORES * SC_SUBCORES
    chunk = SC_GATHER_CHUNK
    assert t % (n_workers * chunk) == 0 and n_fill % (n_workers * chunk) == 0
    tok_per_worker = t // n_workers
    fill_per_worker = n_fill // n_workers
    mesh = plsc.VectorSubcoreMesh(core_axis_name="c", subcore_axis_name="s",
                                  num_cores=SC_CORES, num_subcores=SC_SUBCORES)

    def body(src_hbm, dest_hbm, fill_hbm, zeros_hbm, out_hbm, idx_v, rows_v):
        wid = lax.axis_index("s") * SC_CORES + lax.axis_index("c")

        pltpu.sync_copy(zeros_hbm, rows_v)

        @pl.loop(0, fill_per_worker // chunk)
        def _(c):
            off = pl.multiple_of(wid * fill_per_worker + c * chunk, SUBLANES)
            pltpu.sync_copy(fill_hbm.at[pl.ds(off, chunk)], idx_v)
            pltpu.sync_copy(rows_v, out_hbm.at[idx_v])

        @pl.loop(0, tok_per_worker // chunk)
        def _(c):
            off = pl.multiple_of(wid * tok_per_worker + c * chunk, SUBLANES)
            pltpu.sync_copy(src_hbm.at[pl.ds(off, chunk)], rows_v)
            for k in range(n_k):
                pltpu.sync_copy(dest_hbm.at[pl.ds(k * t + off, chunk)], idx_v)
                pltpu.sync_copy(rows_v, out_hbm.at[idx_v])

    return pl.kernel(
        body,
        out_type=jax.ShapeDtypeStruct((n_out_rows, d), src.dtype),
        mesh=mesh,
        scratch_types=[pltpu.VMEM((chunk,), jnp.int32),
                       pltpu.VMEM((chunk, d), src.dtype)],
        name="sc_scatter_rows",
    )(src, dest_kt.reshape(-1), fill_idx, jnp.zeros((chunk, d), src.dtype))


EXPERT_SUBROWS = 128


def _expert_kernel(be_ref, nxt_ref, eslot_ref, nsub_ref, nu_ref, x_ref, wgu_hbm, bgu_ref, wd_hbm,
                   bd_ref, y_ref, wgu_f, wd_f, wsem, wgu_b, wd_b):
    i = pl.program_id(0)
    n_used = nu_ref[0]
    expert = be_ref[i]
    changed = jnp.logical_or(i == 0, expert != be_ref[jnp.maximum(i - 1, 0)])

    def weight_copies(e, slot):
        return (pltpu.make_async_copy(wgu_hbm.at[e], wgu_f.at[slot], wsem.at[0, slot]),
                pltpu.make_async_copy(wd_hbm.at[e], wd_f.at[slot], wsem.at[1, slot]))

    @pl.when(jnp.logical_and(changed, i < n_used))
    def _():
        slot = eslot_ref[i]

        @pl.when(i == 0)
        def _():
            for cp in weight_copies(expert, slot):
                cp.start()

        for cp in weight_copies(expert, slot):
            cp.wait()
        nxt = nxt_ref[i]

        @pl.when(nxt >= 0)
        def _():
            for cp in weight_copies(nxt, 1 - slot):
                cp.start()

        wgu_b[...] = wgu_f[slot].astype(BF16)
        wd_b[...] = wd_f[slot].astype(BF16)

    def ffn(rows):
        d_ff = wd_b.shape[0]
        xb = _unpack_bf16_pair(x_ref[0:rows, :]).astype(BF16)
        hgu = jnp.dot(xb, wgu_b[...], preferred_element_type=F32) + bgu_ref[0]
        glu = jnp.minimum(hgu[:, :d_ff], SWIGLU_LIMIT)
        lin = jnp.clip(hgu[:, d_ff:], -SWIGLU_LIMIT, SWIGLU_LIMIT)
        act = glu * _sigmoid(SWIGLU_ALPHA * glu) * (lin + 1.0)
        y = jnp.dot(act.astype(BF16), wd_b[...], preferred_element_type=F32) + bd_ref[0]
        y_ref[0:rows, :] = _pack_bf16_pair(y)
        if rows < y_ref.shape[0]:
            y_ref[rows:, :] = jnp.zeros((y_ref.shape[0] - rows, y_ref.shape[1]), y_ref.dtype)

    n_sub = nsub_ref[i]
    for sub in range(1, y_ref.shape[0] // EXPERT_SUBROWS + 1):
        @pl.when(jnp.logical_and(i < n_used, n_sub == sub))
        def _(sub=sub):
            ffn(sub * EXPERT_SUBROWS)


def _experts(xs, block_expert, block_valid, n_used, wgu, bgu, wd, bd):
    d = wgu.shape[1]
    dp = xs.shape[1]
    assert 2 * dp == d
    n_blocks = block_expert.shape[0]
    n_exp, _, two_f = wgu.shape
    d_ff = wd.shape[1]
    rows = MOE_BLOCK
    n_rows = n_blocks * rows
    blk = jnp.arange(n_blocks, dtype=jnp.int32)
    is_first = jnp.concatenate([jnp.ones((1,), bool), block_expert[1:] != block_expert[:-1]])
    first_used = jnp.logical_and(is_first, blk < n_used[0])
    first_pos = jnp.where(first_used, blk, n_blocks)
    next_first = jnp.concatenate([lax.cummin(first_pos[::-1])[::-1][1:],
                                  jnp.full((1,), n_blocks, jnp.int32)])
    next_expert = jnp.where(next_first < n_blocks,
                            block_expert[jnp.minimum(next_first, n_blocks - 1)], -1).astype(jnp.int32)
    expert_slot = ((jnp.cumsum(is_first.astype(jnp.int32)) - 1) % 2).astype(jnp.int32)
    n_sub = jnp.clip(-(-block_valid // EXPERT_SUBROWS), 1, rows // EXPERT_SUBROWS).astype(jnp.int32)

    def last_used(i, be, nxt, es, ns, nu):
        return (jnp.minimum(i, jnp.maximum(nu[0] - 1, 0)), 0)

    grid_spec = pltpu.PrefetchScalarGridSpec(
        num_scalar_prefetch=5,
        grid=(n_blocks,),
        in_specs=[
            pl.BlockSpec((rows, dp), last_used),
            pl.BlockSpec(memory_space=pl.ANY),
            pl.BlockSpec((1, 1, two_f), lambda i, be, nxt, es, ns, nu: (be[i], 0, 0)),
            pl.BlockSpec(memory_space=pl.ANY),
            pl.BlockSpec((1, 1, d), lambda i, be, nxt, es, ns, nu: (be[i], 0, 0)),
        ],
        out_specs=pl.BlockSpec((rows, dp), last_used),
        scratch_shapes=[pltpu.VMEM((2, d, two_f), F32),
                        pltpu.VMEM((2, d_ff, d), F32),
                        pltpu.SemaphoreType.DMA((2, 2)),
                        pltpu.VMEM((d, two_f), BF16),
                        pltpu.VMEM((d_ff, d), BF16)],
    )
    return pl.pallas_call(
        _expert_kernel,
        grid_spec=grid_spec,
        out_shape=jax.ShapeDtypeStruct((n_rows, dp), jnp.uint32),
        compiler_params=_cparams(("arbitrary",)),
        name="experts",
    )(block_expert, next_expert, expert_slot, n_sub, n_used, xs, wgu, bgu.reshape(n_exp, 1, two_f), wd,
      bd.reshape(n_exp, 1, d))


def _combine_kernel(alpha, *refs):
    ys_refs = refs[:TOP_K]
    gw_ref, h_ref, lg_ref, lb_ref = refs[TOP_K:TOP_K + 4]
    o_ref = refs[-1]
    gw = gw_ref[...]
    acc = alpha * h_ref[...]
    for k in range(TOP_K):
        acc = acc + gw[:, k:k + 1] * _unpack_bf16_pair(ys_refs[k][...])
    o_ref[...] = _layer_norm(acc, lg_ref[...], lb_ref[...])


def _combine(alpha, y_slots, gw_rows, h, lg, lb, tm, out_prev, row0, total_rows):
    rows, d = h.shape
    tm = min(tm, rows)
    assert rows % tm == 0 and row0 % tm == 0
    nsteps = rows // tm
    blk0 = row0 // tm
    ys_specs = [pl.BlockSpec((tm, d // 2), functools.partial(lambda i, k: (k * nsteps + i, 0), k=k))
                for k in range(TOP_K)]
    in_specs = ys_specs + [pl.BlockSpec((tm, LANES), lambda i: (i, 0)),
                           pl.BlockSpec((tm, d), lambda i: (i, 0)),
                           pl.BlockSpec((1, d), lambda i: (0, 0)),
                           pl.BlockSpec((1, d), lambda i: (0, 0))]
    args = [y_slots] * TOP_K + [gw_rows, h, lg, lb]
    aliases = {}
    if out_prev is not None:
        in_specs.append(pl.BlockSpec(memory_space=pl.ANY))
        aliases = {len(args): 0}
        args.append(out_prev)
    return pl.pallas_call(
        functools.partial(_combine_kernel, alpha),
        grid=(nsteps,),
        in_specs=in_specs,
        out_specs=pl.BlockSpec((tm, d), lambda i: (i + blk0, 0)),
        out_shape=jax.ShapeDtypeStruct((total_rows, d), F32),
        input_output_aliases=aliases,
        compiler_params=_cparams(("parallel",)),
        name="combine_ln2",
    )(*args)


def _pack_in_proj(w_in, d_inner, n_ssm_heads, attn_dim):
    gn = SSM_GROUPS * D_STATE
    kv_dim = KV_HEADS * HEAD_DIM
    d = w_in.shape[0]
    o = 0
    seg = {}
    for name, width in (("z", d_inner), ("x", d_inner), ("B", gn), ("C", gn), ("dt", n_ssm_heads),
                        ("q", attn_dim), ("k", kv_dim), ("v", kv_dim), ("gs", d), ("ga", d)):
        seg[name] = w_in[:, o:o + width]
        o += width
    assert o == w_in.shape[1]
    dt_pad = jnp.pad(seg["dt"], ((0, 0), (0, LANES - n_ssm_heads)))
    order = (("z", seg["z"]), ("x", seg["x"]), ("gs", seg["gs"]), ("ga", seg["ga"]),
             ("B", seg["B"]), ("C", seg["C"]), ("dt", dt_pad))
    col, off = {}, 0
    for name, w in order:
        col[name] = off
        assert off % w.shape[1] == 0
        off += w.shape[1]
    w_f32grp = jnp.concatenate([w for _, w in order], axis=1).astype(BF16)
    w_qkv = jnp.concatenate([seg["q"], seg["k"], seg["v"]], axis=1).astype(BF16)
    return w_f32grp, w_qkv, col


def _largest_tile(n, cap):
    best = LANES
    for k in range(1, n // LANES + 1):
        if n % (k * LANES) == 0 and k * LANES <= cap:
            best = k * LANES
    return best


def _layer(h_in, batch, seq, alpha, w_in, conv_w, conv_b, dt_bias, a_log, d_skip, ssm_norm_w,
           w_ssm_out, attn_sinks, w_attn_out, b_gates, w_mix_out, ln1_g, ln1_b, w_router, b_router,
           w_gate_up, b_gate_up, w_down, b_down, ln2_g, ln2_b):
    t, d = h_in.shape
    d_inner = ssm_norm_w.shape[0]
    n_ssm_heads = dt_bias.shape[0]
    attn_dim = w_attn_out.shape[0]
    n_heads = attn_sinks.shape[0]
    n_exp = w_router.shape[1]
    gn = SSM_GROUPS * D_STATE

    w_f32grp, w_qkv, col = _pack_in_proj(w_in, d_inner, n_ssm_heads, attn_dim)
    proj = _matmul(h_in, w_f32grp, F32, IN_PROJ_ROW_TILE,
                   _largest_tile(w_f32grp.shape[1], IN_PROJ_MAX_COL_TILE))
    qkv = _matmul(h_in, w_qkv, BF16, IN_PROJ_ROW_TILE, _largest_tile(w_qkv.shape[1], QKV_MAX_COL_TILE))

    pad_h = (0, LANES - n_ssm_heads)
    y_ssm = _ssd(proj, batch, seq, col,
                 conv_w[:, :d_inner], conv_w[:, d_inner:d_inner + gn], conv_w[:, d_inner + gn:],
                 conv_b[None, :d_inner], conv_b[None, d_inner:d_inner + gn], conv_b[None, d_inner + gn:],
                 jnp.pad(dt_bias, pad_h)[None, :], jnp.pad(a_log, pad_h)[None, :],
                 jnp.repeat(d_skip, SSM_HEAD_DIM)[None, :], ssm_norm_w[None, :])
    y_att = _swa(qkv, attn_sinks, batch, seq, n_heads)

    wso, wao, wmix = w_ssm_out.astype(BF16), w_attn_out.astype(BF16), w_mix_out.astype(BF16)

    n_parts = MOE_SPLITS if t % (MOE_SPLITS * SC_CORES * SC_SUBCORES * SC_GATHER_CHUNK) == 0 else 1
    rows = t // n_parts
    out = None
    scatter_operands = ()
    for part in range(n_parts):
        row0 = part * rows
        h1, h1_packed = _merge(alpha, y_ssm, y_att, proj, col, h_in, wso, wao, wmix, b_gates[None, :d],
                               b_gates[None, d:], ln1_g[None, :], ln1_b[None, :], MERGE_ROW_TILE, row0, rows,
                               scatter_operands)
        idx_kt, gw_kt, rank_kt, counts = _router(h1, w_router, b_router[:, None], ROUTER_ROW_TILE)

        counts = counts[:, 0]
        padded = (counts + MOE_BLOCK - 1) // MOE_BLOCK * MOE_BLOCK
        pad_end = jnp.cumsum(padded)
        pad_start = pad_end - padded
        n_blocks = -(-(rows * TOP_K) // MOE_BLOCK) + n_exp
        n_rows = n_blocks * MOE_BLOCK
        expert_ids = jnp.arange(n_exp, dtype=jnp.int32)
        pad_start_of_slot = jnp.sum(
            jnp.where(idx_kt[None] == expert_ids[:, None, None], pad_start[:, None, None], 0), axis=0)
        dest_kt = pad_start_of_slot + rank_kt
        fill_rows = (pad_start + counts)[:, None] + jnp.arange(MOE_BLOCK, dtype=jnp.int32)[None, :]
        spare_rows = n_rows + jnp.arange(n_exp * MOE_BLOCK, dtype=jnp.int32).reshape(n_exp, MOE_BLOCK)
        fill_idx = jnp.where(fill_rows < pad_end[:, None], fill_rows, spare_rows).reshape(-1)
        block_row0 = jnp.arange(n_blocks, dtype=jnp.int32) * MOE_BLOCK
        block_expert = jnp.minimum(
            jnp.sum((pad_end[None, :] <= block_row0[:, None]).astype(jnp.int32), axis=1), n_exp - 1)
        n_used = (pad_end[-1:] // MOE_BLOCK).astype(jnp.int32)
        real_end_of_block = jnp.sum(jnp.where(block_expert[:, None] == expert_ids[None, :],
                                              (pad_start + counts)[None, :], 0), axis=1)
        block_valid = jnp.clip(real_end_of_block - block_row0, 0, MOE_BLOCK)
        gw_rows = jnp.pad(gw_kt.T, ((0, 0), (0, LANES - TOP_K)))

        scatter_operands = (dest_kt, fill_idx)
        xs = _sc_scatter_rows(h1_packed, dest_kt, fill_idx, n_rows + n_exp * MOE_BLOCK)
        ys = _experts(xs, block_expert, block_valid, n_used, w_gate_up, b_gate_up, w_down, b_down)
        y_slots = _sc_gather_rows(ys, dest_kt.reshape(-1))
        out = _combine(alpha, y_slots, gw_rows, h1, ln2_g[None, :], ln2_b[None, :], COMBINE_ROW_TILE,
                       out, row0, t)
    return out


def kernel(x, w_in, conv_w, conv_b, dt_bias, a_log, d_skip, ssm_norm_w, w_ssm_out, attn_sinks,
           w_attn_out, b_gates, w_mix_out, ln1_g, ln1_b, w_router, b_router, w_gate_up, b_gate_up,
           w_down, b_down, ln2_g, ln2_b):
    batch, seq, d = x.shape
    depth = w_in.shape[0]
    alpha = (2 * depth) ** 0.25
    h = x.reshape(batch * seq, d)
    for i in range(depth):
        h = _layer(h, batch, seq, alpha, w_in[i], conv_w[i], conv_b[i], dt_bias[i], a_log[i],
                   d_skip[i], ssm_norm_w[i], w_ssm_out[i], attn_sinks[i], w_attn_out[i], b_gates[i],
                   w_mix_out[i], ln1_g[i], ln1_b[i], w_router[i], b_router[i], w_gate_up[i],
                   b_gate_up[i], w_down[i], b_down[i], ln2_g[i], ln2_b[i])
    return h.reshape(batch, seq, d)
```

```python
import functools

import jax
import jax.numpy as jnp
from jax import lax
from jax.experimental import pallas as pl
from jax.experimental.pallas import tpu as pltpu
from jax.experimental.pallas import tpu_sc as plsc

SSM_HEAD_DIM = 64
SSM_GROUPS = 4
D_STATE = 128
CONV_WIDTH = 4
SSD_CHUNK = 128
SEQS_PER_STEP = 2
KV_HEADS = 4
HEAD_DIM = 64
WINDOW = 128
TOP_K = 4
SWIGLU_LIMIT = 7.0
SWIGLU_ALPHA = 1.702
MOE_BLOCK = 1024
LN_EPS = 1e-5
RMS_EPS = 1e-5

LANES = 128
SUBLANES = 8
NEG_BIG = -1e30
LOG2_E = 1.4426950408889634
F32 = jnp.float32
BF16 = jnp.bfloat16
VMEM_LIMIT = 56 * 1024 * 1024

IN_PROJ_ROW_TILE = 1024
IN_PROJ_MAX_COL_TILE = 2560
QKV_MAX_COL_TILE = 1536
MERGE_ROW_TILE = 512
ROUTER_ROW_TILE = 1024
COMBINE_ROW_TILE = 1024


def _cparams(sem):
    return pltpu.CompilerParams(dimension_semantics=sem, vmem_limit_bytes=VMEM_LIMIT)


def _sigmoid(x):
    return 0.5 + 0.5 * jnp.tanh(0.5 * x)


def _pack_bf16_pair(v):
    w = v.shape[1] // 2
    bits = lax.bitcast_convert_type(v.astype(BF16).astype(F32), jnp.uint32)
    return (bits[:, :w] >> 16) | (bits[:, w:] & jnp.uint32(0xFFFF0000))


def _unpack_bf16_pair(p):
    lo = lax.bitcast_convert_type(p << 16, F32)
    hi = lax.bitcast_convert_type(p & jnp.uint32(0xFFFF0000), F32)
    return jnp.concatenate([lo, hi], axis=1)


def _silu(x):
    h = 0.5 * x
    return h + h * jnp.tanh(h)


def _mm_kernel(x_ref, w_ref, o_ref):
    o_ref[...] = jnp.dot(x_ref[...].astype(BF16), w_ref[...],
                         preferred_element_type=F32).astype(o_ref.dtype)


def _matmul(x, w, out_dtype, tm, tn):
    m, k = x.shape
    n = w.shape[1]
    tm = min(tm, m)
    assert m % tm == 0 and n % tn == 0
    return pl.pallas_call(
        _mm_kernel,
        grid=(n // tn, m // tm),
        in_specs=[pl.BlockSpec((tm, k), lambda j, i: (i, 0)),
                  pl.BlockSpec((k, tn), lambda j, i: (0, j))],
        out_specs=pl.BlockSpec((tm, tn), lambda j, i: (i, j)),
        out_shape=jax.ShapeDtypeStruct((m, n), out_dtype),
        compiler_params=_cparams(("parallel", "parallel")),
        name="in_proj",
    )(x, w)


def _conv_silu(u_ref, prev_ref, w_ref, b_ref, first):
    cur = u_ref[...]
    prev = prev_ref[...]
    prev = jnp.where(first, jnp.zeros_like(prev), prev)
    row8 = lax.broadcasted_iota(jnp.int32, prev.shape, 0)
    acc = b_ref[...] + w_ref[CONV_WIDTH - 1:CONV_WIDTH, :] * cur
    for k in range(1, CONV_WIDTH):
        rolled = pltpu.roll(cur, k, axis=0)
        head = jnp.where(row8 < k, pltpu.roll(prev, k, axis=0), rolled[:SUBLANES])
        shifted = jnp.concatenate([head, rolled[SUBLANES:]], axis=0)
        j = CONV_WIDTH - 1 - k
        acc = acc + w_ref[j:j + 1, :] * shifted
    return _silu(acc)


def _expand_columns(v, e3_ref):
    terms, rest = [], v
    for _ in range(e3_ref.shape[0] // LANES):
        t = rest.astype(BF16)
        terms.append(t)
        rest = rest - t.astype(F32)
    return jnp.dot(jnp.concatenate(terms, axis=1), e3_ref[...], preferred_element_type=F32)


def _ssd_kernel(z_ref, x_ref, b_ref, c_ref, dt_ref, xp_ref, bp_ref, cp_ref,
                cwx_ref, cwb_ref, cwc_ref, cbx_ref, cbb_ref, cbc_ref,
                dtb_ref, alog_ref, dsk_ref, nw_ref, ehead_ref, ecol_ref,
                o_ref, st_ref):
    first = pl.program_id(1) == 0

    @pl.when(first)
    def _():
        st_ref[...] = jnp.zeros_like(st_ref)

    for s in range(z_ref.shape[0]):
        _ssd_chunk(z_ref.at[s], x_ref.at[s], b_ref.at[s], c_ref.at[s], dt_ref.at[s],
                   xp_ref.at[s], bp_ref.at[s], cp_ref.at[s],
                   cwx_ref, cwb_ref, cwc_ref, cbx_ref, cbb_ref, cbc_ref,
                   dtb_ref, alog_ref, dsk_ref, nw_ref, ehead_ref, ecol_ref,
                   o_ref.at[s], st_ref.at[s], first)


def _ssd_chunk(z_ref, x_ref, b_ref, c_ref, dt_ref, xp_ref, bp_ref, cp_ref,
               cwx_ref, cwb_ref, cwc_ref, cbx_ref, cbb_ref, cbc_ref,
               dtb_ref, alog_ref, dsk_ref, nw_ref, ehead_ref, ecol_ref, o_ref, st_ref, first):
    n = SSD_CHUNK
    heads_per_group = st_ref.shape[2] // SSM_HEAD_DIM
    gw = heads_per_group * SSM_HEAD_DIM

    xc = _conv_silu(x_ref, xp_ref, cwx_ref, cbx_ref, first)
    bc = _conv_silu(b_ref, bp_ref, cwb_ref, cbb_ref, first)
    cc = _conv_silu(c_ref, cp_ref, cwc_ref, cbc_ref, first)

    dt_in = dt_ref[...] + dtb_ref[...]
    dt = jnp.maximum(dt_in, 0.0) + jnp.log(1.0 + jnp.exp(-jnp.abs(dt_in)))
    a = dt * (-jnp.exp(alog_ref[...]))
    row = lax.broadcasted_iota(jnp.int32, (n, n), 0)
    col = lax.broadcasted_iota(jnp.int32, (n, n), 1)
    causal = row >= col
    tri = jnp.where(causal, 1.0, 0.0).astype(F32)
    acum = jnp.dot(tri, a, preferred_element_type=F32, precision=lax.Precision.HIGHEST)
    acum2 = acum * LOG2_E
    acum2_t = acum2.T
    a_last = acum[n - 1:n, :]
    lo_mask = lax.broadcasted_iota(jnp.int32, (n, LANES), 1) < SSM_HEAD_DIM
    dt_x = _expand_columns(dt, ehead_ref)
    d2e_x = _expand_columns(jnp.exp(a_last - acum), ehead_ref)
    eac_x = _expand_columns(jnp.exp(acum), ehead_ref)
    acum2_x = _expand_columns(acum2, ecol_ref)

    for g in range(SSM_GROUPS):
        xg = xc[:, g * gw:(g + 1) * gw]
        bg = bc[:, g * D_STATE:(g + 1) * D_STATE]
        cg = cc[:, g * D_STATE:(g + 1) * D_STATE]
        cg_b = cg.astype(BF16)
        bg_t = bg.T.astype(BF16)
        cb = jnp.dot(cg_b, bg_t, preferred_element_type=F32) * tri
        st_prev = st_ref[g]
        y_off = jnp.dot(cg_b, st_prev.astype(BF16), preferred_element_type=F32)
        y_parts, xw_parts, ea_parts = [], [], []
        for j in range(heads_per_group // 2):
            h0 = g * heads_per_group + 2 * j
            lanes = slice(j * LANES, (j + 1) * LANES)
            glanes = slice(g * gw + j * LANES, g * gw + (j + 1) * LANES)
            xs_pair = xg[:, lanes] * dt_x[:, glanes]
            xs_b = xs_pair.astype(BF16)
            xw_parts.append((xs_pair * d2e_x[:, glanes]).astype(BF16))
            ea_parts.append(eac_x[:, glanes])
            ys = []
            for h in (h0, h0 + 1):
                seg2 = acum2_x[:, h * n:(h + 1) * n] - acum2_t[h:h + 1, :]
                decay = jnp.exp2(jnp.minimum(seg2, 0.0))
                m = (cb * decay).astype(BF16)
                ys.append(jnp.dot(m, xs_b, preferred_element_type=F32))
            y_parts.append(jnp.where(lo_mask, ys[0], ys[1]))
        y_diag = jnp.concatenate(y_parts, axis=1)
        ea = jnp.concatenate(ea_parts, axis=1)
        xw = jnp.concatenate(xw_parts, axis=1)
        cols = slice(g * gw, (g + 1) * gw)
        y = y_diag + y_off * ea + xg * dsk_ref[:, cols]
        zg = z_ref[:, cols]
        y = y * _silu(zg)
        ms = jnp.mean(y * y, axis=-1, keepdims=True)
        o_ref[:, cols] = (y * lax.rsqrt(ms + RMS_EPS) * nw_ref[:, cols]).astype(o_ref.dtype)
        st_ref[g] = st_prev * ea[n - 1:n, :] + jnp.dot(bg_t, xw, preferred_element_type=F32)


def _ssd(proj, batch, seq, col, conv_wx, conv_wb, conv_wc, conv_bx, conv_bb, conv_bc,
         dt_bias, a_log, d_skip, norm_w):
    t, width = proj.shape
    d_inner = norm_w.shape[1]
    gn = SSM_GROUPS * D_STATE
    n = SSD_CHUNK
    nc = seq // n
    gw = d_inner // SSM_GROUPS
    ns = SEQS_PER_STEP if batch % SEQS_PER_STEP == 0 else 1
    proj3 = proj.reshape(batch, seq, width)
    n_heads = d_inner // SSM_HEAD_DIM
    head_id = jnp.arange(LANES, dtype=jnp.int32)[:, None]
    e_head = (head_id == (jnp.arange(d_inner, dtype=jnp.int32) // SSM_HEAD_DIM)[None, :]).astype(BF16)
    e_col = (head_id == (jnp.arange(n_heads * n, dtype=jnp.int32) // n)[None, :]).astype(BF16)
    e_head3 = jnp.tile(e_head, (2, 1))
    e_col3 = jnp.tile(e_col, (3, 1))

    def prev_rows(c):
        return jnp.maximum(c * (n // SUBLANES) - 1, 0)

    def const(b, c):
        return (0, 0)

    in_specs = [
        pl.BlockSpec((ns, n, d_inner), lambda b, c: (b, c, col["z"] // d_inner)),
        pl.BlockSpec((ns, n, d_inner), lambda b, c: (b, c, col["x"] // d_inner)),
        pl.BlockSpec((ns, n, gn), lambda b, c: (b, c, col["B"] // gn)),
        pl.BlockSpec((ns, n, gn), lambda b, c: (b, c, col["C"] // gn)),
        pl.BlockSpec((ns, n, LANES), lambda b, c: (b, c, col["dt"] // LANES)),
        pl.BlockSpec((ns, SUBLANES, d_inner), lambda b, c: (b, prev_rows(c), col["x"] // d_inner)),
        pl.BlockSpec((ns, SUBLANES, gn), lambda b, c: (b, prev_rows(c), col["B"] // gn)),
        pl.BlockSpec((ns, SUBLANES, gn), lambda b, c: (b, prev_rows(c), col["C"] // gn)),
        pl.BlockSpec((CONV_WIDTH, d_inner), const),
        pl.BlockSpec((CONV_WIDTH, gn), const),
        pl.BlockSpec((CONV_WIDTH, gn), const),
        pl.BlockSpec((1, d_inner), const),
        pl.BlockSpec((1, gn), const),
        pl.BlockSpec((1, gn), const),
        pl.BlockSpec((1, LANES), const),
        pl.BlockSpec((1, LANES), const),
        pl.BlockSpec((1, d_inner), const),
        pl.BlockSpec((1, d_inner), const),
        pl.BlockSpec(e_head3.shape, const),
        pl.BlockSpec(e_col3.shape, const),
    ]
    out = pl.pallas_call(
        _ssd_kernel,
        grid=(batch // ns, nc),
        in_specs=in_specs,
        out_specs=pl.BlockSpec((ns, n, d_inner), lambda b, c: (b, c, 0)),
        out_shape=jax.ShapeDtypeStruct((batch, seq, d_inner), BF16),
        scratch_shapes=[pltpu.VMEM((ns, SSM_GROUPS, D_STATE, gw), F32)],
        compiler_params=_cparams(("arbitrary", "arbitrary")),
        name="ssd_mixer",
    )(proj3, proj3, proj3, proj3, proj3, proj3, proj3, proj3,
      conv_wx, conv_wb, conv_wc, conv_bx, conv_bb, conv_bc, dt_bias, a_log, d_skip, norm_w,
      e_head3, e_col3)
    return out.reshape(t, d_inner)


def _swa_kernel(sink_ref, q_ref, kc_ref, vc_ref, kp_ref, vp_ref, o_ref):
    for s in range(q_ref.shape[0]):
        _swa_block(sink_ref, q_ref.at[s], kc_ref.at[s], vc_ref.at[s], kp_ref.at[s], vp_ref.at[s],
                   o_ref.at[s])


def _swa_block(sink_ref, q_ref, kc_ref, vc_ref, kp_ref, vp_ref, o_ref):
    i = pl.program_id(1)
    n = WINDOW
    n_heads = q_ref.shape[1] // HEAD_DIM
    grp = n_heads // KV_HEADS
    qpos = lax.broadcasted_iota(jnp.int32, (n, 2 * n), 0) + n
    kpos = lax.broadcasted_iota(jnp.int32, (n, 2 * n), 1)
    diff = qpos - kpos
    mask = (diff >= 0) & (diff < n) & ((kpos >= n) | (i > 0))
    lo_q = lax.broadcasted_iota(jnp.int32, (n, LANES), 1) < HEAD_DIM
    lo_k = lax.broadcasted_iota(jnp.int32, (2 * n, LANES), 1) < HEAD_DIM
    scale = HEAD_DIM ** -0.5
    exp2_scale = scale * LOG2_E
    for p in range(KV_HEADS // 2):
        lanes = slice(p * LANES, (p + 1) * LANES)
        kk = jnp.concatenate([kp_ref[:, lanes], kc_ref[:, lanes]], axis=0).astype(F32)
        vv = jnp.concatenate([vp_ref[:, lanes], vc_ref[:, lanes]], axis=0).astype(F32)
        kk_sw = pltpu.roll(kk, HEAD_DIM, axis=1)
        vv_sw = pltpu.roll(vv, HEAD_DIM, axis=1)
        for par in range(2):
            g = 2 * p + par
            k2 = (jnp.where(lo_k, kk, kk_sw) if par == 0 else jnp.where(lo_k, kk_sw, kk)).astype(BF16)
            v2 = (jnp.where(lo_k, vv, vv_sw) if par == 0 else jnp.where(lo_k, vv_sw, vv)).astype(BF16)
            q_rows = []
            for hq in range(grp):
                h = g * grp + hq
                q_pair = q_ref[:, (h // 2) * LANES:(h // 2 + 1) * LANES]
                keep = lo_q if h % 2 == 0 else jnp.logical_not(lo_q)
                q_rows.append(jnp.where(keep, q_pair, jnp.zeros_like(q_pair)))
            s_all = lax.dot_general(jnp.concatenate(q_rows, axis=0), k2, (((1,), (1,)), ((), ())),
                                    preferred_element_type=F32)
            probs, dens = [], []
            for hq in range(grp):
                s = jnp.where(mask, s_all[hq * n:(hq + 1) * n, :], NEG_BIG)
                sink = sink_ref[g * grp + hq] / scale
                mx = jnp.maximum(jnp.max(s, axis=-1, keepdims=True), sink)
                pr = jnp.exp2((s - mx) * exp2_scale)
                dens.append(jnp.sum(pr, axis=-1, keepdims=True) + jnp.exp2((sink - mx) * exp2_scale))
                probs.append(pr.astype(BF16))
            o_all = jnp.dot(jnp.concatenate(probs, axis=0), v2, preferred_element_type=F32)
            for qp in range(grp // 2):
                outs = [o_all[(2 * qp + hh) * n:(2 * qp + hh + 1) * n, :] / dens[2 * qp + hh] for hh in range(2)]
                pair = (g * grp) // 2 + qp
                o_ref[:, pair * LANES:(pair + 1) * LANES] = jnp.where(lo_q, outs[0], outs[1]).astype(o_ref.dtype)


def _swa(qkv, sinks, batch, seq, n_heads):
    t, width = qkv.shape
    n = WINDOW
    nb = seq // n
    qw = n_heads * HEAD_DIM
    kw = KV_HEADS * HEAD_DIM
    k_blk = qw // kw
    v_blk = k_blk + 1
    ns = SEQS_PER_STEP if batch % SEQS_PER_STEP == 0 else 1
    qkv3 = qkv.reshape(batch, seq, width)

    def prev(i):
        return jnp.maximum(i - 1, 0)

    out = pl.pallas_call(
        _swa_kernel,
        grid=(batch // ns, nb),
        in_specs=[pl.BlockSpec(memory_space=pltpu.SMEM),
                  pl.BlockSpec((ns, n, qw), lambda b, i: (b, i, 0)),
                  pl.BlockSpec((ns, n, kw), lambda b, i: (b, i, k_blk)),
                  pl.BlockSpec((ns, n, kw), lambda b, i: (b, i, v_blk)),
                  pl.BlockSpec((ns, n, kw), lambda b, i: (b, prev(i), k_blk)),
                  pl.BlockSpec((ns, n, kw), lambda b, i: (b, prev(i), v_blk))],
        out_specs=pl.BlockSpec((ns, n, qw), lambda b, i: (b, i, 0)),
        out_shape=jax.ShapeDtypeStruct((batch, seq, qw), BF16),
        compiler_params=_cparams(("parallel", "parallel")),
        name="swa",
    )(sinks, qkv3, qkv3, qkv3, qkv3, qkv3)
    return out.reshape(t, qw)


def _layer_norm(v, g, b):
    mu = jnp.mean(v, axis=-1, keepdims=True)
    d = v - mu
    var = jnp.mean(d * d, axis=-1, keepdims=True)
    return d * lax.rsqrt(var + LN_EPS) * g + b


def _merge_kernel(alpha, ys_ref, ya_ref, gs_ref, ga_ref, x_ref, wso_ref, wao_ref, wmix_ref,
                  bgs_ref, bga_ref, lg_ref, lb_ref, *rest):
    h_ref, hp_ref = rest[-2:]
    y_ssm = jnp.dot(ys_ref[...], wso_ref[...], preferred_element_type=F32)
    y_att = jnp.dot(ya_ref[...], wao_ref[...], preferred_element_type=F32)
    merged = (_sigmoid(gs_ref[...] + bgs_ref[...]) * y_ssm
              + _sigmoid(ga_ref[...] + bga_ref[...]) * y_att)
    mix = jnp.dot(merged.astype(BF16), wmix_ref[...], preferred_element_type=F32)
    h = _layer_norm(alpha * x_ref[...] + mix, lg_ref[...], lb_ref[...])
    h_ref[...] = h
    hp_ref[...] = _pack_bf16_pair(h)


def _merge(alpha, y_ssm, y_att, proj, col, x, wso, wao, wmix, bgs, bga, lg, lb, tm, row0, rows, after):
    d = x.shape[1]
    tm = min(tm, rows)
    assert rows % tm == 0 and row0 % tm == 0
    blk0 = row0 // tm
    di = y_ssm.shape[1]
    da = y_att.shape[1]

    def const(i):
        return (0, 0)

    ordering = list(after)
    return pl.pallas_call(
        functools.partial(_merge_kernel, alpha),
        grid=(rows // tm,),
        in_specs=[pl.BlockSpec((tm, di), lambda i: (i + blk0, 0)),
                  pl.BlockSpec((tm, da), lambda i: (i + blk0, 0)),
                  pl.BlockSpec((tm, d), lambda i: (i + blk0, col["gs"] // d)),
                  pl.BlockSpec((tm, d), lambda i: (i + blk0, col["ga"] // d)),
                  pl.BlockSpec((tm, d), lambda i: (i + blk0, 0)),
                  pl.BlockSpec((di, d), const),
                  pl.BlockSpec((da, d), const),
                  pl.BlockSpec((d, d), const),
                  pl.BlockSpec((1, d), const),
                  pl.BlockSpec((1, d), const),
                  pl.BlockSpec((1, d), const),
                  pl.BlockSpec((1, d), const)] + [pl.BlockSpec(memory_space=pl.ANY)] * len(ordering),
        out_specs=[pl.BlockSpec((tm, d), lambda i: (i, 0)),
                   pl.BlockSpec((tm, d // 2), lambda i: (i, 0))],
        out_shape=[jax.ShapeDtypeStruct((rows, d), F32),
                   jax.ShapeDtypeStruct((rows, d // 2), jnp.uint32)],
        compiler_params=_cparams(("parallel",)),
        name="merge_ln1",
    )(y_ssm, y_att, proj, proj, x, wso, wao, wmix, bgs, bga, lg, lb, *ordering)


ROUTER_SUBTILE = 128


def _router_kernel(h_ref, wr_ref, br_ref, idx_ref, gw_ref, rank_ref, cnt_ref, base_ref):
    i = pl.program_id(0)
    tm = h_ref.shape[0]
    n_exp = br_ref.shape[0]
    sub = ROUTER_SUBTILE

    @pl.when(i == 0)
    def _():
        base_ref[...] = jnp.zeros_like(base_ref)

    h = h_ref[...]
    h1 = h.astype(BF16)
    h2 = (h - h1.astype(F32)).astype(BF16)
    first = jnp.dot(h1, wr_ref[...], preferred_element_type=F32)
    logits_t = (first[:, :LANES] + first[:, LANES:]
                + jnp.dot(h2, wr_ref[:, :LANES], preferred_element_type=F32))
    logits = logits_t.T[:n_exp, :] + br_ref[...]
    eid = lax.broadcasted_iota(jnp.int32, (n_exp, tm), 0)
    work = logits
    vals, idxs = [], []
    for _ in range(TOP_K):
        mx = jnp.max(work, axis=0, keepdims=True)
        sel = jnp.min(jnp.where(work == mx, eid, n_exp), axis=0, keepdims=True)
        vals.append(mx)
        idxs.append(sel)
        work = jnp.where(eid == sel, -jnp.inf, work)
    exps = [jnp.exp(v - vals[0]) for v in vals]
    den = exps[0]
    for e in exps[1:]:
        den = den + e
    r = lax.broadcasted_iota(jnp.int32, (sub, sub), 0)
    c = lax.broadcasted_iota(jnp.int32, (sub, sub), 1)
    tri = jnp.where(r < c, 1.0, 0.0).astype(BF16)
    base = base_ref[...]
    for k in range(TOP_K):
        onehot = eid == idxs[k]
        oh = jnp.where(onehot, 1.0, 0.0)
        ranks = []
        for s in range(tm // sub):
            lanes = slice(s * sub, (s + 1) * sub)
            before = jnp.dot(oh[:, lanes].astype(BF16), tri, preferred_element_type=F32)
            ranks.append(jnp.sum(jnp.where(onehot[:, lanes], before + base, 0.0), axis=0, keepdims=True))
            base = base + jnp.sum(oh[:, lanes], axis=1, keepdims=True)
        idx_ref[k:k + 1, :] = idxs[k]
        gw_ref[k:k + 1, :] = exps[k] / den
        rank_ref[k:k + 1, :] = jnp.concatenate(ranks, axis=1).astype(jnp.int32)
    base_ref[...] = base
    cnt_ref[...] = jnp.broadcast_to(base, cnt_ref.shape).astype(jnp.int32)


def _router(h, w_router, b_router, tm):
    t, d = h.shape
    n_exp = w_router.shape[1]
    tm = min(tm, t)
    assert tm % ROUTER_SUBTILE == 0 and n_exp <= LANES
    w_padded = jnp.pad(w_router, ((0, 0), (0, LANES - n_exp)))
    w_hi = w_padded.astype(BF16)
    w_lo = (w_padded - w_hi.astype(F32)).astype(BF16)
    w_padded = jnp.concatenate([w_hi, w_lo], axis=1)
    outs = pl.pallas_call(
        _router_kernel,
        grid=(t // tm,),
        in_specs=[pl.BlockSpec((tm, d), lambda i: (i, 0)),
                  pl.BlockSpec((d, 2 * LANES), lambda i: (0, 0)),
                  pl.BlockSpec((n_exp, 1), lambda i: (0, 0))],
        out_specs=[pl.BlockSpec((TOP_K, tm), lambda i: (0, i)),
                   pl.BlockSpec((TOP_K, tm), lambda i: (0, i)),
                   pl.BlockSpec((TOP_K, tm), lambda i: (0, i)),
                   pl.BlockSpec((n_exp, LANES), lambda i: (0, 0))],
        out_shape=[jax.ShapeDtypeStruct((TOP_K, t), jnp.int32),
                   jax.ShapeDtypeStruct((TOP_K, t), F32),
                   jax.ShapeDtypeStruct((TOP_K, t), jnp.int32),
                   jax.ShapeDtypeStruct((n_exp, LANES), jnp.int32)],
        scratch_shapes=[pltpu.VMEM((n_exp, 1), F32)],
        compiler_params=_cparams(("arbitrary",)),
        name="router",
    )(h, w_padded, b_router)
    return outs


SC_CORES = 2
SC_SUBCORES = 16
SC_GATHER_CHUNK = 128
MOE_SPLITS = 2


def _sc_gather_rows(table, idx):
    b = idx.shape[0]
    d = table.shape[1]
    n_workers = SC_CORES * SC_SUBCORES
    chunk = SC_GATHER_CHUNK
    assert b % (n_workers * chunk) == 0
    per_worker = b // n_workers
    n_chunks = per_worker // chunk
    mesh = plsc.VectorSubcoreMesh(core_axis_name="c", subcore_axis_name="s",
                                  num_cores=SC_CORES, num_subcores=SC_SUBCORES)

    def body(table_hbm, idx_hbm, out_hbm, idx_v, rows_v, sem):
        wid = lax.axis_index("s") * SC_CORES + lax.axis_index("c")
        base = wid * per_worker

        @pl.loop(0, n_chunks)
        def _(c):
            off = pl.multiple_of(base + c * chunk, SUBLANES)
            pltpu.sync_copy(idx_hbm.at[pl.ds(off, chunk)], idx_v)
            pltpu.async_copy(table_hbm.at[idx_v], rows_v, sem).wait()
            pltpu.sync_copy(rows_v, out_hbm.at[pl.ds(off, chunk)])

    return pl.kernel(
        body,
        out_type=jax.ShapeDtypeStruct((b, d), table.dtype),
        mesh=mesh,
        scratch_types=[pltpu.VMEM((chunk,), jnp.int32),
                       pltpu.VMEM((chunk, d), table.dtype),
                       pltpu.SemaphoreType.DMA],
        name="sc_gather_rows",
    )(table, idx)


def _sc_scatter_rows(src, dest_kt, fill_idx, n_out_rows):
    t, d = src.shape
    n_k = dest_kt.shape[0]
    n_fill = fill_idx.shape[0]
    n_workers = SC_CORES * SC_SUBCORES
    chunk = SC_GATHER_CHUNK
    assert t % (n_workers * chunk) == 0 and n_fill % (n_workers * chunk) == 0
    tok_per_worker = t // n_workers
    fill_per_worker = n_fill // n_workers
    mesh = plsc.VectorSubcoreMesh(core_axis_name="c", subcore_axis_name="s",
                                  num_cores=SC_CORES, num_subcores=SC_SUBCORES)

    def body(src_hbm, dest_hbm, fill_hbm, zeros_hbm, out_hbm, idx_v, rows_v):
        wid = lax.axis_index("s") * SC_CORES + lax.axis_index("c")

        pltpu.sync_copy(zeros_hbm, rows_v)

        @pl.loop(0, fill_per_worker // chunk)
        def _(c):
            off = pl.multiple_of(wid * fill_per_worker + c * chunk, SUBLANES)
            pltpu.sync_copy(fill_hbm.at[pl.ds(off, chunk)], idx_v)
            pltpu.sync_copy(rows_v, out_hbm.at[idx_v])

        @pl.loop(0, tok_per_worker // chunk)
        def _(c):
            off = pl.multiple_of(wid * tok_per_worker + c * chunk, SUBLANES)
            pltpu.sync_copy(src_hbm.at[pl.ds(off, chunk)], rows_v)
            for k in range(n_k):
                pltpu.sync_copy(dest_hbm.at[pl.ds(k * t + off, chunk)], idx_v)
                pltpu.sync_copy(rows_v, out_hbm.at[idx_v])

    return pl.kernel(
        body,
        out_type=jax.ShapeDtypeStruct((n_out_rows, d), src.dtype),
        mesh=mesh,
        scratch_types=[pltpu.VMEM((chunk,), jnp.int32),
                       pltpu.VMEM((chunk, d), src.dtype)],
        name="sc_scatter_rows",
    )(src, dest_kt.reshape(-1), fill_idx, jnp.zeros((chunk, d), src.dtype))


EXPERT_SUBROWS = 128


def _expert_kernel(be_ref, nxt_ref, eslot_ref, nsub_ref, nu_ref, x_ref, wgu_hbm, bgu_ref, wd_hbm,
                   bd_ref, y_ref, wgu_f, wd_f, wsem, wgu_b, wd_b):
    i = pl.program_id(0)
    n_used = nu_ref[0]
    expert = be_ref[i]
    changed = jnp.logical_or(i == 0, expert != be_ref[jnp.maximum(i - 1, 0)])

    def weight_copies(e, slot):
        return (pltpu.make_async_copy(wgu_hbm.at[e], wgu_f.at[slot], wsem.at[0, slot]),
                pltpu.make_async_copy(wd_hbm.at[e], wd_f.at[slot], wsem.at[1, slot]))

    @pl.when(jnp.logical_and(changed, i < n_used))
    def _():
        slot = eslot_ref[i]

        @pl.when(i == 0)
        def _():
            for cp in weight_copies(expert, slot):
                cp.start()

        for cp in weight_copies(expert, slot):
            cp.wait()
        nxt = nxt_ref[i]

        @pl.when(nxt >= 0)
        def _():
            for cp in weight_copies(nxt, 1 - slot):
                cp.start()

        wgu_b[...] = wgu_f[slot].astype(BF16)
        wd_b[...] = wd_f[slot].astype(BF16)

    def ffn(rows):
        d_ff = wd_b.shape[0]
        xb = _unpack_bf16_pair(x_ref[0:rows, :]).astype(BF16)
        half = d_ff // 2
        y = bd_ref[0]
        for j in range(2):
            gcols = slice(j * half, (j + 1) * half)
            lcols = slice(d_ff + j * half, d_ff + (j + 1) * half)
            glu = jnp.dot(xb, wgu_b[:, gcols], preferred_element_type=F32) + bgu_ref[0][:, gcols]
            lin = jnp.dot(xb, wgu_b[:, lcols], preferred_element_type=F32) + bgu_ref[0][:, lcols]
            glu = jnp.minimum(glu, SWIGLU_LIMIT)
            lin = jnp.clip(lin, -SWIGLU_LIMIT, SWIGLU_LIMIT)
            act = glu * _sigmoid(SWIGLU_ALPHA * glu) * (lin + 1.0)
            y = y + jnp.dot(act.astype(BF16), wd_b[gcols, :], preferred_element_type=F32)
        y_ref[0:rows, :] = _pack_bf16_pair(y)
        if rows < y_ref.shape[0]:
            y_ref[rows:, :] = jnp.zeros((y_ref.shape[0] - rows, y_ref.shape[1]), y_ref.dtype)

    n_sub = nsub_ref[i]
    for sub in range(1, y_ref.shape[0] // EXPERT_SUBROWS + 1):
        @pl.when(jnp.logical_and(i < n_used, n_sub == sub))
        def _(sub=sub):
            ffn(sub * EXPERT_SUBROWS)


def _experts(xs, block_expert, block_valid, n_used, wgu, bgu, wd, bd):
    d = wgu.shape[1]
    dp = xs.shape[1]
    assert 2 * dp == d
    n_blocks = block_expert.shape[0]
    n_exp, _, two_f = wgu.shape
    d_ff = wd.shape[1]
    rows = MOE_BLOCK
    n_rows = n_blocks * rows
    blk = jnp.arange(n_blocks, dtype=jnp.int32)
    is_first = jnp.concatenate([jnp.ones((1,), bool), block_expert[1:] != block_expert[:-1]])
    first_used = jnp.logical_and(is_first, blk < n_used[0])
    first_pos = jnp.where(first_used, blk, n_blocks)
    next_first = jnp.concatenate([lax.cummin(first_pos[::-1])[::-1][1:],
                                  jnp.full((1,), n_blocks, jnp.int32)])
    next_expert = jnp.where(next_first < n_blocks,
                            block_expert[jnp.minimum(next_first, n_blocks - 1)], -1).astype(jnp.int32)
    expert_slot = ((jnp.cumsum(is_first.astype(jnp.int32)) - 1) % 2).astype(jnp.int32)
    n_sub = jnp.clip(-(-block_valid // EXPERT_SUBROWS), 1, rows // EXPERT_SUBROWS).astype(jnp.int32)

    def last_used(i, be, nxt, es, ns, nu):
        return (jnp.minimum(i, jnp.maximum(nu[0] - 1, 0)), 0)

    grid_spec = pltpu.PrefetchScalarGridSpec(
        num_scalar_prefetch=5,
        grid=(n_blocks,),
        in_specs=[
            pl.BlockSpec((rows, dp), last_used),
            pl.BlockSpec(memory_space=pl.ANY),
            pl.BlockSpec((1, 1, two_f), lambda i, be, nxt, es, ns, nu: (be[i], 0, 0)),
            pl.BlockSpec(memory_space=pl.ANY),
            pl.BlockSpec((1, 1, d), lambda i, be, nxt, es, ns, nu: (be[i], 0, 0)),
        ],
        out_specs=pl.BlockSpec((rows, dp), last_used),
        scratch_shapes=[pltpu.VMEM((2, d, two_f), F32),
                        pltpu.VMEM((2, d_ff, d), F32),
                        pltpu.SemaphoreType.DMA((2, 2)),
                        pltpu.VMEM((d, two_f), BF16),
                        pltpu.VMEM((d_ff, d), BF16)],
    )
    return pl.pallas_call(
        _expert_kernel,
        grid_spec=grid_spec,
        out_shape=jax.ShapeDtypeStruct((n_rows, dp), jnp.uint32),
        compiler_params=_cparams(("arbitrary",)),
        name="experts",
    )(block_expert, next_expert, expert_slot, n_sub, n_used, xs, wgu, bgu.reshape(n_exp, 1, two_f), wd,
      bd.reshape(n_exp, 1, d))


def _combine_kernel(alpha, *refs):
    ys_refs = refs[:TOP_K]
    gw_ref, h_ref, lg_ref, lb_ref = refs[TOP_K:TOP_K + 4]
    o_ref = refs[-1]
    gw = gw_ref[...]
    acc = alpha * h_ref[...]
    for k in range(TOP_K):
        acc = acc + gw[:, k:k + 1] * _unpack_bf16_pair(ys_refs[k][...])
    o_ref[...] = _layer_norm(acc, lg_ref[...], lb_ref[...])


def _combine(alpha, y_slots, gw_rows, h, lg, lb, tm, out_prev, row0, total_rows):
    rows, d = h.shape
    tm = min(tm, rows)
    assert rows % tm == 0 and row0 % tm == 0
    nsteps = rows // tm
    blk0 = row0 // tm
    ys_specs = [pl.BlockSpec((tm, d // 2), functools.partial(lambda i, k: (k * nsteps + i, 0), k=k))
                for k in range(TOP_K)]
    in_specs = ys_specs + [pl.BlockSpec((tm, LANES), lambda i: (i, 0)),
                           pl.BlockSpec((tm, d), lambda i: (i, 0)),
                           pl.BlockSpec((1, d), lambda i: (0, 0)),
                           pl.BlockSpec((1, d), lambda i: (0, 0))]
    args = [y_slots] * TOP_K + [gw_rows, h, lg, lb]
    aliases = {}
    if out_prev is not None:
        in_specs.append(pl.BlockSpec(memory_space=pl.ANY))
        aliases = {len(args): 0}
        args.append(out_prev)
    return pl.pallas_call(
        functools.partial(_combine_kernel, alpha),
        grid=(nsteps,),
        in_specs=in_specs,
        out_specs=pl.BlockSpec((tm, d), lambda i: (i + blk0, 0)),
        out_shape=jax.ShapeDtypeStruct((total_rows, d), F32),
        input_output_aliases=aliases,
        compiler_params=_cparams(("parallel",)),
        name="combine_ln2",
    )(*args)


def _pack_in_proj(w_in, d_inner, n_ssm_heads, attn_dim):
    gn = SSM_GROUPS * D_STATE
    kv_dim = KV_HEADS * HEAD_DIM
    d = w_in.shape[0]
    o = 0
    seg = {}
    for name, width in (("z", d_inner), ("x", d_inner), ("B", gn), ("C", gn), ("dt", n_ssm_heads),
                        ("q", attn_dim), ("k", kv_dim), ("v", kv_dim), ("gs", d), ("ga", d)):
        seg[name] = w_in[:, o:o + width]
        o += width
    assert o == w_in.shape[1]
    dt_pad = jnp.pad(seg["dt"], ((0, 0), (0, LANES - n_ssm_heads)))
    order = (("z", seg["z"]), ("x", seg["x"]), ("gs", seg["gs"]), ("ga", seg["ga"]),
             ("B", seg["B"]), ("C", seg["C"]), ("dt", dt_pad))
    col, off = {}, 0
    for name, w in order:
        col[name] = off
        assert off % w.shape[1] == 0
        off += w.shape[1]
    w_f32grp = jnp.concatenate([w for _, w in order], axis=1).astype(BF16)
    w_qkv = jnp.concatenate([seg["q"], seg["k"], seg["v"]], axis=1).astype(BF16)
    return w_f32grp, w_qkv, col


def _largest_tile(n, cap):
    best = LANES
    for k in range(1, n // LANES + 1):
        if n % (k * LANES) == 0 and k * LANES <= cap:
            best = k * LANES
    return best


def _layer(h_in, batch, seq, alpha, w_in, conv_w, conv_b, dt_bias, a_log, d_skip, ssm_norm_w,
           w_ssm_out, attn_sinks, w_attn_out, b_gates, w_mix_out, ln1_g, ln1_b, w_router, b_router,
           w_gate_up, b_gate_up, w_down, b_down, ln2_g, ln2_b):
    t, d = h_in.shape
    d_inner = ssm_norm_w.shape[0]
    n_ssm_heads = dt_bias.shape[0]
    attn_dim = w_attn_out.shape[0]
    n_heads = attn_sinks.shape[0]
    n_exp = w_router.shape[1]
    gn = SSM_GROUPS * D_STATE

    w_f32grp, w_qkv, col = _pack_in_proj(w_in, d_inner, n_ssm_heads, attn_dim)
    proj = _matmul(h_in, w_f32grp, F32, IN_PROJ_ROW_TILE,
                   _largest_tile(w_f32grp.shape[1], IN_PROJ_MAX_COL_TILE))
    qkv = _matmul(h_in, w_qkv, BF16, IN_PROJ_ROW_TILE, _largest_tile(w_qkv.shape[1], QKV_MAX_COL_TILE))

    pad_h = (0, LANES - n_ssm_heads)
    y_ssm = _ssd(proj, batch, seq, col,
                 conv_w[:, :d_inner], conv_w[:, d_inner:d_inner + gn], conv_w[:, d_inner + gn:],
                 conv_b[None, :d_inner], conv_b[None, d_inner:d_inner + gn], conv_b[None, d_inner + gn:],
                 jnp.pad(dt_bias, pad_h)[None, :], jnp.pad(a_log, pad_h)[None, :],
                 jnp.repeat(d_skip, SSM_HEAD_DIM)[None, :], ssm_norm_w[None, :])
    y_att = _swa(qkv, attn_sinks, batch, seq, n_heads)

    wso, wao, wmix = w_ssm_out.astype(BF16), w_attn_out.astype(BF16), w_mix_out.astype(BF16)

    n_parts = MOE_SPLITS if t % (MOE_SPLITS * SC_CORES * SC_SUBCORES * SC_GATHER_CHUNK) == 0 else 1
    rows = t // n_parts
    out = None
    scatter_operands = ()
    for part in range(n_parts):
        row0 = part * rows
        h1, h1_packed = _merge(alpha, y_ssm, y_att, proj, col, h_in, wso, wao, wmix, b_gates[None, :d],
                               b_gates[None, d:], ln1_g[None, :], ln1_b[None, :], MERGE_ROW_TILE, row0, rows,
                               scatter_operands)
        idx_kt, gw_kt, rank_kt, counts = _router(h1, w_router, b_router[:, None], ROUTER_ROW_TILE)

        counts = counts[:, 0]
        padded = (counts + MOE_BLOCK - 1) // MOE_BLOCK * MOE_BLOCK
        pad_end = jnp.cumsum(padded)
        pad_start = pad_end - padded
        n_blocks = -(-(rows * TOP_K) // MOE_BLOCK) + n_exp
        n_rows = n_blocks * MOE_BLOCK
        expert_ids = jnp.arange(n_exp, dtype=jnp.int32)
        pad_start_of_slot = jnp.sum(
            jnp.where(idx_kt[None] == expert_ids[:, None, None], pad_start[:, None, None], 0), axis=0)
        dest_kt = pad_start_of_slot + rank_kt
        fill_rows = (pad_start + counts)[:, None] + jnp.arange(MOE_BLOCK, dtype=jnp.int32)[None, :]
        spare_rows = n_rows + jnp.arange(n_exp * MOE_BLOCK, dtype=jnp.int32).reshape(n_exp, MOE_BLOCK)
        fill_idx = jnp.where(fill_rows < pad_end[:, None], fill_rows, spare_rows).reshape(-1)
        block_row0 = jnp.arange(n_blocks, dtype=jnp.int32) * MOE_BLOCK
        block_expert = jnp.minimum(
            jnp.sum((pad_end[None, :] <= block_row0[:, None]).astype(jnp.int32), axis=1), n_exp - 1)
        n_used = (pad_end[-1:] // MOE_BLOCK).astype(jnp.int32)
        real_end_of_block = jnp.sum(jnp.where(block_expert[:, None] == expert_ids[None, :],
                                              (pad_start + counts)[None, :], 0), axis=1)
        block_valid = jnp.clip(real_end_of_block - block_row0, 0, MOE_BLOCK)
        gw_rows = jnp.pad(gw_kt.T, ((0, 0), (0, LANES - TOP_K)))

        scatter_operands = (dest_kt, fill_idx)
        xs = _sc_scatter_rows(h1_packed, dest_kt, fill_idx, n_rows + n_exp * MOE_BLOCK)
        ys = _experts(xs, block_expert, block_valid, n_used, w_gate_up, b_gate_up, w_down, b_down)
        y_slots = _sc_gather_rows(ys, dest_kt.reshape(-1))
        out = _combine(alpha, y_slots, gw_rows, h1, ln2_g[None, :], ln2_b[None, :], COMBINE_ROW_TILE,
                       out, row0, t)
    return out


def kernel(x, w_in, conv_w, conv_b, dt_bias, a_log, d_skip, ssm_norm_w, w_ssm_out, attn_sinks,
           w_attn_out, b_gates, w_mix_out, ln1_g, ln1_b, w_router, b_router, w_gate_up, b_gate_up,
           w_down, b_down, ln2_g, ln2_b):
    batch, seq, d = x.shape
    depth = w_in.shape[0]
    alpha = (2 * depth) ** 0.25
    h = x.reshape(batch * seq, d)
    for i in range(depth):
        h = _layer(h, batch, seq, alpha, w_in[i], conv_w[i], conv_b[i], dt_bias[i], a_log[i],
                   d_skip[i], ssm_norm_w[i], w_ssm_out[i], attn_sinks[i], w_attn_out[i], b_gates[i],
                   w_mix_out[i], ln1_g[i], ln1_b[i], w_router[i], b_router[i], w_gate_up[i],
                   b_gate_up[i], w_down[i], b_down[i], ln2_g[i], ln2_b[i])
    return h.reshape(batch, seq, d)
```
